```python
import math
import jax, jax.numpy as jnp
from jax import lax
import numpy as np


D_MODEL = 2048
BATCH = 8
SEQ = 8192
DEPTH = 4

F32 = jnp.float32
EPS = 1e-6

HG_HEADS = 6
HG_DK = 128
HG_DV = 128
HG_WIDTH = HG_HEADS * HG_DV
HG_CHUNK = 64

RET_HEADS = 6
RET_DK = 64
RET_DV = 128
RET_WIDTH = RET_HEADS * RET_DV
RET_CHUNK = 128
ROPE_BASE = 10000.0

DIL_SLOTS = 4
DIL_HD = 128
DIL_GROUPS = ((128, 1), (512, 4), (2048, 16))
DIL_WIDTH = DIL_SLOTS * DIL_HD
DIL_HEADS = DIL_SLOTS * len(DIL_GROUPS)

MIX_WIDTH = HG_WIDTH + RET_WIDTH + DIL_WIDTH
D_FF = 4 * D_MODEL
REL_BUCKETS = 32
REL_MAX_DIST = 1024

IN_SPLITS = (HG_HEADS * HG_DK, HG_WIDTH, HG_HEADS * HG_DK, HG_HEADS * HG_DK, HG_WIDTH,
             RET_HEADS * RET_DK, RET_HEADS * RET_DK, RET_WIDTH, RET_WIDTH) + (DIL_WIDTH,) * (3 * len(DIL_GROUPS))
IN_WIDTH = sum(IN_SPLITS)

kernel_name = 'hybrid_hgrn2_retnet_dilated_encoder'


def rms_norm(x, g):
    xf = x.astype(F32)
    y = xf * lax.rsqrt(jnp.mean(xf * xf, axis=-1, keepdims=True) + EPS)
    return (y * g.astype(F32)).astype(x.dtype)


def head_rms(t, gain):
    return t * lax.rsqrt(jnp.mean(t * t, axis=-1, keepdims=True) + EPS) * gain.astype(F32)


def to_heads(t, n_heads):
    B, S, W = t.shape
    return t.astype(F32).reshape(B, S, n_heads, W // n_heads).transpose(0, 2, 1, 3)


def rope(t):
    S, d = t.shape[1], t.shape[-1]
    half = d // 2
    inv = ROPE_BASE ** (-jnp.arange(half, dtype=F32) / half)
    ang = jnp.arange(S, dtype=F32)[:, None] * inv[None, :]
    cos = jnp.cos(ang)[None, :, None, :]
    sin = jnp.sin(ang)[None, :, None, :]
    t1, t2 = t[..., :half], t[..., half:]
    return jnp.concatenate([t1 * cos - t2 * sin, t1 * sin + t2 * cos], axis=-1)


def hgrn2_chunk_scan(q, k, v, log_f):
    B, H, S, DK = q.shape
    DV = v.shape[-1]
    C = HG_CHUNK
    nc = S // C

    def chunks(t):
        return t.reshape(B, H, nc, C, t.shape[-1]).transpose(2, 0, 1, 3, 4)

    mask = jnp.tril(jnp.ones((C, C), dtype=bool))[:, :, None]

    def step(state, inp):
        qc, kc, vc, lc = inp
        b = jnp.cumsum(lc, axis=2)
        diff = b[:, :, :, None, :] - b[:, :, None, :, :]
        decay = jnp.exp(jnp.where(mask, diff, -jnp.inf))
        attn = jnp.einsum('bhik,bhjk,bhijk->bhij', qc, kc, decay)
        out = (jnp.einsum('bhij,bhjv->bhiv', attn, vc)
               + jnp.einsum('bhik,bhkv->bhiv', qc * jnp.exp(b), state))
        b_last = b[:, :, -1:, :]
        state = (jnp.exp(b_last)[:, :, 0, :, None] * state
                 + jnp.einsum('bhjk,bhjv->bhkv', kc * jnp.exp(b_last - b), vc))
        return state, out

    state0 = jnp.zeros((B, H, DK, DV), F32)
    _, out = lax.scan(step, state0, (chunks(q), chunks(k), chunks(v), chunks(log_f)))
    return out.transpose(1, 2, 0, 3, 4).reshape(B, H, S, DV)


def hgrn2_mixer(q, i_in, z_fwd, z_bwd, gate, lb_fwd, lb_bwd, norm_g):
    B, S, _ = q.shape
    qh = to_heads(q, HG_HEADS)
    vh = to_heads(i_in, HG_HEADS)

    def one_direction(z, lb, reverse):
        lbh = lb.astype(F32).reshape(HG_HEADS, 1, HG_DK)
        f = lbh + (1.0 - lbh) * jax.nn.sigmoid(to_heads(z, HG_HEADS))
        k = 1.0 - f
        lf = jnp.log(f)
        if reverse:
            fl = lambda t: jnp.flip(t, axis=2)
            return fl(hgrn2_chunk_scan(fl(qh), fl(k), fl(vh), fl(lf)))
        return hgrn2_chunk_scan(qh, k, vh, lf)

    o = one_direction(z_fwd, lb_fwd, False) + one_direction(z_bwd, lb_bwd, True)
    o = o.transpose(0, 2, 1, 3)
    o = o * lax.rsqrt(jnp.mean(o * o, axis=-1, keepdims=True) + EPS)
    o = o.reshape(B, S, HG_WIDTH) * norm_g.astype(F32)
    return o * jax.nn.silu(gate.astype(F32))


def retention_chunk(q, k, v, log_gamma):
    B, H, S, DK = q.shape
    DV = v.shape[-1]
    C = RET_CHUNK
    nc = S // C
    qc = q.reshape(B, H, nc, C, DK)
    kc = k.reshape(B, H, nc, C, DK)
    vc = v.reshape(B, H, nc, C, DV)
    idx = jnp.arange(C, dtype=F32)
    lg = log_gamma[:, None]
    rel = idx[:, None] - idx[None, :]
    decay = jnp.where(rel >= 0, jnp.exp(lg[:, :, None] * jnp.maximum(rel, 0.0)), 0.0)
    scores = jnp.einsum('bhnid,bhnjd->bhnij', qc, kc) * decay[None, :, None]
    intra = jnp.einsum('bhnij,bhnje->bhnie', scores, vc)
    zeta = jnp.exp(lg * (C - 1 - idx))
    xi = jnp.exp(lg * (idx + 1))
    kv = jnp.einsum('bhnjd,hj,bhnje->nbhde', kc, zeta, vc)
    chunk_decay = jnp.exp(log_gamma * C)[None, :, None, None]

    def step(state, kv_n):
        return chunk_decay * state + kv_n, state

    _, prev = lax.scan(step, jnp.zeros((B, H, DK, DV), F32), kv)
    cross = jnp.einsum('bhnid,hi,nbhde->bhnie', qc, xi, prev)
    return (intra + cross).reshape(B, H, S, DV)


def retention_mixer(q, k, v, gate, norm_g):
    B, S, _ = q.shape
    qh = rope(q.astype(F32).reshape(B, S, RET_HEADS, RET_DK)).transpose(0, 2, 1, 3)
    kh = rope(k.astype(F32).reshape(B, S, RET_HEADS, RET_DK)).transpose(0, 2, 1, 3) * RET_DK ** -0.5
    vh = to_heads(v, RET_HEADS)
    hidx = jnp.arange(RET_HEADS, dtype=F32)
    log_g_fwd = jnp.log1p(-jnp.exp2(-5.0 - hidx))
    log_g_bwd = log_g_fwd[::-1]
    fl = lambda t: jnp.flip(t, axis=2)
    o = retention_chunk(qh, kh, vh, log_g_fwd) + fl(retention_chunk(fl(qh), fl(kh), fl(vh), log_g_bwd))
    o = o.transpose(0, 2, 1, 3)
    mu = jnp.mean(o, axis=-1, keepdims=True)
    var = jnp.mean(jnp.square(o - mu), axis=-1, keepdims=True)
    o = ((o - mu) * lax.rsqrt(var + EPS)).reshape(B, S, RET_WIDTH) * norm_g.astype(F32)
    return o * jax.nn.silu(gate.astype(F32))


def t5_bucket(rel):
    nb = REL_BUCKETS // 2
    max_exact = nb // 2
    sign_off = jnp.where(rel > 0, nb, 0)
    n = jnp.abs(rel)
    nf = jnp.maximum(n, 1).astype(F32)
    large = max_exact + (jnp.log(nf / max_exact) / math.log(REL_MAX_DIST / max_exact)
                         * (nb - max_exact)).astype(jnp.int32)
    large = jnp.minimum(large, nb - 1)
    return sign_off + jnp.where(n < max_exact, n, large)


def dilated_local_attention(q, k, v, bias_table, dil, half):
    B, H, S, D = q.shape
    L = S // dil
    nb = -(-L // half)
    Lp = nb * half

    def to_res(t):
        return t.reshape(B, H, L, dil, D).transpose(0, 1, 3, 2, 4)

    qr, kr, vr = to_res(q), to_res(k), to_res(v)
    qb = jnp.pad(qr, ((0, 0), (0, 0), (0, 0), (0, Lp - L), (0, 0))).reshape(B, H, dil, nb, half, D)

    def band(t):
        tp = jnp.pad(t, ((0, 0), (0, 0), (0, 0), (half, Lp - L + half), (0, 0)))
        tp = tp.reshape(B, H, dil, nb + 2, half, D)
        return jnp.concatenate([tp[:, :, :, :-2], tp[:, :, :, 1:-1], tp[:, :, :, 2:]], axis=4)

    kb, vb = band(kr), band(vr)
    ii = jnp.arange(half)[:, None]
    jj = jnp.arange(3 * half)[None, :]
    rel = jj - half - ii
    bias = bias_table.astype(F32)[t5_bucket(rel * dil)].transpose(2, 0, 1)
    key_idx = jnp.arange(nb)[:, None, None] * half + jj[None] - half
    valid = (jnp.abs(rel) <= half)[None] & (key_idx >= 0) & (key_idx < L)
    s = jnp.einsum('bhrnqd,bhrnkd->bhrnqk', qb, kb) + bias[None, :, None, None]
    s = jnp.where(valid, s, -jnp.inf)
    m = jnp.max(s, axis=-1, keepdims=True)
    p = jnp.exp(s - m)
    den = jnp.sum(p, axis=-1)
    o = jnp.einsum('bhrnqk,bhrnkd->bhrnqd', p, vb) / den[..., None]
    lse = m[..., 0] + jnp.log(den)
    o = o.reshape(B, H, dil, Lp, D)[:, :, :, :L].transpose(0, 1, 3, 2, 4).reshape(B, H, S, D)
    lse = lse.reshape(B, H, dil, Lp)[..., :L].transpose(0, 1, 3, 2).reshape(B, H, S)
    return o, lse


def dilated_mixer(parts, rel_bias, q_gain, k_gain):
    B, S, _ = parts[0].shape
    outs, lses = [], []
    for g, (window, dil) in enumerate(DIL_GROUPS):
        half = window // (2 * dil)
        q = head_rms(parts[3 * g].astype(F32).reshape(B, S, DIL_SLOTS, DIL_HD), q_gain) * DIL_HD ** -0.5
        k = head_rms(parts[3 * g + 1].astype(F32).reshape(B, S, DIL_SLOTS, DIL_HD), k_gain)
        v = parts[3 * g + 2].astype(F32).reshape(B, S, DIL_SLOTS, DIL_HD)
        tbl = rel_bias[:, g * DIL_SLOTS:(g + 1) * DIL_SLOTS]
        o, lse = dilated_local_attention(q.transpose(0, 2, 1, 3), k.transpose(0, 2, 1, 3),
                                         v.transpose(0, 2, 1, 3), tbl, dil, half)
        outs.append(o)
        lses.append(lse)
    w = jax.nn.softmax(jnp.stack(lses, axis=0), axis=0)
    o = jnp.sum(w[..., None] * jnp.stack(outs, axis=0), axis=0)
    return o.transpose(0, 2, 1, 3).reshape(B, S, DIL_WIDTH)


def _fwd_setup_inputs(seed: int = 0) -> dict:
    key = jax.random.key(seed)
    ks = jax.random.split(key, 14)

    def nrm(k, shape, scale):
        return jax.random.normal(k, shape, F32) * scale

    return {
        'x': nrm(ks[0], (BATCH, SEQ, D_MODEL), 1.0),
        'w_in': nrm(ks[1], (DEPTH, D_MODEL, IN_WIDTH), D_MODEL ** -0.5),
        'w_out': nrm(ks[2], (DEPTH, MIX_WIDTH, D_MODEL), MIX_WIDTH ** -0.5),
        'w_up': nrm(ks[3], (DEPTH, D_MODEL, D_FF), D_MODEL ** -0.5),
        'w_down': nrm(ks[4], (DEPTH, D_FF, D_MODEL), D_FF ** -0.5),
        'norm_mix': 1.0 + nrm(ks[5], (DEPTH, D_MODEL), 0.02),
        'norm_mlp': 1.0 + nrm(ks[6], (DEPTH, D_MODEL), 0.02),
        'hg_lb_fwd': nrm(ks[7], (DEPTH, HG_HEADS * HG_DK), 0.5),
        'hg_lb_bwd': nrm(ks[8], (DEPTH, HG_HEADS * HG_DK), 0.5),
        'hg_norm': 1.0 + nrm(ks[9], (DEPTH, HG_WIDTH), 0.02),
        'ret_norm': 1.0 + nrm(ks[10], (DEPTH, RET_WIDTH), 0.02),
        'q_norm': 1.0 + nrm(ks[11], (DEPTH, DIL_HD), 0.02),
        'k_norm': 1.0 + nrm(ks[12], (DEPTH, DIL_HD), 0.02),
        'rel_bias': nrm(ks[13], (REL_BUCKETS, DIL_HEADS), 0.1),
    }


def _fwd_reference(x, w_in, w_out, w_up, w_down, norm_mix, norm_mlp, hg_lb_fwd, hg_lb_bwd,
              hg_norm, ret_norm, q_norm, k_norm, rel_bias):
    lb_fwd_all = jnp.cumsum(jax.nn.softmax(hg_lb_fwd.astype(F32), axis=0), axis=0)
    lb_bwd_all = jnp.cumsum(jax.nn.softmax(hg_lb_bwd.astype(F32), axis=0), axis=0)
    offsets = np.cumsum(IN_SPLITS)[:-1].tolist()
    for l in range(DEPTH):
        h = rms_norm(x, norm_mix[l]) @ w_in[l]
        p = jnp.split(h, offsets, axis=-1)
        y_a = hgrn2_mixer(p[0], p[1], p[2], p[3], p[4],
                          lb_fwd_all[l] - lb_fwd_all[0], lb_bwd_all[l] - lb_bwd_all[0], hg_norm[l])
        y_b = retention_mixer(p[5], p[6], p[7], p[8], ret_norm[l])
        y_c = dilated_mixer(p[9:], rel_bias, q_norm[l], k_norm[l])
        y = jnp.concatenate([y_a, y_b, y_c], axis=-1).astype(x.dtype)
        x = x + y @ w_out[l]
        hm = rms_norm(x, norm_mlp[l])
        x = x + jnp.square(jax.nn.relu(hm @ w_up[l])) @ w_down[l]
    return x


import jax as _jax
import jax.numpy as _jnp

TWIN_FORMAT = 'train_step'
FWD_PARAMS = ['x', 'w_in', 'w_out', 'w_up', 'w_down', 'norm_mix', 'norm_mlp', 'hg_lb_fwd', 'hg_lb_bwd', 'hg_norm', 'ret_norm', 'q_norm', 'k_norm', 'rel_bias']
TWIN_WEIGHTS = ['w_in', 'w_out', 'w_up', 'w_down', 'norm_mix', 'norm_mlp', 'hg_lb_fwd', 'hg_lb_bwd', 'hg_norm', 'ret_norm', 'q_norm', 'k_norm', 'rel_bias']
TWIN_DIFF_INPUT = 'x'
TWIN_INPUTS = ['x', 'w_in', 'w_out', 'w_up', 'w_down', 'norm_mix', 'norm_mlp', 'hg_lb_fwd', 'hg_lb_bwd', 'hg_norm', 'ret_norm', 'q_norm', 'k_norm', 'rel_bias', 'loss_target', 'm_w_in', 'm_w_out', 'm_w_up', 'm_w_down', 'm_norm_mix', 'm_norm_mlp', 'm_hg_lb_fwd', 'm_hg_lb_bwd', 'm_hg_norm', 'm_ret_norm', 'm_q_norm', 'm_k_norm', 'm_rel_bias', 'v_w_in', 'v_w_out', 'v_w_up', 'v_w_down', 'v_norm_mix', 'v_norm_mlp', 'v_hg_lb_fwd', 'v_hg_lb_bwd', 'v_hg_norm', 'v_ret_norm', 'v_q_norm', 'v_k_norm', 'v_rel_bias']
TWIN_OUTPUTS = ['loss', 'grad_x', 'grad_w_in', 'grad_w_out', 'grad_w_up', 'grad_w_down', 'grad_norm_mix', 'grad_norm_mlp', 'grad_hg_lb_fwd', 'grad_hg_lb_bwd', 'grad_hg_norm', 'grad_ret_norm', 'grad_q_norm', 'grad_k_norm', 'grad_rel_bias', 'delta_w_in', 'delta_w_out', 'delta_w_up', 'delta_w_down', 'delta_norm_mix', 'delta_norm_mlp', 'delta_hg_lb_fwd', 'delta_hg_lb_bwd', 'delta_hg_norm', 'delta_ret_norm', 'delta_q_norm', 'delta_k_norm', 'delta_rel_bias', 'new_m_w_in', 'new_m_w_out', 'new_m_w_up', 'new_m_w_down', 'new_m_norm_mix', 'new_m_norm_mlp', 'new_m_hg_lb_fwd', 'new_m_hg_lb_bwd', 'new_m_hg_norm', 'new_m_ret_norm', 'new_m_q_norm', 'new_m_k_norm', 'new_m_rel_bias', 'new_v_w_in', 'new_v_w_out', 'new_v_w_up', 'new_v_w_down', 'new_v_norm_mix', 'new_v_norm_mlp', 'new_v_hg_lb_fwd', 'new_v_hg_lb_bwd', 'new_v_hg_norm', 'new_v_ret_norm', 'new_v_q_norm', 'new_v_k_norm', 'new_v_rel_bias']
TWIN_LEAF_KINDS = {'loss': 'loss', 'grad_x': 'grad_x', 'grad_w_in': 'grad_w', 'grad_w_out': 'grad_w', 'grad_w_up': 'grad_w', 'grad_w_down': 'grad_w', 'grad_norm_mix': 'grad_w', 'grad_norm_mlp': 'grad_w', 'grad_hg_lb_fwd': 'grad_w', 'grad_hg_lb_bwd': 'grad_w', 'grad_hg_norm': 'grad_w', 'grad_ret_norm': 'grad_w', 'grad_q_norm': 'grad_w', 'grad_k_norm': 'grad_w', 'grad_rel_bias': 'grad_w', 'delta_w_in': 'delta_w', 'delta_w_out': 'delta_w', 'delta_w_up': 'delta_w', 'delta_w_down': 'delta_w', 'delta_norm_mix': 'delta_w', 'delta_norm_mlp': 'delta_w', 'delta_hg_lb_fwd': 'delta_w', 'delta_hg_lb_bwd': 'delta_w', 'delta_hg_norm': 'delta_w', 'delta_ret_norm': 'delta_w', 'delta_q_norm': 'delta_w', 'delta_k_norm': 'delta_w', 'delta_rel_bias': 'delta_w', 'new_m_w_in': 'new_m', 'new_m_w_out': 'new_m', 'new_m_w_up': 'new_m', 'new_m_w_down': 'new_m', 'new_m_norm_mix': 'new_m', 'new_m_norm_mlp': 'new_m', 'new_m_hg_lb_fwd': 'new_m', 'new_m_hg_lb_bwd': 'new_m', 'new_m_hg_norm': 'new_m', 'new_m_ret_norm': 'new_m', 'new_m_q_norm': 'new_m', 'new_m_k_norm': 'new_m', 'new_m_rel_bias': 'new_m', 'new_v_w_in': 'new_v', 'new_v_w_out': 'new_v', 'new_v_w_up': 'new_v', 'new_v_w_down': 'new_v', 'new_v_norm_mix': 'new_v', 'new_v_norm_mlp': 'new_v', 'new_v_hg_lb_fwd': 'new_v', 'new_v_hg_lb_bwd': 'new_v', 'new_v_hg_norm': 'new_v', 'new_v_ret_norm': 'new_v', 'new_v_q_norm': 'new_v', 'new_v_k_norm': 'new_v', 'new_v_rel_bias': 'new_v'}


def _forward(args):
    return _fwd_reference(*[args[k] for k in FWD_PARAMS])


def _output_shape():
    def fwd():
        inp = _fwd_setup_inputs(0)
        return _fwd_reference(*[inp[k] for k in FWD_PARAMS])
    out = _jax.eval_shape(fwd)
    return out.shape, out.dtype

N_MICROBATCH = 1
ADAM_LR = 0.001
ADAM_B1 = 0.9
ADAM_B2 = 0.999
ADAM_EPS = 1e-08
ADAM_WD = 0.01
ADAM_STEP = 10
PER_EXAMPLE_BATCH_AXIS = {'x': 0, 'loss_target': 0}
SHARED_INPUTS = []
_WEIGHT_DTYPES = {'w_in': _jnp.float32, 'w_out': _jnp.float32, 'w_up': _jnp.float32, 'w_down': _jnp.float32, 'norm_mix': _jnp.float32, 'norm_mlp': _jnp.float32, 'hg_lb_fwd': _jnp.float32, 'hg_lb_bwd': _jnp.float32, 'hg_norm': _jnp.float32, 'ret_norm': _jnp.float32, 'q_norm': _jnp.float32, 'k_norm': _jnp.float32, 'rel_bias': _jnp.float32}
MOMENT_SCALE = {'w_in': 2.397995e+00, 'w_out': 3.634945e+00, 'w_up': 6.550500e+00, 'w_down': 2.492665e+01, 'norm_mix': 1.136602e+01, 'norm_mlp': 9.744431e+01, 'hg_lb_fwd': 1.650532e-01, 'hg_lb_bwd': 1.512707e-01, 'hg_norm': 1.499497e+01, 'ret_norm': 1.479510e+01, 'q_norm': 1.268491e+00, 'k_norm': 1.258792e+00, 'rel_bias': 3.212174e+00}


def _to_microbatches(a, axis):
    t = _jnp.moveaxis(a, axis, 0)
    t = t.reshape((N_MICROBATCH, t.shape[0] // N_MICROBATCH) + t.shape[1:])
    return _jnp.moveaxis(t, 1, axis + 1)


def setup_inputs(seed: int = 0) -> dict:
    inp = _fwd_setup_inputs(seed)
    key = _jax.random.fold_in(_jax.random.key(seed), 7919)
    shape, _ = _output_shape()
    out = dict(inp)
    out["loss_target"] = _jax.random.normal(_jax.random.fold_in(key, 0), shape, _jnp.float32)
    for i, name in enumerate(TWIN_WEIGHTS):
        w = inp[name].astype(_jnp.float32)
        if MOMENT_SCALE is None:
            s = _jnp.sqrt(_jnp.mean(_jnp.square(w)) + 1e-30)
        else:
            s = MOMENT_SCALE[name]
        km, kv = _jax.random.split(_jax.random.fold_in(key, i + 1))
        out[name] = w
        out["m_" + name] = s * _jax.random.normal(km, w.shape, _jnp.float32)
        out["v_" + name] = (s * s) * _jax.random.uniform(kv, w.shape, _jnp.float32, 0.5, 1.5)
    if N_MICROBATCH > 1:
        for name, axis in PER_EXAMPLE_BATCH_AXIS.items():
            out[name] = _to_microbatches(out[name], axis)
    return {'x': out['x'], 'w_in': out['w_in'], 'w_out': out['w_out'], 'w_up': out['w_up'], 'w_down': out['w_down'], 'norm_mix': out['norm_mix'], 'norm_mlp': out['norm_mlp'], 'hg_lb_fwd': out['hg_lb_fwd'], 'hg_lb_bwd': out['hg_lb_bwd'], 'hg_norm': out['hg_norm'], 'ret_norm': out['ret_norm'], 'q_norm': out['q_norm'], 'k_norm': out['k_norm'], 'rel_bias': out['rel_bias'], 'loss_target': out['loss_target'], 'm_w_in': out['m_w_in'], 'm_w_out': out['m_w_out'], 'm_w_up': out['m_w_up'], 'm_w_down': out['m_w_down'], 'm_norm_mix': out['m_norm_mix'], 'm_norm_mlp': out['m_norm_mlp'], 'm_hg_lb_fwd': out['m_hg_lb_fwd'], 'm_hg_lb_bwd': out['m_hg_lb_bwd'], 'm_hg_norm': out['m_hg_norm'], 'm_ret_norm': out['m_ret_norm'], 'm_q_norm': out['m_q_norm'], 'm_k_norm': out['m_k_norm'], 'm_rel_bias': out['m_rel_bias'], 'v_w_in': out['v_w_in'], 'v_w_out': out['v_w_out'], 'v_w_up': out['v_w_up'], 'v_w_down': out['v_w_down'], 'v_norm_mix': out['v_norm_mix'], 'v_norm_mlp': out['v_norm_mlp'], 'v_hg_lb_fwd': out['v_hg_lb_fwd'], 'v_hg_lb_bwd': out['v_hg_lb_bwd'], 'v_hg_norm': out['v_hg_norm'], 'v_ret_norm': out['v_ret_norm'], 'v_q_norm': out['v_q_norm'], 'v_k_norm': out['v_k_norm'], 'v_rel_bias': out['v_rel_bias']}


def _loss(weights, diff, rest, loss_target):
    with _jax.named_scope("forward"):
        args = {**rest, TWIN_DIFF_INPUT: diff, **{k: w.astype(_WEIGHT_DTYPES[k]) for k, w in weights.items()}}
        y = _forward(args)
    with _jax.named_scope("loss_head"):
        err = _jnp.square(y.astype(_jnp.float32) - loss_target)
        return 0.5 * _jnp.sum(_jnp.mean(err, axis=-1)) if err.ndim else 0.5 * err


def _adamw(w, g, m, v):
    m = ADAM_B1 * m + (1.0 - ADAM_B1) * g
    v = ADAM_B2 * v + (1.0 - ADAM_B2) * _jnp.square(g)
    m_hat = m / (1.0 - ADAM_B1 ** ADAM_STEP)
    v_hat = v / (1.0 - ADAM_B2 ** ADAM_STEP)
    delta = -ADAM_LR * (m_hat / (_jnp.sqrt(v_hat) + ADAM_EPS) + ADAM_WD * w)
    return delta, m, v


def reference(x, w_in, w_out, w_up, w_down, norm_mix, norm_mlp, hg_lb_fwd, hg_lb_bwd, hg_norm, ret_norm, q_norm, k_norm, rel_bias, loss_target, m_w_in, m_w_out, m_w_up, m_w_down, m_norm_mix, m_norm_mlp, m_hg_lb_fwd, m_hg_lb_bwd, m_hg_norm, m_ret_norm, m_q_norm, m_k_norm, m_rel_bias, v_w_in, v_w_out, v_w_up, v_w_down, v_norm_mix, v_norm_mlp, v_hg_lb_fwd, v_hg_lb_bwd, v_hg_norm, v_ret_norm, v_q_norm, v_k_norm, v_rel_bias):
    given = dict(x=x, w_in=w_in, w_out=w_out, w_up=w_up, w_down=w_down, norm_mix=norm_mix, norm_mlp=norm_mlp, hg_lb_fwd=hg_lb_fwd, hg_lb_bwd=hg_lb_bwd, hg_norm=hg_norm, ret_norm=ret_norm, q_norm=q_norm, k_norm=k_norm, rel_bias=rel_bias, loss_target=loss_target, m_w_in=m_w_in, m_w_out=m_w_out, m_w_up=m_w_up, m_w_down=m_w_down, m_norm_mix=m_norm_mix, m_norm_mlp=m_norm_mlp, m_hg_lb_fwd=m_hg_lb_fwd, m_hg_lb_bwd=m_hg_lb_bwd, m_hg_norm=m_hg_norm, m_ret_norm=m_ret_norm, m_q_norm=m_q_norm, m_k_norm=m_k_norm, m_rel_bias=m_rel_bias, v_w_in=v_w_in, v_w_out=v_w_out, v_w_up=v_w_up, v_w_down=v_w_down, v_norm_mix=v_norm_mix, v_norm_mlp=v_norm_mlp, v_hg_lb_fwd=v_hg_lb_fwd, v_hg_lb_bwd=v_hg_lb_bwd, v_hg_norm=v_hg_norm, v_ret_norm=v_ret_norm, v_q_norm=v_q_norm, v_k_norm=v_k_norm, v_rel_bias=v_rel_bias)
    weights = {n: given[n] for n in TWIN_WEIGHTS}
    shared = {n: given[n] for n in SHARED_INPUTS}
    per_example = {n: given[n] for n in ['x']}
    grad_fn = _jax.value_and_grad(_loss, argnums=(0, 1))

    def one_microbatch(ex, loss_target):
        ex = dict(ex)
        diff = ex.pop(TWIN_DIFF_INPUT)
        return grad_fn(weights, diff, {**shared, **ex}, loss_target)

    if N_MICROBATCH == 1:
        loss, (grad_w, grad_x) = one_microbatch(per_example, given["loss_target"])
    else:
        def body(carry, xs):
            loss_sum, grad_sum = carry
            l_k, (gw_k, gx_k) = one_microbatch(xs[0], xs[1])
            with _jax.named_scope("update"):
                return (loss_sum + l_k, _jax.tree.map(_jnp.add, grad_sum, gw_k)), gx_k

        init = (_jnp.zeros((), _jnp.float32), _jax.tree.map(_jnp.zeros_like, weights))
        (loss, grad_w), grad_x = _jax.lax.scan(body, init, (per_example, given["loss_target"]))
    with _jax.named_scope("update"):
        delta_w, new_m, new_v = {}, {}, {}
        for n in TWIN_WEIGHTS:
            delta_w[n], new_m[n], new_v[n] = _adamw(weights[n], grad_w[n], given["m_" + n], given["v_" + n])
    return (loss, grad_x, *[grad_w[n] for n in TWIN_WEIGHTS], *[delta_w[n] for n in TWIN_WEIGHTS],
            *[new_m[n] for n in TWIN_WEIGHTS], *[new_v[n] for n in TWIN_WEIGHTS])
```

```python
import functools
import math

import jax
import jax.numpy as jnp
from jax import lax
from jax.experimental import pallas as pl
from jax.experimental.pallas import tpu as pltpu

F32 = jnp.float32
BF16 = jnp.bfloat16
EPS = 1e-6

D_MODEL = 2048
DEPTH = 4
HG_HEADS = 6
HG_W = 768
RET_HEADS = 6
RET_DK = 64
RET_W = 768
RET_QK_W = RET_HEADS * RET_DK
RET_CHUNK = 128
ROPE_BASE = 10000.0
DIL_SLOTS = 4
DIL_HD = 128
DIL_GROUPS = ((128, 1), (512, 4), (2048, 16))
DIL_HALF = 64
DIL_W = 512
D_FF = 4 * D_MODEL
IN_W = 10752
REL_BUCKETS = 32
REL_MAX_DIST = 1024

OFF_HG_Q, OFF_HG_V, OFF_HG_ZF, OFF_HG_ZB, OFF_HG_GATE = 0, 768, 1536, 2304, 3072
OFF_RET_Q, OFF_RET_K, OFF_RET_V, OFF_RET_GATE = 3840, 4224, 4608, 5376
OFF_DIL = 6144

N_CHIPS = 4
N_DEV = 8
IN_SHARD = IN_W // N_CHIPS
FF_SHARD = D_FF // N_CHIPS

ADAM_LR, ADAM_B1, ADAM_B2, ADAM_EPS, ADAM_WD, ADAM_STEP = 0.001, 0.9, 0.999, 1e-08, 0.01, 10

VMEM_LIMIT = 56 * 1024 * 1024
HG_T = 256
HG_C = 64
RET_T = 256
DIL_TQ = 256
NEG = -1e30

NN = (((1,), (0,)), ((), ()))
NT = (((1,), (1,)), ((), ()))
TN = (((0,), (0,)), ((), ()))
MESH = pl.DeviceIdType.MESH
ANY = pl.BlockSpec(memory_space=pl.ANY)


def _cp(*sem):
    return pltpu.CompilerParams(dimension_semantics=sem, vmem_limit_bytes=VMEM_LIMIT)


def _mxu(a, b, dn):
    return lax.dot_general(a.astype(BF16), b.astype(BF16), dn, preferred_element_type=F32)


@jax.custom_vjp
def dot_nn(a, b):
    return _mxu(a, b, NN)


dot_nn.defvjp(lambda a, b: (_mxu(a, b, NN), (a, b)),
              lambda r, g: (_mxu(g, r[1], NT), _mxu(r[0], g, TN)))


@jax.custom_vjp
def dot_nt(a, b):
    return _mxu(a, b, NT)


dot_nt.defvjp(lambda a, b: (_mxu(a, b, NT), (a, b)),
              lambda r, g: (_mxu(g, r[1], NN), _mxu(g, r[0], TN)))


@jax.custom_vjp
def dot_tn(a, b):
    return _mxu(a, b, TN)


dot_tn.defvjp(lambda a, b: (_mxu(a, b, TN), (a, b)),
              lambda r, g: (_mxu(r[1], g, NT), _mxu(r[0], g, NN)))


def _split3(v):
    hi = v.astype(BF16)
    r1 = v - hi.astype(F32)
    mid = r1.astype(BF16)
    lo = (r1 - mid.astype(F32)).astype(BF16)
    return hi, mid, lo


def _exact_mask_dot(m, v, dn):
    mb = m.astype(BF16)
    hi, mid, lo = _split3(v)
    f = lambda p: lax.dot_general(mb, p, dn, preferred_element_type=F32)
    return (f(lo) + f(mid)) + f(hi)


@jax.custom_vjp
def cumdot(m, v):
    return _exact_mask_dot(m, v, NN)


cumdot.defvjp(lambda m, v: (_exact_mask_dot(m, v, NN), m),
              lambda m, g: (jnp.zeros_like(m), _exact_mask_dot(m, g, TN)))


def _sigmoid(z):
    return 1.0 / (1.0 + jnp.exp(-z))


def _head_rms(t, g):
    return t * lax.rsqrt(jnp.mean(t * t, axis=-1, keepdims=True) + EPS) * g


def _mm(name, a, b, *, mode, grid, a_spec, b_spec, tm, tn, extras=(), extra_specs=(), epi, out_shape, out_specs):
    nk = grid[2]
    n_ex = len(extras)

    def body(*refs):
        a_ref, b_ref = refs[0], refs[1]
        ex = refs[2:2 + n_ex]
        outs = refs[2 + n_ex:-1]
        acc = refs[-1]
        k = pl.program_id(2)

        @pl.when(k == 0)
        def _():
            acc[...] = jnp.zeros_like(acc)

        acc[...] += _mxu(a_ref[...], b_ref[...], {"nn": NN, "nt": NT, "tn": TN}[mode])

        @pl.when(k == nk - 1)
        def _():
            epi(acc[...], ex, outs)

    return pl.pallas_call(
        body, name=name, grid=grid,
        in_specs=[a_spec, b_spec, *extra_specs],
        out_specs=out_specs, out_shape=out_shape,
        scratch_shapes=[pltpu.VMEM((tm, tn), F32)],
        compiler_params=_cp("parallel", "parallel", "arbitrary"),
    )(a, b, *extras)


def _epi_store(acc, ex, outs):
    outs[0][...] = acc.astype(outs[0].dtype)


def _epi_residual(acc, ex, outs):
    outs[0][...] = ex[0][...] + acc


def _epi_up(acc, ex, outs):
    outs[0][...] = acc
    outs[1][...] = jnp.square(jnp.maximum(acc, 0.0)).astype(BF16)


def _epi_dact(acc, ex, outs):
    outs[0][...] = (acc * (2.0 * jnp.maximum(ex[0][...], 0.0))).astype(BF16)


def _ij(i, j, k):
    return (i, j)


def proj_in(xn, win_g, l):
    S = xn.shape[0]
    tm, tn = 1024, 896
    return _mm("proj_in", xn, win_g, mode="nn", grid=(S // tm, IN_W // tn, 1), tm=tm, tn=tn,
               a_spec=pl.BlockSpec((tm, D_MODEL), lambda i, j, k: (i, 0)),
               b_spec=pl.BlockSpec((None, None, D_MODEL, tn), lambda i, j, k: (l, j // 3, 0, j % 3)),
               epi=_epi_store, out_shape=jax.ShapeDtypeStruct((S, IN_W), F32),
               out_specs=pl.BlockSpec((tm, tn), _ij))


def proj_out(y, wout_g, l, x):
    S = y.shape[0]
    tm, tn = 1024, 1024
    return _mm("proj_out", y, wout_g, mode="nn", grid=(S // tm, D_MODEL // tn, 1), tm=tm, tn=tn,
               a_spec=pl.BlockSpec((tm, D_MODEL), lambda i, j, k: (i, 0)),
               b_spec=pl.BlockSpec((None, D_MODEL, tn), lambda i, j, k: (l, 0, j)),
               extras=(x,), extra_specs=(pl.BlockSpec((tm, tn), _ij),),
               epi=_epi_residual, out_shape=jax.ShapeDtypeStruct((S, D_MODEL), F32),
               out_specs=pl.BlockSpec((tm, tn), _ij))


def proj_up(hm, wup_g, l):
    S = hm.shape[0]
    tm, tn = 1024, 1024
    return _mm("proj_up", hm, wup_g, mode="nn", grid=(S // tm, D_FF // tn, 1), tm=tm, tn=tn,
               a_spec=pl.BlockSpec((tm, D_MODEL), lambda i, j, k: (i, 0)),
               b_spec=pl.BlockSpec((None, None, D_MODEL, tn), lambda i, j, k: (l, j // 2, 0, j % 2)),
               epi=_epi_up,
               out_shape=(jax.ShapeDtypeStruct((S, D_FF), F32), jax.ShapeDtypeStruct((S, D_FF), BF16)),
               out_specs=(pl.BlockSpec((tm, tn), _ij), pl.BlockSpec((tm, tn), _ij)))


def proj_down(a, wdown_g, l, x):
    S = a.shape[0]
    tm, tn, tk = 1024, 1024, 2048
    return _mm("proj_down", a, wdown_g, mode="nn", grid=(S // tm, D_MODEL // tn, D_FF // tk), tm=tm, tn=tn,
               a_spec=pl.BlockSpec((tm, tk), lambda i, j, k: (i, k)),
               b_spec=pl.BlockSpec((None, tk, tn), lambda i, j, k: (l, k, j)),
               extras=(x,), extra_specs=(pl.BlockSpec((tm, tn), _ij),),
               epi=_epi_residual, out_shape=jax.ShapeDtypeStruct((S, D_MODEL), F32),
               out_specs=pl.BlockSpec((tm, tn), _ij))


def bwd_down_act(dx, wdown_g, l, u):
    S = dx.shape[0]
    tm, tn = 1024, 1024
    return _mm("bwd_down_act", dx, wdown_g, mode="nt", grid=(S // tm, D_FF // tn, 1), tm=tm, tn=tn,
               a_spec=pl.BlockSpec((tm, D_MODEL), lambda i, j, k: (i, 0)),
               b_spec=pl.BlockSpec((None, tn, D_MODEL), lambda i, j, k: (l, j, 0)),
               extras=(u,), extra_specs=(pl.BlockSpec((tm, tn), _ij),),
               epi=_epi_dact, out_shape=jax.ShapeDtypeStruct((S, D_FF), BF16),
               out_specs=pl.BlockSpec((tm, tn), _ij))


def bwd_up(du, wup_g, l):
    S = du.shape[0]
    tm, tn, tk = 1024, 1024, FF_SHARD
    return _mm("bwd_up", du, wup_g, mode="nt", grid=(S // tm, D_MODEL // tn, D_FF // tk), tm=tm, tn=tn,
               a_spec=pl.BlockSpec((tm, tk), lambda i, j, k: (i, k)),
               b_spec=pl.BlockSpec((None, None, tn, tk), lambda i, j, k: (l, k, j, 0)),
               epi=_epi_store, out_shape=jax.ShapeDtypeStruct((S, D_MODEL), F32),
               out_specs=pl.BlockSpec((tm, tn), _ij))


def bwd_out(dx, wout_g, l):
    S = dx.shape[0]
    tm, tn = 1024, 1024
    return _mm("bwd_out", dx, wout_g, mode="nt", grid=(S // tm, D_MODEL // tn, 1), tm=tm, tn=tn,
               a_spec=pl.BlockSpec((tm, D_MODEL), lambda i, j, k: (i, 0)),
               b_spec=pl.BlockSpec((None, tn, D_MODEL), lambda i, j, k: (l, j, 0)),
               epi=_epi_store, out_shape=jax.ShapeDtypeStruct((S, D_MODEL), F32),
               out_specs=pl.BlockSpec((tm, tn), _ij))


def bwd_in(dh, win_g, l):
    S = dh.shape[0]
    tm, tn, tk = 1024, 1024, 896
    return _mm("bwd_in", dh, win_g, mode="nt", grid=(S // tm, D_MODEL // tn, IN_W // tk), tm=tm, tn=tn,
               a_spec=pl.BlockSpec((tm, tk), lambda i, j, k: (i, k)),
               b_spec=pl.BlockSpec((None, None, tn, tk), lambda i, j, k: (l, k // 3, j, k % 3)),
               epi=_epi_store, out_shape=jax.ShapeDtypeStruct((S, D_MODEL), F32),
               out_specs=pl.BlockSpec((tm, tn), _ij))


def wgrad(name, a, g, *, m, n, n_shard):
    S = a.shape[0]
    tm, tk = 1024, 1024
    tn = 896 if n_shard % 896 == 0 else 1024
    per = n_shard // tn
    if n_shard == n:
        out_shape = jax.ShapeDtypeStruct((m, n), F32)
        out_spec = pl.BlockSpec((tm, tn), _ij)
    else:
        out_shape = jax.ShapeDtypeStruct((N_CHIPS, m, n_shard), F32)
        out_spec = pl.BlockSpec((None, tm, tn), lambda i, j, k: (j // per, i, j % per))
    return _mm(name, a, g, mode="tn", grid=(m // tm, n // tn, S // tk), tm=tm, tn=tn,
               a_spec=pl.BlockSpec((tk, tm), lambda i, j, k: (k, i)),
               b_spec=pl.BlockSpec((tk, tn), lambda i, j, k: (k, j)),
               epi=_epi_store, out_shape=out_shape, out_specs=out_spec)


NORM_T = 256


def rmsnorm_fwd(x, g):
    S = x.shape[0]

    def body(x_ref, g_ref, o_ref):
        xv = x_ref[...]
        r = lax.rsqrt(jnp.mean(xv * xv, axis=-1, keepdims=True) + EPS)
        o_ref[...] = ((xv * r) * g_ref[...]).astype(BF16)

    return pl.pallas_call(
        body, name="rmsnorm_fwd", grid=(S // NORM_T,),
        in_specs=[pl.BlockSpec((NORM_T, D_MODEL), lambda i: (i, 0)), pl.BlockSpec((1, D_MODEL), lambda i: (0, 0))],
        out_specs=pl.BlockSpec((NORM_T, D_MODEL), lambda i: (i, 0)),
        out_shape=jax.ShapeDtypeStruct((S, D_MODEL), BF16), compiler_params=_cp("parallel"),
    )(x, g)


def rmsnorm_bwd(x, g, dxn, dres):
    S = x.shape[0]

    def body(x_ref, g_ref, dxn_ref, dres_ref, dx_ref, dg_ref):
        @pl.when(pl.program_id(0) == 0)
        def _():
            dg_ref[...] = jnp.zeros_like(dg_ref)

        xv, gv, d = x_ref[...], g_ref[...], dxn_ref[...]
        r = lax.rsqrt(jnp.mean(xv * xv, axis=-1, keepdims=True) + EPS)
        gd = gv * d
        dx_ref[...] = dres_ref[...] + r * gd - xv * ((r * r * r) * jnp.mean(xv * gd, axis=-1, keepdims=True))
        dg_ref[...] += jnp.sum(d * (xv * r), axis=0, keepdims=True)

    row = pl.BlockSpec((NORM_T, D_MODEL), lambda i: (i, 0))
    vec = pl.BlockSpec((1, D_MODEL), lambda i: (0, 0))
    return pl.pallas_call(
        body, name="rmsnorm_bwd", grid=(S // NORM_T,),
        in_specs=[row, vec, row, row], out_specs=(row, vec),
        out_shape=(jax.ShapeDtypeStruct((S, D_MODEL), F32), jax.ShapeDtypeStruct((1, D_MODEL), F32)),
        compiler_params=_cp("arbitrary"),
    )(x, g, dxn, dres)


def loss_head(y, tgt):
    S = y.shape[0]

    def body(y_ref, t_ref, dy_ref, l_ref):
        @pl.when(pl.program_id(0) == 0)
        def _():
            l_ref[...] = jnp.zeros_like(l_ref)

        e = y_ref[...] - t_ref[...]
        dy_ref[...] = e * (1.0 / D_MODEL)
        l_ref[...] += jnp.sum(e * e) * (0.5 / D_MODEL)

    row = pl.BlockSpec((NORM_T, D_MODEL), lambda i: (i, 0))
    return pl.pallas_call(
        body, name="loss_head", grid=(S // NORM_T,),
        in_specs=[row, row], out_specs=(row, pl.BlockSpec((1, 128), lambda i: (0, 0))),
        out_shape=(jax.ShapeDtypeStruct((S, D_MODEL), F32), jax.ShapeDtypeStruct((1, 128), F32)),
        compiler_params=_cp("arbitrary"),
    )(y, tgt)


def _hg_block(qs, vs, zs, lb, sT, reverse):
    n = len(qs)
    row = lax.broadcasted_iota(jnp.int32, (HG_C, HG_C), 0)
    col = lax.broadcasted_iota(jnp.int32, (HG_C, HG_C), 1)
    tri = (row <= col) if reverse else (row >= col)
    m = tri.astype(F32)
    rsel = lax.broadcasted_iota(jnp.int32, (HG_C, 128), 0)
    ref_rows = ((rsel >= HG_C // 2) if reverse else (rsel <= HG_C // 2)).astype(F32)
    outs = [None] * n
    for c in (range(n - 1, -1, -1) if reverse else range(n)):
        f = lb + (1.0 - lb) * _sigmoid(zs[c])
        kc = 1.0 - f
        lc = jnp.log(f)
        b = cumdot(m, lc)
        btot = jnp.sum(lc, axis=0, keepdims=True)
        bref = lax.stop_gradient(jnp.sum(lc * ref_rows, axis=0, keepdims=True))
        qe = qs[c] * jnp.exp(jnp.minimum(b - bref, 80.0))
        ke = kc * jnp.exp(jnp.minimum(bref - b, 80.0))
        att = jnp.where(tri, dot_nt(qe, ke), 0.0)
        outs[c] = dot_nn(att, vs[c]) + dot_nt(qs[c] * jnp.exp(b), sT)
        sT = sT * jnp.exp(btot) + dot_tn(vs[c], kc * jnp.exp(btot - b))
    return outs, sT


def _chunks(ref, c, n):
    return [ref[i * c:(i + 1) * c, :] for i in range(n)]


def hg_scan_fwd(h, lbf, lbb):
    S = h.shape[0]
    nb = S // HG_T
    n = HG_T // HG_C

    def body(qf, vf, zf, qb, vb, zb, lbf_ref, lbb_ref, of_ref, ob_ref, sf_ref, sb_ref, stf, stb):
        @pl.when(pl.program_id(1) == 0)
        def _():
            stf[...] = jnp.zeros_like(stf)
            stb[...] = jnp.zeros_like(stb)

        for (q, v, z, lb_ref, o_ref, s_ref, st, rev) in ((qf, vf, zf, lbf_ref, of_ref, sf_ref, stf, False),
                                                         (qb, vb, zb, lbb_ref, ob_ref, sb_ref, stb, True)):
            s_ref[0, 0] = st[...]
            outs, s_new = _hg_block(_chunks(q, HG_C, n), _chunks(v, HG_C, n), _chunks(z, HG_C, n),
                                    lb_ref[...], st[...], rev)
            for c in range(n):
                o_ref[c * HG_C:(c + 1) * HG_C, :] = outs[c]
            st[...] = s_new

    def col(off, rev):
        return pl.BlockSpec((HG_T, 128), (lambda hh, t: (nb - 1 - t, off // 128 + hh)) if rev
                            else (lambda hh, t: (t, off // 128 + hh)))

    lb_spec = pl.BlockSpec((1, 128), lambda hh, t: (0, hh))
    st_f = pl.BlockSpec((1, 1, 128, 128), lambda hh, t: (hh, t, 0, 0))
    st_b = pl.BlockSpec((1, 1, 128, 128), lambda hh, t: (hh, nb - 1 - t, 0, 0))
    return pl.pallas_call(
        body, name="hg_scan_fwd", grid=(HG_HEADS, nb),
        in_specs=[col(OFF_HG_Q, False), col(OFF_HG_V, False), col(OFF_HG_ZF, False),
                  col(OFF_HG_Q, True), col(OFF_HG_V, True), col(OFF_HG_ZB, True), lb_spec, lb_spec],
        out_specs=(col(0, False), col(0, True), st_f, st_b),
        out_shape=(jax.ShapeDtypeStruct((S, HG_W), F32), jax.ShapeDtypeStruct((S, HG_W), F32),
                   jax.ShapeDtypeStruct((HG_HEADS, nb, 128, 128), F32),
                   jax.ShapeDtypeStruct((HG_HEADS, nb, 128, 128), F32)),
        scratch_shapes=[pltpu.VMEM((128, 128), F32), pltpu.VMEM((128, 128), F32)],
        compiler_params=_cp("arbitrary", "arbitrary"),
    )(h, h, h, h, h, h, lbf, lbb)


def hg_scan_bwd(h, lbf, lbb, do, sf, sb):
    S = h.shape[0]
    nb = S // HG_T
    n = HG_T // HG_C

    def body(qf, vf, zf, dof, sfin, qb, vb, zb, dob, sbin, lbf_ref, lbb_ref,
             dqf, dvf, dzf, dqb, dvb, dzb, dlbf, dlbb, dsf, dsb):
        @pl.when(pl.program_id(1) == 0)
        def _():
            for r in (dsf, dsb, dlbf, dlbb):
                r[...] = jnp.zeros_like(r)

        for (q, v, z, dor, sin, lb_ref, dq, dv, dz, dlb, ds, rev) in (
                (qf, vf, zf, dof, sfin, lbf_ref, dqf, dvf, dzf, dlbf, dsf, False),
                (qb, vb, zb, dob, sbin, lbb_ref, dqb, dvb, dzb, dlbb, dsb, True)):
            fn = functools.partial(_hg_block, reverse=rev)
            _, vjp = jax.vjp(fn, _chunks(q, HG_C, n), _chunks(v, HG_C, n), _chunks(z, HG_C, n), lb_ref[...], sin[0, 0])
            dqs, dvs, dzs, dlb_v, ds_in = vjp((_chunks(dor, HG_C, n), ds[...]))
            for c in range(n):
                sl = slice(c * HG_C, (c + 1) * HG_C)
                dq[sl, :] = dqs[c]
                dv[sl, :] = dvs[c]
                dz[sl, :] = dzs[c]
            dlb[...] += dlb_v
            ds[...] = ds_in

    def col(off, fwd_scan):
        return pl.BlockSpec((HG_T, 128), (lambda hh, t: (nb - 1 - t, off // 128 + hh)) if fwd_scan
                            else (lambda hh, t: (t, off // 128 + hh)))

    lb_spec = pl.BlockSpec((1, 128), lambda hh, t: (0, hh))
    st_f = pl.BlockSpec((1, 1, 128, 128), lambda hh, t: (hh, nb - 1 - t, 0, 0))
    st_b = pl.BlockSpec((1, 1, 128, 128), lambda hh, t: (hh, t, 0, 0))
    full = jax.ShapeDtypeStruct((S, HG_W), F32)
    vec = jax.ShapeDtypeStruct((1, HG_W), F32)
    return pl.pallas_call(
        body, name="hg_scan_bwd", grid=(HG_HEADS, nb),
        in_specs=[col(OFF_HG_Q, True), col(OFF_HG_V, True), col(OFF_HG_ZF, True), col(0, True), st_f,
                  col(OFF_HG_Q, False), col(OFF_HG_V, False), col(OFF_HG_ZB, False), col(0, False), st_b,
                  lb_spec, lb_spec],
        out_specs=(col(0, True), col(0, True), col(0, True), col(0, False), col(0, False), col(0, False),
                   lb_spec, lb_spec),
        out_shape=(full, full, full, full, full, full, vec, vec),
        scratch_shapes=[pltpu.VMEM((128, 128), F32), pltpu.VMEM((128, 128), F32)],
        compiler_params=_cp("arbitrary", "arbitrary"),
    )(h, h, h, do, sf, h, h, h, do, sb, lbf, lbb)


GN_T = 1024


def _gated_norm(o, gate, g, center):
    if center:
        o = o - jnp.mean(o, axis=-1, keepdims=True)
    o = o * lax.rsqrt(jnp.mean(o * o, axis=-1, keepdims=True) + EPS)
    return (o * g) * (gate * _sigmoid(gate))


def gated_norm_fwd(name, of, ob, h, gate_off, g, center):
    S = of.shape[0]

    def body(of_ref, ob_ref, gate_ref, g_ref, y_ref):
        y_ref[...] = _gated_norm(of_ref[...] + ob_ref[...], gate_ref[...], g_ref[...], center)

    blk = pl.BlockSpec((GN_T, 128), lambda hh, i: (i, hh))
    return pl.pallas_call(
        body, name=name, grid=(6, S // GN_T),
        in_specs=[blk, blk, pl.BlockSpec((GN_T, 128), lambda hh, i: (i, gate_off // 128 + hh)),
                  pl.BlockSpec((1, 128), lambda hh, i: (0, hh))],
        out_specs=blk, out_shape=jax.ShapeDtypeStruct((S, 768), F32),
        compiler_params=_cp("parallel", "parallel"),
    )(of, ob, h, g)


def gated_norm_bwd(name, of, ob, h, gate_off, g, dy, dy_off, center):
    S = of.shape[0]

    def body(of_ref, ob_ref, gate_ref, g_ref, dy_ref, do_ref, dgate_ref, dg_ref):
        @pl.when(pl.program_id(1) == 0)
        def _():
            dg_ref[...] = jnp.zeros_like(dg_ref)

        fn = functools.partial(_gated_norm, center=center)
        _, vjp = jax.vjp(fn, of_ref[...] + ob_ref[...], gate_ref[...], g_ref[...])
        do, dgate, dg = vjp(dy_ref[...])
        do_ref[...] = do
        dgate_ref[...] = dgate
        dg_ref[...] += dg

    blk = pl.BlockSpec((GN_T, 128), lambda hh, i: (i, hh))
    vec = pl.BlockSpec((1, 128), lambda hh, i: (0, hh))
    return pl.pallas_call(
        body, name=name, grid=(6, S // GN_T),
        in_specs=[blk, blk, pl.BlockSpec((GN_T, 128), lambda hh, i: (i, gate_off // 128 + hh)), vec,
                  pl.BlockSpec((GN_T, 128), lambda hh, i: (i, dy_off // 128 + hh))],
        out_specs=(blk, blk, vec),
        out_shape=(jax.ShapeDtypeStruct((S, 768), F32), jax.ShapeDtypeStruct((S, 768), F32),
                   jax.ShapeDtypeStruct((1, 768), F32)),
        compiler_params=_cp("arbitrary", "arbitrary"),
    )(of, ob, h, g, dy)


def _ret_consts(S):
    half = RET_DK // 2
    inv = ROPE_BASE ** (-jnp.arange(half, dtype=F32) / half)
    ang = jnp.arange(S, dtype=F32)[:, None] * inv[None, :]
    cos, sin = jnp.cos(ang), jnp.sin(ang)
    cos_t = jnp.tile(jnp.concatenate([cos, cos], axis=1), (1, RET_HEADS))
    sin_t = jnp.tile(jnp.concatenate([-sin, sin], axis=1), (1, RET_HEADS))
    hidx = jnp.arange(RET_HEADS, dtype=F32)
    lg_f = jnp.log1p(-jnp.exp2(-5.0 - hidx))
    C = RET_CHUNK
    idx = jnp.arange(C, dtype=F32)
    rel = idx[:, None] - idx[None, :]

    def one(lg, reverse):
        lgc = lg[:, None]
        decay = jnp.where(rel >= 0, jnp.exp(lgc[:, :, None] * jnp.maximum(rel, 0.0)), 0.0)
        zeta = jnp.exp(lgc * (C - 1 - idx))
        xi = jnp.exp(lgc * (idx + 1))
        if reverse:
            decay = decay[:, ::-1, ::-1]
            zeta, xi = zeta[:, ::-1], xi[:, ::-1]
        wide = lambda t: jnp.repeat(t.T, RET_DK, axis=1)
        gam_w = jnp.broadcast_to(jnp.repeat(jnp.exp(lg * C), 128)[None, :], (8, RET_W))
        return decay, wide(xi), wide(zeta), gam_w

    hm = (jnp.arange(RET_QK_W)[None, :] // RET_DK == jnp.arange(8)[:, None]).astype(F32)
    return (cos_t, sin_t, hm) + one(lg_f, False) + one(lg_f[::-1], True)


def _rope(t, cos, sin_signed):
    lane = lax.broadcasted_iota(jnp.int32, t.shape, 1)
    first = (lane & (RET_DK - 1)) < RET_DK // 2
    partner = jnp.where(first, pltpu.roll(t, RET_QK_W - RET_DK // 2, 1), pltpu.roll(t, RET_DK // 2, 1))
    return t * cos + partner * sin_signed


def _ret_block(qs, ks, vs, st, dec, xi, zeta, gam, hms, reverse):
    n = len(qs)
    outs = [None] * n
    st = list(st)
    for c in (range(n - 1, -1, -1) if reverse else range(n)):
        qx = qs[c] * xi
        kz = ks[c] * zeta
        row = []
        for hh in range(RET_HEADS):
            sc = dot_nt(qs[c] * hms[hh], ks[c]) * dec[hh]
            row.append(dot_nn(sc, vs[c][hh]) + dot_nn(qx, st[hh]))
            st[hh] = st[hh] * gam[hh] + dot_tn(kz * hms[hh], vs[c][hh])
        outs[c] = row
    return outs, st


def _ret_inputs(q_ref, k_ref, v_ref, cos_ref, sin_ref):
    n = RET_T // RET_CHUNK
    qr = _rope(q_ref[...], cos_ref[...], sin_ref[...])
    kr = _rope(k_ref[...], cos_ref[...], sin_ref[...]) * (RET_DK ** -0.5)
    qs = [qr[c * RET_CHUNK:(c + 1) * RET_CHUNK] for c in range(n)]
    ks = [kr[c * RET_CHUNK:(c + 1) * RET_CHUNK] for c in range(n)]
    vs = [[v_ref[c * RET_CHUNK:(c + 1) * RET_CHUNK, hh * 128:(hh + 1) * 128] for hh in range(RET_HEADS)]
          for c in range(n)]
    return qs, ks, vs


def _ret_dir_consts(dec_ref, xi_ref, zeta_ref, gam_ref, hm_ref):
    dec = [dec_ref[hh] for hh in range(RET_HEADS)]
    gam = [gam_ref[0:1, hh * 128:(hh + 1) * 128] for hh in range(RET_HEADS)]
    hms = [hm_ref[hh:hh + 1, :] for hh in range(RET_HEADS)]
    return dec, xi_ref[...], zeta_ref[...], gam, hms


def _ret_rows(nb, rev):
    def rows(width, colblk):
        return pl.BlockSpec((RET_T, width), (lambda t: (nb - 1 - t, colblk)) if rev else (lambda t: (t, colblk)))
    return rows


def _const_spec(shape):
    nd = len(shape)
    return pl.BlockSpec(shape, lambda t: (0,) * nd)


def ret_scan_fwd(h, consts):
    S = h.shape[0]
    nb = S // RET_T
    n = RET_T // RET_CHUNK
    cos_t, sin_t, hm, dec_f, xi_f, zeta_f, gam_f, dec_b, xi_b, zeta_b, gam_b = consts

    def body(qf, kf, vf, cf, sf, qb, kb, vb, cb, sb_, hm_ref, decf, xif, zetaf, gamf, decb, xib, zetab, gamb,
             of_ref, ob_ref, sfo, sbo, stf, stb):
        @pl.when(pl.program_id(0) == 0)
        def _():
            stf[...] = jnp.zeros_like(stf)
            stb[...] = jnp.zeros_like(stb)

        for (q, k, v, cs, sn, dr, xr, zr, gr, o_ref, so, st, rev) in (
                (qf, kf, vf, cf, sf, decf, xif, zetaf, gamf, of_ref, sfo, stf, False),
                (qb, kb, vb, cb, sb_, decb, xib, zetab, gamb, ob_ref, sbo, stb, True)):
            so[0] = st[...]
            qs, ks, vs = _ret_inputs(q, k, v, cs, sn)
            dec, xi, zeta, gam, hms = _ret_dir_consts(dr, xr, zr, gr, hm_ref)
            st_in = [st[:, hh * 128:(hh + 1) * 128] for hh in range(RET_HEADS)]
            outs, st_new = _ret_block(qs, ks, vs, st_in, dec, xi, zeta, gam, hms, rev)
            for c in range(n):
                for hh in range(RET_HEADS):
                    o_ref[c * RET_CHUNK:(c + 1) * RET_CHUNK, hh * 128:(hh + 1) * 128] = outs[c][hh]
            for hh in range(RET_HEADS):
                st[:, hh * 128:(hh + 1) * 128] = st_new[hh]

    rf, rb = _ret_rows(nb, False), _ret_rows(nb, True)
    cspecs = [_const_spec(a.shape) for a in (hm, dec_f, xi_f, zeta_f, gam_f, dec_b, xi_b, zeta_b, gam_b)]
    st_shape = jax.ShapeDtypeStruct((nb, RET_QK_W, RET_W), F32)
    qc, kc, vc = OFF_RET_Q // RET_QK_W, OFF_RET_K // RET_QK_W, OFF_RET_V // RET_W
    return pl.pallas_call(
        body, name="ret_scan_fwd", grid=(nb,),
        in_specs=[rf(RET_QK_W, qc), rf(RET_QK_W, kc), rf(RET_W, vc), rf(RET_QK_W, 0), rf(RET_QK_W, 0),
                  rb(RET_QK_W, qc), rb(RET_QK_W, kc), rb(RET_W, vc), rb(RET_QK_W, 0), rb(RET_QK_W, 0)] + cspecs,
        out_specs=(rf(RET_W, 0), rb(RET_W, 0),
                   pl.BlockSpec((1, RET_QK_W, RET_W), lambda t: (t, 0, 0)),
                   pl.BlockSpec((1, RET_QK_W, RET_W), lambda t: (nb - 1 - t, 0, 0))),
        out_shape=(jax.ShapeDtypeStruct((S, RET_W), F32), jax.ShapeDtypeStruct((S, RET_W), F32), st_shape, st_shape),
        scratch_shapes=[pltpu.VMEM((RET_QK_W, RET_W), F32), pltpu.VMEM((RET_QK_W, RET_W), F32)],
        compiler_params=_cp("arbitrary"),
    )(h, h, h, cos_t, sin_t, h, h, h, cos_t, sin_t, hm, dec_f, xi_f, zeta_f, gam_f, dec_b, xi_b, zeta_b, gam_b)


def ret_scan_bwd(h, consts, do, sf, sb):
    S = h.shape[0]
    nb = S // RET_T
    n = RET_T // RET_CHUNK
    cos_t, sin_t, hm, dec_f, xi_f, zeta_f, gam_f, dec_b, xi_b, zeta_b, gam_b = consts

    def body(qf, kf, vf, cf, sf_, dof, sfin, qb, kb, vb, cb, sb_, dob, sbin,
             hm_ref, decf, xif, zetaf, gamf, decb, xib, zetab, gamb,
             dqf, dkf, dvf, dqb, dkb, dvb, dsf, dsb):
        @pl.when(pl.program_id(0) == 0)
        def _():
            dsf[...] = jnp.zeros_like(dsf)
            dsb[...] = jnp.zeros_like(dsb)

        for (q, k, v, cs, sn, dor, sin, dr, xr, zr, gr, dq, dk, dv, ds, rev) in (
                (qf, kf, vf, cf, sf_, dof, sfin, decf, xif, zetaf, gamf, dqf, dkf, dvf, dsf, False),
                (qb, kb, vb, cb, sb_, dob, sbin, decb, xib, zetab, gamb, dqb, dkb, dvb, dsb, True)):
            qs, ks, vs = _ret_inputs(q, k, v, cs, sn)
            dec, xi, zeta, gam, hms = _ret_dir_consts(dr, xr, zr, gr, hm_ref)
            st_in = [sin[0, :, hh * 128:(hh + 1) * 128] for hh in range(RET_HEADS)]
            fn = lambda a, b_, c_, d_: _ret_block(a, b_, c_, d_, dec, xi, zeta, gam, hms, rev)
            _, vjp = jax.vjp(fn, qs, ks, vs, st_in)
            dos = [[dor[c * RET_CHUNK:(c + 1) * RET_CHUNK, hh * 128:(hh + 1) * 128] for hh in range(RET_HEADS)]
                   for c in range(n)]
            dst = [ds[:, hh * 128:(hh + 1) * 128] for hh in range(RET_HEADS)]
            dqs, dks, dvs, dst_in = vjp((dos, dst))
            cosv, sinv = cs[...], sn[...]
            dq[...] = _rope(jnp.concatenate(dqs, axis=0), cosv, -sinv)
            dk[...] = _rope(jnp.concatenate(dks, axis=0) * (RET_DK ** -0.5), cosv, -sinv)
            for c in range(n):
                for hh in range(RET_HEADS):
                    dv[c * RET_CHUNK:(c + 1) * RET_CHUNK, hh * 128:(hh + 1) * 128] = dvs[c][hh]
            for hh in range(RET_HEADS):
                ds[:, hh * 128:(hh + 1) * 128] = dst_in[hh]

    rf, rb = _ret_rows(nb, True), _ret_rows(nb, False)
    cspecs = [_const_spec(a.shape) for a in (hm, dec_f, xi_f, zeta_f, gam_f, dec_b, xi_b, zeta_b, gam_b)]
    qk = jax.ShapeDtypeStruct((S, RET_QK_W), F32)
    vv = jax.ShapeDtypeStruct((S, RET_W), F32)
    qc, kc, vc = OFF_RET_Q // RET_QK_W, OFF_RET_K // RET_QK_W, OFF_RET_V // RET_W
    return pl.pallas_call(
        body, name="ret_scan_bwd", grid=(nb,),
        in_specs=[rf(RET_QK_W, qc), rf(RET_QK_W, kc), rf(RET_W, vc), rf(RET_QK_W, 0), rf(RET_QK_W, 0), rf(RET_W, 0),
                  pl.BlockSpec((1, RET_QK_W, RET_W), lambda t: (nb - 1 - t, 0, 0)),
                  rb(RET_QK_W, qc), rb(RET_QK_W, kc), rb(RET_W, vc), rb(RET_QK_W, 0), rb(RET_QK_W, 0), rb(RET_W, 0),
                  pl.BlockSpec((1, RET_QK_W, RET_W), lambda t: (t, 0, 0))] + cspecs,
        out_specs=(rf(RET_QK_W, 0), rf(RET_QK_W, 0), rf(RET_W, 0), rb(RET_QK_W, 0), rb(RET_QK_W, 0), rb(RET_W, 0)),
        out_shape=(qk, qk, vv, qk, qk, vv),
        scratch_shapes=[pltpu.VMEM((RET_QK_W, RET_W), F32), pltpu.VMEM((RET_QK_W, RET_W), F32)],
        compiler_params=_cp("arbitrary"),
    )(h, h, h, cos_t, sin_t, do, sf, h, h, h, cos_t, sin_t, do, sb,
      hm, dec_f, xi_f, zeta_f, gam_f, dec_b, xi_b, zeta_b, gam_b)


def _t5_bucket(rel):
    nb = REL_BUCKETS // 2
    max_exact = nb // 2
    sign_off = jnp.where(rel > 0, nb, 0)
    n = jnp.abs(rel)
    nf = jnp.maximum(n, 1).astype(F32)
    large = max_exact + (jnp.log(nf / max_exact) / math.log(REL_MAX_DIST / max_exact)
                         * (nb - max_exact)).astype(jnp.int32)
    large = jnp.minimum(large, nb - 1)
    return sign_off + jnp.where(n < max_exact, n, large)


def _dil_buckets(dil):
    tq, tb = DIL_TQ, DIL_TQ + 2 * DIL_HALF
    rel_q = jnp.arange(tb)[None, :] - DIL_HALF - jnp.arange(tq)[:, None]
    rel_k = jnp.arange(tq)[None, :] + DIL_HALF - jnp.arange(tb)[:, None]
    return _t5_bucket(rel_q * dil), _t5_bucket(rel_k * dil)


def _dil_col(g, j, r, s):
    return (r * IN_W + OFF_DIL + (3 * g + j) * DIL_W) // 128 + s


def _dil_specs(L):
    nq = DIL_TQ // DIL_HALF
    last = L // DIL_HALF - 1

    def cur(colfn):
        return pl.BlockSpec((DIL_TQ, 128), lambda s, r, n: (n, colfn(r, s)))

    def prev(colfn):
        return pl.BlockSpec((DIL_HALF, 128), lambda s, r, n: (jnp.maximum(n * nq - 1, 0), colfn(r, s)))

    def nxt(colfn):
        return pl.BlockSpec((DIL_HALF, 128), lambda s, r, n: (jnp.minimum((n + 1) * nq, last), colfn(r, s)))

    return prev, cur, nxt


def dil_attn_fwd(h, g, dil, bias, qg, kg):
    S = h.shape[0]
    L = S // dil
    hv = h.reshape(L, dil * IN_W)
    tb = DIL_TQ + 2 * DIL_HALF

    def body(q_ref, kp, kc, kn, vp, vc, vn, bias_ref, qg_ref, kg_ref, o_ref, lse_ref):
        n = pl.program_id(2)
        q = _head_rms(q_ref[...], qg_ref[...]) * (DIL_HD ** -0.5)
        kb = _head_rms(jnp.concatenate([kp[...], kc[...], kn[...]], axis=0), kg_ref[...])
        vb = jnp.concatenate([vp[...], vc[...], vn[...]], axis=0)
        s = _mxu(q, kb, NT) + bias_ref[0]
        ii = lax.broadcasted_iota(jnp.int32, (DIL_TQ, tb), 0)
        jj = lax.broadcasted_iota(jnp.int32, (DIL_TQ, tb), 1)
        kabs = n * DIL_TQ - DIL_HALF + jj
        valid = (jnp.abs(jj - DIL_HALF - ii) <= DIL_HALF) & (kabs >= 0) & (kabs < L)
        s = jnp.where(valid, s, NEG)
        m = jnp.max(s, axis=-1, keepdims=True)
        p = jnp.exp(s - m)
        den = jnp.sum(p, axis=-1, keepdims=True)
        o_ref[...] = _mxu(p, vb, NN) / den
        lse_ref[...] = jnp.broadcast_to(m + jnp.log(den), (DIL_TQ, 128))

    prev, cur, nxt = _dil_specs(L)
    qc = lambda r, s: _dil_col(g, 0, r, s)
    kc_ = lambda r, s: _dil_col(g, 1, r, s)
    vc_ = lambda r, s: _dil_col(g, 2, r, s)
    oc = lambda r, s: r * DIL_SLOTS + s
    vec = pl.BlockSpec((1, 128), lambda s, r, n: (0, 0))
    out = jax.ShapeDtypeStruct((L, dil * DIL_W), F32)
    o, lse = pl.pallas_call(
        body, name=f"dil_attn_fwd{g}", grid=(DIL_SLOTS, dil, L // DIL_TQ),
        in_specs=[cur(qc), prev(kc_), cur(kc_), nxt(kc_), prev(vc_), cur(vc_), nxt(vc_),
                  pl.BlockSpec((1, DIL_TQ, tb), lambda s, r, n: (s, 0, 0)), vec, vec],
        out_specs=(cur(oc), cur(oc)), out_shape=(out, out),
        compiler_params=_cp("parallel", "parallel", "parallel"),
    )(hv, hv, hv, hv, hv, hv, hv, bias, qg, kg)
    return o.reshape(S, DIL_W), lse.reshape(S, DIL_W)


def dil_combine(os_, lses):
    S = os_[0].shape[0]

    def body(o1, o2, o3, l1, l2, l3, y_ref, lt_ref):
        a, b, c = l1[...], l2[...], l3[...]
        m = jnp.maximum(jnp.maximum(a, b), c)
        ea, eb, ec = jnp.exp(a - m), jnp.exp(b - m), jnp.exp(c - m)
        den = ea + eb + ec
        y_ref[...] = (ea * o1[...] + eb * o2[...] + ec * o3[...]) / den
        lt_ref[...] = m + jnp.log(den)

    blk = pl.BlockSpec((GN_T, DIL_W), lambda i: (i, 0))
    out = jax.ShapeDtypeStruct((S, DIL_W), F32)
    return pl.pallas_call(
        body, name="dil_combine", grid=(S // GN_T,), in_specs=[blk] * 6, out_specs=(blk, blk),
        out_shape=(out, out), compiler_params=_cp("parallel"),
    )(*os_, *lses)


def dil_delta(dy, yc):
    S = yc.shape[0]

    def body(dy_ref, y_ref, d_ref):
        d_ref[...] = jnp.broadcast_to(jnp.sum(dy_ref[...] * y_ref[...], axis=-1, keepdims=True), (GN_T, 128))

    return pl.pallas_call(
        body, name="dil_delta", grid=(S // GN_T, DIL_SLOTS),
        in_specs=[pl.BlockSpec((GN_T, 128), lambda i, s: (i, (HG_W + RET_W) // 128 + s)),
                  pl.BlockSpec((GN_T, 128), lambda i, s: (i, s))],
        out_specs=pl.BlockSpec((GN_T, 128), lambda i, s: (i, s)),
        out_shape=jax.ShapeDtypeStruct((S, DIL_W), F32), compiler_params=_cp("parallel", "parallel"),
    )(dy, yc)


def dil_attn_bwd(h, g, dil, bias_q, bias_k, qg, kg, dy, lse_t, delta):
    S = h.shape[0]
    L = S // dil
    hv = h.reshape(L, dil * IN_W)
    dyv = dy.reshape(L, dil * D_MODEL)
    lv = lse_t.reshape(L, dil * DIL_W)
    dv_ = delta.reshape(L, dil * DIL_W)
    tq, tb = DIL_TQ, DIL_TQ + 2 * DIL_HALF
    scale = DIL_HD ** -0.5

    def body(qp, qc, qn, kp, kc, kn, vp, vc, vn, dp_, dc, dn, lp, lc, ln, ep, ec, en, bq_ref, bk_ref, qg_ref, kg_ref,
             dq_ref, dk_ref, dv_ref, dbias_ref, dqg_ref, dkg_ref):
        r, n = pl.program_id(1), pl.program_id(2)

        @pl.when((r == 0) & (n == 0))
        def _():
            for ref in (dbias_ref, dqg_ref, dkg_ref):
                ref[...] = jnp.zeros_like(ref)

        cat = lambda a, b, c: jnp.concatenate([a[...], b[...], c[...]], axis=0)
        qgv, kgv = qg_ref[...], kg_ref[...]
        qfn = lambda t, gg: _head_rms(t, gg) * scale
        qn_c, q_vjp = jax.vjp(qfn, qc[...], qgv)
        k_band = _head_rms(cat(kp, kc, kn), kgv)
        v_band = cat(vp, vc, vn)
        ii = lax.broadcasted_iota(jnp.int32, (tq, tb), 0)
        jj = lax.broadcasted_iota(jnp.int32, (tq, tb), 1)
        kabs = n * tq - DIL_HALF + jj
        valid = (jnp.abs(jj - DIL_HALF - ii) <= DIL_HALF) & (kabs >= 0) & (kabs < L)
        s = _mxu(qn_c, k_band, NT) + bq_ref[0]
        p = jnp.where(valid, jnp.exp(jnp.where(valid, s, NEG) - lc[:, 0:1]), 0.0)
        ds = p * (_mxu(dc[...], v_band, NT) - ec[:, 0:1])
        dbias_ref[0] += ds
        dq, dqg = q_vjp(_mxu(ds, k_band, NN))
        dq_ref[...] = dq
        dqg_ref[0] += dqg
        kn_c, k_vjp = jax.vjp(_head_rms, kc[...], kgv)
        q_band = qfn(cat(qp, qc, qn), qgv)
        do_band = cat(dp_, dc, dn)
        i2 = lax.broadcasted_iota(jnp.int32, (tb, tq), 0)
        j2 = lax.broadcasted_iota(jnp.int32, (tb, tq), 1)
        qabs = n * tq - DIL_HALF + i2
        valid2 = (jnp.abs(j2 + DIL_HALF - i2) <= DIL_HALF) & (qabs >= 0) & (qabs < L)
        s2 = _mxu(q_band, kn_c, NT) + bk_ref[0]
        lse_band = cat(lp, lc, ln)[:, 0:1]
        p2 = jnp.where(valid2, jnp.exp(jnp.where(valid2, s2, NEG) - lse_band), 0.0)
        dv_ref[...] = _mxu(p2, do_band, TN)
        ds2 = p2 * (_mxu(do_band, vc[...], NT) - cat(ep, ec, en)[:, 0:1])
        dk, dkg = k_vjp(_mxu(ds2, q_band, TN))
        dk_ref[...] = dk
        dkg_ref[0] += dkg

    prev, cur, nxt = _dil_specs(L)
    three = lambda colfn: [prev(colfn), cur(colfn), nxt(colfn)]
    qc_ = lambda r, s: _dil_col(g, 0, r, s)
    kc_ = lambda r, s: _dil_col(g, 1, r, s)
    vc_ = lambda r, s: _dil_col(g, 2, r, s)
    dyc = lambda r, s: (r * D_MODEL + HG_W + RET_W) // 128 + s
    oc = lambda r, s: r * DIL_SLOTS + s
    vec = pl.BlockSpec((1, 128), lambda s, r, n: (0, 0))
    acc_vec = pl.BlockSpec((1, 1, 128), lambda s, r, n: (s, 0, 0))
    out = jax.ShapeDtypeStruct((L, dil * DIL_W), F32)
    dq, dk, dv, dbias, dqg, dkg = pl.pallas_call(
        body, name=f"dil_attn_bwd{g}", grid=(DIL_SLOTS, dil, L // tq),
        in_specs=three(qc_) + three(kc_) + three(vc_) + three(dyc) + three(oc) + three(oc)
        + [pl.BlockSpec((1, tq, tb), lambda s, r, n: (s, 0, 0)), pl.BlockSpec((1, tb, tq), lambda s, r, n: (s, 0, 0)),
           vec, vec],
        out_specs=(cur(oc), cur(oc), cur(oc), pl.BlockSpec((1, tq, tb), lambda s, r, n: (s, 0, 0)), acc_vec, acc_vec),
        out_shape=(out, out, out, jax.ShapeDtypeStruct((DIL_SLOTS, tq, tb), F32),
                   jax.ShapeDtypeStruct((DIL_SLOTS, 1, 128), F32), jax.ShapeDtypeStruct((DIL_SLOTS, 1, 128), F32)),
        compiler_params=_cp("arbitrary", "arbitrary", "arbitrary"),
    )(hv, hv, hv, hv, hv, hv, hv, hv, hv, dyv, dyv, dyv, lv, lv, lv, dv_, dv_, dv_, bias_q, bias_k, qg, kg)
    return dq.reshape(S, DIL_W), dk.reshape(S, DIL_W), dv.reshape(S, DIL_W), dbias, dqg, dkg


def _lb_eff(p):
    a = jnp.cumsum(jax.nn.softmax(p.astype(F32), axis=0), axis=0)
    return a - a[0:1]


def _dil_bias(rel_bias, g, dil):
    bq, bk = _dil_buckets(dil)
    tbl = rel_bias[:, g * DIL_SLOTS:(g + 1) * DIL_SLOTS]
    return jnp.transpose(tbl[bq], (2, 0, 1)), jnp.transpose(tbl[bk], (2, 0, 1))


def _layer_fwd(x, l, prm, wts, rc, biases):
    win_g, wout_g, wup_g, wdown_g = wts
    row = lambda a: a[l][None]
    xn = rmsnorm_fwd(x, row(prm["norm_mix"]))
    h = proj_in(xn, win_g, l)
    hof, hob, hsf, hsb = hg_scan_fwd(h, row(prm["lbf"]), row(prm["lbb"]))
    ya = gated_norm_fwd("hg_out", hof, hob, h, OFF_HG_GATE, row(prm["hg_norm"]), False)
    rof, rob, rsf, rsb = ret_scan_fwd(h, rc)
    yb = gated_norm_fwd("ret_out", rof, rob, h, OFF_RET_GATE, row(prm["ret_norm"]), True)
    os_, lses = [], []
    for g, (_, dil) in enumerate(DIL_GROUPS):
        o, lse = dil_attn_fwd(h, g, dil, biases[g][0], row(prm["q_norm"]), row(prm["k_norm"]))
        os_.append(o)
        lses.append(lse)
    yc, lse_t = dil_combine(os_, lses)
    y = jnp.concatenate([ya, yb, yc], axis=1).astype(BF16)
    x2 = proj_out(y, wout_g, l, x)
    hm = rmsnorm_fwd(x2, row(prm["norm_mlp"]))
    u, act = proj_up(hm, wup_g, l)
    x3 = proj_down(act, wdown_g, l, x2)
    saved = dict(x=x, xn=xn, h=h, hof=hof, hob=hob, hsf=hsf, hsb=hsb, rof=rof, rob=rob, rsf=rsf, rsb=rsb,
                 yc=yc, lse_t=lse_t, y=y, x2=x2, hm=hm, u=u, act=act)
    return x3, saved


def _layer_bwd(dx3, l, prm, wts, rc, biases, sv):
    win_g, wout_g, wup_g, wdown_g = wts
    row = lambda a: a[l][None]
    h = sv["h"]
    du = bwd_down_act(dx3, wdown_g, l, sv["u"])
    g_down = wgrad("wgrad_down", sv["act"], dx3, m=D_FF, n=D_MODEL, n_shard=D_MODEL)
    dhm = bwd_up(du, wup_g, l)
    g_up = wgrad("wgrad_up", sv["hm"], du, m=D_MODEL, n=D_FF, n_shard=FF_SHARD)
    dx2, dg_mlp = rmsnorm_bwd(sv["x2"], row(prm["norm_mlp"]), dhm, dx3)
    dy = bwd_out(dx2, wout_g, l)
    g_out = wgrad("wgrad_out", sv["y"], dx2, m=D_MODEL, n=D_MODEL, n_shard=D_MODEL)
    hdo, hdgate, dg_hg = gated_norm_bwd("hg_out_bwd", sv["hof"], sv["hob"], h, OFF_HG_GATE, row(prm["hg_norm"]),
                                        dy, 0, False)
    hdqf, hdvf, hdzf, hdqb, hdvb, hdzb, dlbf, dlbb = hg_scan_bwd(h, row(prm["lbf"]), row(prm["lbb"]), hdo,
                                                                 sv["hsf"], sv["hsb"])
    rdo, rdgate, dg_ret = gated_norm_bwd("ret_out_bwd", sv["rof"], sv["rob"], h, OFF_RET_GATE, row(prm["ret_norm"]),
                                         dy, HG_W, True)
    rdqf, rdkf, rdvf, rdqb, rdkb, rdvb = ret_scan_bwd(h, rc, rdo, sv["rsf"], sv["rsb"])
    delta = dil_delta(dy, sv["yc"])
    dil_parts, dbiases = [], []
    dqg = jnp.zeros((1, DIL_HD), F32)
    dkg = jnp.zeros((1, DIL_HD), F32)
    for g, (_, dil) in enumerate(DIL_GROUPS):
        dq, dk, dv, dbias, dqg_g, dkg_g = dil_attn_bwd(h, g, dil, biases[g][0], biases[g][1], row(prm["q_norm"]),
                                                       row(prm["k_norm"]), dy, sv["lse_t"], delta)
        dil_parts += [dq, dk, dv]
        dbiases.append(dbias)
        dqg = dqg + jnp.sum(dqg_g, axis=0)
        dkg = dkg + jnp.sum(dkg_g, axis=0)
    dh = jnp.concatenate([hdqf + hdqb, hdvf + hdvb, hdzf, hdzb, hdgate,
                          rdqf + rdqb, rdkf + rdkb, rdvf + rdvb, rdgate] + dil_parts, axis=1).astype(BF16)
    dxn = bwd_in(dh, win_g, l)
    g_in = wgrad("wgrad_in", sv["xn"], dh, m=D_MODEL, n=IN_W, n_shard=IN_SHARD)
    dx, dg_mix = rmsnorm_bwd(sv["x"], row(prm["norm_mix"]), dxn, dx2)
    big = (g_in, g_out.reshape(N_CHIPS, D_MODEL // N_CHIPS, D_MODEL), g_up,
           g_down.reshape(N_CHIPS, D_FF // N_CHIPS, D_MODEL))
    small = dict(norm_mix=dg_mix, norm_mlp=dg_mlp, lbf=dlbf, lbb=dlbb, hg_norm=dg_hg, ret_norm=dg_ret,
                 q_norm=dqg, k_norm=dkg)
    return dx, big, small, dbiases


def _rel_bias_grad(dbias_layers):
    cols = []
    for g, (_, dil) in enumerate(DIL_GROUPS):
        bq, _ = _dil_buckets(dil)
        onehot = jax.nn.one_hot(bq, REL_BUCKETS, dtype=F32)
        tot = dbias_layers[0][g]
        for d in dbias_layers[1:]:
            tot = tot + d[g]
        cols.append(jnp.einsum("sij,ijb->bs", tot, onehot, precision=lax.Precision.HIGHEST))
    return jnp.concatenate(cols, axis=1)


def local_step(x, tgt, wts, prm_in):
    S = x.shape[0]
    prm = dict(prm_in)
    prm["lbf"], lbf_vjp = jax.vjp(_lb_eff, prm_in["hg_lb_fwd"])
    prm["lbb"], lbb_vjp = jax.vjp(_lb_eff, prm_in["hg_lb_bwd"])
    rc = _ret_consts(S)
    biases = [_dil_bias(prm["rel_bias"], g, dil) for g, (_, dil) in enumerate(DIL_GROUPS)]
    saved = []
    for l in range(DEPTH):
        x, sv = _layer_fwd(x, l, prm, wts, rc, biases)
        saved.append(sv)
    dx, loss_row = loss_head(x, tgt)
    big, small, dbias_layers = [None] * DEPTH, [None] * DEPTH, [None] * DEPTH
    for l in range(DEPTH - 1, -1, -1):
        dx, big[l], small[l], dbias_layers[l] = _layer_bwd(dx, l, prm, wts, rc, biases, saved[l])
    sg = {k: jnp.concatenate([small[l][k] for l in range(DEPTH)], axis=0) for k in small[0]}
    sg["rel_bias"] = _rel_bias_grad(dbias_layers)
    return loss_row[0, 0], dx, big, sg, (lbf_vjp, lbb_vjp)


def _place():
    x, y, c = lax.axis_index("x"), lax.axis_index("y"), lax.axis_index("c")
    rels = [(1 - x, y), (x, 1 - y), (1 - x, 1 - y)]
    return x, y, c, 2 * x + y, rels


def _half(c, rows):
    return pl.ds(pl.multiple_of(c * (rows // 2), 16), rows // 2)


def allgather_weights(ws):
    nt = len(ws)
    n_ici = nt * DEPTH * 3

    def body(*refs):
        ins, outs = refs[:nt], refs[nt:2 * nt]
        lsem, isend, irecv, dsend, drecv = refs[2 * nt:]
        x, y, c, p, rels = _place()
        sib = (x, y, 1 - c)
        local, sends, passes = [], [], []
        for t in range(nt):
            rows = ins[t].shape[1]
            mine = _half(c, rows)
            for l in range(DEPTH):
                cp = pltpu.make_async_copy(ins[t].at[l], outs[t].at[l, p], lsem.at[t * DEPTH + l])
                cp.start()
                local.append(cp)
                for r, (rx, ry) in enumerate(rels):
                    k = (t * DEPTH + l) * 3 + r
                    cp = pltpu.make_async_remote_copy(
                        src_ref=ins[t].at[l, mine], dst_ref=outs[t].at[l, p, mine],
                        send_sem=isend.at[k], recv_sem=irecv.at[k], device_id=(rx, ry, c), device_id_type=MESH)
                    cp.start()
                    sends.append(cp)
        for t in range(nt):
            mine = _half(c, ins[t].shape[1])
            for l in range(DEPTH):
                for r, (rx, ry) in enumerate(rels):
                    k = (t * DEPTH + l) * 3 + r
                    landed = outs[t].at[l, 2 * rx + ry, mine]
                    pltpu.make_async_remote_copy(
                        src_ref=ins[t].at[l, mine], dst_ref=landed,
                        send_sem=isend.at[k], recv_sem=irecv.at[k], device_id=(rx, ry, c), device_id_type=MESH).wait_recv()
                    cp = pltpu.make_async_remote_copy(
                        src_ref=landed, dst_ref=landed, send_sem=dsend.at[k], recv_sem=drecv.at[k],
                        device_id=sib, device_id_type=MESH)
                    cp.start()
                    passes.append(cp)
        for t in range(nt):
            other = _half(1 - c, ins[t].shape[1])
            for l in range(DEPTH):
                for r, (rx, ry) in enumerate(rels):
                    k = (t * DEPTH + l) * 3 + r
                    region = outs[t].at[l, 2 * rx + ry, other]
                    pltpu.make_async_remote_copy(
                        src_ref=region, dst_ref=region, send_sem=dsend.at[k], recv_sem=drecv.at[k],
                        device_id=sib, device_id_type=MESH).wait_recv()
        for cp in sends + passes:
            cp.wait_send()
        for cp in local:
            cp.wait()

    return pl.pallas_call(
        body, name="allgather_weights",
        in_specs=[ANY] * nt, out_specs=[ANY] * nt,
        out_shape=[jax.ShapeDtypeStruct((DEPTH, N_CHIPS) + w.shape[1:], w.dtype) for w in ws],
        scratch_shapes=[pltpu.SemaphoreType.DMA((nt * DEPTH,)), pltpu.SemaphoreType.DMA((n_ici,)),
                        pltpu.SemaphoreType.DMA((n_ici,)), pltpu.SemaphoreType.DMA((n_ici,)),
                        pltpu.SemaphoreType.DMA((n_ici,))],
    )(*ws)


def grad_pair_exchange(gs):
    n = len(gs)

    def body(*refs):
        ins, outs = refs[:n], refs[n:2 * n]
        ssem, rsem = refs[2 * n:]
        x, y, c, _, _ = _place()
        cps = []
        for i in range(n):
            cp = pltpu.make_async_remote_copy(
                src_ref=ins[i].at[:, _half(1 - c, ins[i].shape[1]), :], dst_ref=outs[i],
                send_sem=ssem.at[i], recv_sem=rsem.at[i], device_id=(x, y, 1 - c), device_id_type=MESH)
            cp.start()
            cps.append(cp)
        for cp in cps:
            cp.wait_recv()
        for cp in cps:
            cp.wait_send()

    return pl.pallas_call(
        body, name="grad_pair_exchange", in_specs=[ANY] * n, out_specs=[ANY] * n,
        out_shape=[jax.ShapeDtypeStruct((N_CHIPS, g.shape[1] // 2, g.shape[2]), F32) for g in gs],
        scratch_shapes=[pltpu.SemaphoreType.DMA((n,)), pltpu.SemaphoreType.DMA((n,))],
    )(*gs)


def pair_add(name, c_arr, g, got):
    _, rows, cols = g.shape
    hr = rows // 2
    tr = 256
    nblk = hr // tr

    def body(c_ref, g_ref, r_ref, o32, o16):
        s = g_ref[...] + r_ref[...]
        o32[...] = s
        o16[...] = s.astype(BF16)

    blk = pl.BlockSpec((1, tr, cols), lambda pp, i, c_ref: (pp, i, 0))
    return pl.pallas_call(
        body, name=name,
        grid_spec=pltpu.PrefetchScalarGridSpec(
            num_scalar_prefetch=1, grid=(N_CHIPS, nblk),
            in_specs=[pl.BlockSpec((1, tr, cols), lambda pp, i, c_ref: (pp, c_ref[0] * nblk + i, 0)), blk],
            out_specs=(blk, blk)),
        out_shape=(jax.ShapeDtypeStruct((N_CHIPS, hr, cols), F32), jax.ShapeDtypeStruct((N_CHIPS, hr, cols), BF16)),
        compiler_params=_cp("parallel", "parallel"),
    )(c_arr, g, got)


def grad_chip_exchange(cs16, cs32):
    n = len(cs16)

    def body(*refs):
        s16, s32 = refs[:n], refs[n:2 * n]
        got, own = refs[2 * n:3 * n], refs[3 * n:4 * n]
        lsem, ssem, rsem = refs[4 * n:]
        x, y, c, p, rels = _place()
        local, cps = [], []
        for i in range(n):
            cp = pltpu.make_async_copy(s32[i].at[p], own[i], lsem.at[i])
            cp.start()
            local.append(cp)
            for r, (rx, ry) in enumerate(rels):
                cp = pltpu.make_async_remote_copy(
                    src_ref=s16[i].at[2 * rx + ry], dst_ref=got[i].at[r],
                    send_sem=ssem.at[i * 3 + r], recv_sem=rsem.at[i * 3 + r], device_id=(rx, ry, c), device_id_type=MESH)
                cp.start()
                cps.append(cp)
        for cp in cps:
            cp.wait_recv()
        for cp in cps:
            cp.wait_send()
        for cp in local:
            cp.wait()

    return pl.pallas_call(
        body, name="grad_chip_exchange", in_specs=[ANY] * (2 * n), out_specs=[ANY] * (2 * n),
        out_shape=[jax.ShapeDtypeStruct((3,) + a.shape[1:], BF16) for a in cs16]
        + [jax.ShapeDtypeStruct(a.shape[1:], F32) for a in cs32],
        scratch_shapes=[pltpu.SemaphoreType.DMA((n,)), pltpu.SemaphoreType.DMA((3 * n,)),
                        pltpu.SemaphoreType.DMA((3 * n,))],
    )(*cs16, *cs32)


def chip_sum(name, own, got):
    hr, cols = own.shape
    tr = 256

    def body(o_ref, g_ref, t_ref):
        t_ref[...] = ((o_ref[...] + g_ref[0].astype(F32)) + g_ref[1].astype(F32)) + g_ref[2].astype(F32)

    blk = pl.BlockSpec((tr, cols), lambda i: (i, 0))
    return pl.pallas_call(
        body, name=name, grid=(hr // tr,),
        in_specs=[blk, pl.BlockSpec((3, tr, cols), lambda i: (0, i, 0))], out_specs=blk,
        out_shape=jax.ShapeDtypeStruct((hr, cols), F32), compiler_params=_cp("parallel"),
    )(own, got)


def grad_pair_share(tots, n_w):
    n = len(tots)

    def body(*refs):
        ins, outs = refs[:n], refs[n:n + n_w]
        lsem, ssem, rsem = refs[n + n_w:]
        x, y, c, _, _ = _place()
        local, cps = [], []
        for t in range(n_w):
            rows = outs[t].shape[1]
            for l in range(DEPTH):
                i = t * DEPTH + l
                mine = outs[t].at[l, _half(c, rows)]
                cp = pltpu.make_async_copy(ins[i], mine, lsem.at[i])
                cp.start()
                local.append(cp)
                cp = pltpu.make_async_remote_copy(src_ref=ins[i], dst_ref=mine, send_sem=ssem.at[i], recv_sem=rsem.at[i],
                                                  device_id=(x, y, 1 - c), device_id_type=MESH)
                cp.start()
                cps.append((cp, t, l))
        for cp, t, l in cps:
            theirs = outs[t].at[l, _half(1 - c, outs[t].shape[1])]
            pltpu.make_async_remote_copy(src_ref=theirs, dst_ref=theirs, send_sem=ssem.at[t * DEPTH + l],
                                         recv_sem=rsem.at[t * DEPTH + l], device_id=(x, y, 1 - c),
                                         device_id_type=MESH).wait_recv()
        for cp, _, _ in cps:
            cp.wait_send()
        for cp in local:
            cp.wait()

    return pl.pallas_call(
        body, name="grad_pair_share", in_specs=[ANY] * n, out_specs=[ANY] * n_w,
        out_shape=[jax.ShapeDtypeStruct((DEPTH, 2 * tots[t * DEPTH].shape[0], tots[t * DEPTH].shape[1]), F32)
                   for t in range(n_w)],
        scratch_shapes=[pltpu.SemaphoreType.DMA((n,)), pltpu.SemaphoreType.DMA((n,)), pltpu.SemaphoreType.DMA((n,))],
    )(*tots)


SMALL_ROWS = 240


def small_allreduce(v):
    def body(v_ref, o_ref, buf, ssem, rsem):
        x, y, c, _, _ = _place()
        me = 4 * x + 2 * y + c
        buf[me] = v_ref[...]
        for d in range(N_DEV):
            @pl.when(me != d)
            def _():
                pltpu.make_async_remote_copy(
                    src_ref=v_ref, dst_ref=buf.at[me], send_sem=ssem.at[d], recv_sem=rsem.at[me],
                    device_id=(d // 4, (d // 2) % 2, d % 2), device_id_type=MESH).start()
        for d in range(N_DEV):
            @pl.when(me != d)
            def _():
                cp = pltpu.make_async_remote_copy(
                    src_ref=v_ref, dst_ref=buf.at[d], send_sem=ssem.at[d], recv_sem=rsem.at[d],
                    device_id=(d // 4, (d // 2) % 2, d % 2), device_id_type=MESH)
                cp.wait_recv()
                cp.wait_send()
        acc = buf[0]
        for d in range(1, N_DEV):
            acc = acc + buf[d]
        o_ref[...] = acc

    vm = pl.BlockSpec(memory_space=pltpu.VMEM)
    return pl.pallas_call(
        body, name="small_allreduce", in_specs=[vm], out_specs=vm,
        out_shape=jax.ShapeDtypeStruct(v.shape, F32),
        scratch_shapes=[pltpu.VMEM((N_DEV,) + v.shape, F32), pltpu.SemaphoreType.DMA((N_DEV,)),
                        pltpu.SemaphoreType.DMA((N_DEV,))],
    )(v)


def reduce_weight_grads(big):
    n_w = len(big[0])
    gs = [big[l][t] for t in range(n_w) for l in range(DEPTH)]
    got = grad_pair_exchange(gs)
    c_arr = lax.axis_index("c").astype(jnp.int32).reshape(1)
    cs = [pair_add(f"pair_add{i // DEPTH}", c_arr, g, r) for i, (g, r) in enumerate(zip(gs, got))]
    res = grad_chip_exchange([a[1] for a in cs], [a[0] for a in cs])
    n = len(gs)
    tots = [chip_sum(f"chip_sum{i // DEPTH}", res[n + i], res[i]) for i in range(n)]
    return grad_pair_share(tots, n_w)


def adamw(name, w, g, m, v):
    shape = w.shape
    cols = shape[-1]
    flat = [t.reshape(-1, cols) for t in (w, g, m, v)]
    rows = flat[0].shape[0]
    tr = 128 if rows % 128 == 0 else rows

    def body(w_ref, g_ref, m_ref, v_ref, d_ref, mo_ref, vo_ref):
        gv = g_ref[...]
        mn = ADAM_B1 * m_ref[...] + (1.0 - ADAM_B1) * gv
        vn = ADAM_B2 * v_ref[...] + (1.0 - ADAM_B2) * jnp.square(gv)
        m_hat = mn / (1.0 - ADAM_B1 ** ADAM_STEP)
        v_hat = vn / (1.0 - ADAM_B2 ** ADAM_STEP)
        d_ref[...] = -ADAM_LR * (m_hat / (jnp.sqrt(v_hat) + ADAM_EPS) + ADAM_WD * w_ref[...])
        mo_ref[...] = mn
        vo_ref[...] = vn

    blk = pl.BlockSpec((tr, cols), lambda i: (i, 0))
    out = jax.ShapeDtypeStruct((rows, cols), F32)
    d, mo, vo = pl.pallas_call(
        body, name=name, grid=(rows // tr,), in_specs=[blk] * 4, out_specs=(blk, blk, blk),
        out_shape=(out, out, out), compiler_params=_cp("parallel"),
    )(*flat)
    return d.reshape(shape), mo.reshape(shape), vo.reshape(shape)


SMALL_NAMES = ("norm_mix", "norm_mlp", "hg_lb_fwd", "hg_lb_bwd", "hg_norm", "ret_norm", "q_norm", "k_norm", "rel_bias")


def _pack_small(d):
    flat = jnp.concatenate([d[k].reshape(-1) for k in SMALL_NAMES])
    return jnp.pad(flat, (0, SMALL_ROWS * 128 - flat.shape[0])).reshape(SMALL_ROWS, 128)


def _unpack_small(v, like):
    flat = v.reshape(-1)
    out, off = {}, 0
    for k in SMALL_NAMES:
        n = like[k].size
        out[k] = flat[off:off + n].reshape(like[k].shape)
        off += n
    return out


def kernel(x, w_in, w_out, w_up, w_down, norm_mix, norm_mlp, hg_lb_fwd, hg_lb_bwd, hg_norm, ret_norm, q_norm, k_norm, rel_bias, loss_target, m_w_in, m_w_out, m_w_up, m_w_down, m_norm_mix, m_norm_mlp, m_hg_lb_fwd, m_hg_lb_bwd, m_hg_norm, m_ret_norm, m_q_norm, m_k_norm, m_rel_bias, v_w_in, v_w_out, v_w_up, v_w_down, v_norm_mix, v_norm_mlp, v_hg_lb_fwd, v_hg_lb_bwd, v_hg_norm, v_ret_norm, v_q_norm, v_k_norm, v_rel_bias):
    big_w = (w_in, w_out, w_up, w_down)
    big_m = (m_w_in, m_w_out, m_w_up, m_w_down)
    big_v = (v_w_in, v_w_out, v_w_up, v_w_down)
    small_w = dict(zip(SMALL_NAMES, (norm_mix, norm_mlp, hg_lb_fwd, hg_lb_bwd, hg_norm, ret_norm, q_norm, k_norm, rel_bias)))
    small_m = dict(zip(SMALL_NAMES, (m_norm_mix, m_norm_mlp, m_hg_lb_fwd, m_hg_lb_bwd, m_hg_norm, m_ret_norm, m_q_norm,
                                     m_k_norm, m_rel_bias)))
    small_v = dict(zip(SMALL_NAMES, (v_norm_mix, v_norm_mlp, v_hg_lb_fwd, v_hg_lb_bwd, v_hg_norm, v_ret_norm, v_q_norm,
                                     v_k_norm, v_rel_bias)))

    win_g, wout_g, wup_g, wdown_g = allgather_weights([w.astype(BF16) for w in big_w])
    wts = (win_g, wout_g.reshape(DEPTH, D_MODEL, D_MODEL), wup_g, wdown_g.reshape(DEPTH, D_FF, D_MODEL))

    loss_part, dx, big, sg, (lbf_vjp, lbb_vjp) = local_step(x[0], loss_target[0], wts, small_w)
    loss = lax.psum(loss_part, ("x", "y", "c"))

    grads_big = reduce_weight_grads(big)

    sg = dict(sg)
    sg["hg_lb_fwd"], sg["hg_lb_bwd"] = sg.pop("lbf"), sg.pop("lbb")
    tot = _unpack_small(small_allreduce(_pack_small(sg)), small_w)
    tot["hg_lb_fwd"] = lbf_vjp(tot["hg_lb_fwd"])[0]
    tot["hg_lb_bwd"] = lbb_vjp(tot["hg_lb_bwd"])[0]
    grads_small = [tot[k] for k in SMALL_NAMES]

    upd_big = [adamw(f"adamw_big{t}", big_w[t], grads_big[t], big_m[t], big_v[t]) for t in range(4)]
    d_s, m_s, v_s = adamw("adamw_small", _pack_small(small_w), _pack_small(tot), _pack_small(small_m), _pack_small(small_v))
    upd_small = [_unpack_small(t, small_w) for t in (d_s, m_s, v_s)]

    outs = [loss, dx[None]] + list(grads_big) + grads_small
    for j in range(3):
        outs += [u[j] for u in upd_big] + [upd_small[j][k] for k in SMALL_NAMES]
    return tuple(outs)
```

```python
import functools
import math

import jax
import jax.numpy as jnp
from jax import lax
from jax.experimental import pallas as pl
from jax.experimental.pallas import tpu as pltpu

F32 = jnp.float32
BF16 = jnp.bfloat16
EPS = 1e-6

D_MODEL = 2048
DEPTH = 4
HG_HEADS = 6
HG_W = 768
RET_HEADS = 6
RET_DK = 64
RET_W = 768
RET_QK_W = RET_HEADS * RET_DK
RET_CHUNK = 128
ROPE_BASE = 10000.0
DIL_SLOTS = 4
DIL_HD = 128
DIL_GROUPS = ((128, 1), (512, 4), (2048, 16))
DIL_HALF = 64
DIL_W = 512
D_FF = 4 * D_MODEL
IN_W = 10752
REL_BUCKETS = 32
REL_MAX_DIST = 1024

OFF_HG_Q, OFF_HG_V, OFF_HG_ZF, OFF_HG_ZB, OFF_HG_GATE = 0, 768, 1536, 2304, 3072
OFF_RET_Q, OFF_RET_K, OFF_RET_V, OFF_RET_GATE = 3840, 4224, 4608, 5376
OFF_DIL = 6144

N_CHIPS = 4
N_DEV = 8
IN_SHARD = IN_W // N_CHIPS
FF_SHARD = D_FF // N_CHIPS

ADAM_LR, ADAM_B1, ADAM_B2, ADAM_EPS, ADAM_WD, ADAM_STEP = 0.001, 0.9, 0.999, 1e-08, 0.01, 10

VMEM_LIMIT = 56 * 1024 * 1024
HG_T = 256
HG_C = 64
RET_T = 256
DIL_TQ = 256
NEG = -1e30

NN = (((1,), (0,)), ((), ()))
NT = (((1,), (1,)), ((), ()))
TN = (((0,), (0,)), ((), ()))
MESH = pl.DeviceIdType.MESH
ANY = pl.BlockSpec(memory_space=pl.ANY)


def _cp(*sem):
    return pltpu.CompilerParams(dimension_semantics=sem, vmem_limit_bytes=VMEM_LIMIT)


def _mxu(a, b, dn):
    return lax.dot_general(a.astype(BF16), b.astype(BF16), dn, preferred_element_type=F32)


@jax.custom_vjp
def dot_nn(a, b):
    return _mxu(a, b, NN)


dot_nn.defvjp(lambda a, b: (_mxu(a, b, NN), (a, b)),
              lambda r, g: (_mxu(g, r[1], NT), _mxu(r[0], g, TN)))


@jax.custom_vjp
def dot_nt(a, b):
    return _mxu(a, b, NT)


dot_nt.defvjp(lambda a, b: (_mxu(a, b, NT), (a, b)),
              lambda r, g: (_mxu(g, r[1], NN), _mxu(g, r[0], TN)))


@jax.custom_vjp
def dot_tn(a, b):
    return _mxu(a, b, TN)


dot_tn.defvjp(lambda a, b: (_mxu(a, b, TN), (a, b)),
              lambda r, g: (_mxu(r[1], g, NT), _mxu(r[0], g, NN)))


def _split3(v):
    hi = v.astype(BF16)
    r1 = v - hi.astype(F32)
    mid = r1.astype(BF16)
    lo = (r1 - mid.astype(F32)).astype(BF16)
    return hi, mid, lo


def _exact_mask_dot(m, v, dn):
    mb = m.astype(BF16)
    hi, mid, lo = _split3(v)
    f = lambda p: lax.dot_general(mb, p, dn, preferred_element_type=F32)
    return (f(lo) + f(mid)) + f(hi)


@jax.custom_vjp
def cumdot(m, v):
    return _exact_mask_dot(m, v, NN)


cumdot.defvjp(lambda m, v: (_exact_mask_dot(m, v, NN), m),
              lambda m, g: (jnp.zeros_like(m), _exact_mask_dot(m, g, TN)))


def _sigmoid(z):
    return 1.0 / (1.0 + jnp.exp(-z))


def _head_rms(t, g):
    return t * lax.rsqrt(jnp.mean(t * t, axis=-1, keepdims=True) + EPS) * g


def _mm(name, a, b, *, mode, grid, a_spec, b_spec, tm, tn, extras=(), extra_specs=(), epi, out_shape, out_specs):
    nk = grid[2]
    n_ex = len(extras)

    def body(*refs):
        a_ref, b_ref = refs[0], refs[1]
        ex = refs[2:2 + n_ex]
        outs = refs[2 + n_ex:-1]
        acc = refs[-1]
        k = pl.program_id(2)

        @pl.when(k == 0)
        def _():
            acc[...] = jnp.zeros_like(acc)

        acc[...] += _mxu(a_ref[...], b_ref[...], {"nn": NN, "nt": NT, "tn": TN}[mode])

        @pl.when(k == nk - 1)
        def _():
            epi(acc[...], ex, outs)

    return pl.pallas_call(
        body, name=name, grid=grid,
        in_specs=[a_spec, b_spec, *extra_specs],
        out_specs=out_specs, out_shape=out_shape,
        scratch_shapes=[pltpu.VMEM((tm, tn), F32)],
        compiler_params=_cp("parallel", "parallel", "arbitrary"),
    )(a, b, *extras)


def _epi_store(acc, ex, outs):
    outs[0][...] = acc.astype(outs[0].dtype)


def _epi_residual(acc, ex, outs):
    outs[0][...] = ex[0][...] + acc


def _epi_up(acc, ex, outs):
    outs[0][...] = acc
    outs[1][...] = jnp.square(jnp.maximum(acc, 0.0)).astype(BF16)


def _epi_dact(acc, ex, outs):
    outs[0][...] = (acc * (2.0 * jnp.maximum(ex[0][...], 0.0))).astype(BF16)


def _ij(i, j, k):
    return (i, j)


def proj_in(xn, win_g, l):
    S = xn.shape[0]
    tm, tn = 1024, 896
    return _mm("proj_in", xn, win_g, mode="nn", grid=(S // tm, IN_W // tn, 1), tm=tm, tn=tn,
               a_spec=pl.BlockSpec((tm, D_MODEL), lambda i, j, k: (i, 0)),
               b_spec=pl.BlockSpec((None, None, D_MODEL, tn), lambda i, j, k: (l, j // 3, 0, j % 3)),
               epi=_epi_store, out_shape=jax.ShapeDtypeStruct((S, IN_W), F32),
               out_specs=pl.BlockSpec((tm, tn), _ij))


def proj_out(y, wout_g, l, x):
    S = y.shape[0]
    tm, tn = 1024, 1024
    return _mm("proj_out", y, wout_g, mode="nn", grid=(S // tm, D_MODEL // tn, 1), tm=tm, tn=tn,
               a_spec=pl.BlockSpec((tm, D_MODEL), lambda i, j, k: (i, 0)),
               b_spec=pl.BlockSpec((None, D_MODEL, tn), lambda i, j, k: (l, 0, j)),
               extras=(x,), extra_specs=(pl.BlockSpec((tm, tn), _ij),),
               epi=_epi_residual, out_shape=jax.ShapeDtypeStruct((S, D_MODEL), F32),
               out_specs=pl.BlockSpec((tm, tn), _ij))


def proj_up(hm, wup_g, l):
    S = hm.shape[0]
    tm, tn = 1024, 1024
    return _mm("proj_up", hm, wup_g, mode="nn", grid=(S // tm, D_FF // tn, 1), tm=tm, tn=tn,
               a_spec=pl.BlockSpec((tm, D_MODEL), lambda i, j, k: (i, 0)),
               b_spec=pl.BlockSpec((None, None, D_MODEL, tn), lambda i, j, k: (l, j // 2, 0, j % 2)),
               epi=_epi_up,
               out_shape=(jax.ShapeDtypeStruct((S, D_FF), F32), jax.ShapeDtypeStruct((S, D_FF), BF16)),
               out_specs=(pl.BlockSpec((tm, tn), _ij), pl.BlockSpec((tm, tn), _ij)))


def proj_down(a, wdown_g, l, x):
    S = a.shape[0]
    tm, tn, tk = 1024, 1024, 2048
    return _mm("proj_down", a, wdown_g, mode="nn", grid=(S // tm, D_MODEL // tn, D_FF // tk), tm=tm, tn=tn,
               a_spec=pl.BlockSpec((tm, tk), lambda i, j, k: (i, k)),
               b_spec=pl.BlockSpec((None, tk, tn), lambda i, j, k: (l, k, j)),
               extras=(x,), extra_specs=(pl.BlockSpec((tm, tn), _ij),),
               epi=_epi_residual, out_shape=jax.ShapeDtypeStruct((S, D_MODEL), F32),
               out_specs=pl.BlockSpec((tm, tn), _ij))


def bwd_down_act(dx, wdown_g, l, u):
    S = dx.shape[0]
    tm, tn = 1024, 1024
    return _mm("bwd_down_act", dx, wdown_g, mode="nt", grid=(S // tm, D_FF // tn, 1), tm=tm, tn=tn,
               a_spec=pl.BlockSpec((tm, D_MODEL), lambda i, j, k: (i, 0)),
               b_spec=pl.BlockSpec((None, tn, D_MODEL), lambda i, j, k: (l, j, 0)),
               extras=(u,), extra_specs=(pl.BlockSpec((tm, tn), _ij),),
               epi=_epi_dact, out_shape=jax.ShapeDtypeStruct((S, D_FF), BF16),
               out_specs=pl.BlockSpec((tm, tn), _ij))


def bwd_up(du, wup_g, l):
    S = du.shape[0]
    tm, tn, tk = 1024, 1024, FF_SHARD
    return _mm("bwd_up", du, wup_g, mode="nt", grid=(S // tm, D_MODEL // tn, D_FF // tk), tm=tm, tn=tn,
               a_spec=pl.BlockSpec((tm, tk), lambda i, j, k: (i, k)),
               b_spec=pl.BlockSpec((None, None, tn, tk), lambda i, j, k: (l, k, j, 0)),
               epi=_epi_store, out_shape=jax.ShapeDtypeStruct((S, D_MODEL), F32),
               out_specs=pl.BlockSpec((tm, tn), _ij))


def bwd_out(dx, wout_g, l):
    S = dx.shape[0]
    tm, tn = 1024, 1024
    return _mm("bwd_out", dx, wout_g, mode="nt", grid=(S // tm, D_MODEL // tn, 1), tm=tm, tn=tn,
               a_spec=pl.BlockSpec((tm, D_MODEL), lambda i, j, k: (i, 0)),
               b_spec=pl.BlockSpec((None, tn, D_MODEL), lambda i, j, k: (l, j, 0)),
               epi=_epi_store, out_shape=jax.ShapeDtypeStruct((S, D_MODEL), F32),
               out_specs=pl.BlockSpec((tm, tn), _ij))


def bwd_in(dh, win_g, l):
    S = dh.shape[0]
    tm, tn, tk = 1024, 1024, 896
    return _mm("bwd_in", dh, win_g, mode="nt", grid=(S // tm, D_MODEL // tn, IN_W // tk), tm=tm, tn=tn,
               a_spec=pl.BlockSpec((tm, tk), lambda i, j, k: (i, k)),
               b_spec=pl.BlockSpec((None, None, tn, tk), lambda i, j, k: (l, k // 3, j, k % 3)),
               epi=_epi_store, out_shape=jax.ShapeDtypeStruct((S, D_MODEL), F32),
               out_specs=pl.BlockSpec((tm, tn), _ij))


def wgrad(name, a, g, *, m, n, n_shard):
    S = a.shape[0]
    tm, tk = 1024, 1024
    tn = 896 if n_shard % 896 == 0 else 1024
    per = n_shard // tn
    if n_shard == n:
        out_shape = jax.ShapeDtypeStruct((m, n), F32)
        out_spec = pl.BlockSpec((tm, tn), _ij)
    else:
        out_shape = jax.ShapeDtypeStruct((N_CHIPS, m, n_shard), F32)
        out_spec = pl.BlockSpec((None, tm, tn), lambda i, j, k: (j // per, i, j % per))
    return _mm(name, a, g, mode="tn", grid=(m // tm, n // tn, S // tk), tm=tm, tn=tn,
               a_spec=pl.BlockSpec((tk, tm), lambda i, j, k: (k, i)),
               b_spec=pl.BlockSpec((tk, tn), lambda i, j, k: (k, j)),
               epi=_epi_store, out_shape=out_shape, out_specs=out_spec)


NORM_T = 256


def rmsnorm_fwd(x, g):
    S = x.shape[0]

    def body(x_ref, g_ref, o_ref):
        xv = x_ref[...]
        r = lax.rsqrt(jnp.mean(xv * xv, axis=-1, keepdims=True) + EPS)
        o_ref[...] = ((xv * r) * g_ref[...]).astype(BF16)

    return pl.pallas_call(
        body, name="rmsnorm_fwd", grid=(S // NORM_T,),
        in_specs=[pl.BlockSpec((NORM_T, D_MODEL), lambda i: (i, 0)), pl.BlockSpec((1, D_MODEL), lambda i: (0, 0))],
        out_specs=pl.BlockSpec((NORM_T, D_MODEL), lambda i: (i, 0)),
        out_shape=jax.ShapeDtypeStruct((S, D_MODEL), BF16), compiler_params=_cp("parallel"),
    )(x, g)


def rmsnorm_bwd(x, g, dxn, dres):
    S = x.shape[0]

    def body(x_ref, g_ref, dxn_ref, dres_ref, dx_ref, dg_ref):
        @pl.when(pl.program_id(0) == 0)
        def _():
            dg_ref[...] = jnp.zeros_like(dg_ref)

        xv, gv, d = x_ref[...], g_ref[...], dxn_ref[...]
        r = lax.rsqrt(jnp.mean(xv * xv, axis=-1, keepdims=True) + EPS)
        gd = gv * d
        dx_ref[...] = dres_ref[...] + r * gd - xv * ((r * r * r) * jnp.mean(xv * gd, axis=-1, keepdims=True))
        dg_ref[...] += jnp.sum(d * (xv * r), axis=0, keepdims=True)

    row = pl.BlockSpec((NORM_T, D_MODEL), lambda i: (i, 0))
    vec = pl.BlockSpec((1, D_MODEL), lambda i: (0, 0))
    return pl.pallas_call(
        body, name="rmsnorm_bwd", grid=(S // NORM_T,),
        in_specs=[row, vec, row, row], out_specs=(row, vec),
        out_shape=(jax.ShapeDtypeStruct((S, D_MODEL), F32), jax.ShapeDtypeStruct((1, D_MODEL), F32)),
        compiler_params=_cp("arbitrary"),
    )(x, g, dxn, dres)


def loss_head(y, tgt):
    S = y.shape[0]

    def body(y_ref, t_ref, dy_ref, l_ref):
        @pl.when(pl.program_id(0) == 0)
        def _():
            l_ref[...] = jnp.zeros_like(l_ref)

        e = y_ref[...] - t_ref[...]
        dy_ref[...] = e * (1.0 / D_MODEL)
        l_ref[...] += jnp.sum(e * e) * (0.5 / D_MODEL)

    row = pl.BlockSpec((NORM_T, D_MODEL), lambda i: (i, 0))
    return pl.pallas_call(
        body, name="loss_head", grid=(S // NORM_T,),
        in_specs=[row, row], out_specs=(row, pl.BlockSpec((1, 128), lambda i: (0, 0))),
        out_shape=(jax.ShapeDtypeStruct((S, D_MODEL), F32), jax.ShapeDtypeStruct((1, 128), F32)),
        compiler_params=_cp("arbitrary"),
    )(y, tgt)


def _hg_block(qs, vs, zs, lb, sT, reverse):
    n = len(qs)
    row = lax.broadcasted_iota(jnp.int32, (HG_C, HG_C), 0)
    col = lax.broadcasted_iota(jnp.int32, (HG_C, HG_C), 1)
    tri = (row <= col) if reverse else (row >= col)
    m = tri.astype(F32)
    rsel = lax.broadcasted_iota(jnp.int32, (HG_C, 128), 0)
    ref_rows = ((rsel >= HG_C // 2) if reverse else (rsel <= HG_C // 2)).astype(F32)
    outs = [None] * n
    for c in (range(n - 1, -1, -1) if reverse else range(n)):
        f = lb + (1.0 - lb) * _sigmoid(zs[c])
        kc = 1.0 - f
        lc = jnp.log(f)
        b = cumdot(m, lc)
        btot = jnp.sum(lc, axis=0, keepdims=True)
        bref = lax.stop_gradient(jnp.sum(lc * ref_rows, axis=0, keepdims=True))
        qe = qs[c] * jnp.exp(jnp.minimum(b - bref, 80.0))
        ke = kc * jnp.exp(jnp.minimum(bref - b, 80.0))
        att = jnp.where(tri, dot_nt(qe, ke), 0.0)
        outs[c] = dot_nn(att, vs[c]) + dot_nt(qs[c] * jnp.exp(b), sT)
        sT = sT * jnp.exp(btot) + dot_tn(vs[c], kc * jnp.exp(btot - b))
    return outs, sT


def _chunks(ref, c, n):
    return [ref[i * c:(i + 1) * c, :] for i in range(n)]


def hg_scan_fwd(h, lbf, lbb):
    S = h.shape[0]
    nb = S // HG_T
    n = HG_T // HG_C

    def body(qf, vf, zf, qb, vb, zb, lbf_ref, lbb_ref, of_ref, ob_ref, sf_ref, sb_ref, stf, stb):
        @pl.when(pl.program_id(1) == 0)
        def _():
            stf[...] = jnp.zeros_like(stf)
            stb[...] = jnp.zeros_like(stb)

        for (q, v, z, lb_ref, o_ref, s_ref, st, rev) in ((qf, vf, zf, lbf_ref, of_ref, sf_ref, stf, False),
                                                         (qb, vb, zb, lbb_ref, ob_ref, sb_ref, stb, True)):
            s_ref[0, 0] = st[...]
            outs, s_new = _hg_block(_chunks(q, HG_C, n), _chunks(v, HG_C, n), _chunks(z, HG_C, n),
                                    lb_ref[...], st[...], rev)
            for c in range(n):
                o_ref[c * HG_C:(c + 1) * HG_C, :] = outs[c]
            st[...] = s_new

    def col(off, rev):
        return pl.BlockSpec((HG_T, 128), (lambda hh, t: (nb - 1 - t, off // 128 + hh)) if rev
                            else (lambda hh, t: (t, off // 128 + hh)))

    lb_spec = pl.BlockSpec((1, 128), lambda hh, t: (0, hh))
    st_f = pl.BlockSpec((1, 1, 128, 128), lambda hh, t: (hh, t, 0, 0))
    st_b = pl.BlockSpec((1, 1, 128, 128), lambda hh, t: (hh, nb - 1 - t, 0, 0))
    return pl.pallas_call(
        body, name="hg_scan_fwd", grid=(HG_HEADS, nb),
        in_specs=[col(OFF_HG_Q, False), col(OFF_HG_V, False), col(OFF_HG_ZF, False),
                  col(OFF_HG_Q, True), col(OFF_HG_V, True), col(OFF_HG_ZB, True), lb_spec, lb_spec],
        out_specs=(col(0, False), col(0, True), st_f, st_b),
        out_shape=(jax.ShapeDtypeStruct((S, HG_W), F32), jax.ShapeDtypeStruct((S, HG_W), F32),
                   jax.ShapeDtypeStruct((HG_HEADS, nb, 128, 128), F32),
                   jax.ShapeDtypeStruct((HG_HEADS, nb, 128, 128), F32)),
        scratch_shapes=[pltpu.VMEM((128, 128), F32), pltpu.VMEM((128, 128), F32)],
        compiler_params=_cp("arbitrary", "arbitrary"),
    )(h, h, h, h, h, h, lbf, lbb)


def hg_scan_bwd(h, lbf, lbb, do, sf, sb):
    S = h.shape[0]
    nb = S // HG_T
    n = HG_T // HG_C

    def body(qf, vf, zf, dof, sfin, qb, vb, zb, dob, sbin, lbf_ref, lbb_ref,
             dqf, dvf, dzf, dqb, dvb, dzb, dlbf, dlbb, dsf, dsb):
        @pl.when(pl.program_id(1) == 0)
        def _():
            for r in (dsf, dsb, dlbf, dlbb):
                r[...] = jnp.zeros_like(r)

        for (q, v, z, dor, sin, lb_ref, dq, dv, dz, dlb, ds, rev) in (
                (qf, vf, zf, dof, sfin, lbf_ref, dqf, dvf, dzf, dlbf, dsf, False),
                (qb, vb, zb, dob, sbin, lbb_ref, dqb, dvb, dzb, dlbb, dsb, True)):
            fn = functools.partial(_hg_block, reverse=rev)
            _, vjp = jax.vjp(fn, _chunks(q, HG_C, n), _chunks(v, HG_C, n), _chunks(z, HG_C, n), lb_ref[...], sin[0, 0])
            dqs, dvs, dzs, dlb_v, ds_in = vjp((_chunks(dor, HG_C, n), ds[...]))
            for c in range(n):
                sl = slice(c * HG_C, (c + 1) * HG_C)
                dq[sl, :] = dqs[c]
                dv[sl, :] = dvs[c]
                dz[sl, :] = dzs[c]
            dlb[...] += dlb_v
            ds[...] = ds_in

    def col(off, fwd_scan):
        return pl.BlockSpec((HG_T, 128), (lambda hh, t: (nb - 1 - t, off // 128 + hh)) if fwd_scan
                            else (lambda hh, t: (t, off // 128 + hh)))

    lb_spec = pl.BlockSpec((1, 128), lambda hh, t: (0, hh))
    st_f = pl.BlockSpec((1, 1, 128, 128), lambda hh, t: (hh, nb - 1 - t, 0, 0))
    st_b = pl.BlockSpec((1, 1, 128, 128), lambda hh, t: (hh, t, 0, 0))
    full = jax.ShapeDtypeStruct((S, HG_W), F32)
    vec = jax.ShapeDtypeStruct((1, HG_W), F32)
    return pl.pallas_call(
        body, name="hg_scan_bwd", grid=(HG_HEADS, nb),
        in_specs=[col(OFF_HG_Q, True), col(OFF_HG_V, True), col(OFF_HG_ZF, True), col(0, True), st_f,
                  col(OFF_HG_Q, False), col(OFF_HG_V, False), col(OFF_HG_ZB, False), col(0, False), st_b,
                  lb_spec, lb_spec],
        out_specs=(col(0, True), col(0, True), col(0, True), col(0, False), col(0, False), col(0, False),
                   lb_spec, lb_spec),
        out_shape=(full, full, full, full, full, full, vec, vec),
        scratch_shapes=[pltpu.VMEM((128, 128), F32), pltpu.VMEM((128, 128), F32)],
        compiler_params=_cp("arbitrary", "arbitrary"),
    )(h, h, h, do, sf, h, h, h, do, sb, lbf, lbb)


GN_T = 1024


def _gated_norm(o, gate, g, center):
    if center:
        o = o - jnp.mean(o, axis=-1, keepdims=True)
    o = o * lax.rsqrt(jnp.mean(o * o, axis=-1, keepdims=True) + EPS)
    return (o * g) * (gate * _sigmoid(gate))


def gated_norm_fwd(name, of, ob, h, gate_off, g, center):
    S = of.shape[0]

    def body(of_ref, ob_ref, gate_ref, g_ref, y_ref):
        y_ref[...] = _gated_norm(of_ref[...] + ob_ref[...], gate_ref[...], g_ref[...], center)

    blk = pl.BlockSpec((GN_T, 128), lambda hh, i: (i, hh))
    return pl.pallas_call(
        body, name=name, grid=(6, S // GN_T),
        in_specs=[blk, blk, pl.BlockSpec((GN_T, 128), lambda hh, i: (i, gate_off // 128 + hh)),
                  pl.BlockSpec((1, 128), lambda hh, i: (0, hh))],
        out_specs=blk, out_shape=jax.ShapeDtypeStruct((S, 768), F32),
        compiler_params=_cp("parallel", "parallel"),
    )(of, ob, h, g)


def gated_norm_bwd(name, of, ob, h, gate_off, g, dy, dy_off, center):
    S = of.shape[0]

    def body(of_ref, ob_ref, gate_ref, g_ref, dy_ref, do_ref, dgate_ref, dg_ref):
        @pl.when(pl.program_id(1) == 0)
        def _():
            dg_ref[...] = jnp.zeros_like(dg_ref)

        fn = functools.partial(_gated_norm, center=center)
        _, vjp = jax.vjp(fn, of_ref[...] + ob_ref[...], gate_ref[...], g_ref[...])
        do, dgate, dg = vjp(dy_ref[...])
        do_ref[...] = do
        dgate_ref[...] = dgate
        dg_ref[...] += dg

    blk = pl.BlockSpec((GN_T, 128), lambda hh, i: (i, hh))
    vec = pl.BlockSpec((1, 128), lambda hh, i: (0, hh))
    return pl.pallas_call(
        body, name=name, grid=(6, S // GN_T),
        in_specs=[blk, blk, pl.BlockSpec((GN_T, 128), lambda hh, i: (i, gate_off // 128 + hh)), vec,
                  pl.BlockSpec((GN_T, 128), lambda hh, i: (i, dy_off // 128 + hh))],
        out_specs=(blk, blk, vec),
        out_shape=(jax.ShapeDtypeStruct((S, 768), F32), jax.ShapeDtypeStruct((S, 768), F32),
                   jax.ShapeDtypeStruct((1, 768), F32)),
        compiler_params=_cp("arbitrary", "arbitrary"),
    )(of, ob, h, g, dy)


def _ret_consts(S):
    half = RET_DK // 2
    inv = ROPE_BASE ** (-jnp.arange(half, dtype=F32) / half)
    ang = jnp.arange(S, dtype=F32)[:, None] * inv[None, :]
    cos, sin = jnp.cos(ang), jnp.sin(ang)
    cos_t = jnp.tile(jnp.concatenate([cos, cos], axis=1), (1, RET_HEADS))
    sin_t = jnp.tile(jnp.concatenate([-sin, sin], axis=1), (1, RET_HEADS))
    hidx = jnp.arange(RET_HEADS, dtype=F32)
    lg_f = jnp.log1p(-jnp.exp2(-5.0 - hidx))
    C = RET_CHUNK
    idx = jnp.arange(C, dtype=F32)
    rel = idx[:, None] - idx[None, :]

    def one(lg, reverse):
        lgc = lg[:, None]
        decay = jnp.where(rel >= 0, jnp.exp(lgc[:, :, None] * jnp.maximum(rel, 0.0)), 0.0)
        zeta = jnp.exp(lgc * (C - 1 - idx))
        xi = jnp.exp(lgc * (idx + 1))
        if reverse:
            decay = decay[:, ::-1, ::-1]
            zeta, xi = zeta[:, ::-1], xi[:, ::-1]
        wide = lambda t: jnp.repeat(t.T, RET_DK, axis=1)
        gam_w = jnp.broadcast_to(jnp.repeat(jnp.exp(lg * C), 128)[None, :], (8, RET_W))
        return decay, wide(xi), wide(zeta), gam_w

    hm = (jnp.arange(RET_QK_W)[None, :] // RET_DK == jnp.arange(8)[:, None]).astype(F32)
    return (cos_t, sin_t, hm) + one(lg_f, False) + one(lg_f[::-1], True)


def _rope(t, cos, sin_signed):
    lane = lax.broadcasted_iota(jnp.int32, t.shape, 1)
    first = (lane & (RET_DK - 1)) < RET_DK // 2
    partner = jnp.where(first, pltpu.roll(t, RET_QK_W - RET_DK // 2, 1), pltpu.roll(t, RET_DK // 2, 1))
    return t * cos + partner * sin_signed


def _ret_block(qs, ks, vs, st, dec, xi, zeta, gam, hms, reverse):
    n = len(qs)
    outs = [None] * n
    st = list(st)
    for c in (range(n - 1, -1, -1) if reverse else range(n)):
        qx = qs[c] * xi
        kz = ks[c] * zeta
        row = []
        for hh in range(RET_HEADS):
            sc = dot_nt(qs[c] * hms[hh], ks[c]) * dec[hh]
            row.append(dot_nn(sc, vs[c][hh]) + dot_nn(qx, st[hh]))
            st[hh] = st[hh] * gam[hh] + dot_tn(kz * hms[hh], vs[c][hh])
        outs[c] = row
    return outs, st


def _ret_inputs(q_ref, k_ref, v_ref, cos_ref, sin_ref):
    n = RET_T // RET_CHUNK
    qr = _rope(q_ref[...], cos_ref[...], sin_ref[...])
    kr = _rope(k_ref[...], cos_ref[...], sin_ref[...]) * (RET_DK ** -0.5)
    qs = [qr[c * RET_CHUNK:(c + 1) * RET_CHUNK] for c in range(n)]
    ks = [kr[c * RET_CHUNK:(c + 1) * RET_CHUNK] for c in range(n)]
    vs = [[v_ref[c * RET_CHUNK:(c + 1) * RET_CHUNK, hh * 128:(hh + 1) * 128] for hh in range(RET_HEADS)]
          for c in range(n)]
    return qs, ks, vs


def _ret_dir_consts(dec_ref, xi_ref, zeta_ref, gam_ref, hm_ref):
    dec = [dec_ref[hh] for hh in range(RET_HEADS)]
    gam = [gam_ref[0:1, hh * 128:(hh + 1) * 128] for hh in range(RET_HEADS)]
    hms = [hm_ref[hh:hh + 1, :] for hh in range(RET_HEADS)]
    return dec, xi_ref[...], zeta_ref[...], gam, hms


def _ret_rows(nb, rev):
    def rows(width, colblk):
        return pl.BlockSpec((RET_T, width), (lambda t: (nb - 1 - t, colblk)) if rev else (lambda t: (t, colblk)))
    return rows


def _const_spec(shape):
    nd = len(shape)
    return pl.BlockSpec(shape, lambda t: (0,) * nd)


def ret_scan_fwd(h, consts):
    S = h.shape[0]
    nb = S // RET_T
    n = RET_T // RET_CHUNK
    cos_t, sin_t, hm, dec_f, xi_f, zeta_f, gam_f, dec_b, xi_b, zeta_b, gam_b = consts

    def body(qf, kf, vf, cf, sf, qb, kb, vb, cb, sb_, hm_ref, decf, xif, zetaf, gamf, decb, xib, zetab, gamb,
             of_ref, ob_ref, sfo, sbo, stf, stb):
        @pl.when(pl.program_id(0) == 0)
        def _():
            stf[...] = jnp.zeros_like(stf)
            stb[...] = jnp.zeros_like(stb)

        for (q, k, v, cs, sn, dr, xr, zr, gr, o_ref, so, st, rev) in (
                (qf, kf, vf, cf, sf, decf, xif, zetaf, gamf, of_ref, sfo, stf, False),
                (qb, kb, vb, cb, sb_, decb, xib, zetab, gamb, ob_ref, sbo, stb, True)):
            so[0] = st[...]
            qs, ks, vs = _ret_inputs(q, k, v, cs, sn)
            dec, xi, zeta, gam, hms = _ret_dir_consts(dr, xr, zr, gr, hm_ref)
            st_in = [st[:, hh * 128:(hh + 1) * 128] for hh in range(RET_HEADS)]
            outs, st_new = _ret_block(qs, ks, vs, st_in, dec, xi, zeta, gam, hms, rev)
            for c in range(n):
                for hh in range(RET_HEADS):
                    o_ref[c * RET_CHUNK:(c + 1) * RET_CHUNK, hh * 128:(hh + 1) * 128] = outs[c][hh]
            for hh in range(RET_HEADS):
                st[:, hh * 128:(hh + 1) * 128] = st_new[hh]

    rf, rb = _ret_rows(nb, False), _ret_rows(nb, True)
    cspecs = [_const_spec(a.shape) for a in (hm, dec_f, xi_f, zeta_f, gam_f, dec_b, xi_b, zeta_b, gam_b)]
    st_shape = jax.ShapeDtypeStruct((nb, RET_QK_W, RET_W), F32)
    qc, kc, vc = OFF_RET_Q // RET_QK_W, OFF_RET_K // RET_QK_W, OFF_RET_V // RET_W
    return pl.pallas_call(
        body, name="ret_scan_fwd", grid=(nb,),
        in_specs=[rf(RET_QK_W, qc), rf(RET_QK_W, kc), rf(RET_W, vc), rf(RET_QK_W, 0), rf(RET_QK_W, 0),
                  rb(RET_QK_W, qc), rb(RET_QK_W, kc), rb(RET_W, vc), rb(RET_QK_W, 0), rb(RET_QK_W, 0)] + cspecs,
        out_specs=(rf(RET_W, 0), rb(RET_W, 0),
                   pl.BlockSpec((1, RET_QK_W, RET_W), lambda t: (t, 0, 0)),
                   pl.BlockSpec((1, RET_QK_W, RET_W), lambda t: (nb - 1 - t, 0, 0))),
        out_shape=(jax.ShapeDtypeStruct((S, RET_W), F32), jax.ShapeDtypeStruct((S, RET_W), F32), st_shape, st_shape),
        scratch_shapes=[pltpu.VMEM((RET_QK_W, RET_W), F32), pltpu.VMEM((RET_QK_W, RET_W), F32)],
        compiler_params=_cp("arbitrary"),
    )(h, h, h, cos_t, sin_t, h, h, h, cos_t, sin_t, hm, dec_f, xi_f, zeta_f, gam_f, dec_b, xi_b, zeta_b, gam_b)


def ret_scan_bwd(h, consts, do, sf, sb):
    S = h.shape[0]
    nb = S // RET_T
    n = RET_T // RET_CHUNK
    cos_t, sin_t, hm, dec_f, xi_f, zeta_f, gam_f, dec_b, xi_b, zeta_b, gam_b = consts

    def body(qf, kf, vf, cf, sf_, dof, sfin, qb, kb, vb, cb, sb_, dob, sbin,
             hm_ref, decf, xif, zetaf, gamf, decb, xib, zetab, gamb,
             dqf, dkf, dvf, dqb, dkb, dvb, dsf, dsb):
        @pl.when(pl.program_id(0) == 0)
        def _():
            dsf[...] = jnp.zeros_like(dsf)
            dsb[...] = jnp.zeros_like(dsb)

        for (q, k, v, cs, sn, dor, sin, dr, xr, zr, gr, dq, dk, dv, ds, rev) in (
                (qf, kf, vf, cf, sf_, dof, sfin, decf, xif, zetaf, gamf, dqf, dkf, dvf, dsf, False),
                (qb, kb, vb, cb, sb_, dob, sbin, decb, xib, zetab, gamb, dqb, dkb, dvb, dsb, True)):
            qs, ks, vs = _ret_inputs(q, k, v, cs, sn)
            dec, xi, zeta, gam, hms = _ret_dir_consts(dr, xr, zr, gr, hm_ref)
            st_in = [sin[0, :, hh * 128:(hh + 1) * 128] for hh in range(RET_HEADS)]
            fn = lambda a, b_, c_, d_: _ret_block(a, b_, c_, d_, dec, xi, zeta, gam, hms, rev)
            _, vjp = jax.vjp(fn, qs, ks, vs, st_in)
            dos = [[dor[c * RET_CHUNK:(c + 1) * RET_CHUNK, hh * 128:(hh + 1) * 128] for hh in range(RET_HEADS)]
                   for c in range(n)]
            dst = [ds[:, hh * 128:(hh + 1) * 128] for hh in range(RET_HEADS)]
            dqs, dks, dvs, dst_in = vjp((dos, dst))
            cosv, sinv = cs[...], sn[...]
            dq[...] = _rope(jnp.concatenate(dqs, axis=0), cosv, -sinv)
            dk[...] = _rope(jnp.concatenate(dks, axis=0) * (RET_DK ** -0.5), cosv, -sinv)
            for c in range(n):
                for hh in range(RET_HEADS):
                    dv[c * RET_CHUNK:(c + 1) * RET_CHUNK, hh * 128:(hh + 1) * 128] = dvs[c][hh]
            for hh in range(RET_HEADS):
                ds[:, hh * 128:(hh + 1) * 128] = dst_in[hh]

    rf, rb = _ret_rows(nb, True), _ret_rows(nb, False)
    cspecs = [_const_spec(a.shape) for a in (hm, dec_f, xi_f, zeta_f, gam_f, dec_b, xi_b, zeta_b, gam_b)]
    qk = jax.ShapeDtypeStruct((S, RET_QK_W), F32)
    vv = jax.ShapeDtypeStruct((S, RET_W), F32)
    qc, kc, vc = OFF_RET_Q // RET_QK_W, OFF_RET_K // RET_QK_W, OFF_RET_V // RET_W
    return pl.pallas_call(
        body, name="ret_scan_bwd", grid=(nb,),
        in_specs=[rf(RET_QK_W, qc), rf(RET_QK_W, kc), rf(RET_W, vc), rf(RET_QK_W, 0), rf(RET_QK_W, 0), rf(RET_W, 0),
                  pl.BlockSpec((1, RET_QK_W, RET_W), lambda t: (nb - 1 - t, 0, 0)),
                  rb(RET_QK_W, qc), rb(RET_QK_W, kc), rb(RET_W, vc), rb(RET_QK_W, 0), rb(RET_QK_W, 0), rb(RET_W, 0),
                  pl.BlockSpec((1, RET_QK_W, RET_W), lambda t: (t, 0, 0))] + cspecs,
        out_specs=(rf(RET_QK_W, 0), rf(RET_QK_W, 0), rf(RET_W, 0), rb(RET_QK_W, 0), rb(RET_QK_W, 0), rb(RET_W, 0)),
        out_shape=(qk, qk, vv, qk, qk, vv),
        scratch_shapes=[pltpu.VMEM((RET_QK_W, RET_W), F32), pltpu.VMEM((RET_QK_W, RET_W), F32)],
        compiler_params=_cp("arbitrary"),
    )(h, h, h, cos_t, sin_t, do, sf, h, h, h, cos_t, sin_t, do, sb,
      hm, dec_f, xi_f, zeta_f, gam_f, dec_b, xi_b, zeta_b, gam_b)


def _t5_bucket(rel):
    nb = REL_BUCKETS // 2
    max_exact = nb // 2
    sign_off = jnp.where(rel > 0, nb, 0)
    n = jnp.abs(rel)
    nf = jnp.maximum(n, 1).astype(F32)
    large = max_exact + (jnp.log(nf / max_exact) / math.log(REL_MAX_DIST / max_exact)
                         * (nb - max_exact)).astype(jnp.int32)
    large = jnp.minimum(large, nb - 1)
    return sign_off + jnp.where(n < max_exact, n, large)


def _dil_buckets(dil):
    tq, tb = DIL_TQ, DIL_TQ + 2 * DIL_HALF
    rel_q = jnp.arange(tb)[None, :] - DIL_HALF - jnp.arange(tq)[:, None]
    rel_k = jnp.arange(tq)[None, :] + DIL_HALF - jnp.arange(tb)[:, None]
    return _t5_bucket(rel_q * dil), _t5_bucket(rel_k * dil)


def dil_view(h, g, dil):
    base = OFF_DIL + 3 * g * DIL_W
    if dil == 1:
        return h, IN_W, base
    return h[:, base:base + 3 * DIL_W].reshape(h.shape[0] // dil, dil * 3 * DIL_W), 3 * DIL_W, 0


def _dil_col(view, j):
    _, width, base = view
    return lambda r, s: (r * width + base + j * DIL_W) // 128 + s


def _dil_specs(L):
    nq = DIL_TQ // DIL_HALF
    last = L // DIL_HALF - 1

    def cur(colfn):
        return pl.BlockSpec((DIL_TQ, 128), lambda s, r, n: (n, colfn(r, s)))

    def prev(colfn):
        return pl.BlockSpec((DIL_HALF, 128), lambda s, r, n: (jnp.maximum(n * nq - 1, 0), colfn(r, s)))

    def nxt(colfn):
        return pl.BlockSpec((DIL_HALF, 128), lambda s, r, n: (jnp.minimum((n + 1) * nq, last), colfn(r, s)))

    return prev, cur, nxt


def dil_attn_fwd(view, S, g, dil, bias, qg, kg):
    L = S // dil
    hv = view[0]
    tb = DIL_TQ + 2 * DIL_HALF

    def body(q_ref, kp, kc, kn, vp, vc, vn, bias_ref, qg_ref, kg_ref, o_ref, lse_ref):
        n = pl.program_id(2)
        q = _head_rms(q_ref[...], qg_ref[...]) * (DIL_HD ** -0.5)
        kb = _head_rms(jnp.concatenate([kp[...], kc[...], kn[...]], axis=0), kg_ref[...])
        vb = jnp.concatenate([vp[...], vc[...], vn[...]], axis=0)
        s = _mxu(q, kb, NT) + bias_ref[0]
        ii = lax.broadcasted_iota(jnp.int32, (DIL_TQ, tb), 0)
        jj = lax.broadcasted_iota(jnp.int32, (DIL_TQ, tb), 1)
        kabs = n * DIL_TQ - DIL_HALF + jj
        valid = (jnp.abs(jj - DIL_HALF - ii) <= DIL_HALF) & (kabs >= 0) & (kabs < L)
        s = jnp.where(valid, s, NEG)
        m = jnp.max(s, axis=-1, keepdims=True)
        p = jnp.exp(s - m)
        den = jnp.sum(p, axis=-1, keepdims=True)
        o_ref[...] = _mxu(p, vb, NN) / den
        lse_ref[...] = jnp.broadcast_to(m + jnp.log(den), (DIL_TQ, 128))

    prev, cur, nxt = _dil_specs(L)
    qc, kc_, vc_ = (_dil_col(view, j) for j in range(3))
    oc = lambda r, s: r * DIL_SLOTS + s
    vec = pl.BlockSpec((1, 128), lambda s, r, n: (0, 0))
    out = jax.ShapeDtypeStruct((L, dil * DIL_W), F32)
    o, lse = pl.pallas_call(
        body, name=f"dil_attn_fwd{g}", grid=(DIL_SLOTS, dil, L // DIL_TQ),
        in_specs=[cur(qc), prev(kc_), cur(kc_), nxt(kc_), prev(vc_), cur(vc_), nxt(vc_),
                  pl.BlockSpec((1, DIL_TQ, tb), lambda s, r, n: (s, 0, 0)), vec, vec],
        out_specs=(cur(oc), cur(oc)), out_shape=(out, out),
        compiler_params=_cp("parallel", "parallel", "parallel"),
    )(hv, hv, hv, hv, hv, hv, hv, bias, qg, kg)
    return o.reshape(S, DIL_W), lse.reshape(S, DIL_W)


def dil_combine(os_, lses):
    S = os_[0].shape[0]

    def body(o1, o2, o3, l1, l2, l3, y_ref, lt_ref):
        a, b, c = l1[...], l2[...], l3[...]
        m = jnp.maximum(jnp.maximum(a, b), c)
        ea, eb, ec = jnp.exp(a - m), jnp.exp(b - m), jnp.exp(c - m)
        den = ea + eb + ec
        y_ref[...] = (ea * o1[...] + eb * o2[...] + ec * o3[...]) / den
        lt_ref[...] = m + jnp.log(den)

    blk = pl.BlockSpec((GN_T, DIL_W), lambda i: (i, 0))
    out = jax.ShapeDtypeStruct((S, DIL_W), F32)
    return pl.pallas_call(
        body, name="dil_combine", grid=(S // GN_T,), in_specs=[blk] * 6, out_specs=(blk, blk),
        out_shape=(out, out), compiler_params=_cp("parallel"),
    )(*os_, *lses)


def dil_delta(dy, yc):
    S = yc.shape[0]

    def body(dy_ref, y_ref, d_ref):
        d_ref[...] = jnp.broadcast_to(jnp.sum(dy_ref[...] * y_ref[...], axis=-1, keepdims=True), (GN_T, 128))

    return pl.pallas_call(
        body, name="dil_delta", grid=(S // GN_T, DIL_SLOTS),
        in_specs=[pl.BlockSpec((GN_T, 128), lambda i, s: (i, (HG_W + RET_W) // 128 + s)),
                  pl.BlockSpec((GN_T, 128), lambda i, s: (i, s))],
        out_specs=pl.BlockSpec((GN_T, 128), lambda i, s: (i, s)),
        out_shape=jax.ShapeDtypeStruct((S, DIL_W), F32), compiler_params=_cp("parallel", "parallel"),
    )(dy, yc)


def dil_attn_bwd(view, S, g, dil, bias_q, bias_k, qg, kg, dy, lse_t, delta):
    L = S // dil
    hv = view[0]
    if dil == 1:
        dyv, dyc = dy, lambda r, s: (HG_W + RET_W) // 128 + s
    else:
        dyv, dyc = dy[:, HG_W + RET_W:].reshape(L, dil * DIL_W), lambda r, s: r * DIL_SLOTS + s
    lv = lse_t.reshape(L, dil * DIL_W)
    dv_ = delta.reshape(L, dil * DIL_W)
    tq, tb = DIL_TQ, DIL_TQ + 2 * DIL_HALF
    scale = DIL_HD ** -0.5

    def body(qp, qc, qn, kp, kc, kn, vp, vc, vn, dp_, dc, dn, lp, lc, ln, ep, ec, en, bq_ref, bk_ref, qg_ref, kg_ref,
             dq_ref, dk_ref, dv_ref, dbias_ref, dqg_ref, dkg_ref):
        r, n = pl.program_id(1), pl.program_id(2)

        @pl.when((r == 0) & (n == 0))
        def _():
            for ref in (dbias_ref, dqg_ref, dkg_ref):
                ref[...] = jnp.zeros_like(ref)

        cat = lambda a, b, c: jnp.concatenate([a[...], b[...], c[...]], axis=0)
        qgv, kgv = qg_ref[...], kg_ref[...]
        qfn = lambda t, gg: _head_rms(t, gg) * scale
        qn_c, q_vjp = jax.vjp(qfn, qc[...], qgv)
        k_band = _head_rms(cat(kp, kc, kn), kgv)
        v_band = cat(vp, vc, vn)
        ii = lax.broadcasted_iota(jnp.int32, (tq, tb), 0)
        jj = lax.broadcasted_iota(jnp.int32, (tq, tb), 1)
        kabs = n * tq - DIL_HALF + jj
        valid = (jnp.abs(jj - DIL_HALF - ii) <= DIL_HALF) & (kabs >= 0) & (kabs < L)
        s = _mxu(qn_c, k_band, NT) + bq_ref[0]
        p = jnp.where(valid, jnp.exp(jnp.where(valid, s, NEG) - lc[:, 0:1]), 0.0)
        ds = p * (_mxu(dc[...], v_band, NT) - ec[:, 0:1])
        dbias_ref[0] += ds
        dq, dqg = q_vjp(_mxu(ds, k_band, NN))
        dq_ref[...] = dq
        dqg_ref[0] += dqg
        kn_c, k_vjp = jax.vjp(_head_rms, kc[...], kgv)
        q_band = qfn(cat(qp, qc, qn), qgv)
        do_band = cat(dp_, dc, dn)
        i2 = lax.broadcasted_iota(jnp.int32, (tb, tq), 0)
        j2 = lax.broadcasted_iota(jnp.int32, (tb, tq), 1)
        qabs = n * tq - DIL_HALF + i2
        valid2 = (jnp.abs(j2 + DIL_HALF - i2) <= DIL_HALF) & (qabs >= 0) & (qabs < L)
        s2 = _mxu(q_band, kn_c, NT) + bk_ref[0]
        lse_band = cat(lp, lc, ln)[:, 0:1]
        p2 = jnp.where(valid2, jnp.exp(jnp.where(valid2, s2, NEG) - lse_band), 0.0)
        dv_ref[...] = _mxu(p2, do_band, TN)
        ds2 = p2 * (_mxu(do_band, vc[...], NT) - cat(ep, ec, en)[:, 0:1])
        dk, dkg = k_vjp(_mxu(ds2, q_band, TN))
        dk_ref[...] = dk
        dkg_ref[0] += dkg

    prev, cur, nxt = _dil_specs(L)
    three = lambda colfn: [prev(colfn), cur(colfn), nxt(colfn)]
    qc_, kc_, vc_ = (_dil_col(view, j) for j in range(3))
    oc = lambda r, s: r * DIL_SLOTS + s
    vec = pl.BlockSpec((1, 128), lambda s, r, n: (0, 0))
    acc_vec = pl.BlockSpec((1, 1, 128), lambda s, r, n: (s, 0, 0))
    out = jax.ShapeDtypeStruct((L, dil * DIL_W), F32)
    dq, dk, dv, dbias, dqg, dkg = pl.pallas_call(
        body, name=f"dil_attn_bwd{g}", grid=(DIL_SLOTS, dil, L // tq),
        in_specs=three(qc_) + three(kc_) + three(vc_) + three(dyc) + three(oc) + three(oc)
        + [pl.BlockSpec((1, tq, tb), lambda s, r, n: (s, 0, 0)), pl.BlockSpec((1, tb, tq), lambda s, r, n: (s, 0, 0)),
           vec, vec],
        out_specs=(cur(oc), cur(oc), cur(oc), pl.BlockSpec((1, tq, tb), lambda s, r, n: (s, 0, 0)), acc_vec, acc_vec),
        out_shape=(out, out, out, jax.ShapeDtypeStruct((DIL_SLOTS, tq, tb), F32),
                   jax.ShapeDtypeStruct((DIL_SLOTS, 1, 128), F32), jax.ShapeDtypeStruct((DIL_SLOTS, 1, 128), F32)),
        compiler_params=_cp("arbitrary", "arbitrary", "arbitrary"),
    )(hv, hv, hv, hv, hv, hv, hv, hv, hv, dyv, dyv, dyv, lv, lv, lv, dv_, dv_, dv_, bias_q, bias_k, qg, kg)
    return dq.reshape(S, DIL_W), dk.reshape(S, DIL_W), dv.reshape(S, DIL_W), dbias, dqg, dkg


def _lb_eff(p):
    a = jnp.cumsum(jax.nn.softmax(p.astype(F32), axis=0), axis=0)
    return a - a[0:1]


def _dil_bias(rel_bias, g, dil):
    tbl = rel_bias[:, g * DIL_SLOTS:(g + 1) * DIL_SLOTS]
    return tuple(jnp.einsum("ijb,bs->sij", jax.nn.one_hot(b, REL_BUCKETS, dtype=F32), tbl,
                            precision=lax.Precision.HIGHEST) for b in _dil_buckets(dil))


def _layer_fwd(x, l, prm, wts, rc, biases):
    win_g, wout_g, wup_g, wdown_g = wts
    row = lambda a: a[l][None]
    xn = rmsnorm_fwd(x, row(prm["norm_mix"]))
    h = proj_in(xn, win_g, l)
    hof, hob, hsf, hsb = hg_scan_fwd(h, row(prm["lbf"]), row(prm["lbb"]))
    ya = gated_norm_fwd("hg_out", hof, hob, h, OFF_HG_GATE, row(prm["hg_norm"]), False)
    rof, rob, rsf, rsb = ret_scan_fwd(h, rc)
    yb = gated_norm_fwd("ret_out", rof, rob, h, OFF_RET_GATE, row(prm["ret_norm"]), True)
    os_, lses, views = [], [], []
    for g, (_, dil) in enumerate(DIL_GROUPS):
        views.append(dil_view(h, g, dil))
        o, lse = dil_attn_fwd(views[g], h.shape[0], g, dil, biases[g][0], row(prm["q_norm"]), row(prm["k_norm"]))
        os_.append(o)
        lses.append(lse)
    yc, lse_t = dil_combine(os_, lses)
    y = jnp.concatenate([ya, yb, yc], axis=1).astype(BF16)
    x2 = proj_out(y, wout_g, l, x)
    hm = rmsnorm_fwd(x2, row(prm["norm_mlp"]))
    u, act = proj_up(hm, wup_g, l)
    x3 = proj_down(act, wdown_g, l, x2)
    saved = dict(x=x, xn=xn, h=h, hof=hof, hob=hob, hsf=hsf, hsb=hsb, rof=rof, rob=rob, rsf=rsf, rsb=rsb,
                 yc=yc, lse_t=lse_t, y=y, x2=x2, hm=hm, u=u, act=act, views=views)
    return x3, saved


def _layer_bwd(dx3, l, prm, wts, rc, biases, sv):
    win_g, wout_g, wup_g, wdown_g = wts
    row = lambda a: a[l][None]
    h = sv["h"]
    du = bwd_down_act(dx3, wdown_g, l, sv["u"])
    g_down = wgrad("wgrad_down", sv["act"], dx3, m=D_FF, n=D_MODEL, n_shard=D_MODEL)
    dhm = bwd_up(du, wup_g, l)
    g_up = wgrad("wgrad_up", sv["hm"], du, m=D_MODEL, n=D_FF, n_shard=FF_SHARD)
    dx2, dg_mlp = rmsnorm_bwd(sv["x2"], row(prm["norm_mlp"]), dhm, dx3)
    dy = bwd_out(dx2, wout_g, l)
    g_out = wgrad("wgrad_out", sv["y"], dx2, m=D_MODEL, n=D_MODEL, n_shard=D_MODEL)
    hdo, hdgate, dg_hg = gated_norm_bwd("hg_out_bwd", sv["hof"], sv["hob"], h, OFF_HG_GATE, row(prm["hg_norm"]),
                                        dy, 0, False)
    hdqf, hdvf, hdzf, hdqb, hdvb, hdzb, dlbf, dlbb = hg_scan_bwd(h, row(prm["lbf"]), row(prm["lbb"]), hdo,
                                                                 sv["hsf"], sv["hsb"])
    rdo, rdgate, dg_ret = gated_norm_bwd("ret_out_bwd", sv["rof"], sv["rob"], h, OFF_RET_GATE, row(prm["ret_norm"]),
                                         dy, HG_W, True)
    rdqf, rdkf, rdvf, rdqb, rdkb, rdvb = ret_scan_bwd(h, rc, rdo, sv["rsf"], sv["rsb"])
    delta = dil_delta(dy, sv["yc"])
    dil_parts, dbiases = [], []
    dqg = jnp.zeros((1, DIL_HD), F32)
    dkg = jnp.zeros((1, DIL_HD), F32)
    for g, (_, dil) in enumerate(DIL_GROUPS):
        dq, dk, dv, dbias, dqg_g, dkg_g = dil_attn_bwd(sv["views"][g], h.shape[0], g, dil, biases[g][0], biases[g][1],
                                                       row(prm["q_norm"]), row(prm["k_norm"]), dy, sv["lse_t"], delta)
        dil_parts += [dq, dk, dv]
        dbiases.append(dbias)
        dqg = dqg + jnp.sum(dqg_g, axis=0)
        dkg = dkg + jnp.sum(dkg_g, axis=0)
    dh = jnp.concatenate([hdqf + hdqb, hdvf + hdvb, hdzf, hdzb, hdgate,
                          rdqf + rdqb, rdkf + rdkb, rdvf + rdvb, rdgate] + dil_parts, axis=1).astype(BF16)
    dxn = bwd_in(dh, win_g, l)
    g_in = wgrad("wgrad_in", sv["xn"], dh, m=D_MODEL, n=IN_W, n_shard=IN_SHARD)
    dx, dg_mix = rmsnorm_bwd(sv["x"], row(prm["norm_mix"]), dxn, dx2)
    big = (g_in, g_out.reshape(N_CHIPS, D_MODEL // N_CHIPS, D_MODEL), g_up,
           g_down.reshape(N_CHIPS, D_FF // N_CHIPS, D_MODEL))
    small = dict(norm_mix=dg_mix, norm_mlp=dg_mlp, lbf=dlbf, lbb=dlbb, hg_norm=dg_hg, ret_norm=dg_ret,
                 q_norm=dqg, k_norm=dkg)
    return dx, big, small, dbiases


def _rel_bias_grad(dbias_layers):
    cols = []
    for g, (_, dil) in enumerate(DIL_GROUPS):
        bq, _ = _dil_buckets(dil)
        onehot = jax.nn.one_hot(bq, REL_BUCKETS, dtype=F32)
        tot = dbias_layers[0][g]
        for d in dbias_layers[1:]:
            tot = tot + d[g]
        cols.append(jnp.einsum("sij,ijb->bs", tot, onehot, precision=lax.Precision.HIGHEST))
    return jnp.concatenate(cols, axis=1)


def local_step(x, tgt, wts, prm_in):
    S = x.shape[0]
    prm = dict(prm_in)
    prm["lbf"], lbf_vjp = jax.vjp(_lb_eff, prm_in["hg_lb_fwd"])
    prm["lbb"], lbb_vjp = jax.vjp(_lb_eff, prm_in["hg_lb_bwd"])
    rc = _ret_consts(S)
    biases = [_dil_bias(prm["rel_bias"], g, dil) for g, (_, dil) in enumerate(DIL_GROUPS)]
    saved = []
    for l in range(DEPTH):
        x, sv = _layer_fwd(x, l, prm, wts, rc, biases)
        saved.append(sv)
    dx, loss_row = loss_head(x, tgt)
    big, small, dbias_layers = [None] * DEPTH, [None] * DEPTH, [None] * DEPTH
    for l in range(DEPTH - 1, -1, -1):
        dx, big[l], small[l], dbias_layers[l] = _layer_bwd(dx, l, prm, wts, rc, biases, saved[l])
    sg = {k: jnp.concatenate([small[l][k] for l in range(DEPTH)], axis=0) for k in small[0]}
    sg["rel_bias"] = _rel_bias_grad(dbias_layers)
    return loss_row[0, 0], dx, big, sg, (lbf_vjp, lbb_vjp)


def _place():
    x, y, c = lax.axis_index("x"), lax.axis_index("y"), lax.axis_index("c")
    rels = [(1 - x, y), (x, 1 - y), (1 - x, 1 - y)]
    return x, y, c, 2 * x + y, rels


def _half(c, rows):
    return pl.ds(pl.multiple_of(c * (rows // 2), 16), rows // 2)


def place_own(name, p_arr, w):
    depth, rows, cols = w.shape
    tr = 512

    def body(p_ref, w_ref, o_ref):
        o_ref[0, 0] = w_ref[0].astype(BF16)

    return pl.pallas_call(
        body, name=name,
        grid_spec=pltpu.PrefetchScalarGridSpec(
            num_scalar_prefetch=1, grid=(depth, rows // tr),
            in_specs=[pl.BlockSpec((1, tr, cols), lambda l, i, p: (l, i, 0))],
            out_specs=pl.BlockSpec((1, 1, tr, cols), lambda l, i, p: (l, p[0], i, 0))),
        out_shape=jax.ShapeDtypeStruct((depth, N_CHIPS, rows, cols), BF16),
        compiler_params=_cp("parallel", "parallel"),
    )(p_arr, w)


def allgather_weights(gs):
    nt = len(gs)
    n_ici = nt * DEPTH * 3

    def body(*refs):
        bufs = refs[nt:2 * nt]
        isend, irecv, dsend, drecv = refs[2 * nt:]
        x, y, c, p, rels = _place()
        sib = (x, y, 1 - c)
        sends, passes = [], []
        order = [(l, t) for l in range(DEPTH) for t in range(nt)]
        for l, t in order:
            mine = _half(c, bufs[t].shape[2])
            for r, (rx, ry) in enumerate(rels):
                k = (t * DEPTH + l) * 3 + r
                own = bufs[t].at[l, p, mine]
                cp = pltpu.make_async_remote_copy(
                    src_ref=own, dst_ref=own, send_sem=isend.at[k], recv_sem=irecv.at[k],
                    device_id=(rx, ry, c), device_id_type=MESH)
                cp.start()
                sends.append(cp)
        for l, t in order:
            mine = _half(c, bufs[t].shape[2])
            for r, (rx, ry) in enumerate(rels):
                k = (t * DEPTH + l) * 3 + r
                landed = bufs[t].at[l, 2 * rx + ry, mine]
                pltpu.make_async_remote_copy(
                    src_ref=landed, dst_ref=landed, send_sem=isend.at[k], recv_sem=irecv.at[k],
                    device_id=(rx, ry, c), device_id_type=MESH).wait_recv()
                cp = pltpu.make_async_remote_copy(
                    src_ref=landed, dst_ref=landed, send_sem=dsend.at[k], recv_sem=drecv.at[k],
                    device_id=sib, device_id_type=MESH)
                cp.start()
                passes.append(cp)
        for l, t in order:
            other = _half(1 - c, bufs[t].shape[2])
            for r, (rx, ry) in enumerate(rels):
                k = (t * DEPTH + l) * 3 + r
                region = bufs[t].at[l, 2 * rx + ry, other]
                pltpu.make_async_remote_copy(
                    src_ref=region, dst_ref=region, send_sem=dsend.at[k], recv_sem=drecv.at[k],
                    device_id=sib, device_id_type=MESH).wait_recv()
        for cp in sends + passes:
            cp.wait_send()

    return pl.pallas_call(
        body, name="allgather_weights",
        in_specs=[ANY] * nt, out_specs=[ANY] * nt,
        out_shape=[jax.ShapeDtypeStruct(g.shape, g.dtype) for g in gs],
        input_output_aliases={t: t for t in range(nt)},
        scratch_shapes=[pltpu.SemaphoreType.DMA((n_ici,)), pltpu.SemaphoreType.DMA((n_ici,)),
                        pltpu.SemaphoreType.DMA((n_ici,)), pltpu.SemaphoreType.DMA((n_ici,))],
    )(*gs)


def grad_pair_exchange(gs):
    n = len(gs)

    def body(*refs):
        ins, outs = refs[:n], refs[n:2 * n]
        ssem, rsem = refs[2 * n:]
        x, y, c, _, _ = _place()
        cps = []
        for i in range(n):
            cp = pltpu.make_async_remote_copy(
                src_ref=ins[i].at[:, _half(1 - c, ins[i].shape[1]), :], dst_ref=outs[i],
                send_sem=ssem.at[i], recv_sem=rsem.at[i], device_id=(x, y, 1 - c), device_id_type=MESH)
            cp.start()
            cps.append(cp)
        for cp in cps:
            cp.wait_recv()
        for cp in cps:
            cp.wait_send()

    return pl.pallas_call(
        body, name="grad_pair_exchange", in_specs=[ANY] * n, out_specs=[ANY] * n,
        out_shape=[jax.ShapeDtypeStruct((N_CHIPS, g.shape[1] // 2, g.shape[2]), F32) for g in gs],
        scratch_shapes=[pltpu.SemaphoreType.DMA((n,)), pltpu.SemaphoreType.DMA((n,))],
    )(*gs)


def pair_add(name, c_arr, g, got):
    _, rows, cols = g.shape
    hr = rows // 2
    tr = 256
    nblk = hr // tr

    def body(c_ref, g_ref, r_ref, o32, o16):
        s = g_ref[...] + r_ref[...]
        o32[...] = s
        o16[...] = s.astype(BF16)

    blk = pl.BlockSpec((1, tr, cols), lambda pp, i, c_ref: (pp, i, 0))
    return pl.pallas_call(
        body, name=name,
        grid_spec=pltpu.PrefetchScalarGridSpec(
            num_scalar_prefetch=1, grid=(N_CHIPS, nblk),
            in_specs=[pl.BlockSpec((1, tr, cols), lambda pp, i, c_ref: (pp, c_ref[0] * nblk + i, 0)), blk],
            out_specs=(blk, blk)),
        out_shape=(jax.ShapeDtypeStruct((N_CHIPS, hr, cols), F32), jax.ShapeDtypeStruct((N_CHIPS, hr, cols), BF16)),
        compiler_params=_cp("parallel", "parallel"),
    )(c_arr, g, got)


def grad_chip_exchange(cs16):
    n = len(cs16)

    def body(*refs):
        s16, got = refs[:n], refs[n:2 * n]
        ssem, rsem = refs[2 * n:]
        x, y, c, p, rels = _place()
        cps = []
        for i in range(n):
            for r, (rx, ry) in enumerate(rels):
                cp = pltpu.make_async_remote_copy(
                    src_ref=s16[i].at[2 * rx + ry], dst_ref=got[i].at[r],
                    send_sem=ssem.at[i * 3 + r], recv_sem=rsem.at[i * 3 + r], device_id=(rx, ry, c), device_id_type=MESH)
                cp.start()
                cps.append(cp)
        for cp in cps:
            cp.wait_recv()
        for cp in cps:
            cp.wait_send()

    return pl.pallas_call(
        body, name="grad_chip_exchange", in_specs=[ANY] * n, out_specs=[ANY] * n,
        out_shape=[jax.ShapeDtypeStruct((3,) + a.shape[1:], BF16) for a in cs16],
        scratch_shapes=[pltpu.SemaphoreType.DMA((3 * n,)), pltpu.SemaphoreType.DMA((3 * n,))],
    )(*cs16)


def chip_sum(name, pc_arr, l, cs32, got, prev):
    _, hr, cols = cs32.shape
    tr = 256
    nblk = hr // tr

    def body(pc_ref, o_ref, g_ref, *rest):
        rest[-1][0] = ((o_ref[0] + g_ref[0].astype(F32)) + g_ref[1].astype(F32)) + g_ref[2].astype(F32)

    return pl.pallas_call(
        body, name=name,
        grid_spec=pltpu.PrefetchScalarGridSpec(
            num_scalar_prefetch=1, grid=(nblk,),
            in_specs=[pl.BlockSpec((1, tr, cols), lambda i, pc: (pc[0], i, 0)),
                      pl.BlockSpec((3, tr, cols), lambda i, pc: (0, i, 0))] + ([] if prev is None else [ANY]),
            out_specs=pl.BlockSpec((1, tr, cols), lambda i, pc: (l, pc[1] * nblk + i, 0))),
        out_shape=jax.ShapeDtypeStruct((DEPTH, 2 * hr, cols), F32),
        input_output_aliases={} if prev is None else {3: 0},
        compiler_params=_cp("arbitrary"),
    )(*((pc_arr, cs32, got) + (() if prev is None else (prev,))))


def grad_pair_share(halves):
    n_w = len(halves)
    n = n_w * DEPTH

    def body(*refs):
        bufs = refs[n_w:2 * n_w]
        ssem, rsem = refs[2 * n_w:]
        x, y, c, _, _ = _place()
        cps = []
        for t in range(n_w):
            for l in range(DEPTH):
                mine = bufs[t].at[l, _half(c, bufs[t].shape[1])]
                cp = pltpu.make_async_remote_copy(src_ref=mine, dst_ref=mine, send_sem=ssem.at[t * DEPTH + l],
                                                  recv_sem=rsem.at[t * DEPTH + l], device_id=(x, y, 1 - c),
                                                  device_id_type=MESH)
                cp.start()
                cps.append(cp)
        for t in range(n_w):
            for l in range(DEPTH):
                theirs = bufs[t].at[l, _half(1 - c, bufs[t].shape[1])]
                pltpu.make_async_remote_copy(src_ref=theirs, dst_ref=theirs, send_sem=ssem.at[t * DEPTH + l],
                                             recv_sem=rsem.at[t * DEPTH + l], device_id=(x, y, 1 - c),
                                             device_id_type=MESH).wait_recv()
        for cp in cps:
            cp.wait_send()

    return pl.pallas_call(
        body, name="grad_pair_share", in_specs=[ANY] * n_w, out_specs=[ANY] * n_w,
        out_shape=[jax.ShapeDtypeStruct(a.shape, F32) for a in halves],
        input_output_aliases={t: t for t in range(n_w)},
        scratch_shapes=[pltpu.SemaphoreType.DMA((n,)), pltpu.SemaphoreType.DMA((n,))],
    )(*halves)


SMALL_ROWS = 240


def small_allreduce(v):
    def body(v_ref, o_ref, buf, ssem, rsem):
        x, y, c, _, _ = _place()
        me = 4 * x + 2 * y + c
        buf[me] = v_ref[...]
        for d in range(N_DEV):
            @pl.when(me != d)
            def _():
                pltpu.make_async_remote_copy(
                    src_ref=v_ref, dst_ref=buf.at[me], send_sem=ssem.at[d], recv_sem=rsem.at[me],
                    device_id=(d // 4, (d // 2) % 2, d % 2), device_id_type=MESH).start()
        for d in range(N_DEV):
            @pl.when(me != d)
            def _():
                cp = pltpu.make_async_remote_copy(
                    src_ref=v_ref, dst_ref=buf.at[d], send_sem=ssem.at[d], recv_sem=rsem.at[d],
                    device_id=(d // 4, (d // 2) % 2, d % 2), device_id_type=MESH)
                cp.wait_recv()
                cp.wait_send()
        acc = buf[0]
        for d in range(1, N_DEV):
            acc = acc + buf[d]
        o_ref[...] = acc

    vm = pl.BlockSpec(memory_space=pltpu.VMEM)
    return pl.pallas_call(
        body, name="small_allreduce", in_specs=[vm], out_specs=vm,
        out_shape=jax.ShapeDtypeStruct(v.shape, F32),
        scratch_shapes=[pltpu.VMEM((N_DEV,) + v.shape, F32), pltpu.SemaphoreType.DMA((N_DEV,)),
                        pltpu.SemaphoreType.DMA((N_DEV,))],
    )(v)


def reduce_weight_grads(big):
    n_w = len(big[0])
    gs = [big[l][t] for t in range(n_w) for l in range(DEPTH)]
    got = grad_pair_exchange(gs)
    c_arr = lax.axis_index("c").astype(jnp.int32).reshape(1)
    pc_arr = jnp.stack([2 * lax.axis_index("x") + lax.axis_index("y"), lax.axis_index("c")]).astype(jnp.int32)
    cs = [pair_add(f"pair_add{i // DEPTH}", c_arr, g, r) for i, (g, r) in enumerate(zip(gs, got))]
    arrived = grad_chip_exchange([a[1] for a in cs])
    halves = []
    for t in range(n_w):
        acc = None
        for l in range(DEPTH):
            acc = chip_sum(f"chip_sum{t}", pc_arr, l, cs[t * DEPTH + l][0], arrived[t * DEPTH + l], acc)
        halves.append(acc)
    return grad_pair_share(halves)


def adamw(name, w, g, m, v):
    shape = w.shape
    cols = shape[-1]
    flat = [t.reshape(-1, cols) for t in (w, g, m, v)]
    rows = flat[0].shape[0]
    tr = 128 if rows % 128 == 0 else rows

    def body(w_ref, g_ref, m_ref, v_ref, d_ref, mo_ref, vo_ref):
        gv = g_ref[...]
        mn = ADAM_B1 * m_ref[...] + (1.0 - ADAM_B1) * gv
        vn = ADAM_B2 * v_ref[...] + (1.0 - ADAM_B2) * jnp.square(gv)
        m_hat = mn / (1.0 - ADAM_B1 ** ADAM_STEP)
        v_hat = vn / (1.0 - ADAM_B2 ** ADAM_STEP)
        d_ref[...] = -ADAM_LR * (m_hat / (jnp.sqrt(v_hat) + ADAM_EPS) + ADAM_WD * w_ref[...])
        mo_ref[...] = mn
        vo_ref[...] = vn

    blk = pl.BlockSpec((tr, cols), lambda i: (i, 0))
    out = jax.ShapeDtypeStruct((rows, cols), F32)
    d, mo, vo = pl.pallas_call(
        body, name=name, grid=(rows // tr,), in_specs=[blk] * 4, out_specs=(blk, blk, blk),
        out_shape=(out, out, out), compiler_params=_cp("parallel"),
    )(*flat)
    return d.reshape(shape), mo.reshape(shape), vo.reshape(shape)


SMALL_NAMES = ("norm_mix", "norm_mlp", "hg_lb_fwd", "hg_lb_bwd", "hg_norm", "ret_norm", "q_norm", "k_norm", "rel_bias")


def _pack_small(d):
    flat = jnp.concatenate([d[k].reshape(-1) for k in SMALL_NAMES])
    return jnp.pad(flat, (0, SMALL_ROWS * 128 - flat.shape[0])).reshape(SMALL_ROWS, 128)


def _unpack_small(v, like):
    flat = v.reshape(-1)
    out, off = {}, 0
    for k in SMALL_NAMES:
        n = like[k].size
        out[k] = flat[off:off + n].reshape(like[k].shape)
        off += n
    return out


def kernel(x, w_in, w_out, w_up, w_down, norm_mix, norm_mlp, hg_lb_fwd, hg_lb_bwd, hg_norm, ret_norm, q_norm, k_norm, rel_bias, loss_target, m_w_in, m_w_out, m_w_up, m_w_down, m_norm_mix, m_norm_mlp, m_hg_lb_fwd, m_hg_lb_bwd, m_hg_norm, m_ret_norm, m_q_norm, m_k_norm, m_rel_bias, v_w_in, v_w_out, v_w_up, v_w_down, v_norm_mix, v_norm_mlp, v_hg_lb_fwd, v_hg_lb_bwd, v_hg_norm, v_ret_norm, v_q_norm, v_k_norm, v_rel_bias):
    big_w = (w_in, w_out, w_up, w_down)
    big_m = (m_w_in, m_w_out, m_w_up, m_w_down)
    big_v = (v_w_in, v_w_out, v_w_up, v_w_down)
    small_w = dict(zip(SMALL_NAMES, (norm_mix, norm_mlp, hg_lb_fwd, hg_lb_bwd, hg_norm, ret_norm, q_norm, k_norm, rel_bias)))
    small_m = dict(zip(SMALL_NAMES, (m_norm_mix, m_norm_mlp, m_hg_lb_fwd, m_hg_lb_bwd, m_hg_norm, m_ret_norm, m_q_norm,
                                     m_k_norm, m_rel_bias)))
    small_v = dict(zip(SMALL_NAMES, (v_norm_mix, v_norm_mlp, v_hg_lb_fwd, v_hg_lb_bwd, v_hg_norm, v_ret_norm, v_q_norm,
                                     v_k_norm, v_rel_bias)))

    p_arr = (2 * lax.axis_index("x") + lax.axis_index("y")).astype(jnp.int32).reshape(1)
    win_g, wout_g, wup_g, wdown_g = allgather_weights([place_own(f"place_own{t}", p_arr, w)
                                                       for t, w in enumerate(big_w)])
    wts = (win_g, wout_g.reshape(DEPTH, D_MODEL, D_MODEL), wup_g, wdown_g.reshape(DEPTH, D_FF, D_MODEL))

    loss_part, dx, big, sg, (lbf_vjp, lbb_vjp) = local_step(x[0], loss_target[0], wts, small_w)
    loss = lax.psum(loss_part, ("x", "y", "c"))

    grads_big = reduce_weight_grads(big)

    sg = dict(sg)
    sg["hg_lb_fwd"], sg["hg_lb_bwd"] = sg.pop("lbf"), sg.pop("lbb")
    tot = _unpack_small(small_allreduce(_pack_small(sg)), small_w)
    tot["hg_lb_fwd"] = lbf_vjp(tot["hg_lb_fwd"])[0]
    tot["hg_lb_bwd"] = lbb_vjp(tot["hg_lb_bwd"])[0]
    grads_small = [tot[k] for k in SMALL_NAMES]

    upd_big = [adamw(f"adamw_big{t}", big_w[t], grads_big[t], big_m[t], big_v[t]) for t in range(4)]
    d_s, m_s, v_s = adamw("adamw_small", _pack_small(small_w), _pack_small(tot), _pack_small(small_m), _pack_small(small_v))
    upd_small = [_unpack_small(t, small_w) for t in (d_s, m_s, v_s)]

    outs = [loss, dx[None]] + list(grads_big) + grads_small
    for j in range(3):
        outs += [u[j] for u in upd_big] + [upd_small[j][k] for k in SMALL_NAMES]
    return tuple(outs)
```

```python
import functools
import math

import jax
import jax.numpy as jnp
from jax import lax
from jax.experimental import pallas as pl
from jax.experimental.pallas import tpu as pltpu

F32 = jnp.float32
BF16 = jnp.bfloat16
EPS = 1e-6

D_MODEL = 2048
DEPTH = 4
HG_HEADS = 6
HG_W = 768
RET_HEADS = 6
RET_DK = 64
RET_W = 768
RET_QK_W = RET_HEADS * RET_DK
RET_CHUNK = 128
ROPE_BASE = 10000.0
DIL_SLOTS = 4
DIL_HD = 128
DIL_GROUPS = ((128, 1), (512, 4), (2048, 16))
DIL_HALF = 64
DIL_W = 512
D_FF = 4 * D_MODEL
IN_W = 10752
REL_BUCKETS = 32
REL_MAX_DIST = 1024

OFF_HG_Q, OFF_HG_V, OFF_HG_ZF, OFF_HG_ZB, OFF_HG_GATE = 0, 768, 1536, 2304, 3072
OFF_RET_Q, OFF_RET_K, OFF_RET_V, OFF_RET_GATE = 3840, 4224, 4608, 5376
OFF_DIL = 6144

N_CHIPS = 4
N_DEV = 8
IN_SHARD = IN_W // N_CHIPS
FF_SHARD = D_FF // N_CHIPS

ADAM_LR, ADAM_B1, ADAM_B2, ADAM_EPS, ADAM_WD, ADAM_STEP = 0.001, 0.9, 0.999, 1e-08, 0.01, 10

VMEM_LIMIT = 56 * 1024 * 1024
HG_T = 512
HG_C = 64
RET_T = 256
DIL_TQ = 256
NEG = -1e30

NN = (((1,), (0,)), ((), ()))
NT = (((1,), (1,)), ((), ()))
TN = (((0,), (0,)), ((), ()))
MESH = pl.DeviceIdType.MESH
ANY = pl.BlockSpec(memory_space=pl.ANY)


def _cp(*sem):
    return pltpu.CompilerParams(dimension_semantics=sem, vmem_limit_bytes=VMEM_LIMIT)


def _mxu(a, b, dn):
    return lax.dot_general(a.astype(BF16), b.astype(BF16), dn, preferred_element_type=F32)


@jax.custom_vjp
def dot_nn(a, b):
    return _mxu(a, b, NN)


dot_nn.defvjp(lambda a, b: (_mxu(a, b, NN), (a, b)),
              lambda r, g: (_mxu(g, r[1], NT), _mxu(r[0], g, TN)))


@jax.custom_vjp
def dot_nt(a, b):
    return _mxu(a, b, NT)


dot_nt.defvjp(lambda a, b: (_mxu(a, b, NT), (a, b)),
              lambda r, g: (_mxu(g, r[1], NN), _mxu(g, r[0], TN)))


@jax.custom_vjp
def dot_tn(a, b):
    return _mxu(a, b, TN)


dot_tn.defvjp(lambda a, b: (_mxu(a, b, TN), (a, b)),
              lambda r, g: (_mxu(r[1], g, NT), _mxu(r[0], g, NN)))


def _split3(v):
    hi = v.astype(BF16)
    r1 = v - hi.astype(F32)
    mid = r1.astype(BF16)
    lo = (r1 - mid.astype(F32)).astype(BF16)
    return hi, mid, lo


def _exact_mask_dot(m, v, dn):
    mb = m.astype(BF16)
    hi, mid, lo = _split3(v)
    f = lambda p: lax.dot_general(mb, p, dn, preferred_element_type=F32)
    return (f(lo) + f(mid)) + f(hi)


@jax.custom_vjp
def cumdot(m, v):
    return _exact_mask_dot(m, v, NN)


cumdot.defvjp(lambda m, v: (_exact_mask_dot(m, v, NN), m),
              lambda m, g: (jnp.zeros_like(m), _exact_mask_dot(m, g, TN)))


def _sigmoid(z):
    return 1.0 / (1.0 + jnp.exp(-z))


def _head_rms(t, g):
    return t * lax.rsqrt(jnp.mean(t * t, axis=-1, keepdims=True) + EPS) * g


class Rider:
    def __init__(self, arrays, out_shapes, sems, ops):
        self.arrays, self.out_shapes, self.sems, self.ops = list(arrays), list(out_shapes), list(sems), ops

    def start(self, ins, outs, sems):
        for cp in self.ops(ins, outs, *sems):
            cp.start()

    def finish(self, ins, outs, sems):
        cps = self.ops(ins, outs, *sems)
        for cp in cps:
            cp.wait_recv()
        for cp in cps:
            cp.wait_send()


def _hosted(name, body, rider, *, grid, in_specs, out_specs, out_shape, scratch_shapes, sem, operands):
    n_in, n_out, n_scr = len(in_specs), len(out_specs), len(scratch_shapes)
    r_in = len(rider.arrays) if rider else 0
    r_out = len(rider.out_shapes) if rider else 0
    last = tuple(g - 1 for g in grid)

    def kernel_body(*refs):
        ins, refs = refs[:n_in], refs[n_in:]
        rins, refs = refs[:r_in], refs[r_in:]
        outs, refs = refs[:n_out], refs[n_out:]
        routs, refs = refs[:r_out], refs[r_out:]
        scr, rsems = refs[:n_scr], refs[n_scr:]
        if rider:
            ids = [pl.program_id(d) for d in range(len(grid))]
            first = functools.reduce(lambda p, q: p & q, [i == 0 for i in ids])
            done = functools.reduce(lambda p, q: p & q, [i == e for i, e in zip(ids, last)])
            pl.when(first)(lambda: rider.start(rins, routs, rsems))
        body(ins, outs, scr)
        if rider:
            pl.when(done)(lambda: rider.finish(rins, routs, rsems))

    res = pl.pallas_call(
        kernel_body, name=name, grid=grid,
        in_specs=list(in_specs) + [ANY] * r_in,
        out_specs=list(out_specs) + [ANY] * r_out,
        out_shape=list(out_shape) + (rider.out_shapes if rider else []),
        scratch_shapes=list(scratch_shapes) + ([pltpu.SemaphoreType.DMA((n,)) for n in rider.sems] if rider else []),
        compiler_params=_cp(*(("arbitrary",) * len(grid) if rider else sem)),
    )(*operands, *(rider.arrays if rider else []))
    return res[:n_out], res[n_out:]


def _mm(name, a, b, *, mode, grid, a_spec, b_spec, tm, tn, extras=(), extra_specs=(), epi, out_shape, out_specs,
        rider=None):
    nk = grid[2]
    single = not isinstance(out_shape, (tuple, list))
    if single:
        out_shape, out_specs = [out_shape], [out_specs]

    def body(ins, outs, scr):
        a_ref, b_ref, ex = ins[0], ins[1], ins[2:]
        acc = scr[0]
        k = pl.program_id(2)

        @pl.when(k == 0)
        def _():
            acc[...] = jnp.zeros_like(acc)

        acc[...] += _mxu(a_ref[...], b_ref[...], {"nn": NN, "nt": NT, "tn": TN}[mode])

        @pl.when(k == nk - 1)
        def _():
            epi(acc[...], ex, outs)

    outs, carried = _hosted(name, body, rider, grid=grid, in_specs=[a_spec, b_spec, *extra_specs],
                            out_specs=out_specs, out_shape=out_shape, scratch_shapes=[pltpu.VMEM((tm, tn), F32)],
                            sem=("parallel", "parallel", "arbitrary"), operands=(a, b, *extras))
    res = outs[0] if single else tuple(outs)
    return (res, carried) if rider else res


def _epi_store(acc, ex, outs):
    outs[0][...] = acc.astype(outs[0].dtype)


def _epi_residual(acc, ex, outs):
    outs[0][...] = ex[0][...] + acc


def _epi_up(acc, ex, outs):
    outs[0][...] = acc
    outs[1][...] = jnp.square(jnp.maximum(acc, 0.0)).astype(BF16)


def _epi_dact(acc, ex, outs):
    outs[0][...] = (acc * (2.0 * jnp.maximum(ex[0][...], 0.0))).astype(BF16)


def _ij(i, j, k):
    return (i, j)


def proj_in(xn, win_g, l):
    S = xn.shape[0]
    tm, tn = 512, IN_SHARD
    return _mm("proj_in", xn, win_g, mode="nn", grid=(IN_W // tn, S // tm, 1), tm=tm, tn=tn,
               a_spec=pl.BlockSpec((tm, D_MODEL), lambda j, i, k: (i, 0)),
               b_spec=pl.BlockSpec((None, None, D_MODEL, tn), lambda j, i, k: (l, j, 0, 0)),
               epi=_epi_store, out_shape=jax.ShapeDtypeStruct((S, IN_W), F32),
               out_specs=pl.BlockSpec((tm, tn), lambda j, i, k: (i, j)))


def proj_out(y, wout_g, l, x):
    S = y.shape[0]
    tm, tn = 1024, 1024
    return _mm("proj_out", y, wout_g, mode="nn", grid=(S // tm, D_MODEL // tn, 1), tm=tm, tn=tn,
               a_spec=pl.BlockSpec((tm, D_MODEL), lambda i, j, k: (i, 0)),
               b_spec=pl.BlockSpec((None, D_MODEL, tn), lambda i, j, k: (l, 0, j)),
               extras=(x,), extra_specs=(pl.BlockSpec((tm, tn), _ij),),
               epi=_epi_residual, out_shape=jax.ShapeDtypeStruct((S, D_MODEL), F32),
               out_specs=pl.BlockSpec((tm, tn), _ij))


def proj_up(hm, wup_g, l):
    S = hm.shape[0]
    tm, tn = 1024, 1024
    return _mm("proj_up", hm, wup_g, mode="nn", grid=(S // tm, D_FF // tn, 1), tm=tm, tn=tn,
               a_spec=pl.BlockSpec((tm, D_MODEL), lambda i, j, k: (i, 0)),
               b_spec=pl.BlockSpec((None, None, D_MODEL, tn), lambda i, j, k: (l, j // 2, 0, j % 2)),
               epi=_epi_up,
               out_shape=(jax.ShapeDtypeStruct((S, D_FF), F32), jax.ShapeDtypeStruct((S, D_FF), BF16)),
               out_specs=(pl.BlockSpec((tm, tn), _ij), pl.BlockSpec((tm, tn), _ij)))


def proj_down(a, wdown_g, l, x):
    S = a.shape[0]
    tm, tn, tk = 1024, 1024, 2048
    return _mm("proj_down", a, wdown_g, mode="nn", grid=(S // tm, D_MODEL // tn, D_FF // tk), tm=tm, tn=tn,
               a_spec=pl.BlockSpec((tm, tk), lambda i, j, k: (i, k)),
               b_spec=pl.BlockSpec((None, tk, tn), lambda i, j, k: (l, k, j)),
               extras=(x,), extra_specs=(pl.BlockSpec((tm, tn), _ij),),
               epi=_epi_residual, out_shape=jax.ShapeDtypeStruct((S, D_MODEL), F32),
               out_specs=pl.BlockSpec((tm, tn), _ij))


def bwd_down_act(dx, wdown_g, l, u, rider=None):
    S = dx.shape[0]
    tm, tn = 1024, 1024
    return _mm("bwd_down_act", dx, wdown_g, mode="nt", grid=(S // tm, D_FF // tn, 1), tm=tm, tn=tn,
               a_spec=pl.BlockSpec((tm, D_MODEL), lambda i, j, k: (i, 0)),
               b_spec=pl.BlockSpec((None, tn, D_MODEL), lambda i, j, k: (l, j, 0)),
               extras=(u,), extra_specs=(pl.BlockSpec((tm, tn), _ij),),
               epi=_epi_dact, out_shape=jax.ShapeDtypeStruct((S, D_FF), BF16),
               out_specs=pl.BlockSpec((tm, tn), _ij), rider=rider)


def bwd_up(du, wup_g, l):
    S = du.shape[0]
    tm, tn, tk = 1024, 1024, FF_SHARD
    return _mm("bwd_up", du, wup_g, mode="nt", grid=(S // tm, D_MODEL // tn, D_FF // tk), tm=tm, tn=tn,
               a_spec=pl.BlockSpec((tm, tk), lambda i, j, k: (i, k)),
               b_spec=pl.BlockSpec((None, None, tn, tk), lambda i, j, k: (l, k, j, 0)),
               epi=_epi_store, out_shape=jax.ShapeDtypeStruct((S, D_MODEL), F32),
               out_specs=pl.BlockSpec((tm, tn), _ij))


def bwd_out(dx, wout_g, l):
    S = dx.shape[0]
    tm, tn = 1024, 1024
    return _mm("bwd_out", dx, wout_g, mode="nt", grid=(S // tm, D_MODEL // tn, 1), tm=tm, tn=tn,
               a_spec=pl.BlockSpec((tm, D_MODEL), lambda i, j, k: (i, 0)),
               b_spec=pl.BlockSpec((None, tn, D_MODEL), lambda i, j, k: (l, j, 0)),
               epi=_epi_store, out_shape=jax.ShapeDtypeStruct((S, D_MODEL), F32),
               out_specs=pl.BlockSpec((tm, tn), _ij))


def bwd_in(dh, win_g, l):
    S = dh.shape[0]
    tm, tn, tk = 1024, 1024, IN_SHARD
    return _mm("bwd_in", dh, win_g, mode="nt", grid=(S // tm, D_MODEL // tn, IN_W // tk), tm=tm, tn=tn,
               a_spec=pl.BlockSpec((tm, tk), lambda i, j, k: (i, k)),
               b_spec=pl.BlockSpec((None, None, tn, tk), lambda i, j, k: (l, k, j, 0)),
               epi=_epi_store, out_shape=jax.ShapeDtypeStruct((S, D_MODEL), F32),
               out_specs=pl.BlockSpec((tm, tn), _ij))


def wgrad(name, a, g, *, m, n, n_shard):
    S = a.shape[0]
    tm, tn, tk = (512, IN_SHARD, 1024) if n_shard == IN_SHARD else (1024, 1024, 1024)
    per = n_shard // tn
    if n_shard == n:
        out_shape = jax.ShapeDtypeStruct((m, n), F32)
        out_spec = pl.BlockSpec((tm, tn), _ij)
    else:
        out_shape = jax.ShapeDtypeStruct((N_CHIPS, m, n_shard), F32)
        out_spec = pl.BlockSpec((None, tm, tn), lambda i, j, k: (j // per, i, j % per))
    return _mm(name, a, g, mode="tn", grid=(m // tm, n // tn, S // tk), tm=tm, tn=tn,
               a_spec=pl.BlockSpec((tk, tm), lambda i, j, k: (k, i)),
               b_spec=pl.BlockSpec((tk, tn), lambda i, j, k: (k, j)),
               epi=_epi_store, out_shape=out_shape, out_specs=out_spec)


NORM_T = 256


def rmsnorm_fwd(x, g):
    S = x.shape[0]

    def body(x_ref, g_ref, o_ref):
        xv = x_ref[...]
        r = lax.rsqrt(jnp.mean(xv * xv, axis=-1, keepdims=True) + EPS)
        o_ref[...] = ((xv * r) * g_ref[...]).astype(BF16)

    return pl.pallas_call(
        body, name="rmsnorm_fwd", grid=(S // NORM_T,),
        in_specs=[pl.BlockSpec((NORM_T, D_MODEL), lambda i: (i, 0)), pl.BlockSpec((1, D_MODEL), lambda i: (0, 0))],
        out_specs=pl.BlockSpec((NORM_T, D_MODEL), lambda i: (i, 0)),
        out_shape=jax.ShapeDtypeStruct((S, D_MODEL), BF16), compiler_params=_cp("parallel"),
    )(x, g)


def rmsnorm_bwd(x, g, dxn, dres):
    S = x.shape[0]

    def body(x_ref, g_ref, dxn_ref, dres_ref, dx_ref, dg_ref):
        @pl.when(pl.program_id(0) == 0)
        def _():
            dg_ref[...] = jnp.zeros_like(dg_ref)

        xv, gv, d = x_ref[...], g_ref[...], dxn_ref[...]
        r = lax.rsqrt(jnp.mean(xv * xv, axis=-1, keepdims=True) + EPS)
        gd = gv * d
        dx_ref[...] = dres_ref[...] + r * gd - xv * ((r * r * r) * jnp.mean(xv * gd, axis=-1, keepdims=True))
        dg_ref[...] += jnp.sum(d * (xv * r), axis=0, keepdims=True)

    row = pl.BlockSpec((NORM_T, D_MODEL), lambda i: (i, 0))
    vec = pl.BlockSpec((1, D_MODEL), lambda i: (0, 0))
    return pl.pallas_call(
        body, name="rmsnorm_bwd", grid=(S // NORM_T,),
        in_specs=[row, vec, row, row], out_specs=(row, vec),
        out_shape=(jax.ShapeDtypeStruct((S, D_MODEL), F32), jax.ShapeDtypeStruct((1, D_MODEL), F32)),
        compiler_params=_cp("arbitrary"),
    )(x, g, dxn, dres)


def loss_head(y, tgt):
    S = y.shape[0]

    def body(y_ref, t_ref, dy_ref, l_ref):
        @pl.when(pl.program_id(0) == 0)
        def _():
            l_ref[...] = jnp.zeros_like(l_ref)

        e = y_ref[...] - t_ref[...]
        dy_ref[...] = e * (1.0 / D_MODEL)
        l_ref[...] += jnp.sum(e * e) * (0.5 / D_MODEL)

    row = pl.BlockSpec((NORM_T, D_MODEL), lambda i: (i, 0))
    return pl.pallas_call(
        body, name="loss_head", grid=(S // NORM_T,),
        in_specs=[row, row], out_specs=(row, pl.BlockSpec((1, 128), lambda i: (0, 0))),
        out_shape=(jax.ShapeDtypeStruct((S, D_MODEL), F32), jax.ShapeDtypeStruct((1, 128), F32)),
        compiler_params=_cp("arbitrary"),
    )(y, tgt)


def _hg_block(qs, vs, zs, lb, sT, reverse):
    n = len(qs)
    row = lax.broadcasted_iota(jnp.int32, (HG_C, HG_C), 0)
    col = lax.broadcasted_iota(jnp.int32, (HG_C, HG_C), 1)
    tri = (row <= col) if reverse else (row >= col)
    m = tri.astype(F32)
    rsel = lax.broadcasted_iota(jnp.int32, (HG_C, 128), 0)
    ref_rows = ((rsel >= HG_C // 2) if reverse else (rsel <= HG_C // 2)).astype(F32)
    outs = [None] * n
    for c in (range(n - 1, -1, -1) if reverse else range(n)):
        f = lb + (1.0 - lb) * _sigmoid(zs[c])
        kc = 1.0 - f
        lc = jnp.log(f)
        b = cumdot(m, lc)
        btot = jnp.sum(lc, axis=0, keepdims=True)
        bref = lax.stop_gradient(jnp.sum(lc * ref_rows, axis=0, keepdims=True))
        qe = qs[c] * jnp.exp(jnp.minimum(b - bref, 80.0))
        ke = kc * jnp.exp(jnp.minimum(bref - b, 80.0))
        att = jnp.where(tri, dot_nt(qe, ke), 0.0)
        outs[c] = dot_nn(att, vs[c]) + dot_nt(qs[c] * jnp.exp(b), sT)
        sT = sT * jnp.exp(btot) + dot_tn(vs[c], kc * jnp.exp(btot - b))
    return outs, sT


def _chunks(ref, c, n):
    return [ref[i * c:(i + 1) * c, :] for i in range(n)]


def hg_scan_fwd(h, lbf, lbb):
    S = h.shape[0]
    nb = S // HG_T
    n = HG_T // HG_C

    def body(qf, vf, zf, qb, vb, zb, lbf_ref, lbb_ref, of_ref, ob_ref, sf_ref, sb_ref, stf, stb):
        @pl.when(pl.program_id(1) == 0)
        def _():
            stf[...] = jnp.zeros_like(stf)
            stb[...] = jnp.zeros_like(stb)

        for (q, v, z, lb_ref, o_ref, s_ref, st, rev) in ((qf, vf, zf, lbf_ref, of_ref, sf_ref, stf, False),
                                                         (qb, vb, zb, lbb_ref, ob_ref, sb_ref, stb, True)):
            s_ref[0, 0] = st[...]
            outs, s_new = _hg_block(_chunks(q, HG_C, n), _chunks(v, HG_C, n), _chunks(z, HG_C, n),
                                    lb_ref[...], st[...], rev)
            for c in range(n):
                o_ref[c * HG_C:(c + 1) * HG_C, :] = outs[c]
            st[...] = s_new

    def col(off, rev):
        return pl.BlockSpec((HG_T, 128), (lambda hh, t: (nb - 1 - t, off // 128 + hh)) if rev
                            else (lambda hh, t: (t, off // 128 + hh)))

    lb_spec = pl.BlockSpec((1, 128), lambda hh, t: (0, hh))
    st_f = pl.BlockSpec((1, 1, 128, 128), lambda hh, t: (hh, t, 0, 0))
    st_b = pl.BlockSpec((1, 1, 128, 128), lambda hh, t: (hh, nb - 1 - t, 0, 0))
    return pl.pallas_call(
        body, name="hg_scan_fwd", grid=(HG_HEADS, nb),
        in_specs=[col(OFF_HG_Q, False), col(OFF_HG_V, False), col(OFF_HG_ZF, False),
                  col(OFF_HG_Q, True), col(OFF_HG_V, True), col(OFF_HG_ZB, True), lb_spec, lb_spec],
        out_specs=(col(0, False), col(0, True), st_f, st_b),
        out_shape=(jax.ShapeDtypeStruct((S, HG_W), F32), jax.ShapeDtypeStruct((S, HG_W), F32),
                   jax.ShapeDtypeStruct((HG_HEADS, nb, 128, 128), F32),
                   jax.ShapeDtypeStruct((HG_HEADS, nb, 128, 128), F32)),
        scratch_shapes=[pltpu.VMEM((128, 128), F32), pltpu.VMEM((128, 128), F32)],
        compiler_params=_cp("arbitrary", "arbitrary"),
    )(h, h, h, h, h, h, lbf, lbb)


def hg_scan_bwd(h, lbf, lbb, do, sf, sb, rider=None):
    S = h.shape[0]
    nb = S // HG_T
    n = HG_T // HG_C

    def body(ins, outs, scr):
        qf, vf, zf, dof, sfin, qb, vb, zb, dob, sbin, lbf_ref, lbb_ref = ins
        dqf, dvf, dzf, dqb, dvb, dzb, dlbf, dlbb = outs
        dsf, dsb = scr

        @pl.when(pl.program_id(1) == 0)
        def _():
            for r in (dsf, dsb, dlbf, dlbb):
                r[...] = jnp.zeros_like(r)

        for (q, v, z, dor, sin, lb_ref, dq, dv, dz, dlb, ds, rev) in (
                (qf, vf, zf, dof, sfin, lbf_ref, dqf, dvf, dzf, dlbf, dsf, False),
                (qb, vb, zb, dob, sbin, lbb_ref, dqb, dvb, dzb, dlbb, dsb, True)):
            fn = functools.partial(_hg_block, reverse=rev)
            _, vjp = jax.vjp(fn, _chunks(q, HG_C, n), _chunks(v, HG_C, n), _chunks(z, HG_C, n), lb_ref[...], sin[0, 0])
            dqs, dvs, dzs, dlb_v, ds_in = vjp((_chunks(dor, HG_C, n), ds[...]))
            for c in range(n):
                sl = slice(c * HG_C, (c + 1) * HG_C)
                dq[sl, :] = dqs[c]
                dv[sl, :] = dvs[c]
                dz[sl, :] = dzs[c]
            dlb[...] += dlb_v
            ds[...] = ds_in

    def col(off, fwd_scan):
        return pl.BlockSpec((HG_T, 128), (lambda hh, t: (nb - 1 - t, off // 128 + hh)) if fwd_scan
                            else (lambda hh, t: (t, off // 128 + hh)))

    lb_spec = pl.BlockSpec((1, 128), lambda hh, t: (0, hh))
    st_f = pl.BlockSpec((1, 1, 128, 128), lambda hh, t: (hh, nb - 1 - t, 0, 0))
    st_b = pl.BlockSpec((1, 1, 128, 128), lambda hh, t: (hh, t, 0, 0))
    full = jax.ShapeDtypeStruct((S, HG_W), F32)
    vec = jax.ShapeDtypeStruct((1, HG_W), F32)
    outs, carried = _hosted(
        "hg_scan_bwd", body, rider, grid=(HG_HEADS, nb),
        in_specs=[col(OFF_HG_Q, True), col(OFF_HG_V, True), col(OFF_HG_ZF, True), col(0, True), st_f,
                  col(OFF_HG_Q, False), col(OFF_HG_V, False), col(OFF_HG_ZB, False), col(0, False), st_b,
                  lb_spec, lb_spec],
        out_specs=[col(0, True), col(0, True), col(0, True), col(0, False), col(0, False), col(0, False),
                   lb_spec, lb_spec],
        out_shape=[full, full, full, full, full, full, vec, vec],
        scratch_shapes=[pltpu.VMEM((128, 128), F32), pltpu.VMEM((128, 128), F32)],
        sem=("arbitrary", "arbitrary"), operands=(h, h, h, do, sf, h, h, h, do, sb, lbf, lbb))
    return (tuple(outs), carried) if rider else tuple(outs)


GN_T = 1024


def _gated_norm(o, gate, g, center):
    if center:
        o = o - jnp.mean(o, axis=-1, keepdims=True)
    o = o * lax.rsqrt(jnp.mean(o * o, axis=-1, keepdims=True) + EPS)
    return (o * g) * (gate * _sigmoid(gate))


def gated_norm_fwd(name, of, ob, h, gate_off, g, center):
    S = of.shape[0]

    def body(of_ref, ob_ref, gate_ref, g_ref, y_ref):
        y_ref[...] = _gated_norm(of_ref[...] + ob_ref[...], gate_ref[...], g_ref[...], center)

    blk = pl.BlockSpec((GN_T, 128), lambda hh, i: (i, hh))
    return pl.pallas_call(
        body, name=name, grid=(6, S // GN_T),
        in_specs=[blk, blk, pl.BlockSpec((GN_T, 128), lambda hh, i: (i, gate_off // 128 + hh)),
                  pl.BlockSpec((1, 128), lambda hh, i: (0, hh))],
        out_specs=blk, out_shape=jax.ShapeDtypeStruct((S, 768), F32),
        compiler_params=_cp("parallel", "parallel"),
    )(of, ob, h, g)


def gated_norm_bwd(name, of, ob, h, gate_off, g, dy, dy_off, center):
    S = of.shape[0]

    def body(of_ref, ob_ref, gate_ref, g_ref, dy_ref, do_ref, dgate_ref, dg_ref):
        @pl.when(pl.program_id(1) == 0)
        def _():
            dg_ref[...] = jnp.zeros_like(dg_ref)

        fn = functools.partial(_gated_norm, center=center)
        _, vjp = jax.vjp(fn, of_ref[...] + ob_ref[...], gate_ref[...], g_ref[...])
        do, dgate, dg = vjp(dy_ref[...])
        do_ref[...] = do
        dgate_ref[...] = dgate
        dg_ref[...] += dg

    blk = pl.BlockSpec((GN_T, 128), lambda hh, i: (i, hh))
    vec = pl.BlockSpec((1, 128), lambda hh, i: (0, hh))
    return pl.pallas_call(
        body, name=name, grid=(6, S // GN_T),
        in_specs=[blk, blk, pl.BlockSpec((GN_T, 128), lambda hh, i: (i, gate_off // 128 + hh)), vec,
                  pl.BlockSpec((GN_T, 128), lambda hh, i: (i, dy_off // 128 + hh))],
        out_specs=(blk, blk, vec),
        out_shape=(jax.ShapeDtypeStruct((S, 768), F32), jax.ShapeDtypeStruct((S, 768), F32),
                   jax.ShapeDtypeStruct((1, 768), F32)),
        compiler_params=_cp("arbitrary", "arbitrary"),
    )(of, ob, h, g, dy)


def _ret_consts(S):
    half = RET_DK // 2
    inv = ROPE_BASE ** (-jnp.arange(half, dtype=F32) / half)
    ang = jnp.arange(S, dtype=F32)[:, None] * inv[None, :]
    cos, sin = jnp.cos(ang), jnp.sin(ang)
    cos_t = jnp.tile(jnp.concatenate([cos, cos], axis=1), (1, RET_HEADS))
    sin_t = jnp.tile(jnp.concatenate([-sin, sin], axis=1), (1, RET_HEADS))
    hidx = jnp.arange(RET_HEADS, dtype=F32)
    lg_f = jnp.log1p(-jnp.exp2(-5.0 - hidx))
    C = RET_CHUNK
    idx = jnp.arange(C, dtype=F32)
    rel = idx[:, None] - idx[None, :]

    def one(lg, reverse):
        lgc = lg[:, None]
        decay = jnp.where(rel >= 0, jnp.exp(lgc[:, :, None] * jnp.maximum(rel, 0.0)), 0.0)
        zeta = jnp.exp(lgc * (C - 1 - idx))
        xi = jnp.exp(lgc * (idx + 1))
        if reverse:
            decay = decay[:, ::-1, ::-1]
            zeta, xi = zeta[:, ::-1], xi[:, ::-1]
        wide = lambda t: jnp.repeat(t.T, RET_DK, axis=1)
        gam_w = jnp.broadcast_to(jnp.repeat(jnp.exp(lg * C), 128)[None, :], (8, RET_W))
        return decay, wide(xi), wide(zeta), gam_w

    hm = (jnp.arange(RET_QK_W)[None, :] // RET_DK == jnp.arange(8)[:, None]).astype(F32)
    return (cos_t, sin_t, hm) + one(lg_f, False) + one(lg_f[::-1], True)


def _rope(t, cos, sin_signed):
    lane = lax.broadcasted_iota(jnp.int32, t.shape, 1)
    first = (lane & (RET_DK - 1)) < RET_DK // 2
    partner = jnp.where(first, pltpu.roll(t, RET_QK_W - RET_DK // 2, 1), pltpu.roll(t, RET_DK // 2, 1))
    return t * cos + partner * sin_signed


def _ret_block(qs, ks, vs, st, dec, xi, zeta, gam, hms, reverse):
    n = len(qs)
    outs = [None] * n
    st = list(st)
    for c in (range(n - 1, -1, -1) if reverse else range(n)):
        qx = qs[c] * xi
        kz = ks[c] * zeta
        row = []
        for hh in range(RET_HEADS):
            sc = dot_nt(qs[c] * hms[hh], ks[c]) * dec[hh]
            row.append(dot_nn(sc, vs[c][hh]) + dot_nn(qx, st[hh]))
            st[hh] = st[hh] * gam[hh] + dot_tn(kz * hms[hh], vs[c][hh])
        outs[c] = row
    return outs, st


def _ret_inputs(q_ref, k_ref, v_ref, cos_ref, sin_ref):
    n = RET_T // RET_CHUNK
    qr = _rope(q_ref[...], cos_ref[...], sin_ref[...])
    kr = _rope(k_ref[...], cos_ref[...], sin_ref[...]) * (RET_DK ** -0.5)
    qs = [qr[c * RET_CHUNK:(c + 1) * RET_CHUNK] for c in range(n)]
    ks = [kr[c * RET_CHUNK:(c + 1) * RET_CHUNK] for c in range(n)]
    vs = [[v_ref[c * RET_CHUNK:(c + 1) * RET_CHUNK, hh * 128:(hh + 1) * 128] for hh in range(RET_HEADS)]
          for c in range(n)]
    return qs, ks, vs


def _ret_dir_consts(dec_ref, xi_ref, zeta_ref, gam_ref, hm_ref):
    dec = [dec_ref[hh] for hh in range(RET_HEADS)]
    gam = [gam_ref[0:1, hh * 128:(hh + 1) * 128] for hh in range(RET_HEADS)]
    hms = [hm_ref[hh:hh + 1, :] for hh in range(RET_HEADS)]
    return dec, xi_ref[...], zeta_ref[...], gam, hms


def _ret_rows(nb, rev):
    def rows(width, colblk):
        return pl.BlockSpec((RET_T, width), (lambda t: (nb - 1 - t, colblk)) if rev else (lambda t: (t, colblk)))
    return rows


def _const_spec(shape):
    nd = len(shape)
    return pl.BlockSpec(shape, lambda t: (0,) * nd)


def ret_scan_fwd(h, consts):
    S = h.shape[0]
    nb = S // RET_T
    n = RET_T // RET_CHUNK
    cos_t, sin_t, hm, dec_f, xi_f, zeta_f, gam_f, dec_b, xi_b, zeta_b, gam_b = consts

    def body(qf, kf, vf, cf, sf, qb, kb, vb, cb, sb_, hm_ref, decf, xif, zetaf, gamf, decb, xib, zetab, gamb,
             of_ref, ob_ref, sfo, sbo, stf, stb):
        @pl.when(pl.program_id(0) == 0)
        def _():
            stf[...] = jnp.zeros_like(stf)
            stb[...] = jnp.zeros_like(stb)

        for (q, k, v, cs, sn, dr, xr, zr, gr, o_ref, so, st, rev) in (
                (qf, kf, vf, cf, sf, decf, xif, zetaf, gamf, of_ref, sfo, stf, False),
                (qb, kb, vb, cb, sb_, decb, xib, zetab, gamb, ob_ref, sbo, stb, True)):
            so[0] = st[...]
            qs, ks, vs = _ret_inputs(q, k, v, cs, sn)
            dec, xi, zeta, gam, hms = _ret_dir_consts(dr, xr, zr, gr, hm_ref)
            st_in = [st[:, hh * 128:(hh + 1) * 128] for hh in range(RET_HEADS)]
            outs, st_new = _ret_block(qs, ks, vs, st_in, dec, xi, zeta, gam, hms, rev)
            for c in range(n):
                for hh in range(RET_HEADS):
                    o_ref[c * RET_CHUNK:(c + 1) * RET_CHUNK, hh * 128:(hh + 1) * 128] = outs[c][hh]
            for hh in range(RET_HEADS):
                st[:, hh * 128:(hh + 1) * 128] = st_new[hh]

    rf, rb = _ret_rows(nb, False), _ret_rows(nb, True)
    cspecs = [_const_spec(a.shape) for a in (hm, dec_f, xi_f, zeta_f, gam_f, dec_b, xi_b, zeta_b, gam_b)]
    st_shape = jax.ShapeDtypeStruct((nb, RET_QK_W, RET_W), F32)
    qc, kc, vc = OFF_RET_Q // RET_QK_W, OFF_RET_K // RET_QK_W, OFF_RET_V // RET_W
    return pl.pallas_call(
        body, name="ret_scan_fwd", grid=(nb,),
        in_specs=[rf(RET_QK_W, qc), rf(RET_QK_W, kc), rf(RET_W, vc), rf(RET_QK_W, 0), rf(RET_QK_W, 0),
                  rb(RET_QK_W, qc), rb(RET_QK_W, kc), rb(RET_W, vc), rb(RET_QK_W, 0), rb(RET_QK_W, 0)] + cspecs,
        out_specs=(rf(RET_W, 0), rb(RET_W, 0),
                   pl.BlockSpec((1, RET_QK_W, RET_W), lambda t: (t, 0, 0)),
                   pl.BlockSpec((1, RET_QK_W, RET_W), lambda t: (nb - 1 - t, 0, 0))),
        out_shape=(jax.ShapeDtypeStruct((S, RET_W), F32), jax.ShapeDtypeStruct((S, RET_W), F32), st_shape, st_shape),
        scratch_shapes=[pltpu.VMEM((RET_QK_W, RET_W), F32), pltpu.VMEM((RET_QK_W, RET_W), F32)],
        compiler_params=_cp("arbitrary"),
    )(h, h, h, cos_t, sin_t, h, h, h, cos_t, sin_t, hm, dec_f, xi_f, zeta_f, gam_f, dec_b, xi_b, zeta_b, gam_b)


def ret_scan_bwd(h, consts, do, sf, sb):
    S = h.shape[0]
    nb = S // RET_T
    n = RET_T // RET_CHUNK
    cos_t, sin_t, hm, dec_f, xi_f, zeta_f, gam_f, dec_b, xi_b, zeta_b, gam_b = consts

    def body(qf, kf, vf, cf, sf_, dof, sfin, qb, kb, vb, cb, sb_, dob, sbin,
             hm_ref, decf, xif, zetaf, gamf, decb, xib, zetab, gamb,
             dqf, dkf, dvf, dqb, dkb, dvb, dsf, dsb):
        @pl.when(pl.program_id(0) == 0)
        def _():
            dsf[...] = jnp.zeros_like(dsf)
            dsb[...] = jnp.zeros_like(dsb)

        for (q, k, v, cs, sn, dor, sin, dr, xr, zr, gr, dq, dk, dv, ds, rev) in (
                (qf, kf, vf, cf, sf_, dof, sfin, decf, xif, zetaf, gamf, dqf, dkf, dvf, dsf, False),
                (qb, kb, vb, cb, sb_, dob, sbin, decb, xib, zetab, gamb, dqb, dkb, dvb, dsb, True)):
            qs, ks, vs = _ret_inputs(q, k, v, cs, sn)
            dec, xi, zeta, gam, hms = _ret_dir_consts(dr, xr, zr, gr, hm_ref)
            st_in = [sin[0, :, hh * 128:(hh + 1) * 128] for hh in range(RET_HEADS)]
            fn = lambda a, b_, c_, d_: _ret_block(a, b_, c_, d_, dec, xi, zeta, gam, hms, rev)
            _, vjp = jax.vjp(fn, qs, ks, vs, st_in)
            dos = [[dor[c * RET_CHUNK:(c + 1) * RET_CHUNK, hh * 128:(hh + 1) * 128] for hh in range(RET_HEADS)]
                   for c in range(n)]
            dst = [ds[:, hh * 128:(hh + 1) * 128] for hh in range(RET_HEADS)]
            dqs, dks, dvs, dst_in = vjp((dos, dst))
            cosv, sinv = cs[...], sn[...]
            dq[...] = _rope(jnp.concatenate(dqs, axis=0), cosv, -sinv)
            dk[...] = _rope(jnp.concatenate(dks, axis=0) * (RET_DK ** -0.5), cosv, -sinv)
            for c in range(n):
                for hh in range(RET_HEADS):
                    dv[c * RET_CHUNK:(c + 1) * RET_CHUNK, hh * 128:(hh + 1) * 128] = dvs[c][hh]
            for hh in range(RET_HEADS):
                ds[:, hh * 128:(hh + 1) * 128] = dst_in[hh]

    rf, rb = _ret_rows(nb, True), _ret_rows(nb, False)
    cspecs = [_const_spec(a.shape) for a in (hm, dec_f, xi_f, zeta_f, gam_f, dec_b, xi_b, zeta_b, gam_b)]
    qk = jax.ShapeDtypeStruct((S, RET_QK_W), F32)
    vv = jax.ShapeDtypeStruct((S, RET_W), F32)
    qc, kc, vc = OFF_RET_Q // RET_QK_W, OFF_RET_K // RET_QK_W, OFF_RET_V // RET_W
    return pl.pallas_call(
        body, name="ret_scan_bwd", grid=(nb,),
        in_specs=[rf(RET_QK_W, qc), rf(RET_QK_W, kc), rf(RET_W, vc), rf(RET_QK_W, 0), rf(RET_QK_W, 0), rf(RET_W, 0),
                  pl.BlockSpec((1, RET_QK_W, RET_W), lambda t: (nb - 1 - t, 0, 0)),
                  rb(RET_QK_W, qc), rb(RET_QK_W, kc), rb(RET_W, vc), rb(RET_QK_W, 0), rb(RET_QK_W, 0), rb(RET_W, 0),
                  pl.BlockSpec((1, RET_QK_W, RET_W), lambda t: (t, 0, 0))] + cspecs,
        out_specs=(rf(RET_QK_W, 0), rf(RET_QK_W, 0), rf(RET_W, 0), rb(RET_QK_W, 0), rb(RET_QK_W, 0), rb(RET_W, 0)),
        out_shape=(qk, qk, vv, qk, qk, vv),
        scratch_shapes=[pltpu.VMEM((RET_QK_W, RET_W), F32), pltpu.VMEM((RET_QK_W, RET_W), F32)],
        compiler_params=_cp("arbitrary"),
    )(h, h, h, cos_t, sin_t, do, sf, h, h, h, cos_t, sin_t, do, sb,
      hm, dec_f, xi_f, zeta_f, gam_f, dec_b, xi_b, zeta_b, gam_b)


def _t5_bucket(rel):
    nb = REL_BUCKETS // 2
    max_exact = nb // 2
    sign_off = jnp.where(rel > 0, nb, 0)
    n = jnp.abs(rel)
    nf = jnp.maximum(n, 1).astype(F32)
    large = max_exact + (jnp.log(nf / max_exact) / math.log(REL_MAX_DIST / max_exact)
                         * (nb - max_exact)).astype(jnp.int32)
    large = jnp.minimum(large, nb - 1)
    return sign_off + jnp.where(n < max_exact, n, large)


def _dil_buckets(dil):
    tq, tb = DIL_TQ, DIL_TQ + 2 * DIL_HALF
    rel_q = jnp.arange(tb)[None, :] - DIL_HALF - jnp.arange(tq)[:, None]
    rel_k = jnp.arange(tq)[None, :] + DIL_HALF - jnp.arange(tb)[:, None]
    return _t5_bucket(rel_q * dil), _t5_bucket(rel_k * dil)


def dil_view(h, g, dil):
    base = OFF_DIL + 3 * g * DIL_W
    if dil == 1:
        return h, IN_W, base
    return h[:, base:base + 3 * DIL_W].reshape(h.shape[0] // dil, dil * 3 * DIL_W), 3 * DIL_W, 0


def _dil_col(view, j):
    _, width, base = view
    return lambda r, s: (r * width + base + j * DIL_W) // 128 + s


def _dil_specs(L):
    nq = DIL_TQ // DIL_HALF
    last = L // DIL_HALF - 1

    def cur(colfn):
        return pl.BlockSpec((DIL_TQ, 128), lambda s, r, n: (n, colfn(r, s)))

    def prev(colfn):
        return pl.BlockSpec((DIL_HALF, 128), lambda s, r, n: (jnp.maximum(n * nq - 1, 0), colfn(r, s)))

    def nxt(colfn):
        return pl.BlockSpec((DIL_HALF, 128), lambda s, r, n: (jnp.minimum((n + 1) * nq, last), colfn(r, s)))

    return prev, cur, nxt


def dil_attn_fwd(view, S, g, dil, bias, qg, kg):
    L = S // dil
    hv = view[0]
    tb = DIL_TQ + 2 * DIL_HALF

    def body(q_ref, kp, kc, kn, vp, vc, vn, bias_ref, qg_ref, kg_ref, o_ref, lse_ref):
        n = pl.program_id(2)
        q = _head_rms(q_ref[...], qg_ref[...]) * (DIL_HD ** -0.5)
        kb = _head_rms(jnp.concatenate([kp[...], kc[...], kn[...]], axis=0), kg_ref[...])
        vb = jnp.concatenate([vp[...], vc[...], vn[...]], axis=0)
        s = _mxu(q, kb, NT) + bias_ref[0]
        ii = lax.broadcasted_iota(jnp.int32, (DIL_TQ, tb), 0)
        jj = lax.broadcasted_iota(jnp.int32, (DIL_TQ, tb), 1)
        kabs = n * DIL_TQ - DIL_HALF + jj
        valid = (jnp.abs(jj - DIL_HALF - ii) <= DIL_HALF) & (kabs >= 0) & (kabs < L)
        s = jnp.where(valid, s, NEG)
        m = jnp.max(s, axis=-1, keepdims=True)
        p = jnp.exp(s - m)
        den = jnp.sum(p, axis=-1, keepdims=True)
        o_ref[...] = _mxu(p, vb, NN) / den
        lse_ref[...] = jnp.broadcast_to(m + jnp.log(den), (DIL_TQ, 128))

    prev, cur, nxt = _dil_specs(L)
    qc, kc_, vc_ = (_dil_col(view, j) for j in range(3))
    oc = lambda r, s: r * DIL_SLOTS + s
    vec = pl.BlockSpec((1, 128), lambda s, r, n: (0, 0))
    out = jax.ShapeDtypeStruct((L, dil * DIL_W), F32)
    o, lse = pl.pallas_call(
        body, name=f"dil_attn_fwd{g}", grid=(DIL_SLOTS, dil, L // DIL_TQ),
        in_specs=[cur(qc), prev(kc_), cur(kc_), nxt(kc_), prev(vc_), cur(vc_), nxt(vc_),
                  pl.BlockSpec((1, DIL_TQ, tb), lambda s, r, n: (s, 0, 0)), vec, vec],
        out_specs=(cur(oc), cur(oc)), out_shape=(out, out),
        compiler_params=_cp("parallel", "parallel", "parallel"),
    )(hv, hv, hv, hv, hv, hv, hv, bias, qg, kg)
    return o.reshape(S, DIL_W), lse.reshape(S, DIL_W)


def dil_combine(os_, lses):
    S = os_[0].shape[0]

    def body(o1, o2, o3, l1, l2, l3, y_ref, lt_ref):
        a, b, c = l1[...], l2[...], l3[...]
        m = jnp.maximum(jnp.maximum(a, b), c)
        ea, eb, ec = jnp.exp(a - m), jnp.exp(b - m), jnp.exp(c - m)
        den = ea + eb + ec
        y_ref[...] = (ea * o1[...] + eb * o2[...] + ec * o3[...]) / den
        lt_ref[...] = m + jnp.log(den)

    blk = pl.BlockSpec((GN_T, DIL_W), lambda i: (i, 0))
    out = jax.ShapeDtypeStruct((S, DIL_W), F32)
    return pl.pallas_call(
        body, name="dil_combine", grid=(S // GN_T,), in_specs=[blk] * 6, out_specs=(blk, blk),
        out_shape=(out, out), compiler_params=_cp("parallel"),
    )(*os_, *lses)


def dil_delta(dy, yc):
    S = yc.shape[0]

    def body(dy_ref, y_ref, d_ref):
        d_ref[...] = jnp.broadcast_to(jnp.sum(dy_ref[...] * y_ref[...], axis=-1, keepdims=True), (GN_T, 128))

    return pl.pallas_call(
        body, name="dil_delta", grid=(S // GN_T, DIL_SLOTS),
        in_specs=[pl.BlockSpec((GN_T, 128), lambda i, s: (i, (HG_W + RET_W) // 128 + s)),
                  pl.BlockSpec((GN_T, 128), lambda i, s: (i, s))],
        out_specs=pl.BlockSpec((GN_T, 128), lambda i, s: (i, s)),
        out_shape=jax.ShapeDtypeStruct((S, DIL_W), F32), compiler_params=_cp("parallel", "parallel"),
    )(dy, yc)


def dil_attn_bwd(view, S, g, dil, bias_q, bias_k, qg, kg, dy, lse_t, delta):
    L = S // dil
    hv = view[0]
    if dil == 1:
        dyv, dyc = dy, lambda r, s: (HG_W + RET_W) // 128 + s
    else:
        dyv, dyc = dy[:, HG_W + RET_W:].reshape(L, dil * DIL_W), lambda r, s: r * DIL_SLOTS + s
    lv = lse_t.reshape(L, dil * DIL_W)
    dv_ = delta.reshape(L, dil * DIL_W)
    tq, tb = DIL_TQ, DIL_TQ + 2 * DIL_HALF
    scale = DIL_HD ** -0.5

    def body(qp, qc, qn, kp, kc, kn, vp, vc, vn, dp_, dc, dn, lp, lc, ln, ep, ec, en, bq_ref, bk_ref, qg_ref, kg_ref,
             dq_ref, dk_ref, dv_ref, dbias_ref, dqg_ref, dkg_ref):
        r, n = pl.program_id(1), pl.program_id(2)

        @pl.when((r == 0) & (n == 0))
        def _():
            for ref in (dbias_ref, dqg_ref, dkg_ref):
                ref[...] = jnp.zeros_like(ref)

        cat = lambda a, b, c: jnp.concatenate([a[...], b[...], c[...]], axis=0)
        qgv, kgv = qg_ref[...], kg_ref[...]
        qfn = lambda t, gg: _head_rms(t, gg) * scale
        qn_c, q_vjp = jax.vjp(qfn, qc[...], qgv)
        k_band = _head_rms(cat(kp, kc, kn), kgv)
        v_band = cat(vp, vc, vn)
        ii = lax.broadcasted_iota(jnp.int32, (tq, tb), 0)
        jj = lax.broadcasted_iota(jnp.int32, (tq, tb), 1)
        kabs = n * tq - DIL_HALF + jj
        valid = (jnp.abs(jj - DIL_HALF - ii) <= DIL_HALF) & (kabs >= 0) & (kabs < L)
        s = _mxu(qn_c, k_band, NT) + bq_ref[0]
        p = jnp.where(valid, jnp.exp(jnp.where(valid, s, NEG) - lc[:, 0:1]), 0.0)
        ds = p * (_mxu(dc[...], v_band, NT) - ec[:, 0:1])
        dbias_ref[0] += ds
        dq, dqg = q_vjp(_mxu(ds, k_band, NN))
        dq_ref[...] = dq
        dqg_ref[0] += dqg
        kn_c, k_vjp = jax.vjp(_head_rms, kc[...], kgv)
        q_band = qfn(cat(qp, qc, qn), qgv)
        do_band = cat(dp_, dc, dn)
        i2 = lax.broadcasted_iota(jnp.int32, (tb, tq), 0)
        j2 = lax.broadcasted_iota(jnp.int32, (tb, tq), 1)
        qabs = n * tq - DIL_HALF + i2
        valid2 = (jnp.abs(j2 + DIL_HALF - i2) <= DIL_HALF) & (qabs >= 0) & (qabs < L)
        s2 = _mxu(q_band, kn_c, NT) + bk_ref[0]
        lse_band = cat(lp, lc, ln)[:, 0:1]
        p2 = jnp.where(valid2, jnp.exp(jnp.where(valid2, s2, NEG) - lse_band), 0.0)
        dv_ref[...] = _mxu(p2, do_band, TN)
        ds2 = p2 * (_mxu(do_band, vc[...], NT) - cat(ep, ec, en)[:, 0:1])
        dk, dkg = k_vjp(_mxu(ds2, q_band, TN))
        dk_ref[...] = dk
        dkg_ref[0] += dkg

    prev, cur, nxt = _dil_specs(L)
    three = lambda colfn: [prev(colfn), cur(colfn), nxt(colfn)]
    qc_, kc_, vc_ = (_dil_col(view, j) for j in range(3))
    oc = lambda r, s: r * DIL_SLOTS + s
    vec = pl.BlockSpec((1, 128), lambda s, r, n: (0, 0))
    acc_vec = pl.BlockSpec((1, 1, 128), lambda s, r, n: (s, 0, 0))
    out = jax.ShapeDtypeStruct((L, dil * DIL_W), F32)
    dq, dk, dv, dbias, dqg, dkg = pl.pallas_call(
        body, name=f"dil_attn_bwd{g}", grid=(DIL_SLOTS, dil, L // tq),
        in_specs=three(qc_) + three(kc_) + three(vc_) + three(dyc) + three(oc) + three(oc)
        + [pl.BlockSpec((1, tq, tb), lambda s, r, n: (s, 0, 0)), pl.BlockSpec((1, tb, tq), lambda s, r, n: (s, 0, 0)),
           vec, vec],
        out_specs=(cur(oc), cur(oc), cur(oc), pl.BlockSpec((1, tq, tb), lambda s, r, n: (s, 0, 0)), acc_vec, acc_vec),
        out_shape=(out, out, out, jax.ShapeDtypeStruct((DIL_SLOTS, tq, tb), F32),
                   jax.ShapeDtypeStruct((DIL_SLOTS, 1, 128), F32), jax.ShapeDtypeStruct((DIL_SLOTS, 1, 128), F32)),
        compiler_params=_cp("arbitrary", "arbitrary", "arbitrary"),
    )(hv, hv, hv, hv, hv, hv, hv, hv, hv, dyv, dyv, dyv, lv, lv, lv, dv_, dv_, dv_, bias_q, bias_k, qg, kg)
    return dq.reshape(S, DIL_W), dk.reshape(S, DIL_W), dv.reshape(S, DIL_W), dbias, dqg, dkg


def _lb_eff(p):
    a = jnp.cumsum(jax.nn.softmax(p.astype(F32), axis=0), axis=0)
    return a - a[0:1]


def _dil_bias(rel_bias, g, dil):
    tbl = rel_bias[:, g * DIL_SLOTS:(g + 1) * DIL_SLOTS]
    return tuple(jnp.einsum("ijb,bs->sij", jax.nn.one_hot(b, REL_BUCKETS, dtype=F32), tbl,
                            precision=lax.Precision.HIGHEST) for b in _dil_buckets(dil))


def _layer_fwd(x, l, prm, wts, rc, biases):
    win_g, wout_g, wup_g, wdown_g = wts
    row = lambda a: a[l][None]
    xn = rmsnorm_fwd(x, row(prm["norm_mix"]))
    h = proj_in(xn, win_g, l)
    hof, hob, hsf, hsb = hg_scan_fwd(h, row(prm["lbf"]), row(prm["lbb"]))
    ya = gated_norm_fwd("hg_out", hof, hob, h, OFF_HG_GATE, row(prm["hg_norm"]), False)
    rof, rob, rsf, rsb = ret_scan_fwd(h, rc)
    yb = gated_norm_fwd("ret_out", rof, rob, h, OFF_RET_GATE, row(prm["ret_norm"]), True)
    os_, lses, views = [], [], []
    for g, (_, dil) in enumerate(DIL_GROUPS):
        views.append(dil_view(h, g, dil))
        o, lse = dil_attn_fwd(views[g], h.shape[0], g, dil, biases[g][0], row(prm["q_norm"]), row(prm["k_norm"]))
        os_.append(o)
        lses.append(lse)
    yc, lse_t = dil_combine(os_, lses)
    y = jnp.concatenate([ya, yb, yc], axis=1).astype(BF16)
    x2 = proj_out(y, wout_g, l, x)
    hm = rmsnorm_fwd(x2, row(prm["norm_mlp"]))
    u, act = proj_up(hm, wup_g, l)
    x3 = proj_down(act, wdown_g, l, x2)
    saved = dict(x=x, xn=xn, h=h, hof=hof, hob=hob, hsf=hsf, hsb=hsb, rof=rof, rob=rob, rsf=rsf, rsb=rsb,
                 yc=yc, lse_t=lse_t, y=y, x2=x2, hm=hm, u=u, act=act, views=views)
    return x3, saved


def _layer_bwd(dx3, l, prm, wts, rc, biases, sv, reducer=None):
    win_g, wout_g, wup_g, wdown_g = wts
    row = lambda a: a[l][None]
    h = sv["h"]
    rider = reducer.pair_rider() if reducer else None
    du = bwd_down_act(dx3, wdown_g, l, sv["u"], rider)
    if rider:
        du, got = du
        reducer.pair_done(got)
    g_down = wgrad("wgrad_down", sv["act"], dx3, m=D_FF, n=D_MODEL, n_shard=D_MODEL)
    dhm = bwd_up(du, wup_g, l)
    g_up = wgrad("wgrad_up", sv["hm"], du, m=D_MODEL, n=D_FF, n_shard=FF_SHARD)
    dx2, dg_mlp = rmsnorm_bwd(sv["x2"], row(prm["norm_mlp"]), dhm, dx3)
    dy = bwd_out(dx2, wout_g, l)
    g_out = wgrad("wgrad_out", sv["y"], dx2, m=D_MODEL, n=D_MODEL, n_shard=D_MODEL)
    hdo, hdgate, dg_hg = gated_norm_bwd("hg_out_bwd", sv["hof"], sv["hob"], h, OFF_HG_GATE, row(prm["hg_norm"]),
                                        dy, 0, False)
    rider = reducer.chip_rider() if reducer else None
    hg_grads = hg_scan_bwd(h, row(prm["lbf"]), row(prm["lbb"]), hdo, sv["hsf"], sv["hsb"], rider)
    if rider:
        hg_grads, arrived = hg_grads
        reducer.chip_done(arrived)
    hdqf, hdvf, hdzf, hdqb, hdvb, hdzb, dlbf, dlbb = hg_grads
    rdo, rdgate, dg_ret = gated_norm_bwd("ret_out_bwd", sv["rof"], sv["rob"], h, OFF_RET_GATE, row(prm["ret_norm"]),
                                         dy, HG_W, True)
    rdqf, rdkf, rdvf, rdqb, rdkb, rdvb = ret_scan_bwd(h, rc, rdo, sv["rsf"], sv["rsb"])
    delta = dil_delta(dy, sv["yc"])
    dil_parts, dbiases = [], []
    dqg = jnp.zeros((1, DIL_HD), F32)
    dkg = jnp.zeros((1, DIL_HD), F32)
    for g, (_, dil) in enumerate(DIL_GROUPS):
        dq, dk, dv, dbias, dqg_g, dkg_g = dil_attn_bwd(sv["views"][g], h.shape[0], g, dil, biases[g][0], biases[g][1],
                                                       row(prm["q_norm"]), row(prm["k_norm"]), dy, sv["lse_t"], delta)
        dil_parts += [dq, dk, dv]
        dbiases.append(dbias)
        dqg = dqg + jnp.sum(dqg_g, axis=0)
        dkg = dkg + jnp.sum(dkg_g, axis=0)
    dh = jnp.concatenate([hdqf + hdqb, hdvf + hdvb, hdzf, hdzb, hdgate,
                          rdqf + rdqb, rdkf + rdkb, rdvf + rdvb, rdgate] + dil_parts, axis=1).astype(BF16)
    dxn = bwd_in(dh, win_g, l)
    g_in = wgrad("wgrad_in", sv["xn"], dh, m=D_MODEL, n=IN_W, n_shard=IN_SHARD)
    dx, dg_mix = rmsnorm_bwd(sv["x"], row(prm["norm_mix"]), dxn, dx2)
    big = (g_in, g_out.reshape(N_CHIPS, D_MODEL // N_CHIPS, D_MODEL), g_up,
           g_down.reshape(N_CHIPS, D_FF // N_CHIPS, D_MODEL))
    small = dict(norm_mix=dg_mix, norm_mlp=dg_mlp, lbf=dlbf, lbb=dlbb, hg_norm=dg_hg, ret_norm=dg_ret,
                 q_norm=dqg, k_norm=dkg)
    if reducer:
        reducer.push(l, big)
    return dx, big, small, dbiases


def _rel_bias_grad(dbias_layers):
    cols = []
    for g, (_, dil) in enumerate(DIL_GROUPS):
        bq, _ = _dil_buckets(dil)
        onehot = jax.nn.one_hot(bq, REL_BUCKETS, dtype=F32)
        tot = dbias_layers[0][g]
        for d in dbias_layers[1:]:
            tot = tot + d[g]
        cols.append(jnp.einsum("sij,ijb->bs", tot, onehot, precision=lax.Precision.HIGHEST))
    return jnp.concatenate(cols, axis=1)


def local_step(x, tgt, wts, prm_in, reducer=None):
    S = x.shape[0]
    prm = dict(prm_in)
    prm["lbf"], lbf_vjp = jax.vjp(_lb_eff, prm_in["hg_lb_fwd"])
    prm["lbb"], lbb_vjp = jax.vjp(_lb_eff, prm_in["hg_lb_bwd"])
    rc = _ret_consts(S)
    biases = [_dil_bias(prm["rel_bias"], g, dil) for g, (_, dil) in enumerate(DIL_GROUPS)]
    saved = []
    for l in range(DEPTH):
        x, sv = _layer_fwd(x, l, prm, wts, rc, biases)
        saved.append(sv)
    dx, loss_row = loss_head(x, tgt)
    big, small, dbias_layers = [None] * DEPTH, [None] * DEPTH, [None] * DEPTH
    for l in range(DEPTH - 1, -1, -1):
        dx, big[l], small[l], dbias_layers[l] = _layer_bwd(dx, l, prm, wts, rc, biases, saved[l], reducer)
    sg = {k: jnp.concatenate([small[l][k] for l in range(DEPTH)], axis=0) for k in small[0]}
    sg["rel_bias"] = _rel_bias_grad(dbias_layers)
    return loss_row[0, 0], dx, (reducer.finish() if reducer else big), sg, (lbf_vjp, lbb_vjp)


def _place():
    x, y, c = lax.axis_index("x"), lax.axis_index("y"), lax.axis_index("c")
    rels = [(1 - x, y), (x, 1 - y), (1 - x, 1 - y)]
    return x, y, c, 2 * x + y, rels


def _half(c, rows):
    return pl.ds(pl.multiple_of(c * (rows // 2), 16), rows // 2)


def place_own(name, p_arr, w):
    depth, rows, cols = w.shape
    tr = 512

    def body(p_ref, w_ref, o_ref):
        o_ref[0, 0] = w_ref[0].astype(BF16)

    return pl.pallas_call(
        body, name=name,
        grid_spec=pltpu.PrefetchScalarGridSpec(
            num_scalar_prefetch=1, grid=(depth, rows // tr),
            in_specs=[pl.BlockSpec((1, tr, cols), lambda l, i, p: (l, i, 0))],
            out_specs=pl.BlockSpec((1, 1, tr, cols), lambda l, i, p: (l, p[0], i, 0))),
        out_shape=jax.ShapeDtypeStruct((depth, N_CHIPS, rows, cols), BF16),
        compiler_params=_cp("parallel", "parallel"),
    )(p_arr, w)


def allgather_weights(gs):
    nt = len(gs)
    n_ici = nt * DEPTH * 3

    def body(*refs):
        bufs = refs[nt:2 * nt]
        isend, irecv, dsend, drecv = refs[2 * nt:]
        x, y, c, p, rels = _place()
        sib = (x, y, 1 - c)
        sends, passes = [], []
        order = [(l, t) for l in range(DEPTH) for t in range(nt)]
        for l, t in order:
            mine = _half(c, bufs[t].shape[2])
            for r, (rx, ry) in enumerate(rels):
                k = (t * DEPTH + l) * 3 + r
                own = bufs[t].at[l, p, mine]
                cp = pltpu.make_async_remote_copy(
                    src_ref=own, dst_ref=own, send_sem=isend.at[k], recv_sem=irecv.at[k],
                    device_id=(rx, ry, c), device_id_type=MESH)
                cp.start()
                sends.append(cp)
        for l, t in order:
            mine = _half(c, bufs[t].shape[2])
            for r, (rx, ry) in enumerate(rels):
                k = (t * DEPTH + l) * 3 + r
                landed = bufs[t].at[l, 2 * rx + ry, mine]
                pltpu.make_async_remote_copy(
                    src_ref=landed, dst_ref=landed, send_sem=isend.at[k], recv_sem=irecv.at[k],
                    device_id=(rx, ry, c), device_id_type=MESH).wait_recv()
                cp = pltpu.make_async_remote_copy(
                    src_ref=landed, dst_ref=landed, send_sem=dsend.at[k], recv_sem=drecv.at[k],
                    device_id=sib, device_id_type=MESH)
                cp.start()
                passes.append(cp)
        for l, t in order:
            other = _half(1 - c, bufs[t].shape[2])
            for r, (rx, ry) in enumerate(rels):
                k = (t * DEPTH + l) * 3 + r
                region = bufs[t].at[l, 2 * rx + ry, other]
                pltpu.make_async_remote_copy(
                    src_ref=region, dst_ref=region, send_sem=dsend.at[k], recv_sem=drecv.at[k],
                    device_id=sib, device_id_type=MESH).wait_recv()
        for cp in sends + passes:
            cp.wait_send()

    return pl.pallas_call(
        body, name="allgather_weights",
        in_specs=[ANY] * nt, out_specs=[ANY] * nt,
        out_shape=[jax.ShapeDtypeStruct(g.shape, g.dtype) for g in gs],
        input_output_aliases={t: t for t in range(nt)},
        scratch_shapes=[pltpu.SemaphoreType.DMA((n_ici,)), pltpu.SemaphoreType.DMA((n_ici,)),
                        pltpu.SemaphoreType.DMA((n_ici,)), pltpu.SemaphoreType.DMA((n_ici,))],
    )(*gs)


def run_alone(name, rider):
    n_in, n_out = len(rider.arrays), len(rider.out_shapes)

    def body(*refs):
        ins, outs, sems = refs[:n_in], refs[n_in:n_in + n_out], refs[n_in + n_out:]
        rider.start(ins, outs, sems)
        rider.finish(ins, outs, sems)

    return pl.pallas_call(
        body, name=name, in_specs=[ANY] * n_in, out_specs=[ANY] * n_out, out_shape=rider.out_shapes,
        scratch_shapes=[pltpu.SemaphoreType.DMA((n,)) for n in rider.sems],
    )(*rider.arrays)


def pair_exchange_rider(gs):
    n = len(gs)

    def ops(ins, outs, ssem, rsem):
        x, y, c, _, _ = _place()
        return [pltpu.make_async_remote_copy(
            src_ref=ins[i].at[:, _half(1 - c, ins[i].shape[1]), :], dst_ref=outs[i],
            send_sem=ssem.at[i], recv_sem=rsem.at[i], device_id=(x, y, 1 - c), device_id_type=MESH) for i in range(n)]

    return Rider(gs, [jax.ShapeDtypeStruct((N_CHIPS, g.shape[1] // 2, g.shape[2]), F32) for g in gs], [n, n], ops)


def pair_add(name, c_arr, g, got):
    _, rows, cols = g.shape
    hr = rows // 2
    tr = 256
    nblk = hr // tr

    def body(c_ref, g_ref, r_ref, o32, o16):
        s = g_ref[...] + r_ref[...]
        o32[...] = s
        o16[...] = s.astype(BF16)

    blk = pl.BlockSpec((1, tr, cols), lambda pp, i, c_ref: (pp, i, 0))
    return pl.pallas_call(
        body, name=name,
        grid_spec=pltpu.PrefetchScalarGridSpec(
            num_scalar_prefetch=1, grid=(N_CHIPS, nblk),
            in_specs=[pl.BlockSpec((1, tr, cols), lambda pp, i, c_ref: (pp, c_ref[0] * nblk + i, 0)), blk],
            out_specs=(blk, blk)),
        out_shape=(jax.ShapeDtypeStruct((N_CHIPS, hr, cols), F32), jax.ShapeDtypeStruct((N_CHIPS, hr, cols), BF16)),
        compiler_params=_cp("parallel", "parallel"),
    )(c_arr, g, got)


def chip_exchange_rider(cs16):
    n = len(cs16)

    def ops(ins, outs, ssem, rsem):
        x, y, c, p, rels = _place()
        return [pltpu.make_async_remote_copy(
            src_ref=ins[i].at[2 * rx + ry], dst_ref=outs[i].at[r], send_sem=ssem.at[i * 3 + r],
            recv_sem=rsem.at[i * 3 + r], device_id=(rx, ry, c), device_id_type=MESH)
            for i in range(n) for r, (rx, ry) in enumerate(rels)]

    return Rider(cs16, [jax.ShapeDtypeStruct((3,) + a.shape[1:], BF16) for a in cs16], [3 * n, 3 * n], ops)


def chip_sum(name, pc_arr, l, cs32, got, prev):
    _, hr, cols = cs32.shape
    tr = 256
    nblk = hr // tr

    def body(pc_ref, o_ref, g_ref, *rest):
        rest[-1][0] = ((o_ref[0] + g_ref[0].astype(F32)) + g_ref[1].astype(F32)) + g_ref[2].astype(F32)

    return pl.pallas_call(
        body, name=name,
        grid_spec=pltpu.PrefetchScalarGridSpec(
            num_scalar_prefetch=1, grid=(nblk,),
            in_specs=[pl.BlockSpec((1, tr, cols), lambda i, pc: (pc[0], i, 0)),
                      pl.BlockSpec((3, tr, cols), lambda i, pc: (0, i, 0))] + ([] if prev is None else [ANY]),
            out_specs=pl.BlockSpec((1, tr, cols), lambda i, pc: (l, pc[1] * nblk + i, 0))),
        out_shape=jax.ShapeDtypeStruct((DEPTH, 2 * hr, cols), F32),
        input_output_aliases={} if prev is None else {3: 0},
        compiler_params=_cp("arbitrary"),
    )(*((pc_arr, cs32, got) + (() if prev is None else (prev,))))


def grad_pair_share(halves):
    n_w = len(halves)
    n = n_w * DEPTH

    def body(*refs):
        bufs = refs[n_w:2 * n_w]
        ssem, rsem = refs[2 * n_w:]
        x, y, c, _, _ = _place()
        cps = []
        for t in range(n_w):
            for l in range(DEPTH):
                mine = bufs[t].at[l, _half(c, bufs[t].shape[1])]
                cp = pltpu.make_async_remote_copy(src_ref=mine, dst_ref=mine, send_sem=ssem.at[t * DEPTH + l],
                                                  recv_sem=rsem.at[t * DEPTH + l], device_id=(x, y, 1 - c),
                                                  device_id_type=MESH)
                cp.start()
                cps.append(cp)
        for t in range(n_w):
            for l in range(DEPTH):
                theirs = bufs[t].at[l, _half(1 - c, bufs[t].shape[1])]
                pltpu.make_async_remote_copy(src_ref=theirs, dst_ref=theirs, send_sem=ssem.at[t * DEPTH + l],
                                             recv_sem=rsem.at[t * DEPTH + l], device_id=(x, y, 1 - c),
                                             device_id_type=MESH).wait_recv()
        for cp in cps:
            cp.wait_send()

    return pl.pallas_call(
        body, name="grad_pair_share", in_specs=[ANY] * n_w, out_specs=[ANY] * n_w,
        out_shape=[jax.ShapeDtypeStruct(a.shape, F32) for a in halves],
        input_output_aliases={t: t for t in range(n_w)},
        scratch_shapes=[pltpu.SemaphoreType.DMA((n,)), pltpu.SemaphoreType.DMA((n,))],
    )(*halves)


SMALL_ROWS = 240


def small_allreduce(v):
    def body(v_ref, o_ref, buf, ssem, rsem):
        x, y, c, _, _ = _place()
        me = 4 * x + 2 * y + c
        buf[me] = v_ref[...]
        for d in range(N_DEV):
            @pl.when(me != d)
            def _():
                pltpu.make_async_remote_copy(
                    src_ref=v_ref, dst_ref=buf.at[me], send_sem=ssem.at[d], recv_sem=rsem.at[me],
                    device_id=(d // 4, (d // 2) % 2, d % 2), device_id_type=MESH).start()
        for d in range(N_DEV):
            @pl.when(me != d)
            def _():
                cp = pltpu.make_async_remote_copy(
                    src_ref=v_ref, dst_ref=buf.at[d], send_sem=ssem.at[d], recv_sem=rsem.at[d],
                    device_id=(d // 4, (d // 2) % 2, d % 2), device_id_type=MESH)
                cp.wait_recv()
                cp.wait_send()
        acc = buf[0]
        for d in range(1, N_DEV):
            acc = acc + buf[d]
        o_ref[...] = acc

    vm = pl.BlockSpec(memory_space=pltpu.VMEM)
    return pl.pallas_call(
        body, name="small_allreduce", in_specs=[vm], out_specs=vm,
        out_shape=jax.ShapeDtypeStruct(v.shape, F32),
        scratch_shapes=[pltpu.VMEM((N_DEV,) + v.shape, F32), pltpu.SemaphoreType.DMA((N_DEV,)),
                        pltpu.SemaphoreType.DMA((N_DEV,))],
    )(v)


class GradReducer:
    def __init__(self):
        self.c_arr = lax.axis_index("c").astype(jnp.int32).reshape(1)
        self.pc_arr = jnp.stack([2 * lax.axis_index("x") + lax.axis_index("y"), lax.axis_index("c")]).astype(jnp.int32)
        self.fresh = None
        self.paired = None
        self.acc = [None] * 4

    def push(self, l, gs):
        self.fresh = (l, list(gs))

    def pair_rider(self):
        return pair_exchange_rider(self.fresh[1]) if self.fresh else None

    def pair_done(self, got):
        l, gs = self.fresh
        self.fresh = None
        self.paired = (l, [pair_add(f"pair_add{t}", self.c_arr, g, r) for t, (g, r) in enumerate(zip(gs, got))])

    def chip_rider(self):
        return chip_exchange_rider([a[1] for a in self.paired[1]]) if self.paired else None

    def chip_done(self, arrived):
        l, cs = self.paired
        self.paired = None
        self.acc = [chip_sum(f"chip_sum{t}", self.pc_arr, l, cs[t][0], arrived[t], self.acc[t]) for t in range(4)]

    def finish(self):
        self.pair_done(run_alone("grad_pair_exchange", self.pair_rider()))
        self.chip_done(run_alone("grad_chip_exchange", self.chip_rider()))
        return grad_pair_share(self.acc)


def adamw(name, w, g, m, v):
    shape = w.shape
    cols = shape[-1]
    flat = [t.reshape(-1, cols) for t in (w, g, m, v)]
    rows = flat[0].shape[0]
    tr = 128 if rows % 128 == 0 else rows

    def body(w_ref, g_ref, m_ref, v_ref, d_ref, mo_ref, vo_ref):
        gv = g_ref[...]
        mn = ADAM_B1 * m_ref[...] + (1.0 - ADAM_B1) * gv
        vn = ADAM_B2 * v_ref[...] + (1.0 - ADAM_B2) * jnp.square(gv)
        m_hat = mn / (1.0 - ADAM_B1 ** ADAM_STEP)
        v_hat = vn / (1.0 - ADAM_B2 ** ADAM_STEP)
        d_ref[...] = -ADAM_LR * (m_hat / (jnp.sqrt(v_hat) + ADAM_EPS) + ADAM_WD * w_ref[...])
        mo_ref[...] = mn
        vo_ref[...] = vn

    blk = pl.BlockSpec((tr, cols), lambda i: (i, 0))
    out = jax.ShapeDtypeStruct((rows, cols), F32)
    d, mo, vo = pl.pallas_call(
        body, name=name, grid=(rows // tr,), in_specs=[blk] * 4, out_specs=(blk, blk, blk),
        out_shape=(out, out, out), compiler_params=_cp("parallel"),
    )(*flat)
    return d.reshape(shape), mo.reshape(shape), vo.reshape(shape)


SMALL_NAMES = ("norm_mix", "norm_mlp", "hg_lb_fwd", "hg_lb_bwd", "hg_norm", "ret_norm", "q_norm", "k_norm", "rel_bias")


def _pack_small(d):
    flat = jnp.concatenate([d[k].reshape(-1) for k in SMALL_NAMES])
    return jnp.pad(flat, (0, SMALL_ROWS * 128 - flat.shape[0])).reshape(SMALL_ROWS, 128)


def _unpack_small(v, like):
    flat = v.reshape(-1)
    out, off = {}, 0
    for k in SMALL_NAMES:
        n = like[k].size
        out[k] = flat[off:off + n].reshape(like[k].shape)
        off += n
    return out


def kernel(x, w_in, w_out, w_up, w_down, norm_mix, norm_mlp, hg_lb_fwd, hg_lb_bwd, hg_norm, ret_norm, q_norm, k_norm, rel_bias, loss_target, m_w_in, m_w_out, m_w_up, m_w_down, m_norm_mix, m_norm_mlp, m_hg_lb_fwd, m_hg_lb_bwd, m_hg_norm, m_ret_norm, m_q_norm, m_k_norm, m_rel_bias, v_w_in, v_w_out, v_w_up, v_w_down, v_norm_mix, v_norm_mlp, v_hg_lb_fwd, v_hg_lb_bwd, v_hg_norm, v_ret_norm, v_q_norm, v_k_norm, v_rel_bias):
    big_w = (w_in, w_out, w_up, w_down)
    big_m = (m_w_in, m_w_out, m_w_up, m_w_down)
    big_v = (v_w_in, v_w_out, v_w_up, v_w_down)
    small_w = dict(zip(SMALL_NAMES, (norm_mix, norm_mlp, hg_lb_fwd, hg_lb_bwd, hg_norm, ret_norm, q_norm, k_norm, rel_bias)))
    small_m = dict(zip(SMALL_NAMES, (m_norm_mix, m_norm_mlp, m_hg_lb_fwd, m_hg_lb_bwd, m_hg_norm, m_ret_norm, m_q_norm,
                                     m_k_norm, m_rel_bias)))
    small_v = dict(zip(SMALL_NAMES, (v_norm_mix, v_norm_mlp, v_hg_lb_fwd, v_hg_lb_bwd, v_hg_norm, v_ret_norm, v_q_norm,
                                     v_k_norm, v_rel_bias)))

    p_arr = (2 * lax.axis_index("x") + lax.axis_index("y")).astype(jnp.int32).reshape(1)
    win_g, wout_g, wup_g, wdown_g = allgather_weights([place_own(f"place_own{t}", p_arr, w)
                                                       for t, w in enumerate(big_w)])
    wts = (win_g, wout_g.reshape(DEPTH, D_MODEL, D_MODEL), wup_g, wdown_g.reshape(DEPTH, D_FF, D_MODEL))

    loss_part, dx, grads_big, sg, (lbf_vjp, lbb_vjp) = local_step(x[0], loss_target[0], wts, small_w, GradReducer())
    loss = lax.psum(loss_part, ("x", "y", "c"))

    sg = dict(sg)
    sg["hg_lb_fwd"], sg["hg_lb_bwd"] = sg.pop("lbf"), sg.pop("lbb")
    tot = _unpack_small(small_allreduce(_pack_small(sg)), small_w)
    tot["hg_lb_fwd"] = lbf_vjp(tot["hg_lb_fwd"])[0]
    tot["hg_lb_bwd"] = lbb_vjp(tot["hg_lb_bwd"])[0]
    grads_small = [tot[k] for k in SMALL_NAMES]

    upd_big = [adamw(f"adamw_big{t}", big_w[t], grads_big[t], big_m[t], big_v[t]) for t in range(4)]
    d_s, m_s, v_s = adamw("adamw_small", _pack_small(small_w), _pack_small(tot), _pack_small(small_m), _pack_small(small_v))
    upd_small = [_unpack_small(t, small_w) for t in (d_s, m_s, v_s)]

    outs = [loss, dx[None]] + list(grads_big) + grads_small
    for j in range(3):
        outs += [u[j] for u in upd_big] + [upd_small[j][k] for k in SMALL_NAMES]
    return tuple(outs)
```

```python
import functools
import math

import jax
import jax.numpy as jnp
from jax import lax
from jax.experimental import pallas as pl
from jax.experimental.pallas import tpu as pltpu

F32 = jnp.float32
BF16 = jnp.bfloat16
EPS = 1e-6

D_MODEL = 2048
DEPTH = 4
HG_HEADS = 6
HG_W = 768
RET_HEADS = 6
RET_DK = 64
RET_W = 768
RET_QK_W = RET_HEADS * RET_DK
RET_CHUNK = 128
ROPE_BASE = 10000.0
DIL_SLOTS = 4
DIL_HD = 128
DIL_GROUPS = ((128, 1), (512, 4), (2048, 16))
DIL_HALF = 64
DIL_W = 512
D_FF = 4 * D_MODEL
IN_W = 10752
REL_BUCKETS = 32
REL_MAX_DIST = 1024

OFF_HG_Q, OFF_HG_V, OFF_HG_ZF, OFF_HG_ZB, OFF_HG_GATE = 0, 768, 1536, 2304, 3072
OFF_RET_Q, OFF_RET_K, OFF_RET_V, OFF_RET_GATE = 3840, 4224, 4608, 5376
OFF_DIL = 6144

N_CHIPS = 4
N_DEV = 8
IN_SHARD = IN_W // N_CHIPS
FF_SHARD = D_FF // N_CHIPS

ADAM_LR, ADAM_B1, ADAM_B2, ADAM_EPS, ADAM_WD, ADAM_STEP = 0.001, 0.9, 0.999, 1e-08, 0.01, 10

VMEM_LIMIT = 56 * 1024 * 1024
HG_T = 512
HG_C = 64
RET_T = 256
DIL_TQ = 256
NEG = -1e30

NN = (((1,), (0,)), ((), ()))
NT = (((1,), (1,)), ((), ()))
TN = (((0,), (0,)), ((), ()))
MESH = pl.DeviceIdType.MESH
ANY = pl.BlockSpec(memory_space=pl.ANY)


def _cp(*sem):
    return pltpu.CompilerParams(dimension_semantics=sem, vmem_limit_bytes=VMEM_LIMIT)


def _mxu(a, b, dn):
    return lax.dot_general(a.astype(BF16), b.astype(BF16), dn, preferred_element_type=F32)


@jax.custom_vjp
def dot_nn(a, b):
    return _mxu(a, b, NN)


dot_nn.defvjp(lambda a, b: (_mxu(a, b, NN), (a, b)),
              lambda r, g: (_mxu(g, r[1], NT), _mxu(r[0], g, TN)))


@jax.custom_vjp
def dot_nt(a, b):
    return _mxu(a, b, NT)


dot_nt.defvjp(lambda a, b: (_mxu(a, b, NT), (a, b)),
              lambda r, g: (_mxu(g, r[1], NN), _mxu(g, r[0], TN)))


@jax.custom_vjp
def dot_tn(a, b):
    return _mxu(a, b, TN)


dot_tn.defvjp(lambda a, b: (_mxu(a, b, TN), (a, b)),
              lambda r, g: (_mxu(r[1], g, NT), _mxu(r[0], g, NN)))


def _split3(v):
    hi = v.astype(BF16)
    r1 = v - hi.astype(F32)
    mid = r1.astype(BF16)
    lo = (r1 - mid.astype(F32)).astype(BF16)
    return hi, mid, lo


def _exact_mask_dot(m, v, dn):
    mb = m.astype(BF16)
    hi, mid, lo = _split3(v)
    f = lambda p: lax.dot_general(mb, p, dn, preferred_element_type=F32)
    return (f(lo) + f(mid)) + f(hi)


@jax.custom_vjp
def cumdot(m, v):
    return _exact_mask_dot(m, v, NN)


cumdot.defvjp(lambda m, v: (_exact_mask_dot(m, v, NN), m),
              lambda m, g: (jnp.zeros_like(m), _exact_mask_dot(m, g, TN)))


def _sigmoid(z):
    return 1.0 / (1.0 + jnp.exp(-z))


def _head_rms(t, g):
    return t * lax.rsqrt(jnp.mean(t * t, axis=-1, keepdims=True) + EPS) * g


class Rider:
    def __init__(self, arrays, out_shapes, sems, ops):
        self.arrays, self.sems, self.ops = list(arrays), list(sems), ops
        self.in_place = out_shapes is None
        self.out_shapes = [jax.ShapeDtypeStruct(a.shape, a.dtype) for a in arrays] if self.in_place else list(out_shapes)

    def aliases(self, n_in, n_out):
        return {n_in + i: n_out + i for i in range(len(self.arrays))} if self.in_place else {}

    def start(self, ins, outs, sems):
        for send, _ in self.ops(ins, outs, *sems):
            send.start()

    def finish(self, ins, outs, sems):
        cps = self.ops(ins, outs, *sems)
        for _, arrive in cps:
            arrive.wait_recv()
        for send, _ in cps:
            send.wait_send()

    def __add__(self, other):
        assert self.in_place and other.in_place
        na, sa = len(self.arrays), len(self.sems)
        ops = lambda ins, outs, *sems: (self.ops(ins[:na], outs[:na], *sems[:sa])
                                        + other.ops(ins[na:], outs[na:], *sems[sa:]))
        return Rider(self.arrays + other.arrays, None, self.sems + other.sems, ops)


def _hosted(name, body, rider, *, grid, in_specs, out_specs, out_shape, scratch_shapes, sem, operands):
    n_in, n_out, n_scr = len(in_specs), len(out_specs), len(scratch_shapes)
    r_in = len(rider.arrays) if rider else 0
    r_out = len(rider.out_shapes) if rider else 0
    last = tuple(g - 1 for g in grid)

    def kernel_body(*refs):
        ins, refs = refs[:n_in], refs[n_in:]
        rins, refs = refs[:r_in], refs[r_in:]
        outs, refs = refs[:n_out], refs[n_out:]
        routs, refs = refs[:r_out], refs[r_out:]
        scr, rsems = refs[:n_scr], refs[n_scr:]
        if rider:
            ids = [pl.program_id(d) for d in range(len(grid))]
            first = functools.reduce(lambda p, q: p & q, [i == 0 for i in ids])
            done = functools.reduce(lambda p, q: p & q, [i == e for i, e in zip(ids, last)])
            pl.when(first)(lambda: rider.start(rins, routs, rsems))
        body(ins, outs, scr)
        if rider:
            pl.when(done)(lambda: rider.finish(rins, routs, rsems))

    res = pl.pallas_call(
        kernel_body, name=name, grid=grid,
        in_specs=list(in_specs) + [ANY] * r_in,
        out_specs=list(out_specs) + [ANY] * r_out,
        out_shape=list(out_shape) + (rider.out_shapes if rider else []),
        scratch_shapes=list(scratch_shapes) + ([pltpu.SemaphoreType.DMA((n,)) for n in rider.sems] if rider else []),
        input_output_aliases=rider.aliases(n_in, n_out) if rider else {},
        compiler_params=_cp(*(("arbitrary",) * len(grid) if rider else sem)),
    )(*operands, *(rider.arrays if rider else []))
    return res[:n_out], res[n_out:]


def _mm(name, a, b, *, mode, grid, a_spec, b_spec, tm, tn, extras=(), extra_specs=(), epi, out_shape, out_specs,
        rider=None):
    nk = grid[2]
    single = not isinstance(out_shape, (tuple, list))
    if single:
        out_shape, out_specs = [out_shape], [out_specs]

    def body(ins, outs, scr):
        a_ref, b_ref, ex = ins[0], ins[1], ins[2:]
        acc = scr[0]
        k = pl.program_id(2)

        @pl.when(k == 0)
        def _():
            acc[...] = jnp.zeros_like(acc)

        acc[...] += _mxu(a_ref[...], b_ref[...], {"nn": NN, "nt": NT, "tn": TN}[mode])

        @pl.when(k == nk - 1)
        def _():
            epi(acc[...], ex, outs)

    outs, carried = _hosted(name, body, rider, grid=grid, in_specs=[a_spec, b_spec, *extra_specs],
                            out_specs=out_specs, out_shape=out_shape, scratch_shapes=[pltpu.VMEM((tm, tn), F32)],
                            sem=("parallel", "parallel", "arbitrary"), operands=(a, b, *extras))
    res = outs[0] if single else tuple(outs)
    return (res, carried) if rider else res


def _epi_store(acc, ex, outs):
    outs[0][...] = acc.astype(outs[0].dtype)


def _epi_residual(acc, ex, outs):
    outs[0][...] = ex[0][...] + acc


def _epi_up(acc, ex, outs):
    outs[0][...] = acc
    outs[1][...] = jnp.square(jnp.maximum(acc, 0.0)).astype(BF16)


def _epi_dact(acc, ex, outs):
    outs[0][...] = (acc * (2.0 * jnp.maximum(ex[0][...], 0.0))).astype(BF16)


def _ij(i, j, k):
    return (i, j)


def proj_in(xn, win_g, rider=None):
    S = xn.shape[0]
    tm, tn = 512, IN_SHARD
    return _mm("proj_in", xn, win_g, mode="nn", grid=(IN_W // tn, S // tm, 1), tm=tm, tn=tn,
               a_spec=pl.BlockSpec((tm, D_MODEL), lambda j, i, k: (i, 0)),
               b_spec=pl.BlockSpec((None, D_MODEL, tn), lambda j, i, k: (j, 0, 0)),
               epi=_epi_store, out_shape=jax.ShapeDtypeStruct((S, IN_W), F32),
               out_specs=pl.BlockSpec((tm, tn), lambda j, i, k: (i, j)), rider=rider)


def proj_out(y, wout_g, x):
    S = y.shape[0]
    tm, tn = 1024, 1024
    return _mm("proj_out", y, wout_g, mode="nn", grid=(S // tm, D_MODEL // tn, 1), tm=tm, tn=tn,
               a_spec=pl.BlockSpec((tm, D_MODEL), lambda i, j, k: (i, 0)),
               b_spec=pl.BlockSpec((D_MODEL, tn), lambda i, j, k: (0, j)),
               extras=(x,), extra_specs=(pl.BlockSpec((tm, tn), _ij),),
               epi=_epi_residual, out_shape=jax.ShapeDtypeStruct((S, D_MODEL), F32),
               out_specs=pl.BlockSpec((tm, tn), _ij))


def proj_up(hm, wup_g):
    S = hm.shape[0]
    tm, tn = 1024, 1024
    return _mm("proj_up", hm, wup_g, mode="nn", grid=(S // tm, D_FF // tn, 1), tm=tm, tn=tn,
               a_spec=pl.BlockSpec((tm, D_MODEL), lambda i, j, k: (i, 0)),
               b_spec=pl.BlockSpec((None, D_MODEL, tn), lambda i, j, k: (j // 2, 0, j % 2)),
               epi=_epi_up,
               out_shape=(jax.ShapeDtypeStruct((S, D_FF), F32), jax.ShapeDtypeStruct((S, D_FF), BF16)),
               out_specs=(pl.BlockSpec((tm, tn), _ij), pl.BlockSpec((tm, tn), _ij)))


def proj_down(a, wdown_g, x):
    S = a.shape[0]
    tm, tn, tk = 1024, 1024, 2048
    return _mm("proj_down", a, wdown_g, mode="nn", grid=(S // tm, D_MODEL // tn, D_FF // tk), tm=tm, tn=tn,
               a_spec=pl.BlockSpec((tm, tk), lambda i, j, k: (i, k)),
               b_spec=pl.BlockSpec((tk, tn), lambda i, j, k: (k, j)),
               extras=(x,), extra_specs=(pl.BlockSpec((tm, tn), _ij),),
               epi=_epi_residual, out_shape=jax.ShapeDtypeStruct((S, D_MODEL), F32),
               out_specs=pl.BlockSpec((tm, tn), _ij))


def bwd_down_act(dx, wdown_g, u, rider=None):
    S = dx.shape[0]
    tm, tn = 1024, 1024
    return _mm("bwd_down_act", dx, wdown_g, mode="nt", grid=(S // tm, D_FF // tn, 1), tm=tm, tn=tn,
               a_spec=pl.BlockSpec((tm, D_MODEL), lambda i, j, k: (i, 0)),
               b_spec=pl.BlockSpec((tn, D_MODEL), lambda i, j, k: (j, 0)),
               extras=(u,), extra_specs=(pl.BlockSpec((tm, tn), _ij),),
               epi=_epi_dact, out_shape=jax.ShapeDtypeStruct((S, D_FF), BF16),
               out_specs=pl.BlockSpec((tm, tn), _ij), rider=rider)


def bwd_up(du, wup_g):
    S = du.shape[0]
    tm, tn, tk = 1024, 1024, FF_SHARD
    return _mm("bwd_up", du, wup_g, mode="nt", grid=(S // tm, D_MODEL // tn, D_FF // tk), tm=tm, tn=tn,
               a_spec=pl.BlockSpec((tm, tk), lambda i, j, k: (i, k)),
               b_spec=pl.BlockSpec((None, tn, tk), lambda i, j, k: (k, j, 0)),
               epi=_epi_store, out_shape=jax.ShapeDtypeStruct((S, D_MODEL), F32),
               out_specs=pl.BlockSpec((tm, tn), _ij))


def bwd_out(dx, wout_g):
    S = dx.shape[0]
    tm, tn = 1024, 1024
    return _mm("bwd_out", dx, wout_g, mode="nt", grid=(S // tm, D_MODEL // tn, 1), tm=tm, tn=tn,
               a_spec=pl.BlockSpec((tm, D_MODEL), lambda i, j, k: (i, 0)),
               b_spec=pl.BlockSpec((tn, D_MODEL), lambda i, j, k: (j, 0)),
               epi=_epi_store, out_shape=jax.ShapeDtypeStruct((S, D_MODEL), F32),
               out_specs=pl.BlockSpec((tm, tn), _ij))


def bwd_in(dh, win_g):
    S = dh.shape[0]
    tm, tn, tk = 1024, 1024, IN_SHARD
    return _mm("bwd_in", dh, win_g, mode="nt", grid=(S // tm, D_MODEL // tn, IN_W // tk), tm=tm, tn=tn,
               a_spec=pl.BlockSpec((tm, tk), lambda i, j, k: (i, k)),
               b_spec=pl.BlockSpec((None, tn, tk), lambda i, j, k: (k, j, 0)),
               epi=_epi_store, out_shape=jax.ShapeDtypeStruct((S, D_MODEL), F32),
               out_specs=pl.BlockSpec((tm, tn), _ij))


def wgrad(name, a, g, *, m, n, n_shard):
    S = a.shape[0]
    tm, tn, tk = (512, IN_SHARD, 1024) if n_shard == IN_SHARD else (1024, 1024, 1024)
    per = n_shard // tn
    if n_shard == n:
        out_shape = jax.ShapeDtypeStruct((m, n), F32)
        out_spec = pl.BlockSpec((tm, tn), _ij)
    else:
        out_shape = jax.ShapeDtypeStruct((N_CHIPS, m, n_shard), F32)
        out_spec = pl.BlockSpec((None, tm, tn), lambda i, j, k: (j // per, i, j % per))
    return _mm(name, a, g, mode="tn", grid=(m // tm, n // tn, S // tk), tm=tm, tn=tn,
               a_spec=pl.BlockSpec((tk, tm), lambda i, j, k: (k, i)),
               b_spec=pl.BlockSpec((tk, tn), lambda i, j, k: (k, j)),
               epi=_epi_store, out_shape=out_shape, out_specs=out_spec)


NORM_T = 256


def rmsnorm_fwd(x, g):
    S = x.shape[0]

    def body(x_ref, g_ref, o_ref):
        xv = x_ref[...]
        r = lax.rsqrt(jnp.mean(xv * xv, axis=-1, keepdims=True) + EPS)
        o_ref[...] = ((xv * r) * g_ref[...]).astype(BF16)

    return pl.pallas_call(
        body, name="rmsnorm_fwd", grid=(S // NORM_T,),
        in_specs=[pl.BlockSpec((NORM_T, D_MODEL), lambda i: (i, 0)), pl.BlockSpec((1, D_MODEL), lambda i: (0, 0))],
        out_specs=pl.BlockSpec((NORM_T, D_MODEL), lambda i: (i, 0)),
        out_shape=jax.ShapeDtypeStruct((S, D_MODEL), BF16), compiler_params=_cp("parallel"),
    )(x, g)


def rmsnorm_bwd(x, g, dxn, dres):
    S = x.shape[0]

    def body(x_ref, g_ref, dxn_ref, dres_ref, dx_ref, dg_ref):
        @pl.when(pl.program_id(0) == 0)
        def _():
            dg_ref[...] = jnp.zeros_like(dg_ref)

        xv, gv, d = x_ref[...], g_ref[...], dxn_ref[...]
        r = lax.rsqrt(jnp.mean(xv * xv, axis=-1, keepdims=True) + EPS)
        gd = gv * d
        dx_ref[...] = dres_ref[...] + r * gd - xv * ((r * r * r) * jnp.mean(xv * gd, axis=-1, keepdims=True))
        dg_ref[...] += jnp.sum(d * (xv * r), axis=0, keepdims=True)

    row = pl.BlockSpec((NORM_T, D_MODEL), lambda i: (i, 0))
    vec = pl.BlockSpec((1, D_MODEL), lambda i: (0, 0))
    return pl.pallas_call(
        body, name="rmsnorm_bwd", grid=(S // NORM_T,),
        in_specs=[row, vec, row, row], out_specs=(row, vec),
        out_shape=(jax.ShapeDtypeStruct((S, D_MODEL), F32), jax.ShapeDtypeStruct((1, D_MODEL), F32)),
        compiler_params=_cp("arbitrary"),
    )(x, g, dxn, dres)


def loss_head(y, tgt):
    S = y.shape[0]

    def body(y_ref, t_ref, dy_ref, l_ref):
        @pl.when(pl.program_id(0) == 0)
        def _():
            l_ref[...] = jnp.zeros_like(l_ref)

        e = y_ref[...] - t_ref[...]
        dy_ref[...] = e * (1.0 / D_MODEL)
        l_ref[...] += jnp.sum(e * e) * (0.5 / D_MODEL)

    row = pl.BlockSpec((NORM_T, D_MODEL), lambda i: (i, 0))
    return pl.pallas_call(
        body, name="loss_head", grid=(S // NORM_T,),
        in_specs=[row, row], out_specs=(row, pl.BlockSpec((1, 128), lambda i: (0, 0))),
        out_shape=(jax.ShapeDtypeStruct((S, D_MODEL), F32), jax.ShapeDtypeStruct((1, 128), F32)),
        compiler_params=_cp("arbitrary"),
    )(y, tgt)


def _hg_block(qs, vs, zs, lb, sT, reverse):
    n = len(qs)
    row = lax.broadcasted_iota(jnp.int32, (HG_C, HG_C), 0)
    col = lax.broadcasted_iota(jnp.int32, (HG_C, HG_C), 1)
    tri = (row <= col) if reverse else (row >= col)
    m = tri.astype(F32)
    rsel = lax.broadcasted_iota(jnp.int32, (HG_C, 128), 0)
    ref_rows = ((rsel >= HG_C // 2) if reverse else (rsel <= HG_C // 2)).astype(F32)
    outs = [None] * n
    for c in (range(n - 1, -1, -1) if reverse else range(n)):
        f = lb + (1.0 - lb) * _sigmoid(zs[c])
        kc = 1.0 - f
        lc = jnp.log(f)
        b = cumdot(m, lc)
        btot = jnp.sum(lc, axis=0, keepdims=True)
        bref = lax.stop_gradient(jnp.sum(lc * ref_rows, axis=0, keepdims=True))
        qe = qs[c] * jnp.exp(jnp.minimum(b - bref, 80.0))
        ke = kc * jnp.exp(jnp.minimum(bref - b, 80.0))
        att = jnp.where(tri, dot_nt(qe, ke), 0.0)
        outs[c] = dot_nn(att, vs[c]) + dot_nt(qs[c] * jnp.exp(b), sT)
        sT = sT * jnp.exp(btot) + dot_tn(vs[c], kc * jnp.exp(btot - b))
    return outs, sT


def _chunks(ref, c, n):
    return [ref[i * c:(i + 1) * c, :] for i in range(n)]


def hg_scan_fwd(h, lbf, lbb, rider=None):
    S = h.shape[0]
    nb = S // HG_T
    n = HG_T // HG_C

    def body(ins, outs, scr):
        qf, vf, zf, qb, vb, zb, lbf_ref, lbb_ref = ins
        of_ref, ob_ref, sf_ref, sb_ref = outs
        stf, stb = scr

        @pl.when(pl.program_id(1) == 0)
        def _():
            stf[...] = jnp.zeros_like(stf)
            stb[...] = jnp.zeros_like(stb)

        for (q, v, z, lb_ref, o_ref, s_ref, st, rev) in ((qf, vf, zf, lbf_ref, of_ref, sf_ref, stf, False),
                                                         (qb, vb, zb, lbb_ref, ob_ref, sb_ref, stb, True)):
            s_ref[0, 0] = st[...]
            outs, s_new = _hg_block(_chunks(q, HG_C, n), _chunks(v, HG_C, n), _chunks(z, HG_C, n),
                                    lb_ref[...], st[...], rev)
            for c in range(n):
                o_ref[c * HG_C:(c + 1) * HG_C, :] = outs[c]
            st[...] = s_new

    def col(off, rev):
        return pl.BlockSpec((HG_T, 128), (lambda hh, t: (nb - 1 - t, off // 128 + hh)) if rev
                            else (lambda hh, t: (t, off // 128 + hh)))

    lb_spec = pl.BlockSpec((1, 128), lambda hh, t: (0, hh))
    st_f = pl.BlockSpec((1, 1, 128, 128), lambda hh, t: (hh, t, 0, 0))
    st_b = pl.BlockSpec((1, 1, 128, 128), lambda hh, t: (hh, nb - 1 - t, 0, 0))
    outs, carried = _hosted(
        "hg_scan_fwd", body, rider, grid=(HG_HEADS, nb),
        in_specs=[col(OFF_HG_Q, False), col(OFF_HG_V, False), col(OFF_HG_ZF, False),
                  col(OFF_HG_Q, True), col(OFF_HG_V, True), col(OFF_HG_ZB, True), lb_spec, lb_spec],
        out_specs=[col(0, False), col(0, True), st_f, st_b],
        out_shape=[jax.ShapeDtypeStruct((S, HG_W), F32), jax.ShapeDtypeStruct((S, HG_W), F32),
                   jax.ShapeDtypeStruct((HG_HEADS, nb, 128, 128), F32),
                   jax.ShapeDtypeStruct((HG_HEADS, nb, 128, 128), F32)],
        scratch_shapes=[pltpu.VMEM((128, 128), F32), pltpu.VMEM((128, 128), F32)],
        sem=("arbitrary", "arbitrary"), operands=(h, h, h, h, h, h, lbf, lbb))
    return (tuple(outs), carried) if rider else tuple(outs)


def hg_scan_bwd(h, lbf, lbb, do, sf, sb, rider=None):
    S = h.shape[0]
    nb = S // HG_T
    n = HG_T // HG_C

    def body(ins, outs, scr):
        qf, vf, zf, dof, sfin, qb, vb, zb, dob, sbin, lbf_ref, lbb_ref = ins
        dqf, dvf, dzf, dqb, dvb, dzb, dlbf, dlbb = outs
        dsf, dsb = scr

        @pl.when(pl.program_id(1) == 0)
        def _():
            for r in (dsf, dsb, dlbf, dlbb):
                r[...] = jnp.zeros_like(r)

        for (q, v, z, dor, sin, lb_ref, dq, dv, dz, dlb, ds, rev) in (
                (qf, vf, zf, dof, sfin, lbf_ref, dqf, dvf, dzf, dlbf, dsf, False),
                (qb, vb, zb, dob, sbin, lbb_ref, dqb, dvb, dzb, dlbb, dsb, True)):
            fn = functools.partial(_hg_block, reverse=rev)
            _, vjp = jax.vjp(fn, _chunks(q, HG_C, n), _chunks(v, HG_C, n), _chunks(z, HG_C, n), lb_ref[...], sin[0, 0])
            dqs, dvs, dzs, dlb_v, ds_in = vjp((_chunks(dor, HG_C, n), ds[...]))
            for c in range(n):
                sl = slice(c * HG_C, (c + 1) * HG_C)
                dq[sl, :] = dqs[c]
                dv[sl, :] = dvs[c]
                dz[sl, :] = dzs[c]
            dlb[...] += dlb_v
            ds[...] = ds_in

    def col(off, fwd_scan):
        return pl.BlockSpec((HG_T, 128), (lambda hh, t: (nb - 1 - t, off // 128 + hh)) if fwd_scan
                            else (lambda hh, t: (t, off // 128 + hh)))

    lb_spec = pl.BlockSpec((1, 128), lambda hh, t: (0, hh))
    st_f = pl.BlockSpec((1, 1, 128, 128), lambda hh, t: (hh, nb - 1 - t, 0, 0))
    st_b = pl.BlockSpec((1, 1, 128, 128), lambda hh, t: (hh, t, 0, 0))
    full = jax.ShapeDtypeStruct((S, HG_W), F32)
    vec = jax.ShapeDtypeStruct((1, HG_W), F32)
    outs, carried = _hosted(
        "hg_scan_bwd", body, rider, grid=(HG_HEADS, nb),
        in_specs=[col(OFF_HG_Q, True), col(OFF_HG_V, True), col(OFF_HG_ZF, True), col(0, True), st_f,
                  col(OFF_HG_Q, False), col(OFF_HG_V, False), col(OFF_HG_ZB, False), col(0, False), st_b,
                  lb_spec, lb_spec],
        out_specs=[col(0, True), col(0, True), col(0, True), col(0, False), col(0, False), col(0, False),
                   lb_spec, lb_spec],
        out_shape=[full, full, full, full, full, full, vec, vec],
        scratch_shapes=[pltpu.VMEM((128, 128), F32), pltpu.VMEM((128, 128), F32)],
        sem=("arbitrary", "arbitrary"), operands=(h, h, h, do, sf, h, h, h, do, sb, lbf, lbb))
    return (tuple(outs), carried) if rider else tuple(outs)


GN_T = 1024


def _gated_norm(o, gate, g, center):
    if center:
        o = o - jnp.mean(o, axis=-1, keepdims=True)
    o = o * lax.rsqrt(jnp.mean(o * o, axis=-1, keepdims=True) + EPS)
    return (o * g) * (gate * _sigmoid(gate))


def gated_norm_fwd(name, of, ob, h, gate_off, g, center):
    S = of.shape[0]

    def body(of_ref, ob_ref, gate_ref, g_ref, y_ref):
        y_ref[...] = _gated_norm(of_ref[...] + ob_ref[...], gate_ref[...], g_ref[...], center)

    blk = pl.BlockSpec((GN_T, 128), lambda hh, i: (i, hh))
    return pl.pallas_call(
        body, name=name, grid=(6, S // GN_T),
        in_specs=[blk, blk, pl.BlockSpec((GN_T, 128), lambda hh, i: (i, gate_off // 128 + hh)),
                  pl.BlockSpec((1, 128), lambda hh, i: (0, hh))],
        out_specs=blk, out_shape=jax.ShapeDtypeStruct((S, 768), F32),
        compiler_params=_cp("parallel", "parallel"),
    )(of, ob, h, g)


def gated_norm_bwd(name, of, ob, h, gate_off, g, dy, dy_off, center):
    S = of.shape[0]

    def body(of_ref, ob_ref, gate_ref, g_ref, dy_ref, do_ref, dgate_ref, dg_ref):
        @pl.when(pl.program_id(1) == 0)
        def _():
            dg_ref[...] = jnp.zeros_like(dg_ref)

        fn = functools.partial(_gated_norm, center=center)
        _, vjp = jax.vjp(fn, of_ref[...] + ob_ref[...], gate_ref[...], g_ref[...])
        do, dgate, dg = vjp(dy_ref[...])
        do_ref[...] = do
        dgate_ref[...] = dgate
        dg_ref[...] += dg

    blk = pl.BlockSpec((GN_T, 128), lambda hh, i: (i, hh))
    vec = pl.BlockSpec((1, 128), lambda hh, i: (0, hh))
    return pl.pallas_call(
        body, name=name, grid=(6, S // GN_T),
        in_specs=[blk, blk, pl.BlockSpec((GN_T, 128), lambda hh, i: (i, gate_off // 128 + hh)), vec,
                  pl.BlockSpec((GN_T, 128), lambda hh, i: (i, dy_off // 128 + hh))],
        out_specs=(blk, blk, vec),
        out_shape=(jax.ShapeDtypeStruct((S, 768), F32), jax.ShapeDtypeStruct((S, 768), F32),
                   jax.ShapeDtypeStruct((1, 768), F32)),
        compiler_params=_cp("arbitrary", "arbitrary"),
    )(of, ob, h, g, dy)


def _ret_consts(S):
    half = RET_DK // 2
    inv = ROPE_BASE ** (-jnp.arange(half, dtype=F32) / half)
    ang = jnp.arange(S, dtype=F32)[:, None] * inv[None, :]
    cos, sin = jnp.cos(ang), jnp.sin(ang)
    cos_t = jnp.tile(jnp.concatenate([cos, cos], axis=1), (1, RET_HEADS))
    sin_t = jnp.tile(jnp.concatenate([-sin, sin], axis=1), (1, RET_HEADS))
    hidx = jnp.arange(RET_HEADS, dtype=F32)
    lg_f = jnp.log1p(-jnp.exp2(-5.0 - hidx))
    C = RET_CHUNK
    idx = jnp.arange(C, dtype=F32)
    rel = idx[:, None] - idx[None, :]

    def one(lg, reverse):
        lgc = lg[:, None]
        decay = jnp.where(rel >= 0, jnp.exp(lgc[:, :, None] * jnp.maximum(rel, 0.0)), 0.0)
        zeta = jnp.exp(lgc * (C - 1 - idx))
        xi = jnp.exp(lgc * (idx + 1))
        if reverse:
            decay = decay[:, ::-1, ::-1]
            zeta, xi = zeta[:, ::-1], xi[:, ::-1]
        wide = lambda t: jnp.repeat(t.T, RET_DK, axis=1)
        gam_w = jnp.broadcast_to(jnp.repeat(jnp.exp(lg * C), 128)[None, :], (8, RET_W))
        return decay, wide(xi), wide(zeta), gam_w

    hm = (jnp.arange(RET_QK_W)[None, :] // RET_DK == jnp.arange(8)[:, None]).astype(F32)
    return (cos_t, sin_t, hm) + one(lg_f, False) + one(lg_f[::-1], True)


def _rope(t, cos, sin_signed):
    lane = lax.broadcasted_iota(jnp.int32, t.shape, 1)
    first = (lane & (RET_DK - 1)) < RET_DK // 2
    partner = jnp.where(first, pltpu.roll(t, RET_QK_W - RET_DK // 2, 1), pltpu.roll(t, RET_DK // 2, 1))
    return t * cos + partner * sin_signed


def _ret_block(qs, ks, vs, st, dec, xi, zeta, gam, hms, reverse):
    n = len(qs)
    outs = [None] * n
    st = list(st)
    for c in (range(n - 1, -1, -1) if reverse else range(n)):
        qx = qs[c] * xi
        kz = ks[c] * zeta
        row = []
        for hh in range(RET_HEADS):
            sc = dot_nt(qs[c] * hms[hh], ks[c]) * dec[hh]
            row.append(dot_nn(sc, vs[c][hh]) + dot_nn(qx, st[hh]))
            st[hh] = st[hh] * gam[hh] + dot_tn(kz * hms[hh], vs[c][hh])
        outs[c] = row
    return outs, st


def _ret_inputs(q_ref, k_ref, v_ref, cos_ref, sin_ref):
    n = RET_T // RET_CHUNK
    qr = _rope(q_ref[...], cos_ref[...], sin_ref[...])
    kr = _rope(k_ref[...], cos_ref[...], sin_ref[...]) * (RET_DK ** -0.5)
    qs = [qr[c * RET_CHUNK:(c + 1) * RET_CHUNK] for c in range(n)]
    ks = [kr[c * RET_CHUNK:(c + 1) * RET_CHUNK] for c in range(n)]
    vs = [[v_ref[c * RET_CHUNK:(c + 1) * RET_CHUNK, hh * 128:(hh + 1) * 128] for hh in range(RET_HEADS)]
          for c in range(n)]
    return qs, ks, vs


def _ret_dir_consts(dec_ref, xi_ref, zeta_ref, gam_ref, hm_ref):
    dec = [dec_ref[hh] for hh in range(RET_HEADS)]
    gam = [gam_ref[0:1, hh * 128:(hh + 1) * 128] for hh in range(RET_HEADS)]
    hms = [hm_ref[hh:hh + 1, :] for hh in range(RET_HEADS)]
    return dec, xi_ref[...], zeta_ref[...], gam, hms


def _ret_rows(nb, rev):
    def rows(width, colblk):
        return pl.BlockSpec((RET_T, width), (lambda t: (nb - 1 - t, colblk)) if rev else (lambda t: (t, colblk)))
    return rows


def _const_spec(shape):
    nd = len(shape)
    return pl.BlockSpec(shape, lambda t: (0,) * nd)


def ret_scan_fwd(h, consts, rider=None):
    S = h.shape[0]
    nb = S // RET_T
    n = RET_T // RET_CHUNK
    cos_t, sin_t, hm, dec_f, xi_f, zeta_f, gam_f, dec_b, xi_b, zeta_b, gam_b = consts

    def body(ins, outs, scr):
        qf, kf, vf, cf, sf, qb, kb, vb, cb, sb_, hm_ref, decf, xif, zetaf, gamf, decb, xib, zetab, gamb = ins
        of_ref, ob_ref, sfo, sbo = outs
        stf, stb = scr

        @pl.when(pl.program_id(0) == 0)
        def _():
            stf[...] = jnp.zeros_like(stf)
            stb[...] = jnp.zeros_like(stb)

        for (q, k, v, cs, sn, dr, xr, zr, gr, o_ref, so, st, rev) in (
                (qf, kf, vf, cf, sf, decf, xif, zetaf, gamf, of_ref, sfo, stf, False),
                (qb, kb, vb, cb, sb_, decb, xib, zetab, gamb, ob_ref, sbo, stb, True)):
            so[0] = st[...]
            qs, ks, vs = _ret_inputs(q, k, v, cs, sn)
            dec, xi, zeta, gam, hms = _ret_dir_consts(dr, xr, zr, gr, hm_ref)
            st_in = [st[:, hh * 128:(hh + 1) * 128] for hh in range(RET_HEADS)]
            outs, st_new = _ret_block(qs, ks, vs, st_in, dec, xi, zeta, gam, hms, rev)
            for c in range(n):
                for hh in range(RET_HEADS):
                    o_ref[c * RET_CHUNK:(c + 1) * RET_CHUNK, hh * 128:(hh + 1) * 128] = outs[c][hh]
            for hh in range(RET_HEADS):
                st[:, hh * 128:(hh + 1) * 128] = st_new[hh]

    rf, rb = _ret_rows(nb, False), _ret_rows(nb, True)
    cspecs = [_const_spec(a.shape) for a in (hm, dec_f, xi_f, zeta_f, gam_f, dec_b, xi_b, zeta_b, gam_b)]
    st_shape = jax.ShapeDtypeStruct((nb, RET_QK_W, RET_W), F32)
    qc, kc, vc = OFF_RET_Q // RET_QK_W, OFF_RET_K // RET_QK_W, OFF_RET_V // RET_W
    outs, carried = _hosted(
        "ret_scan_fwd", body, rider, grid=(nb,),
        in_specs=[rf(RET_QK_W, qc), rf(RET_QK_W, kc), rf(RET_W, vc), rf(RET_QK_W, 0), rf(RET_QK_W, 0),
                  rb(RET_QK_W, qc), rb(RET_QK_W, kc), rb(RET_W, vc), rb(RET_QK_W, 0), rb(RET_QK_W, 0)] + cspecs,
        out_specs=[rf(RET_W, 0), rb(RET_W, 0),
                   pl.BlockSpec((1, RET_QK_W, RET_W), lambda t: (t, 0, 0)),
                   pl.BlockSpec((1, RET_QK_W, RET_W), lambda t: (nb - 1 - t, 0, 0))],
        out_shape=[jax.ShapeDtypeStruct((S, RET_W), F32), jax.ShapeDtypeStruct((S, RET_W), F32), st_shape, st_shape],
        scratch_shapes=[pltpu.VMEM((RET_QK_W, RET_W), F32), pltpu.VMEM((RET_QK_W, RET_W), F32)],
        sem=("arbitrary",),
        operands=(h, h, h, cos_t, sin_t, h, h, h, cos_t, sin_t, hm, dec_f, xi_f, zeta_f, gam_f, dec_b, xi_b, zeta_b,
                  gam_b))
    return (tuple(outs), carried) if rider else tuple(outs)


def ret_scan_bwd(h, consts, do, sf, sb):
    S = h.shape[0]
    nb = S // RET_T
    n = RET_T // RET_CHUNK
    cos_t, sin_t, hm, dec_f, xi_f, zeta_f, gam_f, dec_b, xi_b, zeta_b, gam_b = consts

    def body(qf, kf, vf, cf, sf_, dof, sfin, qb, kb, vb, cb, sb_, dob, sbin,
             hm_ref, decf, xif, zetaf, gamf, decb, xib, zetab, gamb,
             dqf, dkf, dvf, dqb, dkb, dvb, dsf, dsb):
        @pl.when(pl.program_id(0) == 0)
        def _():
            dsf[...] = jnp.zeros_like(dsf)
            dsb[...] = jnp.zeros_like(dsb)

        for (q, k, v, cs, sn, dor, sin, dr, xr, zr, gr, dq, dk, dv, ds, rev) in (
                (qf, kf, vf, cf, sf_, dof, sfin, decf, xif, zetaf, gamf, dqf, dkf, dvf, dsf, False),
                (qb, kb, vb, cb, sb_, dob, sbin, decb, xib, zetab, gamb, dqb, dkb, dvb, dsb, True)):
            qs, ks, vs = _ret_inputs(q, k, v, cs, sn)
            dec, xi, zeta, gam, hms = _ret_dir_consts(dr, xr, zr, gr, hm_ref)
            st_in = [sin[0, :, hh * 128:(hh + 1) * 128] for hh in range(RET_HEADS)]
            fn = lambda a, b_, c_, d_: _ret_block(a, b_, c_, d_, dec, xi, zeta, gam, hms, rev)
            _, vjp = jax.vjp(fn, qs, ks, vs, st_in)
            dos = [[dor[c * RET_CHUNK:(c + 1) * RET_CHUNK, hh * 128:(hh + 1) * 128] for hh in range(RET_HEADS)]
                   for c in range(n)]
            dst = [ds[:, hh * 128:(hh + 1) * 128] for hh in range(RET_HEADS)]
            dqs, dks, dvs, dst_in = vjp((dos, dst))
            cosv, sinv = cs[...], sn[...]
            dq[...] = _rope(jnp.concatenate(dqs, axis=0), cosv, -sinv)
            dk[...] = _rope(jnp.concatenate(dks, axis=0) * (RET_DK ** -0.5), cosv, -sinv)
            for c in range(n):
                for hh in range(RET_HEADS):
                    dv[c * RET_CHUNK:(c + 1) * RET_CHUNK, hh * 128:(hh + 1) * 128] = dvs[c][hh]
            for hh in range(RET_HEADS):
                ds[:, hh * 128:(hh + 1) * 128] = dst_in[hh]

    rf, rb = _ret_rows(nb, True), _ret_rows(nb, False)
    cspecs = [_const_spec(a.shape) for a in (hm, dec_f, xi_f, zeta_f, gam_f, dec_b, xi_b, zeta_b, gam_b)]
    qk = jax.ShapeDtypeStruct((S, RET_QK_W), F32)
    vv = jax.ShapeDtypeStruct((S, RET_W), F32)
    qc, kc, vc = OFF_RET_Q // RET_QK_W, OFF_RET_K // RET_QK_W, OFF_RET_V // RET_W
    return pl.pallas_call(
        body, name="ret_scan_bwd", grid=(nb,),
        in_specs=[rf(RET_QK_W, qc), rf(RET_QK_W, kc), rf(RET_W, vc), rf(RET_QK_W, 0), rf(RET_QK_W, 0), rf(RET_W, 0),
                  pl.BlockSpec((1, RET_QK_W, RET_W), lambda t: (nb - 1 - t, 0, 0)),
                  rb(RET_QK_W, qc), rb(RET_QK_W, kc), rb(RET_W, vc), rb(RET_QK_W, 0), rb(RET_QK_W, 0), rb(RET_W, 0),
                  pl.BlockSpec((1, RET_QK_W, RET_W), lambda t: (t, 0, 0))] + cspecs,
        out_specs=(rf(RET_QK_W, 0), rf(RET_QK_W, 0), rf(RET_W, 0), rb(RET_QK_W, 0), rb(RET_QK_W, 0), rb(RET_W, 0)),
        out_shape=(qk, qk, vv, qk, qk, vv),
        scratch_shapes=[pltpu.VMEM((RET_QK_W, RET_W), F32), pltpu.VMEM((RET_QK_W, RET_W), F32)],
        compiler_params=_cp("arbitrary"),
    )(h, h, h, cos_t, sin_t, do, sf, h, h, h, cos_t, sin_t, do, sb,
      hm, dec_f, xi_f, zeta_f, gam_f, dec_b, xi_b, zeta_b, gam_b)


def _t5_bucket(rel):
    nb = REL_BUCKETS // 2
    max_exact = nb // 2
    sign_off = jnp.where(rel > 0, nb, 0)
    n = jnp.abs(rel)
    nf = jnp.maximum(n, 1).astype(F32)
    large = max_exact + (jnp.log(nf / max_exact) / math.log(REL_MAX_DIST / max_exact)
                         * (nb - max_exact)).astype(jnp.int32)
    large = jnp.minimum(large, nb - 1)
    return sign_off + jnp.where(n < max_exact, n, large)


def _dil_buckets(dil):
    tq, tb = DIL_TQ, DIL_TQ + 2 * DIL_HALF
    rel_q = jnp.arange(tb)[None, :] - DIL_HALF - jnp.arange(tq)[:, None]
    rel_k = jnp.arange(tq)[None, :] + DIL_HALF - jnp.arange(tb)[:, None]
    return _t5_bucket(rel_q * dil), _t5_bucket(rel_k * dil)


def dil_view(h, g, dil):
    base = OFF_DIL + 3 * g * DIL_W
    if dil == 1:
        return h, IN_W, base
    return h[:, base:base + 3 * DIL_W].reshape(h.shape[0] // dil, dil * 3 * DIL_W), 3 * DIL_W, 0


def _dil_col(view, j):
    _, width, base = view
    return lambda r: (r * width + base + j * DIL_W) // DIL_W


def _dil_specs(L):
    nq = DIL_TQ // DIL_HALF
    last = L // DIL_HALF - 1

    def cur(colfn):
        return pl.BlockSpec((DIL_TQ, DIL_W), lambda r, n: (n, colfn(r)))

    def prev(colfn):
        return pl.BlockSpec((DIL_HALF, DIL_W), lambda r, n: (jnp.maximum(n * nq - 1, 0), colfn(r)))

    def nxt(colfn):
        return pl.BlockSpec((DIL_HALF, DIL_W), lambda r, n: (jnp.minimum((n + 1) * nq, last), colfn(r)))

    return prev, cur, nxt


def _slot(s):
    return slice(s * DIL_HD, (s + 1) * DIL_HD)


def _cat3(a, b, c, s):
    return jnp.concatenate([a[:, _slot(s)], b[:, _slot(s)], c[:, _slot(s)]], axis=0)


def dil_attn_fwd(view, S, g, dil, bias, qg, kg):
    L = S // dil
    hv = view[0]
    tb = DIL_TQ + 2 * DIL_HALF

    def body(q_ref, kp, kc, kn, vp, vc, vn, bias_ref, qg_ref, kg_ref, o_ref, lse_ref):
        n = pl.program_id(1)
        ii = lax.broadcasted_iota(jnp.int32, (DIL_TQ, tb), 0)
        jj = lax.broadcasted_iota(jnp.int32, (DIL_TQ, tb), 1)
        kabs = n * DIL_TQ - DIL_HALF + jj
        valid = (jnp.abs(jj - DIL_HALF - ii) <= DIL_HALF) & (kabs >= 0) & (kabs < L)
        for s in range(DIL_SLOTS):
            q = _head_rms(q_ref[:, _slot(s)], qg_ref[...]) * (DIL_HD ** -0.5)
            kb = _head_rms(_cat3(kp, kc, kn, s), kg_ref[...])
            sc = jnp.where(valid, _mxu(q, kb, NT) + bias_ref[s], NEG)
            m = jnp.max(sc, axis=-1, keepdims=True)
            p = jnp.exp(sc - m)
            den = jnp.sum(p, axis=-1, keepdims=True)
            o_ref[:, _slot(s)] = _mxu(p, _cat3(vp, vc, vn, s), NN) / den
            lse_ref[:, _slot(s)] = jnp.broadcast_to(m + jnp.log(den), (DIL_TQ, DIL_HD))

    prev, cur, nxt = _dil_specs(L)
    qc, kc_, vc_ = (_dil_col(view, j) for j in range(3))
    oc = lambda r: r
    vec = pl.BlockSpec((1, 128), lambda r, n: (0, 0))
    out = jax.ShapeDtypeStruct((L, dil * DIL_W), F32)
    o, lse = pl.pallas_call(
        body, name=f"dil_attn_fwd{g}", grid=(dil, L // DIL_TQ),
        in_specs=[cur(qc), prev(kc_), cur(kc_), nxt(kc_), prev(vc_), cur(vc_), nxt(vc_),
                  pl.BlockSpec((DIL_SLOTS, DIL_TQ, tb), lambda r, n: (0, 0, 0)), vec, vec],
        out_specs=(cur(oc), cur(oc)), out_shape=(out, out),
        compiler_params=_cp("parallel", "parallel"),
    )(hv, hv, hv, hv, hv, hv, hv, bias, qg, kg)
    return o.reshape(S, DIL_W), lse.reshape(S, DIL_W)


def dil_combine(os_, lses):
    S = os_[0].shape[0]

    def body(o1, o2, o3, l1, l2, l3, y_ref, lt_ref):
        a, b, c = l1[...], l2[...], l3[...]
        m = jnp.maximum(jnp.maximum(a, b), c)
        ea, eb, ec = jnp.exp(a - m), jnp.exp(b - m), jnp.exp(c - m)
        den = ea + eb + ec
        y_ref[...] = (ea * o1[...] + eb * o2[...] + ec * o3[...]) / den
        lt_ref[...] = m + jnp.log(den)

    blk = pl.BlockSpec((GN_T, DIL_W), lambda i: (i, 0))
    out = jax.ShapeDtypeStruct((S, DIL_W), F32)
    return pl.pallas_call(
        body, name="dil_combine", grid=(S // GN_T,), in_specs=[blk] * 6, out_specs=(blk, blk),
        out_shape=(out, out), compiler_params=_cp("parallel"),
    )(*os_, *lses)


def dil_delta(dy, yc):
    S = yc.shape[0]

    def body(dy_ref, y_ref, d_ref):
        d_ref[...] = jnp.broadcast_to(jnp.sum(dy_ref[...] * y_ref[...], axis=-1, keepdims=True), (GN_T, 128))

    return pl.pallas_call(
        body, name="dil_delta", grid=(S // GN_T, DIL_SLOTS),
        in_specs=[pl.BlockSpec((GN_T, 128), lambda i, s: (i, (HG_W + RET_W) // 128 + s)),
                  pl.BlockSpec((GN_T, 128), lambda i, s: (i, s))],
        out_specs=pl.BlockSpec((GN_T, 128), lambda i, s: (i, s)),
        out_shape=jax.ShapeDtypeStruct((S, DIL_W), F32), compiler_params=_cp("parallel", "parallel"),
    )(dy, yc)


def dil_attn_bwd(view, S, g, dil, bias_q, bias_k, qg, kg, dy, lse_t, delta):
    L = S // dil
    hv = view[0]
    if dil == 1:
        dyv, dyc = dy, lambda r: (HG_W + RET_W) // DIL_W
    else:
        dyv, dyc = dy[:, HG_W + RET_W:].reshape(L, dil * DIL_W), lambda r: r
    lv = lse_t.reshape(L, dil * DIL_W)
    dv_ = delta.reshape(L, dil * DIL_W)
    tq, tb = DIL_TQ, DIL_TQ + 2 * DIL_HALF
    scale = DIL_HD ** -0.5

    def body(qp, qc, qn, kp, kc, kn, vp, vc, vn, dp_, dc, dn, lp, lc, ln, ep, ec, en, bq_ref, bk_ref, qg_ref, kg_ref,
             dq_ref, dk_ref, dv_ref, dbias_ref, dqg_ref, dkg_ref):
        r, n = pl.program_id(0), pl.program_id(1)

        @pl.when((r == 0) & (n == 0))
        def _():
            for ref in (dbias_ref, dqg_ref, dkg_ref):
                ref[...] = jnp.zeros_like(ref)

        qgv, kgv = qg_ref[...], kg_ref[...]
        qfn = lambda t, gg: _head_rms(t, gg) * scale
        ii = lax.broadcasted_iota(jnp.int32, (tq, tb), 0)
        jj = lax.broadcasted_iota(jnp.int32, (tq, tb), 1)
        kabs = n * tq - DIL_HALF + jj
        valid = (jnp.abs(jj - DIL_HALF - ii) <= DIL_HALF) & (kabs >= 0) & (kabs < L)
        i2 = lax.broadcasted_iota(jnp.int32, (tb, tq), 0)
        j2 = lax.broadcasted_iota(jnp.int32, (tb, tq), 1)
        qabs = n * tq - DIL_HALF + i2
        valid2 = (jnp.abs(j2 + DIL_HALF - i2) <= DIL_HALF) & (qabs >= 0) & (qabs < L)
        for s in range(DIL_SLOTS):
            sl = _slot(s)
            one = slice(s * DIL_HD, s * DIL_HD + 1)
            qn_c, q_vjp = jax.vjp(qfn, qc[:, sl], qgv)
            k_band = _head_rms(_cat3(kp, kc, kn, s), kgv)
            sc = _mxu(qn_c, k_band, NT) + bq_ref[s]
            p = jnp.where(valid, jnp.exp(jnp.where(valid, sc, NEG) - lc[:, one]), 0.0)
            ds = p * (_mxu(dc[:, sl], _cat3(vp, vc, vn, s), NT) - ec[:, one])
            dbias_ref[s] += ds
            dq, dqg = q_vjp(_mxu(ds, k_band, NN))
            dq_ref[:, sl] = dq
            dqg_ref[s] += dqg
            kn_c, k_vjp = jax.vjp(_head_rms, kc[:, sl], kgv)
            q_band = qfn(_cat3(qp, qc, qn, s), qgv)
            do_band = _cat3(dp_, dc, dn, s)
            s2 = _mxu(q_band, kn_c, NT) + bk_ref[s]
            lse_band = jnp.concatenate([lp[:, one], lc[:, one], ln[:, one]], axis=0)
            delta_band = jnp.concatenate([ep[:, one], ec[:, one], en[:, one]], axis=0)
            p2 = jnp.where(valid2, jnp.exp(jnp.where(valid2, s2, NEG) - lse_band), 0.0)
            dv_ref[:, sl] = _mxu(p2, do_band, TN)
            ds2 = p2 * (_mxu(do_band, vc[:, sl], NT) - delta_band)
            dk, dkg = k_vjp(_mxu(ds2, q_band, TN))
            dk_ref[:, sl] = dk
            dkg_ref[s] += dkg

    prev, cur, nxt = _dil_specs(L)
    three = lambda colfn: [prev(colfn), cur(colfn), nxt(colfn)]
    qc_, kc_, vc_ = (_dil_col(view, j) for j in range(3))
    oc = lambda r: r
    vec = pl.BlockSpec((1, 128), lambda r, n: (0, 0))
    acc_vec = pl.BlockSpec((DIL_SLOTS, 1, 128), lambda r, n: (0, 0, 0))
    out = jax.ShapeDtypeStruct((L, dil * DIL_W), F32)
    dq, dk, dv, dbias, dqg, dkg = pl.pallas_call(
        body, name=f"dil_attn_bwd{g}", grid=(dil, L // tq),
        in_specs=three(qc_) + three(kc_) + three(vc_) + three(dyc) + three(oc) + three(oc)
        + [pl.BlockSpec((DIL_SLOTS, tq, tb), lambda r, n: (0, 0, 0)),
           pl.BlockSpec((DIL_SLOTS, tb, tq), lambda r, n: (0, 0, 0)), vec, vec],
        out_specs=(cur(oc), cur(oc), cur(oc), pl.BlockSpec((DIL_SLOTS, tq, tb), lambda r, n: (0, 0, 0)),
                   acc_vec, acc_vec),
        out_shape=(out, out, out, jax.ShapeDtypeStruct((DIL_SLOTS, tq, tb), F32),
                   jax.ShapeDtypeStruct((DIL_SLOTS, 1, 128), F32), jax.ShapeDtypeStruct((DIL_SLOTS, 1, 128), F32)),
        compiler_params=_cp("arbitrary", "arbitrary"),
    )(hv, hv, hv, hv, hv, hv, hv, hv, hv, dyv, dyv, dyv, lv, lv, lv, dv_, dv_, dv_, bias_q, bias_k, qg, kg)
    return dq.reshape(S, DIL_W), dk.reshape(S, DIL_W), dv.reshape(S, DIL_W), dbias, dqg, dkg


def _lb_eff(p):
    a = jnp.cumsum(jax.nn.softmax(p.astype(F32), axis=0), axis=0)
    return a - a[0:1]


def _dil_bias(rel_bias, g, dil):
    tbl = rel_bias[:, g * DIL_SLOTS:(g + 1) * DIL_SLOTS]
    return tuple(jnp.einsum("ijb,bs->sij", jax.nn.one_hot(b, REL_BUCKETS, dtype=F32), tbl,
                            precision=lax.Precision.HIGHEST) for b in _dil_buckets(dil))


def _big_weights(w):
    return w[0], w[1].reshape(D_MODEL, D_MODEL), w[2], w[3].reshape(D_FF, D_MODEL)


def _layer_fwd(x, l, prm, wts, rc, biases, gatherer=None):
    def carrying(stage, call, *args):
        rider = gatherer.rider(l, stage) if gatherer else None
        if rider is None:
            return call(*args)
        res, arrays = call(*args, rider)
        gatherer.done(l, stage, list(arrays))
        return res

    win_g, wout_g, wup_g, wdown_g = _big_weights(wts[l])
    row = lambda a: a[l][None]
    xn = rmsnorm_fwd(x, row(prm["norm_mix"]))
    h = carrying(0, proj_in, xn, win_g)
    hof, hob, hsf, hsb = carrying(1, hg_scan_fwd, h, row(prm["lbf"]), row(prm["lbb"]))
    ya = gated_norm_fwd("hg_out", hof, hob, h, OFF_HG_GATE, row(prm["hg_norm"]), False)
    rof, rob, rsf, rsb = carrying(2, ret_scan_fwd, h, rc)
    yb = gated_norm_fwd("ret_out", rof, rob, h, OFF_RET_GATE, row(prm["ret_norm"]), True)
    os_, lses, views = [], [], []
    for g, (_, dil) in enumerate(DIL_GROUPS):
        views.append(dil_view(h, g, dil))
        o, lse = dil_attn_fwd(views[g], h.shape[0], g, dil, biases[g][0], row(prm["q_norm"]), row(prm["k_norm"]))
        os_.append(o)
        lses.append(lse)
    yc, lse_t = dil_combine(os_, lses)
    y = jnp.concatenate([ya, yb, yc], axis=1).astype(BF16)
    x2 = proj_out(y, wout_g, x)
    hm = rmsnorm_fwd(x2, row(prm["norm_mlp"]))
    u, act = proj_up(hm, wup_g)
    x3 = proj_down(act, wdown_g, x2)
    saved = dict(x=x, xn=xn, h=h, hof=hof, hob=hob, hsf=hsf, hsb=hsb, rof=rof, rob=rob, rsf=rsf, rsb=rsb,
                 yc=yc, lse_t=lse_t, y=y, x2=x2, hm=hm, u=u, act=act, views=views)
    return x3, saved


def _layer_bwd(dx3, l, prm, wts, rc, biases, sv, reducer=None):
    win_g, wout_g, wup_g, wdown_g = _big_weights(wts[l])
    row = lambda a: a[l][None]
    h = sv["h"]
    rider = reducer.pair_rider() if reducer else None
    du = bwd_down_act(dx3, wdown_g, sv["u"], rider)
    if rider:
        du, got = du
        reducer.pair_done(got)
    g_down = wgrad("wgrad_down", sv["act"], dx3, m=D_FF, n=D_MODEL, n_shard=D_MODEL)
    dhm = bwd_up(du, wup_g)
    g_up = wgrad("wgrad_up", sv["hm"], du, m=D_MODEL, n=D_FF, n_shard=FF_SHARD)
    dx2, dg_mlp = rmsnorm_bwd(sv["x2"], row(prm["norm_mlp"]), dhm, dx3)
    dy = bwd_out(dx2, wout_g)
    g_out = wgrad("wgrad_out", sv["y"], dx2, m=D_MODEL, n=D_MODEL, n_shard=D_MODEL)
    hdo, hdgate, dg_hg = gated_norm_bwd("hg_out_bwd", sv["hof"], sv["hob"], h, OFF_HG_GATE, row(prm["hg_norm"]),
                                        dy, 0, False)
    rider = reducer.chip_rider() if reducer else None
    hg_grads = hg_scan_bwd(h, row(prm["lbf"]), row(prm["lbb"]), hdo, sv["hsf"], sv["hsb"], rider)
    if rider:
        hg_grads, arrived = hg_grads
        reducer.chip_done(arrived)
    hdqf, hdvf, hdzf, hdqb, hdvb, hdzb, dlbf, dlbb = hg_grads
    rdo, rdgate, dg_ret = gated_norm_bwd("ret_out_bwd", sv["rof"], sv["rob"], h, OFF_RET_GATE, row(prm["ret_norm"]),
                                         dy, HG_W, True)
    rdqf, rdkf, rdvf, rdqb, rdkb, rdvb = ret_scan_bwd(h, rc, rdo, sv["rsf"], sv["rsb"])
    delta = dil_delta(dy, sv["yc"])
    dil_parts, dbiases = [], []
    dqg = jnp.zeros((1, DIL_HD), F32)
    dkg = jnp.zeros((1, DIL_HD), F32)
    for g, (_, dil) in enumerate(DIL_GROUPS):
        dq, dk, dv, dbias, dqg_g, dkg_g = dil_attn_bwd(sv["views"][g], h.shape[0], g, dil, biases[g][0], biases[g][1],
                                                       row(prm["q_norm"]), row(prm["k_norm"]), dy, sv["lse_t"], delta)
        dil_parts += [dq, dk, dv]
        dbiases.append(dbias)
        dqg = dqg + jnp.sum(dqg_g, axis=0)
        dkg = dkg + jnp.sum(dkg_g, axis=0)
    dh = jnp.concatenate([hdqf + hdqb, hdvf + hdvb, hdzf, hdzb, hdgate,
                          rdqf + rdqb, rdkf + rdkb, rdvf + rdvb, rdgate] + dil_parts, axis=1).astype(BF16)
    dxn = bwd_in(dh, win_g)
    g_in = wgrad("wgrad_in", sv["xn"], dh, m=D_MODEL, n=IN_W, n_shard=IN_SHARD)
    dx, dg_mix = rmsnorm_bwd(sv["x"], row(prm["norm_mix"]), dxn, dx2)
    big = (g_in, g_out.reshape(N_CHIPS, D_MODEL // N_CHIPS, D_MODEL), g_up,
           g_down.reshape(N_CHIPS, D_FF // N_CHIPS, D_MODEL))
    small = dict(norm_mix=dg_mix, norm_mlp=dg_mlp, lbf=dlbf, lbb=dlbb, hg_norm=dg_hg, ret_norm=dg_ret,
                 q_norm=dqg, k_norm=dkg)
    if reducer:
        reducer.push(l, big)
    return dx, big, small, dbiases


def _rel_bias_grad(dbias_layers):
    cols = []
    for g, (_, dil) in enumerate(DIL_GROUPS):
        bq, _ = _dil_buckets(dil)
        onehot = jax.nn.one_hot(bq, REL_BUCKETS, dtype=F32)
        tot = dbias_layers[0][g]
        for d in dbias_layers[1:]:
            tot = tot + d[g]
        cols.append(jnp.einsum("sij,ijb->bs", tot, onehot, precision=lax.Precision.HIGHEST))
    return jnp.concatenate(cols, axis=1)


def local_step(x, tgt, wts, prm_in, reducer=None, gatherer=None):
    S = x.shape[0]
    prm = dict(prm_in)
    prm["lbf"], lbf_vjp = jax.vjp(_lb_eff, prm_in["hg_lb_fwd"])
    prm["lbb"], lbb_vjp = jax.vjp(_lb_eff, prm_in["hg_lb_bwd"])
    rc = _ret_consts(S)
    biases = [_dil_bias(prm["rel_bias"], g, dil) for g, (_, dil) in enumerate(DIL_GROUPS)]
    saved = []
    for l in range(DEPTH):
        x, sv = _layer_fwd(x, l, prm, wts, rc, biases, gatherer)
        saved.append(sv)
    dx, loss_row = loss_head(x, tgt)
    big, small, dbias_layers = [None] * DEPTH, [None] * DEPTH, [None] * DEPTH
    for l in range(DEPTH - 1, -1, -1):
        dx, big[l], small[l], dbias_layers[l] = _layer_bwd(dx, l, prm, wts, rc, biases, saved[l], reducer)
    sg = {k: jnp.concatenate([small[l][k] for l in range(DEPTH)], axis=0) for k in small[0]}
    sg["rel_bias"] = _rel_bias_grad(dbias_layers)
    return loss_row[0, 0], dx, (reducer.finish() if reducer else big), sg, (lbf_vjp, lbb_vjp)


def _place():
    x, y, c = lax.axis_index("x"), lax.axis_index("y"), lax.axis_index("c")
    rels = [(1 - x, y), (x, 1 - y), (1 - x, 1 - y)]
    return x, y, c, 2 * x + y, rels


def _half(c, rows):
    return pl.ds(pl.multiple_of(c * (rows // 2), 16), rows // 2)


def place_own(name, p_arr, w, l):
    _, rows, cols = w.shape
    tr = 512

    def body(p_ref, w_ref, o_ref):
        o_ref[...] = w_ref[...].astype(BF16)

    return pl.pallas_call(
        body, name=name,
        grid_spec=pltpu.PrefetchScalarGridSpec(
            num_scalar_prefetch=1, grid=(rows // tr,),
            in_specs=[pl.BlockSpec((1, tr, cols), lambda i, p: (l, i, 0))],
            out_specs=pl.BlockSpec((1, tr, cols), lambda i, p: (p[0], i, 0))),
        out_shape=jax.ShapeDtypeStruct((N_CHIPS, rows, cols), BF16),
        compiler_params=_cp("parallel"),
    )(p_arr, w)


class WeightGatherer:
    def __init__(self, big_w):
        p_arr = (2 * lax.axis_index("x") + lax.axis_index("y")).astype(jnp.int32).reshape(1)
        self.w = [[place_own(f"place_own{t}", p_arr, w, l) for t, w in enumerate(big_w)] for l in range(DEPTH)]
        self.w[0] = list(run_alone("gather_first_chips", gather_ici_rider(self.w[0])))
        self.w[0] = list(run_alone("gather_first_cores", gather_pass_rider(self.w[0])))

    def rider(self, l, stage):
        if l + 1 == DEPTH:
            return None
        nxt = self.w[l + 1]
        if stage == 0:
            return gather_ici_rider(nxt[0:2])
        if stage == 1:
            return gather_pass_rider(nxt[0:2]) + gather_ici_rider(nxt[2:4])
        return gather_pass_rider(nxt[2:4])

    def done(self, l, stage, arrays):
        nxt = self.w[l + 1]
        if stage == 0:
            nxt[0:2] = arrays
        elif stage == 1:
            nxt[0:4] = arrays
        else:
            nxt[2:4] = arrays


def run_alone(name, rider):
    n_in, n_out = len(rider.arrays), len(rider.out_shapes)

    def body(*refs):
        ins, outs, sems = refs[:n_in], refs[n_in:n_in + n_out], refs[n_in + n_out:]
        rider.start(ins, outs, sems)
        rider.finish(ins, outs, sems)

    return pl.pallas_call(
        body, name=name, in_specs=[ANY] * n_in, out_specs=[ANY] * n_out, out_shape=rider.out_shapes,
        scratch_shapes=[pltpu.SemaphoreType.DMA((n,)) for n in rider.sems],
        input_output_aliases=rider.aliases(0, 0),
    )(*rider.arrays)


def _both(cp):
    return (cp, cp)


def pair_exchange_rider(gs):
    n = len(gs)

    def ops(ins, outs, ssem, rsem):
        x, y, c, _, _ = _place()
        return [_both(pltpu.make_async_remote_copy(
            src_ref=ins[i].at[:, _half(1 - c, ins[i].shape[1]), :], dst_ref=outs[i],
            send_sem=ssem.at[i], recv_sem=rsem.at[i], device_id=(x, y, 1 - c), device_id_type=MESH)) for i in range(n)]

    return Rider(gs, [jax.ShapeDtypeStruct((N_CHIPS, g.shape[1] // 2, g.shape[2]), F32) for g in gs], [n, n], ops)


def gather_ici_rider(bufs):
    n = len(bufs)

    def ops(ins, outs, ssem, rsem):
        x, y, c, p, rels = _place()
        cps = []
        for i in range(n):
            mine = _half(c, outs[i].shape[1])
            for r, (rx, ry) in enumerate(rels):
                k = i * 3 + r
                peer = dict(device_id=(rx, ry, c), device_id_type=MESH, send_sem=ssem.at[k], recv_sem=rsem.at[k])
                own, landing = outs[i].at[p, mine], outs[i].at[2 * rx + ry, mine]
                cps.append((pltpu.make_async_remote_copy(src_ref=own, dst_ref=own, **peer),
                            pltpu.make_async_remote_copy(src_ref=landing, dst_ref=landing, **peer)))
        return cps

    return Rider(bufs, None, [3 * n, 3 * n], ops)


def gather_pass_rider(bufs):
    n = len(bufs)

    def ops(ins, outs, ssem, rsem):
        x, y, c, p, rels = _place()
        cps = []
        for i in range(n):
            rows = outs[i].shape[1]
            for r, (rx, ry) in enumerate(rels):
                k = i * 3 + r
                peer = dict(device_id=(x, y, 1 - c), device_id_type=MESH, send_sem=ssem.at[k], recv_sem=rsem.at[k])
                landed, theirs = outs[i].at[2 * rx + ry, _half(c, rows)], outs[i].at[2 * rx + ry, _half(1 - c, rows)]
                cps.append((pltpu.make_async_remote_copy(src_ref=landed, dst_ref=landed, **peer),
                            pltpu.make_async_remote_copy(src_ref=theirs, dst_ref=theirs, **peer)))
        return cps

    return Rider(bufs, None, [3 * n, 3 * n], ops)


def pair_add(name, c_arr, g, got):
    _, rows, cols = g.shape
    hr = rows // 2
    tr = 256
    nblk = hr // tr

    def body(c_ref, g_ref, r_ref, o32, o16):
        s = g_ref[...] + r_ref[...]
        o32[...] = s
        o16[...] = s.astype(BF16)

    blk = pl.BlockSpec((1, tr, cols), lambda pp, i, c_ref: (pp, i, 0))
    return pl.pallas_call(
        body, name=name,
        grid_spec=pltpu.PrefetchScalarGridSpec(
            num_scalar_prefetch=1, grid=(N_CHIPS, nblk),
            in_specs=[pl.BlockSpec((1, tr, cols), lambda pp, i, c_ref: (pp, c_ref[0] * nblk + i, 0)), blk],
            out_specs=(blk, blk)),
        out_shape=(jax.ShapeDtypeStruct((N_CHIPS, hr, cols), F32), jax.ShapeDtypeStruct((N_CHIPS, hr, cols), BF16)),
        compiler_params=_cp("parallel", "parallel"),
    )(c_arr, g, got)


def chip_exchange_rider(cs16):
    n = len(cs16)

    def ops(ins, outs, ssem, rsem):
        x, y, c, p, rels = _place()
        return [_both(pltpu.make_async_remote_copy(
            src_ref=ins[i].at[2 * rx + ry], dst_ref=outs[i].at[r], send_sem=ssem.at[i * 3 + r],
            recv_sem=rsem.at[i * 3 + r], device_id=(rx, ry, c), device_id_type=MESH))
            for i in range(n) for r, (rx, ry) in enumerate(rels)]

    return Rider(cs16, [jax.ShapeDtypeStruct((3,) + a.shape[1:], BF16) for a in cs16], [3 * n, 3 * n], ops)


def chip_sum(name, pc_arr, l, cs32, got, prev):
    _, hr, cols = cs32.shape
    tr = 256
    nblk = hr // tr

    def body(pc_ref, o_ref, g_ref, *rest):
        rest[-1][0] = ((o_ref[0] + g_ref[0].astype(F32)) + g_ref[1].astype(F32)) + g_ref[2].astype(F32)

    return pl.pallas_call(
        body, name=name,
        grid_spec=pltpu.PrefetchScalarGridSpec(
            num_scalar_prefetch=1, grid=(nblk,),
            in_specs=[pl.BlockSpec((1, tr, cols), lambda i, pc: (pc[0], i, 0)),
                      pl.BlockSpec((3, tr, cols), lambda i, pc: (0, i, 0))] + ([] if prev is None else [ANY]),
            out_specs=pl.BlockSpec((1, tr, cols), lambda i, pc: (l, pc[1] * nblk + i, 0))),
        out_shape=jax.ShapeDtypeStruct((DEPTH, 2 * hr, cols), F32),
        input_output_aliases={} if prev is None else {3: 0},
        compiler_params=_cp("arbitrary"),
    )(*((pc_arr, cs32, got) + (() if prev is None else (prev,))))


def grad_pair_share(halves):
    n_w = len(halves)
    n = n_w * DEPTH

    def body(*refs):
        bufs = refs[n_w:2 * n_w]
        ssem, rsem = refs[2 * n_w:]
        x, y, c, _, _ = _place()
        cps = []
        for t in range(n_w):
            for l in range(DEPTH):
                mine = bufs[t].at[l, _half(c, bufs[t].shape[1])]
                cp = pltpu.make_async_remote_copy(src_ref=mine, dst_ref=mine, send_sem=ssem.at[t * DEPTH + l],
                                                  recv_sem=rsem.at[t * DEPTH + l], device_id=(x, y, 1 - c),
                                                  device_id_type=MESH)
                cp.start()
                cps.append(cp)
        for t in range(n_w):
            for l in range(DEPTH):
                theirs = bufs[t].at[l, _half(1 - c, bufs[t].shape[1])]
                pltpu.make_async_remote_copy(src_ref=theirs, dst_ref=theirs, send_sem=ssem.at[t * DEPTH + l],
                                             recv_sem=rsem.at[t * DEPTH + l], device_id=(x, y, 1 - c),
                                             device_id_type=MESH).wait_recv()
        for cp in cps:
            cp.wait_send()

    return pl.pallas_call(
        body, name="grad_pair_share", in_specs=[ANY] * n_w, out_specs=[ANY] * n_w,
        out_shape=[jax.ShapeDtypeStruct(a.shape, F32) for a in halves],
        input_output_aliases={t: t for t in range(n_w)},
        scratch_shapes=[pltpu.SemaphoreType.DMA((n,)), pltpu.SemaphoreType.DMA((n,))],
    )(*halves)


SMALL_ROWS = 240


def small_allreduce(v):
    def body(v_ref, o_ref, buf, ssem, rsem):
        x, y, c, _, _ = _place()
        me = 4 * x + 2 * y + c
        buf[me] = v_ref[...]
        for d in range(N_DEV):
            @pl.when(me != d)
            def _():
                pltpu.make_async_remote_copy(
                    src_ref=v_ref, dst_ref=buf.at[me], send_sem=ssem.at[d], recv_sem=rsem.at[me],
                    device_id=(d // 4, (d // 2) % 2, d % 2), device_id_type=MESH).start()
        for d in range(N_DEV):
            @pl.when(me != d)
            def _():
                cp = pltpu.make_async_remote_copy(
                    src_ref=v_ref, dst_ref=buf.at[d], send_sem=ssem.at[d], recv_sem=rsem.at[d],
                    device_id=(d // 4, (d // 2) % 2, d % 2), device_id_type=MESH)
                cp.wait_recv()
                cp.wait_send()
        acc = buf[0]
        for d in range(1, N_DEV):
            acc = acc + buf[d]
        o_ref[...] = acc

    vm = pl.BlockSpec(memory_space=pltpu.VMEM)
    return pl.pallas_call(
        body, name="small_allreduce", in_specs=[vm], out_specs=vm,
        out_shape=jax.ShapeDtypeStruct(v.shape, F32),
        scratch_shapes=[pltpu.VMEM((N_DEV,) + v.shape, F32), pltpu.SemaphoreType.DMA((N_DEV,)),
                        pltpu.SemaphoreType.DMA((N_DEV,))],
    )(v)


class GradReducer:
    def __init__(self):
        self.c_arr = lax.axis_index("c").astype(jnp.int32).reshape(1)
        self.pc_arr = jnp.stack([2 * lax.axis_index("x") + lax.axis_index("y"), lax.axis_index("c")]).astype(jnp.int32)
        self.fresh = None
        self.paired = None
        self.acc = [None] * 4

    def push(self, l, gs):
        self.fresh = (l, list(gs))

    def pair_rider(self):
        return pair_exchange_rider(self.fresh[1]) if self.fresh else None

    def pair_done(self, got):
        l, gs = self.fresh
        self.fresh = None
        self.paired = (l, [pair_add(f"pair_add{t}", self.c_arr, g, r) for t, (g, r) in enumerate(zip(gs, got))])

    def chip_rider(self):
        return chip_exchange_rider([a[1] for a in self.paired[1]]) if self.paired else None

    def chip_done(self, arrived):
        l, cs = self.paired
        self.paired = None
        self.acc = [chip_sum(f"chip_sum{t}", self.pc_arr, l, cs[t][0], arrived[t], self.acc[t]) for t in range(4)]

    def finish(self):
        self.pair_done(run_alone("grad_pair_exchange", self.pair_rider()))
        self.chip_done(run_alone("grad_chip_exchange", self.chip_rider()))
        return grad_pair_share(self.acc)


def adamw(name, w, g, m, v):
    shape = w.shape
    cols = shape[-1]
    flat = [t.reshape(-1, cols) for t in (w, g, m, v)]
    rows = flat[0].shape[0]
    tr = 128 if rows % 128 == 0 else rows

    def body(w_ref, g_ref, m_ref, v_ref, d_ref, mo_ref, vo_ref):
        gv = g_ref[...]
        mn = ADAM_B1 * m_ref[...] + (1.0 - ADAM_B1) * gv
        vn = ADAM_B2 * v_ref[...] + (1.0 - ADAM_B2) * jnp.square(gv)
        m_hat = mn / (1.0 - ADAM_B1 ** ADAM_STEP)
        v_hat = vn / (1.0 - ADAM_B2 ** ADAM_STEP)
        d_ref[...] = -ADAM_LR * (m_hat / (jnp.sqrt(v_hat) + ADAM_EPS) + ADAM_WD * w_ref[...])
        mo_ref[...] = mn
        vo_ref[...] = vn

    blk = pl.BlockSpec((tr, cols), lambda i: (i, 0))
    out = jax.ShapeDtypeStruct((rows, cols), F32)
    d, mo, vo = pl.pallas_call(
        body, name=name, grid=(rows // tr,), in_specs=[blk] * 4, out_specs=(blk, blk, blk),
        out_shape=(out, out, out), compiler_params=_cp("parallel"),
    )(*flat)
    return d.reshape(shape), mo.reshape(shape), vo.reshape(shape)


SMALL_NAMES = ("norm_mix", "norm_mlp", "hg_lb_fwd", "hg_lb_bwd", "hg_norm", "ret_norm", "q_norm", "k_norm", "rel_bias")


def _pack_small(d):
    flat = jnp.concatenate([d[k].reshape(-1) for k in SMALL_NAMES])
    return jnp.pad(flat, (0, SMALL_ROWS * 128 - flat.shape[0])).reshape(SMALL_ROWS, 128)


def _unpack_small(v, like):
    flat = v.reshape(-1)
    out, off = {}, 0
    for k in SMALL_NAMES:
        n = like[k].size
        out[k] = flat[off:off + n].reshape(like[k].shape)
        off += n
    return out


def kernel(x, w_in, w_out, w_up, w_down, norm_mix, norm_mlp, hg_lb_fwd, hg_lb_bwd, hg_norm, ret_norm, q_norm, k_norm, rel_bias, loss_target, m_w_in, m_w_out, m_w_up, m_w_down, m_norm_mix, m_norm_mlp, m_hg_lb_fwd, m_hg_lb_bwd, m_hg_norm, m_ret_norm, m_q_norm, m_k_norm, m_rel_bias, v_w_in, v_w_out, v_w_up, v_w_down, v_norm_mix, v_norm_mlp, v_hg_lb_fwd, v_hg_lb_bwd, v_hg_norm, v_ret_norm, v_q_norm, v_k_norm, v_rel_bias):
    big_w = (w_in, w_out, w_up, w_down)
    big_m = (m_w_in, m_w_out, m_w_up, m_w_down)
    big_v = (v_w_in, v_w_out, v_w_up, v_w_down)
    small_w = dict(zip(SMALL_NAMES, (norm_mix, norm_mlp, hg_lb_fwd, hg_lb_bwd, hg_norm, ret_norm, q_norm, k_norm, rel_bias)))
    small_m = dict(zip(SMALL_NAMES, (m_norm_mix, m_norm_mlp, m_hg_lb_fwd, m_hg_lb_bwd, m_hg_norm, m_ret_norm, m_q_norm,
                                     m_k_norm, m_rel_bias)))
    small_v = dict(zip(SMALL_NAMES, (v_norm_mix, v_norm_mlp, v_hg_lb_fwd, v_hg_lb_bwd, v_hg_norm, v_ret_norm, v_q_norm,
                                     v_k_norm, v_rel_bias)))

    gatherer = WeightGatherer(big_w)
    loss_part, dx, grads_big, sg, (lbf_vjp, lbb_vjp) = local_step(x[0], loss_target[0], gatherer.w, small_w,
                                                                  GradReducer(), gatherer)
    loss = lax.psum(loss_part, ("x", "y", "c"))

    sg = dict(sg)
    sg["hg_lb_fwd"], sg["hg_lb_bwd"] = sg.pop("lbf"), sg.pop("lbb")
    tot = _unpack_small(small_allreduce(_pack_small(sg)), small_w)
    tot["hg_lb_fwd"] = lbf_vjp(tot["hg_lb_fwd"])[0]
    tot["hg_lb_bwd"] = lbb_vjp(tot["hg_lb_bwd"])[0]
    grads_small = [tot[k] for k in SMALL_NAMES]

    upd_big = [adamw(f"adamw_big{t}", big_w[t], grads_big[t], big_m[t], big_v[t]) for t in range(4)]
    d_s, m_s, v_s = adamw("adamw_small", _pack_small(small_w), _pack_small(tot), _pack_small(small_m), _pack_small(small_v))
    upd_small = [_unpack_small(t, small_w) for t in (d_s, m_s, v_s)]

    outs = [loss, dx[None]] + list(grads_big) + grads_small
    for j in range(3):
        outs += [u[j] for u in upd_big] + [upd_small[j][k] for k in SMALL_NAMES]
    return tuple(outs)
```

```python
import functools
import math

import jax
import jax.numpy as jnp
from jax import lax
from jax.experimental import pallas as pl
from jax.experimental.pallas import tpu as pltpu

F32 = jnp.float32
BF16 = jnp.bfloat16
EPS = 1e-6

D_MODEL = 2048
DEPTH = 4
HG_HEADS = 6
HG_W = 768
RET_HEADS = 6
RET_DK = 64
RET_W = 768
RET_QK_W = RET_HEADS * RET_DK
RET_CHUNK = 128
ROPE_BASE = 10000.0
DIL_SLOTS = 4
DIL_HD = 128
DIL_GROUPS = ((128, 1), (512, 4), (2048, 16))
DIL_HALF = 64
DIL_W = 512
D_FF = 4 * D_MODEL
IN_W = 10752
REL_BUCKETS = 32
REL_MAX_DIST = 1024

OFF_HG_Q, OFF_HG_V, OFF_HG_ZF, OFF_HG_ZB, OFF_HG_GATE = 0, 768, 1536, 2304, 3072
OFF_RET_Q, OFF_RET_K, OFF_RET_V, OFF_RET_GATE = 3840, 4224, 4608, 5376
OFF_DIL = 6144

N_CHIPS = 4
N_DEV = 8
IN_SHARD = IN_W // N_CHIPS
FF_SHARD = D_FF // N_CHIPS

ADAM_LR, ADAM_B1, ADAM_B2, ADAM_EPS, ADAM_WD, ADAM_STEP = 0.001, 0.9, 0.999, 1e-08, 0.01, 10

VMEM_LIMIT = 56 * 1024 * 1024
HG_T = 512
HG_C = 64
RET_T = 256
DIL_TQ = 256
NEG = -1e30

NN = (((1,), (0,)), ((), ()))
NT = (((1,), (1,)), ((), ()))
TN = (((0,), (0,)), ((), ()))
MESH = pl.DeviceIdType.MESH
ANY = pl.BlockSpec(memory_space=pl.ANY)


def _cp(*sem):
    return pltpu.CompilerParams(dimension_semantics=sem, vmem_limit_bytes=VMEM_LIMIT)


def _mxu(a, b, dn):
    return lax.dot_general(a.astype(BF16), b.astype(BF16), dn, preferred_element_type=F32)


@jax.custom_vjp
def dot_nn(a, b):
    return _mxu(a, b, NN)


dot_nn.defvjp(lambda a, b: (_mxu(a, b, NN), (a, b)),
              lambda r, g: (_mxu(g, r[1], NT), _mxu(r[0], g, TN)))


@jax.custom_vjp
def dot_nt(a, b):
    return _mxu(a, b, NT)


dot_nt.defvjp(lambda a, b: (_mxu(a, b, NT), (a, b)),
              lambda r, g: (_mxu(g, r[1], NN), _mxu(g, r[0], TN)))


@jax.custom_vjp
def dot_tn(a, b):
    return _mxu(a, b, TN)


dot_tn.defvjp(lambda a, b: (_mxu(a, b, TN), (a, b)),
              lambda r, g: (_mxu(r[1], g, NT), _mxu(r[0], g, NN)))


def _split3(v):
    hi = v.astype(BF16)
    r1 = v - hi.astype(F32)
    mid = r1.astype(BF16)
    lo = (r1 - mid.astype(F32)).astype(BF16)
    return hi, mid, lo


def _exact_mask_dot(m, v, dn):
    mb = m.astype(BF16)
    hi, mid, lo = _split3(v)
    f = lambda p: lax.dot_general(mb, p, dn, preferred_element_type=F32)
    return (f(lo) + f(mid)) + f(hi)


@jax.custom_vjp
def cumdot(m, v):
    return _exact_mask_dot(m, v, NN)


cumdot.defvjp(lambda m, v: (_exact_mask_dot(m, v, NN), m),
              lambda m, g: (jnp.zeros_like(m), _exact_mask_dot(m, g, TN)))


def _sigmoid(z):
    return 1.0 / (1.0 + jnp.exp(-z))


def _head_rms(t, g):
    return t * lax.rsqrt(jnp.mean(t * t, axis=-1, keepdims=True) + EPS) * g


class Rider:
    def __init__(self, arrays, out_shapes, sems, ops):
        self.arrays, self.sems, self.ops = list(arrays), list(sems), ops
        self.in_place = out_shapes is None
        self.out_shapes = [jax.ShapeDtypeStruct(a.shape, a.dtype) for a in arrays] if self.in_place else list(out_shapes)

    def aliases(self, n_in, n_out):
        return {n_in + i: n_out + i for i in range(len(self.arrays))} if self.in_place else {}

    def start(self, ins, outs, sems):
        for send, _ in self.ops(ins, outs, *sems):
            send.start()

    def finish(self, ins, outs, sems):
        cps = self.ops(ins, outs, *sems)
        for _, arrive in cps:
            arrive.wait_recv()
        for send, _ in cps:
            send.wait_send()

    def __add__(self, other):
        assert self.in_place and other.in_place
        na, sa = len(self.arrays), len(self.sems)
        ops = lambda ins, outs, *sems: (self.ops(ins[:na], outs[:na], *sems[:sa])
                                        + other.ops(ins[na:], outs[na:], *sems[sa:]))
        return Rider(self.arrays + other.arrays, None, self.sems + other.sems, ops)


def _hosted(name, body, rider, *, grid, in_specs, out_specs, out_shape, scratch_shapes, sem, operands):
    n_in, n_out, n_scr = len(in_specs), len(out_specs), len(scratch_shapes)
    r_in = len(rider.arrays) if rider else 0
    r_out = len(rider.out_shapes) if rider else 0
    last = tuple(g - 1 for g in grid)

    def kernel_body(*refs):
        ins, refs = refs[:n_in], refs[n_in:]
        rins, refs = refs[:r_in], refs[r_in:]
        outs, refs = refs[:n_out], refs[n_out:]
        routs, refs = refs[:r_out], refs[r_out:]
        scr, rsems = refs[:n_scr], refs[n_scr:]
        if rider:
            ids = [pl.program_id(d) for d in range(len(grid))]
            first = functools.reduce(lambda p, q: p & q, [i == 0 for i in ids])
            done = functools.reduce(lambda p, q: p & q, [i == e for i, e in zip(ids, last)])
            pl.when(first)(lambda: rider.start(rins, routs, rsems))
        body(ins, outs, scr)
        if rider:
            pl.when(done)(lambda: rider.finish(rins, routs, rsems))

    res = pl.pallas_call(
        kernel_body, name=name, grid=grid,
        in_specs=list(in_specs) + [ANY] * r_in,
        out_specs=list(out_specs) + [ANY] * r_out,
        out_shape=list(out_shape) + (rider.out_shapes if rider else []),
        scratch_shapes=list(scratch_shapes) + ([pltpu.SemaphoreType.DMA((n,)) for n in rider.sems] if rider else []),
        input_output_aliases=rider.aliases(n_in, n_out) if rider else {},
        compiler_params=_cp(*(("arbitrary",) * len(grid) if rider else sem)),
    )(*operands, *(rider.arrays if rider else []))
    return res[:n_out], res[n_out:]


def _mm(name, a, b, *, mode, grid, a_spec, b_spec, tm, tn, extras=(), extra_specs=(), epi, out_shape, out_specs,
        rider=None):
    nk = grid[2]
    single = not isinstance(out_shape, (tuple, list))
    if single:
        out_shape, out_specs = [out_shape], [out_specs]

    def body(ins, outs, scr):
        a_ref, b_ref, ex = ins[0], ins[1], ins[2:]
        part = _mxu(a_ref[...], b_ref[...], {"nn": NN, "nt": NT, "tn": TN}[mode])
        if nk == 1:
            epi(part, ex, outs)
            return
        acc = scr[0]
        k = pl.program_id(2)

        @pl.when(k == 0)
        def _():
            acc[...] = part

        @pl.when((k > 0) & (k < nk - 1))
        def _():
            acc[...] += part

        @pl.when(k == nk - 1)
        def _():
            epi(acc[...] + part, ex, outs)

    outs, carried = _hosted(name, body, rider, grid=grid, in_specs=[a_spec, b_spec, *extra_specs],
                            out_specs=out_specs, out_shape=out_shape,
                            scratch_shapes=[] if nk == 1 else [pltpu.VMEM((tm, tn), F32)],
                            sem=("parallel", "parallel", "arbitrary"), operands=(a, b, *extras))
    res = outs[0] if single else tuple(outs)
    return (res, carried) if rider else res


def _epi_store(acc, ex, outs):
    outs[0][...] = acc.astype(outs[0].dtype)


def _epi_residual(acc, ex, outs):
    outs[0][...] = ex[0][...] + acc


def _epi_up(acc, ex, outs):
    outs[0][...] = acc
    outs[1][...] = jnp.square(jnp.maximum(acc, 0.0)).astype(BF16)


def _epi_dact(acc, ex, outs):
    outs[0][...] = (acc * (2.0 * jnp.maximum(ex[0][...], 0.0))).astype(BF16)


def _ij(i, j, k):
    return (i, j)


def proj_in(xn, win_g, rider=None):
    S = xn.shape[0]
    tm, tn = 512, IN_SHARD
    return _mm("proj_in", xn, win_g, mode="nn", grid=(IN_W // tn, S // tm, 1), tm=tm, tn=tn,
               a_spec=pl.BlockSpec((tm, D_MODEL), lambda j, i, k: (i, 0)),
               b_spec=pl.BlockSpec((None, D_MODEL, tn), lambda j, i, k: (j, 0, 0)),
               epi=_epi_store, out_shape=jax.ShapeDtypeStruct((S, IN_W), F32),
               out_specs=pl.BlockSpec((tm, tn), lambda j, i, k: (i, j)), rider=rider)


def proj_out(y, wout_g, x):
    S = y.shape[0]
    tm, tn = 1024, 1024
    return _mm("proj_out", y, wout_g, mode="nn", grid=(S // tm, D_MODEL // tn, 1), tm=tm, tn=tn,
               a_spec=pl.BlockSpec((tm, D_MODEL), lambda i, j, k: (i, 0)),
               b_spec=pl.BlockSpec((D_MODEL, tn), lambda i, j, k: (0, j)),
               extras=(x,), extra_specs=(pl.BlockSpec((tm, tn), _ij),),
               epi=_epi_residual, out_shape=jax.ShapeDtypeStruct((S, D_MODEL), F32),
               out_specs=pl.BlockSpec((tm, tn), _ij))


def proj_up(hm, wup_g, rider=None):
    S = hm.shape[0]
    tm, tn = 1024, 1024
    return _mm("proj_up", hm, wup_g, mode="nn", grid=(S // tm, D_FF // tn, 1), tm=tm, tn=tn,
               a_spec=pl.BlockSpec((tm, D_MODEL), lambda i, j, k: (i, 0)),
               b_spec=pl.BlockSpec((None, D_MODEL, tn), lambda i, j, k: (j // 2, 0, j % 2)),
               epi=_epi_up,
               out_shape=(jax.ShapeDtypeStruct((S, D_FF), F32), jax.ShapeDtypeStruct((S, D_FF), BF16)),
               out_specs=(pl.BlockSpec((tm, tn), _ij), pl.BlockSpec((tm, tn), _ij)), rider=rider)


def proj_down(a, wdown_g, x, rider=None):
    S = a.shape[0]
    tm, tn, tk = 1024, 1024, 2048
    return _mm("proj_down", a, wdown_g, mode="nn", grid=(S // tm, D_MODEL // tn, D_FF // tk), tm=tm, tn=tn,
               a_spec=pl.BlockSpec((tm, tk), lambda i, j, k: (i, k)),
               b_spec=pl.BlockSpec((tk, tn), lambda i, j, k: (k, j)),
               extras=(x,), extra_specs=(pl.BlockSpec((tm, tn), _ij),),
               epi=_epi_residual, out_shape=jax.ShapeDtypeStruct((S, D_MODEL), F32),
               out_specs=pl.BlockSpec((tm, tn), _ij), rider=rider)


def bwd_down_act(dx, wdown_g, u, rider=None):
    S = dx.shape[0]
    tm, tn = 1024, 1024
    return _mm("bwd_down_act", dx, wdown_g, mode="nt", grid=(S // tm, D_FF // tn, 1), tm=tm, tn=tn,
               a_spec=pl.BlockSpec((tm, D_MODEL), lambda i, j, k: (i, 0)),
               b_spec=pl.BlockSpec((tn, D_MODEL), lambda i, j, k: (j, 0)),
               extras=(u,), extra_specs=(pl.BlockSpec((tm, tn), _ij),),
               epi=_epi_dact, out_shape=jax.ShapeDtypeStruct((S, D_FF), BF16),
               out_specs=pl.BlockSpec((tm, tn), _ij), rider=rider)


def bwd_up(du, wup_g):
    S = du.shape[0]
    tm, tn, tk = 1024, 1024, FF_SHARD
    return _mm("bwd_up", du, wup_g, mode="nt", grid=(S // tm, D_MODEL // tn, D_FF // tk), tm=tm, tn=tn,
               a_spec=pl.BlockSpec((tm, tk), lambda i, j, k: (i, k)),
               b_spec=pl.BlockSpec((None, tn, tk), lambda i, j, k: (k, j, 0)),
               epi=_epi_store, out_shape=jax.ShapeDtypeStruct((S, D_MODEL), F32),
               out_specs=pl.BlockSpec((tm, tn), _ij))


def bwd_out(dx, wout_g):
    S = dx.shape[0]
    tm, tn = 1024, 1024
    return _mm("bwd_out", dx, wout_g, mode="nt", grid=(S // tm, D_MODEL // tn, 1), tm=tm, tn=tn,
               a_spec=pl.BlockSpec((tm, D_MODEL), lambda i, j, k: (i, 0)),
               b_spec=pl.BlockSpec((tn, D_MODEL), lambda i, j, k: (j, 0)),
               epi=_epi_store, out_shape=jax.ShapeDtypeStruct((S, D_MODEL), F32),
               out_specs=pl.BlockSpec((tm, tn), _ij))


def bwd_in(dh, win_g):
    S = dh.shape[0]
    tm, tn, tk = 1024, 1024, IN_SHARD
    return _mm("bwd_in", dh, win_g, mode="nt", grid=(S // tm, D_MODEL // tn, IN_W // tk), tm=tm, tn=tn,
               a_spec=pl.BlockSpec((tm, tk), lambda i, j, k: (i, k)),
               b_spec=pl.BlockSpec((None, tn, tk), lambda i, j, k: (k, j, 0)),
               epi=_epi_store, out_shape=jax.ShapeDtypeStruct((S, D_MODEL), F32),
               out_specs=pl.BlockSpec((tm, tn), _ij))


def wgrad(name, a, g, *, m, n, n_shard):
    S = a.shape[0]
    tm, tn, tk = (512, IN_SHARD, 1024) if n_shard == IN_SHARD else (1024, 1024, 1024)
    per = n_shard // tn
    if n_shard == n:
        out_shape = jax.ShapeDtypeStruct((m, n), F32)
        out_spec = pl.BlockSpec((tm, tn), _ij)
    else:
        out_shape = jax.ShapeDtypeStruct((N_CHIPS, m, n_shard), F32)
        out_spec = pl.BlockSpec((None, tm, tn), lambda i, j, k: (j // per, i, j % per))
    return _mm(name, a, g, mode="tn", grid=(m // tm, n // tn, S // tk), tm=tm, tn=tn,
               a_spec=pl.BlockSpec((tk, tm), lambda i, j, k: (k, i)),
               b_spec=pl.BlockSpec((tk, tn), lambda i, j, k: (k, j)),
               epi=_epi_store, out_shape=out_shape, out_specs=out_spec)


NORM_T = 256


def rmsnorm_fwd(x, g):
    S = x.shape[0]

    def body(x_ref, g_ref, o_ref):
        xv = x_ref[...]
        r = lax.rsqrt(jnp.mean(xv * xv, axis=-1, keepdims=True) + EPS)
        o_ref[...] = ((xv * r) * g_ref[...]).astype(BF16)

    return pl.pallas_call(
        body, name="rmsnorm_fwd", grid=(S // NORM_T,),
        in_specs=[pl.BlockSpec((NORM_T, D_MODEL), lambda i: (i, 0)), pl.BlockSpec((1, D_MODEL), lambda i: (0, 0))],
        out_specs=pl.BlockSpec((NORM_T, D_MODEL), lambda i: (i, 0)),
        out_shape=jax.ShapeDtypeStruct((S, D_MODEL), BF16), compiler_params=_cp("parallel"),
    )(x, g)


def rmsnorm_bwd(x, g, dxn, dres):
    S = x.shape[0]

    def body(x_ref, g_ref, dxn_ref, dres_ref, dx_ref, dg_ref):
        @pl.when(pl.program_id(0) == 0)
        def _():
            dg_ref[...] = jnp.zeros_like(dg_ref)

        xv, gv, d = x_ref[...], g_ref[...], dxn_ref[...]
        r = lax.rsqrt(jnp.mean(xv * xv, axis=-1, keepdims=True) + EPS)
        gd = gv * d
        dx_ref[...] = dres_ref[...] + r * gd - xv * ((r * r * r) * jnp.mean(xv * gd, axis=-1, keepdims=True))
        dg_ref[...] += jnp.sum(d * (xv * r), axis=0, keepdims=True)

    row = pl.BlockSpec((NORM_T, D_MODEL), lambda i: (i, 0))
    vec = pl.BlockSpec((1, D_MODEL), lambda i: (0, 0))
    return pl.pallas_call(
        body, name="rmsnorm_bwd", grid=(S // NORM_T,),
        in_specs=[row, vec, row, row], out_specs=(row, vec),
        out_shape=(jax.ShapeDtypeStruct((S, D_MODEL), F32), jax.ShapeDtypeStruct((1, D_MODEL), F32)),
        compiler_params=_cp("arbitrary"),
    )(x, g, dxn, dres)


def loss_head(y, tgt):
    S = y.shape[0]

    def body(y_ref, t_ref, dy_ref, l_ref):
        @pl.when(pl.program_id(0) == 0)
        def _():
            l_ref[...] = jnp.zeros_like(l_ref)

        e = y_ref[...] - t_ref[...]
        dy_ref[...] = e * (1.0 / D_MODEL)
        l_ref[...] += jnp.sum(e * e) * (0.5 / D_MODEL)

    row = pl.BlockSpec((NORM_T, D_MODEL), lambda i: (i, 0))
    return pl.pallas_call(
        body, name="loss_head", grid=(S // NORM_T,),
        in_specs=[row, row], out_specs=(row, pl.BlockSpec((1, 128), lambda i: (0, 0))),
        out_shape=(jax.ShapeDtypeStruct((S, D_MODEL), F32), jax.ShapeDtypeStruct((1, 128), F32)),
        compiler_params=_cp("arbitrary"),
    )(y, tgt)


def _hg_block(qs, vs, zs, lb, sT, reverse):
    n = len(qs)
    row = lax.broadcasted_iota(jnp.int32, (HG_C, HG_C), 0)
    col = lax.broadcasted_iota(jnp.int32, (HG_C, HG_C), 1)
    tri = (row <= col) if reverse else (row >= col)
    m = tri.astype(F32)
    rsel = lax.broadcasted_iota(jnp.int32, (HG_C, 128), 0)
    ref_rows = ((rsel >= HG_C // 2) if reverse else (rsel <= HG_C // 2)).astype(F32)
    att, qdec, upd, keep = [None] * n, [None] * n, [None] * n, [None] * n
    for c in range(n):
        f = lb + (1.0 - lb) * _sigmoid(zs[c])
        kc = 1.0 - f
        lc = jnp.log(f)
        b = cumdot(m, lc)
        btot = jnp.sum(lc, axis=0, keepdims=True)
        bref = lax.stop_gradient(jnp.sum(lc * ref_rows, axis=0, keepdims=True))
        qe = qs[c] * jnp.exp(jnp.minimum(b - bref, 80.0))
        ke = kc * jnp.exp(jnp.minimum(bref - b, 80.0))
        att[c] = jnp.where(tri, dot_nt(qe, ke), 0.0)
        qdec[c] = qs[c] * jnp.exp(b)
        upd[c] = dot_tn(vs[c], kc * jnp.exp(btot - b))
        keep[c] = jnp.exp(btot)
    states = [None] * n
    for c in (range(n - 1, -1, -1) if reverse else range(n)):
        states[c] = sT
        sT = sT * keep[c] + upd[c]
    outs = [dot_nn(att[c], vs[c]) + dot_nt(qdec[c], states[c]) for c in range(n)]
    return outs, sT


def _chunks(ref, c, n):
    return [ref[i * c:(i + 1) * c, :] for i in range(n)]


def hg_scan_fwd(h, lbf, lbb):
    S = h.shape[0]
    nb = S // HG_T
    n = HG_T // HG_C

    def body(ins, outs, scr):
        qf, vf, zf, qb, vb, zb, lbf_ref, lbb_ref = ins
        of_ref, ob_ref, sf_ref, sb_ref = outs
        stf, stb = scr

        @pl.when(pl.program_id(1) == 0)
        def _():
            stf[...] = jnp.zeros_like(stf)
            stb[...] = jnp.zeros_like(stb)

        for (q, v, z, lb_ref, o_ref, s_ref, st, rev) in ((qf, vf, zf, lbf_ref, of_ref, sf_ref, stf, False),
                                                         (qb, vb, zb, lbb_ref, ob_ref, sb_ref, stb, True)):
            s_ref[0, 0] = st[...]
            outs, s_new = _hg_block(_chunks(q, HG_C, n), _chunks(v, HG_C, n), _chunks(z, HG_C, n),
                                    lb_ref[...], st[...], rev)
            for c in range(n):
                o_ref[c * HG_C:(c + 1) * HG_C, :] = outs[c]
            st[...] = s_new

    def col(off, rev):
        return pl.BlockSpec((HG_T, 128), (lambda hh, t: (nb - 1 - t, off // 128 + hh)) if rev
                            else (lambda hh, t: (t, off // 128 + hh)))

    lb_spec = pl.BlockSpec((1, 128), lambda hh, t: (0, hh))
    st_f = pl.BlockSpec((1, 1, 128, 128), lambda hh, t: (hh, t, 0, 0))
    st_b = pl.BlockSpec((1, 1, 128, 128), lambda hh, t: (hh, nb - 1 - t, 0, 0))
    outs, carried = _hosted(
        "hg_scan_fwd", body, None, grid=(HG_HEADS, nb),
        in_specs=[col(OFF_HG_Q, False), col(OFF_HG_V, False), col(OFF_HG_ZF, False),
                  col(OFF_HG_Q, True), col(OFF_HG_V, True), col(OFF_HG_ZB, True), lb_spec, lb_spec],
        out_specs=[col(0, False), col(0, True), st_f, st_b],
        out_shape=[jax.ShapeDtypeStruct((S, HG_W), F32), jax.ShapeDtypeStruct((S, HG_W), F32),
                   jax.ShapeDtypeStruct((HG_HEADS, nb, 128, 128), F32),
                   jax.ShapeDtypeStruct((HG_HEADS, nb, 128, 128), F32)],
        scratch_shapes=[pltpu.VMEM((128, 128), F32), pltpu.VMEM((128, 128), F32)],
        sem=("arbitrary", "arbitrary"), operands=(h, h, h, h, h, h, lbf, lbb))
    return tuple(outs)


def hg_scan_bwd(h, lbf, lbb, do, sf, sb, rider=None):
    S = h.shape[0]
    nb = S // HG_T
    n = HG_T // HG_C

    def body(ins, outs, scr):
        qf, vf, zf, dof, sfin, qb, vb, zb, dob, sbin, lbf_ref, lbb_ref = ins
        dqf, dvf, dzf, dqb, dvb, dzb, dlbf, dlbb = outs
        dsf, dsb = scr

        @pl.when(pl.program_id(1) == 0)
        def _():
            for r in (dsf, dsb, dlbf, dlbb):
                r[...] = jnp.zeros_like(r)

        for (q, v, z, dor, sin, lb_ref, dq, dv, dz, dlb, ds, rev) in (
                (qf, vf, zf, dof, sfin, lbf_ref, dqf, dvf, dzf, dlbf, dsf, False),
                (qb, vb, zb, dob, sbin, lbb_ref, dqb, dvb, dzb, dlbb, dsb, True)):
            fn = functools.partial(_hg_block, reverse=rev)
            _, vjp = jax.vjp(fn, _chunks(q, HG_C, n), _chunks(v, HG_C, n), _chunks(z, HG_C, n), lb_ref[...], sin[0, 0])
            dqs, dvs, dzs, dlb_v, ds_in = vjp((_chunks(dor, HG_C, n), ds[...]))
            for c in range(n):
                sl = slice(c * HG_C, (c + 1) * HG_C)
                dq[sl, :] = dqs[c]
                dv[sl, :] = dvs[c]
                dz[sl, :] = dzs[c]
            dlb[...] += dlb_v
            ds[...] = ds_in

    def col(off, fwd_scan):
        return pl.BlockSpec((HG_T, 128), (lambda hh, t: (nb - 1 - t, off // 128 + hh)) if fwd_scan
                            else (lambda hh, t: (t, off // 128 + hh)))

    lb_spec = pl.BlockSpec((1, 128), lambda hh, t: (0, hh))
    st_f = pl.BlockSpec((1, 1, 128, 128), lambda hh, t: (hh, nb - 1 - t, 0, 0))
    st_b = pl.BlockSpec((1, 1, 128, 128), lambda hh, t: (hh, t, 0, 0))
    full = jax.ShapeDtypeStruct((S, HG_W), F32)
    vec = jax.ShapeDtypeStruct((1, HG_W), F32)
    outs, carried = _hosted(
        "hg_scan_bwd", body, rider, grid=(HG_HEADS, nb),
        in_specs=[col(OFF_HG_Q, True), col(OFF_HG_V, True), col(OFF_HG_ZF, True), col(0, True), st_f,
                  col(OFF_HG_Q, False), col(OFF_HG_V, False), col(OFF_HG_ZB, False), col(0, False), st_b,
                  lb_spec, lb_spec],
        out_specs=[col(0, True), col(0, True), col(0, True), col(0, False), col(0, False), col(0, False),
                   lb_spec, lb_spec],
        out_shape=[full, full, full, full, full, full, vec, vec],
        scratch_shapes=[pltpu.VMEM((128, 128), F32), pltpu.VMEM((128, 128), F32)],
        sem=("arbitrary", "arbitrary"), operands=(h, h, h, do, sf, h, h, h, do, sb, lbf, lbb))
    return (tuple(outs), carried) if rider else tuple(outs)


GN_T = 1024


def _gated_norm(o, gate, g, center):
    if center:
        o = o - jnp.mean(o, axis=-1, keepdims=True)
    o = o * lax.rsqrt(jnp.mean(o * o, axis=-1, keepdims=True) + EPS)
    return (o * g) * (gate * _sigmoid(gate))


def gated_norm_fwd(name, of, ob, h, gate_off, g, center):
    S = of.shape[0]

    def body(of_ref, ob_ref, gate_ref, g_ref, y_ref):
        y_ref[...] = _gated_norm(of_ref[...] + ob_ref[...], gate_ref[...], g_ref[...], center)

    blk = pl.BlockSpec((GN_T, 128), lambda hh, i: (i, hh))
    return pl.pallas_call(
        body, name=name, grid=(6, S // GN_T),
        in_specs=[blk, blk, pl.BlockSpec((GN_T, 128), lambda hh, i: (i, gate_off // 128 + hh)),
                  pl.BlockSpec((1, 128), lambda hh, i: (0, hh))],
        out_specs=blk, out_shape=jax.ShapeDtypeStruct((S, 768), F32),
        compiler_params=_cp("parallel", "parallel"),
    )(of, ob, h, g)


def gated_norm_bwd(name, of, ob, h, gate_off, g, dy, dy_off, center):
    S = of.shape[0]

    def body(of_ref, ob_ref, gate_ref, g_ref, dy_ref, do_ref, dgate_ref, dg_ref):
        @pl.when(pl.program_id(1) == 0)
        def _():
            dg_ref[...] = jnp.zeros_like(dg_ref)

        fn = functools.partial(_gated_norm, center=center)
        _, vjp = jax.vjp(fn, of_ref[...] + ob_ref[...], gate_ref[...], g_ref[...])
        do, dgate, dg = vjp(dy_ref[...])
        do_ref[...] = do
        dgate_ref[...] = dgate
        dg_ref[...] += dg

    blk = pl.BlockSpec((GN_T, 128), lambda hh, i: (i, hh))
    vec = pl.BlockSpec((1, 128), lambda hh, i: (0, hh))
    return pl.pallas_call(
        body, name=name, grid=(6, S // GN_T),
        in_specs=[blk, blk, pl.BlockSpec((GN_T, 128), lambda hh, i: (i, gate_off // 128 + hh)), vec,
                  pl.BlockSpec((GN_T, 128), lambda hh, i: (i, dy_off // 128 + hh))],
        out_specs=(blk, blk, vec),
        out_shape=(jax.ShapeDtypeStruct((S, 768), F32), jax.ShapeDtypeStruct((S, 768), F32),
                   jax.ShapeDtypeStruct((1, 768), F32)),
        compiler_params=_cp("arbitrary", "arbitrary"),
    )(of, ob, h, g, dy)


def _ret_consts(S):
    half = RET_DK // 2
    inv = ROPE_BASE ** (-jnp.arange(half, dtype=F32) / half)
    ang = jnp.arange(S, dtype=F32)[:, None] * inv[None, :]
    cos, sin = jnp.cos(ang), jnp.sin(ang)
    cos_t = jnp.tile(jnp.concatenate([cos, cos], axis=1), (1, RET_HEADS))
    sin_t = jnp.tile(jnp.concatenate([-sin, sin], axis=1), (1, RET_HEADS))
    hidx = jnp.arange(RET_HEADS, dtype=F32)
    lg_f = jnp.log1p(-jnp.exp2(-5.0 - hidx))
    C = RET_CHUNK
    idx = jnp.arange(C, dtype=F32)
    rel = idx[:, None] - idx[None, :]

    def one(lg, reverse):
        lgc = lg[:, None]
        decay = jnp.where(rel >= 0, jnp.exp(lgc[:, :, None] * jnp.maximum(rel, 0.0)), 0.0)
        zeta = jnp.exp(lgc * (C - 1 - idx))
        xi = jnp.exp(lgc * (idx + 1))
        if reverse:
            decay = decay[:, ::-1, ::-1]
            zeta, xi = zeta[:, ::-1], xi[:, ::-1]
        wide = lambda t: jnp.repeat(t.T, RET_DK, axis=1)
        gam_w = jnp.broadcast_to(jnp.repeat(jnp.exp(lg * C), 128)[None, :], (8, RET_W))
        return decay, wide(xi), wide(zeta), gam_w

    hm = (jnp.arange(RET_QK_W)[None, :] // RET_DK == jnp.arange(8)[:, None]).astype(F32)
    return (cos_t, sin_t, hm) + one(lg_f, False) + one(lg_f[::-1], True)


def _rope(t, cos, sin_signed):
    lane = lax.broadcasted_iota(jnp.int32, t.shape, 1)
    first = (lane & (RET_DK - 1)) < RET_DK // 2
    partner = jnp.where(first, pltpu.roll(t, RET_QK_W - RET_DK // 2, 1), pltpu.roll(t, RET_DK // 2, 1))
    return t * cos + partner * sin_signed


def _ret_block(qs, ks, vs, st, dec, xi, zeta, gam, hms, reverse):
    n = len(qs)
    heads = range(RET_HEADS)
    qx = [q * xi for q in qs]
    kz = [k * zeta for k in ks]
    sc = [[dot_nt(qs[c] * hms[hh], ks[c]) * dec[hh] for hh in heads] for c in range(n)]
    upd = [[dot_tn(kz[c] * hms[hh], vs[c][hh]) for hh in heads] for c in range(n)]
    st = list(st)
    seen = [None] * n
    for c in (range(n - 1, -1, -1) if reverse else range(n)):
        seen[c] = list(st)
        st = [st[hh] * gam[hh] + upd[c][hh] for hh in heads]
    outs = [[dot_nn(sc[c][hh], vs[c][hh]) + dot_nn(qx[c], seen[c][hh]) for hh in heads] for c in range(n)]
    return outs, st


def _ret_inputs(q_ref, k_ref, v_ref, cos_ref, sin_ref):
    n = RET_T // RET_CHUNK
    qr = _rope(q_ref[...], cos_ref[...], sin_ref[...])
    kr = _rope(k_ref[...], cos_ref[...], sin_ref[...]) * (RET_DK ** -0.5)
    qs = [qr[c * RET_CHUNK:(c + 1) * RET_CHUNK] for c in range(n)]
    ks = [kr[c * RET_CHUNK:(c + 1) * RET_CHUNK] for c in range(n)]
    vs = [[v_ref[c * RET_CHUNK:(c + 1) * RET_CHUNK, hh * 128:(hh + 1) * 128] for hh in range(RET_HEADS)]
          for c in range(n)]
    return qs, ks, vs


def _ret_dir_consts(dec_ref, xi_ref, zeta_ref, gam_ref, hm_ref):
    dec = [dec_ref[hh] for hh in range(RET_HEADS)]
    gam = [gam_ref[0:1, hh * 128:(hh + 1) * 128] for hh in range(RET_HEADS)]
    hms = [hm_ref[hh:hh + 1, :] for hh in range(RET_HEADS)]
    return dec, xi_ref[...], zeta_ref[...], gam, hms


def _ret_rows(nb, rev):
    def rows(width, colblk):
        return pl.BlockSpec((RET_T, width), (lambda t: (nb - 1 - t, colblk)) if rev else (lambda t: (t, colblk)))
    return rows


def _const_spec(shape):
    nd = len(shape)
    return pl.BlockSpec(shape, lambda t: (0,) * nd)


def ret_scan_fwd(h, consts):
    S = h.shape[0]
    nb = S // RET_T
    n = RET_T // RET_CHUNK
    cos_t, sin_t, hm, dec_f, xi_f, zeta_f, gam_f, dec_b, xi_b, zeta_b, gam_b = consts

    def body(ins, outs, scr):
        qf, kf, vf, cf, sf, qb, kb, vb, cb, sb_, hm_ref, decf, xif, zetaf, gamf, decb, xib, zetab, gamb = ins
        of_ref, ob_ref, sfo, sbo = outs
        stf, stb = scr

        @pl.when(pl.program_id(0) == 0)
        def _():
            stf[...] = jnp.zeros_like(stf)
            stb[...] = jnp.zeros_like(stb)

        for (q, k, v, cs, sn, dr, xr, zr, gr, o_ref, so, st, rev) in (
                (qf, kf, vf, cf, sf, decf, xif, zetaf, gamf, of_ref, sfo, stf, False),
                (qb, kb, vb, cb, sb_, decb, xib, zetab, gamb, ob_ref, sbo, stb, True)):
            so[0] = st[...]
            qs, ks, vs = _ret_inputs(q, k, v, cs, sn)
            dec, xi, zeta, gam, hms = _ret_dir_consts(dr, xr, zr, gr, hm_ref)
            st_in = [st[:, hh * 128:(hh + 1) * 128] for hh in range(RET_HEADS)]
            outs, st_new = _ret_block(qs, ks, vs, st_in, dec, xi, zeta, gam, hms, rev)
            for c in range(n):
                for hh in range(RET_HEADS):
                    o_ref[c * RET_CHUNK:(c + 1) * RET_CHUNK, hh * 128:(hh + 1) * 128] = outs[c][hh]
            for hh in range(RET_HEADS):
                st[:, hh * 128:(hh + 1) * 128] = st_new[hh]

    rf, rb = _ret_rows(nb, False), _ret_rows(nb, True)
    cspecs = [_const_spec(a.shape) for a in (hm, dec_f, xi_f, zeta_f, gam_f, dec_b, xi_b, zeta_b, gam_b)]
    st_shape = jax.ShapeDtypeStruct((nb, RET_QK_W, RET_W), F32)
    qc, kc, vc = OFF_RET_Q // RET_QK_W, OFF_RET_K // RET_QK_W, OFF_RET_V // RET_W
    outs, carried = _hosted(
        "ret_scan_fwd", body, None, grid=(nb,),
        in_specs=[rf(RET_QK_W, qc), rf(RET_QK_W, kc), rf(RET_W, vc), rf(RET_QK_W, 0), rf(RET_QK_W, 0),
                  rb(RET_QK_W, qc), rb(RET_QK_W, kc), rb(RET_W, vc), rb(RET_QK_W, 0), rb(RET_QK_W, 0)] + cspecs,
        out_specs=[rf(RET_W, 0), rb(RET_W, 0),
                   pl.BlockSpec((1, RET_QK_W, RET_W), lambda t: (t, 0, 0)),
                   pl.BlockSpec((1, RET_QK_W, RET_W), lambda t: (nb - 1 - t, 0, 0))],
        out_shape=[jax.ShapeDtypeStruct((S, RET_W), F32), jax.ShapeDtypeStruct((S, RET_W), F32), st_shape, st_shape],
        scratch_shapes=[pltpu.VMEM((RET_QK_W, RET_W), F32), pltpu.VMEM((RET_QK_W, RET_W), F32)],
        sem=("arbitrary",),
        operands=(h, h, h, cos_t, sin_t, h, h, h, cos_t, sin_t, hm, dec_f, xi_f, zeta_f, gam_f, dec_b, xi_b, zeta_b,
                  gam_b))
    return tuple(outs)


def ret_scan_bwd(h, consts, do, sf, sb, rider=None):
    S = h.shape[0]
    nb = S // RET_T
    n = RET_T // RET_CHUNK
    cos_t, sin_t, hm, dec_f, xi_f, zeta_f, gam_f, dec_b, xi_b, zeta_b, gam_b = consts

    def body(ins, outs, scr):
        (qf, kf, vf, cf, sf_, dof, sfin, qb, kb, vb, cb, sb_, dob, sbin,
         hm_ref, decf, xif, zetaf, gamf, decb, xib, zetab, gamb) = ins
        dqf, dkf, dvf, dqb, dkb, dvb = outs
        dsf, dsb = scr

        @pl.when(pl.program_id(0) == 0)
        def _():
            dsf[...] = jnp.zeros_like(dsf)
            dsb[...] = jnp.zeros_like(dsb)

        for (q, k, v, cs, sn, dor, sin, dr, xr, zr, gr, dq, dk, dv, ds, rev) in (
                (qf, kf, vf, cf, sf_, dof, sfin, decf, xif, zetaf, gamf, dqf, dkf, dvf, dsf, False),
                (qb, kb, vb, cb, sb_, dob, sbin, decb, xib, zetab, gamb, dqb, dkb, dvb, dsb, True)):
            qs, ks, vs = _ret_inputs(q, k, v, cs, sn)
            dec, xi, zeta, gam, hms = _ret_dir_consts(dr, xr, zr, gr, hm_ref)
            st_in = [sin[0, :, hh * 128:(hh + 1) * 128] for hh in range(RET_HEADS)]
            fn = lambda a, b_, c_, d_: _ret_block(a, b_, c_, d_, dec, xi, zeta, gam, hms, rev)
            _, vjp = jax.vjp(fn, qs, ks, vs, st_in)
            dos = [[dor[c * RET_CHUNK:(c + 1) * RET_CHUNK, hh * 128:(hh + 1) * 128] for hh in range(RET_HEADS)]
                   for c in range(n)]
            dst = [ds[:, hh * 128:(hh + 1) * 128] for hh in range(RET_HEADS)]
            dqs, dks, dvs, dst_in = vjp((dos, dst))
            cosv, sinv = cs[...], sn[...]
            dq[...] = _rope(jnp.concatenate(dqs, axis=0), cosv, -sinv)
            dk[...] = _rope(jnp.concatenate(dks, axis=0) * (RET_DK ** -0.5), cosv, -sinv)
            for c in range(n):
                for hh in range(RET_HEADS):
                    dv[c * RET_CHUNK:(c + 1) * RET_CHUNK, hh * 128:(hh + 1) * 128] = dvs[c][hh]
            for hh in range(RET_HEADS):
                ds[:, hh * 128:(hh + 1) * 128] = dst_in[hh]

    rf, rb = _ret_rows(nb, True), _ret_rows(nb, False)
    cspecs = [_const_spec(a.shape) for a in (hm, dec_f, xi_f, zeta_f, gam_f, dec_b, xi_b, zeta_b, gam_b)]
    qk = jax.ShapeDtypeStruct((S, RET_QK_W), F32)
    vv = jax.ShapeDtypeStruct((S, RET_W), F32)
    qc, kc, vc = OFF_RET_Q // RET_QK_W, OFF_RET_K // RET_QK_W, OFF_RET_V // RET_W
    outs, carried = _hosted(
        "ret_scan_bwd", body, rider, grid=(nb,),
        in_specs=[rf(RET_QK_W, qc), rf(RET_QK_W, kc), rf(RET_W, vc), rf(RET_QK_W, 0), rf(RET_QK_W, 0), rf(RET_W, 0),
                  pl.BlockSpec((1, RET_QK_W, RET_W), lambda t: (nb - 1 - t, 0, 0)),
                  rb(RET_QK_W, qc), rb(RET_QK_W, kc), rb(RET_W, vc), rb(RET_QK_W, 0), rb(RET_QK_W, 0), rb(RET_W, 0),
                  pl.BlockSpec((1, RET_QK_W, RET_W), lambda t: (t, 0, 0))] + cspecs,
        out_specs=[rf(RET_QK_W, 0), rf(RET_QK_W, 0), rf(RET_W, 0), rb(RET_QK_W, 0), rb(RET_QK_W, 0), rb(RET_W, 0)],
        out_shape=[qk, qk, vv, qk, qk, vv],
        scratch_shapes=[pltpu.VMEM((RET_QK_W, RET_W), F32), pltpu.VMEM((RET_QK_W, RET_W), F32)],
        sem=("arbitrary",),
        operands=(h, h, h, cos_t, sin_t, do, sf, h, h, h, cos_t, sin_t, do, sb,
                  hm, dec_f, xi_f, zeta_f, gam_f, dec_b, xi_b, zeta_b, gam_b))
    return (tuple(outs), carried) if rider else tuple(outs)


def _t5_bucket(rel):
    nb = REL_BUCKETS // 2
    max_exact = nb // 2
    sign_off = jnp.where(rel > 0, nb, 0)
    n = jnp.abs(rel)
    nf = jnp.maximum(n, 1).astype(F32)
    large = max_exact + (jnp.log(nf / max_exact) / math.log(REL_MAX_DIST / max_exact)
                         * (nb - max_exact)).astype(jnp.int32)
    large = jnp.minimum(large, nb - 1)
    return sign_off + jnp.where(n < max_exact, n, large)


def _dil_buckets(dil):
    tq, tb = DIL_TQ, DIL_TQ + 2 * DIL_HALF
    rel_q = jnp.arange(tb)[None, :] - DIL_HALF - jnp.arange(tq)[:, None]
    rel_k = jnp.arange(tq)[None, :] + DIL_HALF - jnp.arange(tb)[:, None]
    return _t5_bucket(rel_q * dil), _t5_bucket(rel_k * dil)


def dil_view(h, g, dil):
    base = OFF_DIL + 3 * g * DIL_W
    if dil == 1:
        return h, IN_W, base
    return h[:, base:base + 3 * DIL_W].reshape(h.shape[0] // dil, dil * 3 * DIL_W), 3 * DIL_W, 0


def _dil_col(view, j):
    _, width, base = view
    return lambda r: (r * width + base + j * DIL_W) // DIL_W


def _dil_specs(L):
    nq = DIL_TQ // DIL_HALF
    last = L // DIL_HALF - 1

    def cur(colfn):
        return pl.BlockSpec((DIL_TQ, DIL_W), lambda r, n: (n, colfn(r)))

    def prev(colfn):
        return pl.BlockSpec((DIL_HALF, DIL_W), lambda r, n: (jnp.maximum(n * nq - 1, 0), colfn(r)))

    def nxt(colfn):
        return pl.BlockSpec((DIL_HALF, DIL_W), lambda r, n: (jnp.minimum((n + 1) * nq, last), colfn(r)))

    return prev, cur, nxt


def _slot(s):
    return slice(s * DIL_HD, (s + 1) * DIL_HD)


def _cat3(a, b, c, s):
    return jnp.concatenate([a[:, _slot(s)], b[:, _slot(s)], c[:, _slot(s)]], axis=0)


def dil_attn_fwd(view, S, g, dil, bias, qg, kg):
    L = S // dil
    hv = view[0]
    tb = DIL_TQ + 2 * DIL_HALF

    def body(q_ref, kp, kc, kn, vp, vc, vn, bias_ref, qg_ref, kg_ref, o_ref, lse_ref):
        n = pl.program_id(1)
        ii = lax.broadcasted_iota(jnp.int32, (DIL_TQ, tb), 0)
        jj = lax.broadcasted_iota(jnp.int32, (DIL_TQ, tb), 1)
        kabs = n * DIL_TQ - DIL_HALF + jj
        valid = (jnp.abs(jj - DIL_HALF - ii) <= DIL_HALF) & (kabs >= 0) & (kabs < L)
        for s in range(DIL_SLOTS):
            q = _head_rms(q_ref[:, _slot(s)], qg_ref[...]) * (DIL_HD ** -0.5)
            kb = _head_rms(_cat3(kp, kc, kn, s), kg_ref[...])
            sc = jnp.where(valid, _mxu(q, kb, NT) + bias_ref[s], NEG)
            m = jnp.max(sc, axis=-1, keepdims=True)
            p = jnp.exp(sc - m)
            den = jnp.sum(p, axis=-1, keepdims=True)
            o_ref[:, _slot(s)] = _mxu(p, _cat3(vp, vc, vn, s), NN) / den
            lse_ref[:, _slot(s)] = jnp.broadcast_to(m + jnp.log(den), (DIL_TQ, DIL_HD))

    prev, cur, nxt = _dil_specs(L)
    qc, kc_, vc_ = (_dil_col(view, j) for j in range(3))
    oc = lambda r: r
    vec = pl.BlockSpec((1, 128), lambda r, n: (0, 0))
    out = jax.ShapeDtypeStruct((L, dil * DIL_W), F32)
    o, lse = pl.pallas_call(
        body, name=f"dil_attn_fwd{g}", grid=(dil, L // DIL_TQ),
        in_specs=[cur(qc), prev(kc_), cur(kc_), nxt(kc_), prev(vc_), cur(vc_), nxt(vc_),
                  pl.BlockSpec((DIL_SLOTS, DIL_TQ, tb), lambda r, n: (0, 0, 0)), vec, vec],
        out_specs=(cur(oc), cur(oc)), out_shape=(out, out),
        compiler_params=_cp("parallel", "parallel"),
    )(hv, hv, hv, hv, hv, hv, hv, bias, qg, kg)
    return o.reshape(S, DIL_W), lse.reshape(S, DIL_W)


def dil_combine(os_, lses):
    S = os_[0].shape[0]

    def body(o1, o2, o3, l1, l2, l3, y_ref, lt_ref):
        a, b, c = l1[...], l2[...], l3[...]
        m = jnp.maximum(jnp.maximum(a, b), c)
        ea, eb, ec = jnp.exp(a - m), jnp.exp(b - m), jnp.exp(c - m)
        den = ea + eb + ec
        y_ref[...] = (ea * o1[...] + eb * o2[...] + ec * o3[...]) / den
        lt_ref[...] = m + jnp.log(den)

    blk = pl.BlockSpec((GN_T, DIL_W), lambda i: (i, 0))
    out = jax.ShapeDtypeStruct((S, DIL_W), F32)
    return pl.pallas_call(
        body, name="dil_combine", grid=(S // GN_T,), in_specs=[blk] * 6, out_specs=(blk, blk),
        out_shape=(out, out), compiler_params=_cp("parallel"),
    )(*os_, *lses)


def dil_delta(dy, yc):
    S = yc.shape[0]

    def body(dy_ref, y_ref, d_ref):
        d_ref[...] = jnp.broadcast_to(jnp.sum(dy_ref[...] * y_ref[...], axis=-1, keepdims=True), (GN_T, 128))

    return pl.pallas_call(
        body, name="dil_delta", grid=(S // GN_T, DIL_SLOTS),
        in_specs=[pl.BlockSpec((GN_T, 128), lambda i, s: (i, (HG_W + RET_W) // 128 + s)),
                  pl.BlockSpec((GN_T, 128), lambda i, s: (i, s))],
        out_specs=pl.BlockSpec((GN_T, 128), lambda i, s: (i, s)),
        out_shape=jax.ShapeDtypeStruct((S, DIL_W), F32), compiler_params=_cp("parallel", "parallel"),
    )(dy, yc)


def dil_attn_bwd(view, S, g, dil, bias_q, bias_k, qg, kg, dy, lse_t, delta):
    L = S // dil
    hv = view[0]
    if dil == 1:
        dyv, dyc = dy, lambda r: (HG_W + RET_W) // DIL_W
    else:
        dyv, dyc = dy[:, HG_W + RET_W:].reshape(L, dil * DIL_W), lambda r: r
    lv = lse_t.reshape(L, dil * DIL_W)
    dv_ = delta.reshape(L, dil * DIL_W)
    tq, tb = DIL_TQ, DIL_TQ + 2 * DIL_HALF
    scale = DIL_HD ** -0.5

    def body(qp, qc, qn, kp, kc, kn, vp, vc, vn, dp_, dc, dn, lp, lc, ln, ep, ec, en, bq_ref, bk_ref, qg_ref, kg_ref,
             dq_ref, dk_ref, dv_ref, dbias_ref, dqg_ref, dkg_ref):
        r, n = pl.program_id(0), pl.program_id(1)

        @pl.when((r == 0) & (n == 0))
        def _():
            for ref in (dbias_ref, dqg_ref, dkg_ref):
                ref[...] = jnp.zeros_like(ref)

        qgv, kgv = qg_ref[...], kg_ref[...]
        qfn = lambda t, gg: _head_rms(t, gg) * scale
        ii = lax.broadcasted_iota(jnp.int32, (tq, tb), 0)
        jj = lax.broadcasted_iota(jnp.int32, (tq, tb), 1)
        kabs = n * tq - DIL_HALF + jj
        valid = (jnp.abs(jj - DIL_HALF - ii) <= DIL_HALF) & (kabs >= 0) & (kabs < L)
        i2 = lax.broadcasted_iota(jnp.int32, (tb, tq), 0)
        j2 = lax.broadcasted_iota(jnp.int32, (tb, tq), 1)
        qabs = n * tq - DIL_HALF + i2
        valid2 = (jnp.abs(j2 + DIL_HALF - i2) <= DIL_HALF) & (qabs >= 0) & (qabs < L)
        for s in range(DIL_SLOTS):
            sl = _slot(s)
            one = slice(s * DIL_HD, s * DIL_HD + 1)
            qn_c, q_vjp = jax.vjp(qfn, qc[:, sl], qgv)
            k_band = _head_rms(_cat3(kp, kc, kn, s), kgv)
            sc = _mxu(qn_c, k_band, NT) + bq_ref[s]
            p = jnp.where(valid, jnp.exp(jnp.where(valid, sc, NEG) - lc[:, one]), 0.0)
            ds = p * (_mxu(dc[:, sl], _cat3(vp, vc, vn, s), NT) - ec[:, one])
            dbias_ref[s] += ds
            dq, dqg = q_vjp(_mxu(ds, k_band, NN))
            dq_ref[:, sl] = dq
            dqg_ref[s] += dqg
            kn_c, k_vjp = jax.vjp(_head_rms, kc[:, sl], kgv)
            q_band = qfn(_cat3(qp, qc, qn, s), qgv)
            do_band = _cat3(dp_, dc, dn, s)
            s2 = _mxu(q_band, kn_c, NT) + bk_ref[s]
            lse_band = jnp.concatenate([lp[:, one], lc[:, one], ln[:, one]], axis=0)
            delta_band = jnp.concatenate([ep[:, one], ec[:, one], en[:, one]], axis=0)
            p2 = jnp.where(valid2, jnp.exp(jnp.where(valid2, s2, NEG) - lse_band), 0.0)
            dv_ref[:, sl] = _mxu(p2, do_band, TN)
            ds2 = p2 * (_mxu(do_band, vc[:, sl], NT) - delta_band)
            dk, dkg = k_vjp(_mxu(ds2, q_band, TN))
            dk_ref[:, sl] = dk
            dkg_ref[s] += dkg

    prev, cur, nxt = _dil_specs(L)
    three = lambda colfn: [prev(colfn), cur(colfn), nxt(colfn)]
    qc_, kc_, vc_ = (_dil_col(view, j) for j in range(3))
    oc = lambda r: r
    vec = pl.BlockSpec((1, 128), lambda r, n: (0, 0))
    acc_vec = pl.BlockSpec((DIL_SLOTS, 1, 128), lambda r, n: (0, 0, 0))
    out = jax.ShapeDtypeStruct((L, dil * DIL_W), F32)
    dq, dk, dv, dbias, dqg, dkg = pl.pallas_call(
        body, name=f"dil_attn_bwd{g}", grid=(dil, L // tq),
        in_specs=three(qc_) + three(kc_) + three(vc_) + three(dyc) + three(oc) + three(oc)
        + [pl.BlockSpec((DIL_SLOTS, tq, tb), lambda r, n: (0, 0, 0)),
           pl.BlockSpec((DIL_SLOTS, tb, tq), lambda r, n: (0, 0, 0)), vec, vec],
        out_specs=(cur(oc), cur(oc), cur(oc), pl.BlockSpec((DIL_SLOTS, tq, tb), lambda r, n: (0, 0, 0)),
                   acc_vec, acc_vec),
        out_shape=(out, out, out, jax.ShapeDtypeStruct((DIL_SLOTS, tq, tb), F32),
                   jax.ShapeDtypeStruct((DIL_SLOTS, 1, 128), F32), jax.ShapeDtypeStruct((DIL_SLOTS, 1, 128), F32)),
        compiler_params=_cp("arbitrary", "arbitrary"),
    )(hv, hv, hv, hv, hv, hv, hv, hv, hv, dyv, dyv, dyv, lv, lv, lv, dv_, dv_, dv_, bias_q, bias_k, qg, kg)
    return dq.reshape(S, DIL_W), dk.reshape(S, DIL_W), dv.reshape(S, DIL_W), dbias, dqg, dkg


def _lb_eff(p):
    a = jnp.cumsum(jax.nn.softmax(p.astype(F32), axis=0), axis=0)
    return a - a[0:1]


def _dil_bias(rel_bias, g, dil):
    tbl = rel_bias[:, g * DIL_SLOTS:(g + 1) * DIL_SLOTS]
    return tuple(jnp.einsum("ijb,bs->sij", jax.nn.one_hot(b, REL_BUCKETS, dtype=F32), tbl,
                            precision=lax.Precision.HIGHEST) for b in _dil_buckets(dil))


def _big_weights(w):
    return w[0], w[1].reshape(D_MODEL, D_MODEL), w[2], w[3].reshape(D_FF, D_MODEL)


def _layer_fwd(x, l, prm, wts, rc, biases, gatherer=None):
    def carrying(stage, call, *args):
        rider = gatherer.rider(l, stage) if gatherer else None
        if rider is None:
            return call(*args)
        res, arrays = call(*args, rider)
        gatherer.done(l, stage, list(arrays))
        return res

    win_g, wout_g, wup_g, wdown_g = _big_weights(wts[l])
    row = lambda a: a[l][None]
    xn = rmsnorm_fwd(x, row(prm["norm_mix"]))
    h = carrying(0, proj_in, xn, win_g)
    hof, hob, hsf, hsb = hg_scan_fwd(h, row(prm["lbf"]), row(prm["lbb"]))
    ya = gated_norm_fwd("hg_out", hof, hob, h, OFF_HG_GATE, row(prm["hg_norm"]), False)
    rof, rob, rsf, rsb = ret_scan_fwd(h, rc)
    yb = gated_norm_fwd("ret_out", rof, rob, h, OFF_RET_GATE, row(prm["ret_norm"]), True)
    os_, lses, views = [], [], []
    for g, (_, dil) in enumerate(DIL_GROUPS):
        views.append(dil_view(h, g, dil))
        o, lse = dil_attn_fwd(views[g], h.shape[0], g, dil, biases[g][0], row(prm["q_norm"]), row(prm["k_norm"]))
        os_.append(o)
        lses.append(lse)
    yc, lse_t = dil_combine(os_, lses)
    y = jnp.concatenate([ya, yb, yc], axis=1).astype(BF16)
    x2 = proj_out(y, wout_g, x)
    hm = rmsnorm_fwd(x2, row(prm["norm_mlp"]))
    u, act = carrying(1, proj_up, hm, wup_g)
    x3 = carrying(2, proj_down, act, wdown_g, x2)
    saved = dict(x=x, xn=xn, h=h, hof=hof, hob=hob, hsf=hsf, hsb=hsb, rof=rof, rob=rob, rsf=rsf, rsb=rsb,
                 yc=yc, lse_t=lse_t, y=y, x2=x2, hm=hm, u=u, act=act, views=views)
    return x3, saved


def _layer_bwd(dx3, l, prm, wts, rc, biases, sv, reducer=None):
    win_g, wout_g, wup_g, wdown_g = _big_weights(wts[l])
    row = lambda a: a[l][None]
    h = sv["h"]
    rider = reducer.pair_rider() if reducer else None
    du = bwd_down_act(dx3, wdown_g, sv["u"], rider)
    if rider:
        du, got = du
        reducer.pair_done(got)
    g_down = wgrad("wgrad_down", sv["act"], dx3, m=D_FF, n=D_MODEL, n_shard=D_MODEL)
    dhm = bwd_up(du, wup_g)
    g_up = wgrad("wgrad_up", sv["hm"], du, m=D_MODEL, n=D_FF, n_shard=FF_SHARD)
    dx2, dg_mlp = rmsnorm_bwd(sv["x2"], row(prm["norm_mlp"]), dhm, dx3)
    dy = bwd_out(dx2, wout_g)
    g_out = wgrad("wgrad_out", sv["y"], dx2, m=D_MODEL, n=D_MODEL, n_shard=D_MODEL)
    hdo, hdgate, dg_hg = gated_norm_bwd("hg_out_bwd", sv["hof"], sv["hob"], h, OFF_HG_GATE, row(prm["hg_norm"]),
                                        dy, 0, False)
    rider = reducer.chip_rider(0) if reducer else None
    hg_grads = hg_scan_bwd(h, row(prm["lbf"]), row(prm["lbb"]), hdo, sv["hsf"], sv["hsb"], rider)
    if rider:
        hg_grads, arrived = hg_grads
        reducer.chip_done(0, arrived)
    hdqf, hdvf, hdzf, hdqb, hdvb, hdzb, dlbf, dlbb = hg_grads
    rdo, rdgate, dg_ret = gated_norm_bwd("ret_out_bwd", sv["rof"], sv["rob"], h, OFF_RET_GATE, row(prm["ret_norm"]),
                                         dy, HG_W, True)
    rider = reducer.chip_rider(1) if reducer else None
    ret_grads = ret_scan_bwd(h, rc, rdo, sv["rsf"], sv["rsb"], rider)
    if rider:
        ret_grads, arrived = ret_grads
        reducer.chip_done(1, arrived)
    rdqf, rdkf, rdvf, rdqb, rdkb, rdvb = ret_grads
    delta = dil_delta(dy, sv["yc"])
    dil_parts, dbiases = [], []
    dqg = jnp.zeros((1, DIL_HD), F32)
    dkg = jnp.zeros((1, DIL_HD), F32)
    for g, (_, dil) in enumerate(DIL_GROUPS):
        dq, dk, dv, dbias, dqg_g, dkg_g = dil_attn_bwd(sv["views"][g], h.shape[0], g, dil, biases[g][0], biases[g][1],
                                                       row(prm["q_norm"]), row(prm["k_norm"]), dy, sv["lse_t"], delta)
        dil_parts += [dq, dk, dv]
        dbiases.append(dbias)
        dqg = dqg + jnp.sum(dqg_g, axis=0)
        dkg = dkg + jnp.sum(dkg_g, axis=0)
    dh = jnp.concatenate([hdqf + hdqb, hdvf + hdvb, hdzf, hdzb, hdgate,
                          rdqf + rdqb, rdkf + rdkb, rdvf + rdvb, rdgate] + dil_parts, axis=1).astype(BF16)
    dxn = bwd_in(dh, win_g)
    g_in = wgrad("wgrad_in", sv["xn"], dh, m=D_MODEL, n=IN_W, n_shard=IN_SHARD)
    dx, dg_mix = rmsnorm_bwd(sv["x"], row(prm["norm_mix"]), dxn, dx2)
    big = (g_in, g_out.reshape(N_CHIPS, D_MODEL // N_CHIPS, D_MODEL), g_up,
           g_down.reshape(N_CHIPS, D_FF // N_CHIPS, D_MODEL))
    small = dict(norm_mix=dg_mix, norm_mlp=dg_mlp, lbf=dlbf, lbb=dlbb, hg_norm=dg_hg, ret_norm=dg_ret,
                 q_norm=dqg, k_norm=dkg)
    if reducer:
        reducer.push(l, big)
    return dx, big, small, dbiases


def _rel_bias_grad(dbias_layers):
    cols = []
    for g, (_, dil) in enumerate(DIL_GROUPS):
        bq, _ = _dil_buckets(dil)
        onehot = jax.nn.one_hot(bq, REL_BUCKETS, dtype=F32)
        tot = dbias_layers[0][g]
        for d in dbias_layers[1:]:
            tot = tot + d[g]
        cols.append(jnp.einsum("sij,ijb->bs", tot, onehot, precision=lax.Precision.HIGHEST))
    return jnp.concatenate(cols, axis=1)


def local_step(x, tgt, wts, prm_in, reducer=None, gatherer=None):
    S = x.shape[0]
    prm = dict(prm_in)
    prm["lbf"], lbf_vjp = jax.vjp(_lb_eff, prm_in["hg_lb_fwd"])
    prm["lbb"], lbb_vjp = jax.vjp(_lb_eff, prm_in["hg_lb_bwd"])
    rc = _ret_consts(S)
    biases = [_dil_bias(prm["rel_bias"], g, dil) for g, (_, dil) in enumerate(DIL_GROUPS)]
    saved = []
    for l in range(DEPTH):
        x, sv = _layer_fwd(x, l, prm, wts, rc, biases, gatherer)
        saved.append(sv)
    dx, loss_row = loss_head(x, tgt)
    big, small, dbias_layers = [None] * DEPTH, [None] * DEPTH, [None] * DEPTH
    for l in range(DEPTH - 1, -1, -1):
        dx, big[l], small[l], dbias_layers[l] = _layer_bwd(dx, l, prm, wts, rc, biases, saved[l], reducer)
    sg = {k: jnp.concatenate([small[l][k] for l in range(DEPTH)], axis=0) for k in small[0]}
    sg["rel_bias"] = _rel_bias_grad(dbias_layers)
    return loss_row[0, 0], dx, (reducer.finish() if reducer else big), sg, (lbf_vjp, lbb_vjp)


def _place():
    x, y, c = lax.axis_index("x"), lax.axis_index("y"), lax.axis_index("c")
    rels = [(1 - x, y), (x, 1 - y), (1 - x, 1 - y)]
    return x, y, c, 2 * x + y, rels


def _half(c, rows):
    return pl.ds(pl.multiple_of(c * (rows // 2), 16), rows // 2)


def place_own(name, p_arr, w, l):
    _, rows, cols = w.shape
    tr = 512

    def body(p_ref, w_ref, o_ref):
        o_ref[...] = w_ref[...].astype(BF16)

    return pl.pallas_call(
        body, name=name,
        grid_spec=pltpu.PrefetchScalarGridSpec(
            num_scalar_prefetch=1, grid=(rows // tr,),
            in_specs=[pl.BlockSpec((1, tr, cols), lambda i, p: (l, i, 0))],
            out_specs=pl.BlockSpec((1, tr, cols), lambda i, p: (p[0], i, 0))),
        out_shape=jax.ShapeDtypeStruct((N_CHIPS, rows, cols), BF16),
        compiler_params=_cp("parallel"),
    )(p_arr, w)


class WeightGatherer:
    def __init__(self, big_w):
        p_arr = (2 * lax.axis_index("x") + lax.axis_index("y")).astype(jnp.int32).reshape(1)
        self.w = [[place_own(f"place_own{t}", p_arr, w, l) for t, w in enumerate(big_w)] for l in range(DEPTH)]
        self.w[0] = list(run_alone("gather_first_chips", gather_ici_rider(self.w[0])))
        self.w[0] = list(run_alone("gather_first_cores", gather_pass_rider(self.w[0])))

    def rider(self, l, stage):
        if l + 1 == DEPTH:
            return None
        nxt = self.w[l + 1]
        if stage == 0:
            return gather_ici_rider(nxt[0:2])
        if stage == 1:
            return gather_pass_rider(nxt[0:2]) + gather_ici_rider(nxt[2:4])
        return gather_pass_rider(nxt[2:4])

    def done(self, l, stage, arrays):
        nxt = self.w[l + 1]
        if stage == 0:
            nxt[0:2] = arrays
        elif stage == 1:
            nxt[0:4] = arrays
        else:
            nxt[2:4] = arrays


def run_alone(name, rider):
    n_in, n_out = len(rider.arrays), len(rider.out_shapes)

    def body(*refs):
        ins, outs, sems = refs[:n_in], refs[n_in:n_in + n_out], refs[n_in + n_out:]
        rider.start(ins, outs, sems)
        rider.finish(ins, outs, sems)

    return pl.pallas_call(
        body, name=name, in_specs=[ANY] * n_in, out_specs=[ANY] * n_out, out_shape=rider.out_shapes,
        scratch_shapes=[pltpu.SemaphoreType.DMA((n,)) for n in rider.sems],
        input_output_aliases=rider.aliases(0, 0),
    )(*rider.arrays)


def _both(cp):
    return (cp, cp)


def pair_exchange_rider(gs):
    n = len(gs)

    def ops(ins, outs, ssem, rsem):
        x, y, c, _, _ = _place()
        return [_both(pltpu.make_async_remote_copy(
            src_ref=ins[i].at[:, _half(1 - c, ins[i].shape[1]), :], dst_ref=outs[i],
            send_sem=ssem.at[i], recv_sem=rsem.at[i], device_id=(x, y, 1 - c), device_id_type=MESH)) for i in range(n)]

    return Rider(gs, [jax.ShapeDtypeStruct((N_CHIPS, g.shape[1] // 2, g.shape[2]), F32) for g in gs], [n, n], ops)


def gather_ici_rider(bufs):
    n = len(bufs)

    def ops(ins, outs, ssem, rsem):
        x, y, c, p, rels = _place()
        cps = []
        for i in range(n):
            mine = _half(c, outs[i].shape[1])
            for r, (rx, ry) in enumerate(rels):
                k = i * 3 + r
                peer = dict(device_id=(rx, ry, c), device_id_type=MESH, send_sem=ssem.at[k], recv_sem=rsem.at[k])
                own, landing = outs[i].at[p, mine], outs[i].at[2 * rx + ry, mine]
                cps.append((pltpu.make_async_remote_copy(src_ref=own, dst_ref=own, **peer),
                            pltpu.make_async_remote_copy(src_ref=landing, dst_ref=landing, **peer)))
        return cps

    return Rider(bufs, None, [3 * n, 3 * n], ops)


def gather_pass_rider(bufs):
    n = len(bufs)

    def ops(ins, outs, ssem, rsem):
        x, y, c, p, rels = _place()
        cps = []
        for i in range(n):
            rows = outs[i].shape[1]
            for r, (rx, ry) in enumerate(rels):
                k = i * 3 + r
                peer = dict(device_id=(x, y, 1 - c), device_id_type=MESH, send_sem=ssem.at[k], recv_sem=rsem.at[k])
                landed, theirs = outs[i].at[2 * rx + ry, _half(c, rows)], outs[i].at[2 * rx + ry, _half(1 - c, rows)]
                cps.append((pltpu.make_async_remote_copy(src_ref=landed, dst_ref=landed, **peer),
                            pltpu.make_async_remote_copy(src_ref=theirs, dst_ref=theirs, **peer)))
        return cps

    return Rider(bufs, None, [3 * n, 3 * n], ops)


def pair_add(name, c_arr, g, got):
    _, rows, cols = g.shape
    hr = rows // 2
    tr = 256
    nblk = hr // tr

    def body(c_ref, g_ref, r_ref, o32, o16):
        s = g_ref[...] + r_ref[...]
        o32[...] = s
        o16[...] = s.astype(BF16)

    blk = pl.BlockSpec((1, tr, cols), lambda pp, i, c_ref: (pp, i, 0))
    return pl.pallas_call(
        body, name=name,
        grid_spec=pltpu.PrefetchScalarGridSpec(
            num_scalar_prefetch=1, grid=(N_CHIPS, nblk),
            in_specs=[pl.BlockSpec((1, tr, cols), lambda pp, i, c_ref: (pp, c_ref[0] * nblk + i, 0)), blk],
            out_specs=(blk, blk)),
        out_shape=(jax.ShapeDtypeStruct((N_CHIPS, hr, cols), F32), jax.ShapeDtypeStruct((N_CHIPS, hr, cols), BF16)),
        compiler_params=_cp("parallel", "parallel"),
    )(c_arr, g, got)


def chip_exchange_rider(cs16):
    n = len(cs16)

    def ops(ins, outs, ssem, rsem):
        x, y, c, p, rels = _place()
        return [_both(pltpu.make_async_remote_copy(
            src_ref=ins[i].at[2 * rx + ry], dst_ref=outs[i].at[r], send_sem=ssem.at[i * 3 + r],
            recv_sem=rsem.at[i * 3 + r], device_id=(rx, ry, c), device_id_type=MESH))
            for i in range(n) for r, (rx, ry) in enumerate(rels)]

    return Rider(cs16, [jax.ShapeDtypeStruct((3,) + a.shape[1:], BF16) for a in cs16], [3 * n, 3 * n], ops)


def chip_sum(name, pc_arr, l, cs32, got, prev):
    _, hr, cols = cs32.shape
    tr = 256
    nblk = hr // tr

    def body(pc_ref, o_ref, g_ref, *rest):
        rest[-1][0] = ((o_ref[0] + g_ref[0].astype(F32)) + g_ref[1].astype(F32)) + g_ref[2].astype(F32)

    return pl.pallas_call(
        body, name=name,
        grid_spec=pltpu.PrefetchScalarGridSpec(
            num_scalar_prefetch=1, grid=(nblk,),
            in_specs=[pl.BlockSpec((1, tr, cols), lambda i, pc: (pc[0], i, 0)),
                      pl.BlockSpec((3, tr, cols), lambda i, pc: (0, i, 0))] + ([] if prev is None else [ANY]),
            out_specs=pl.BlockSpec((1, tr, cols), lambda i, pc: (l, pc[1] * nblk + i, 0))),
        out_shape=jax.ShapeDtypeStruct((DEPTH, 2 * hr, cols), F32),
        input_output_aliases={} if prev is None else {3: 0},
        compiler_params=_cp("arbitrary"),
    )(*((pc_arr, cs32, got) + (() if prev is None else (prev,))))


def grad_pair_share(halves):
    n_w = len(halves)
    n = n_w * DEPTH

    def body(*refs):
        bufs = refs[n_w:2 * n_w]
        ssem, rsem = refs[2 * n_w:]
        x, y, c, _, _ = _place()
        cps = []
        for t in range(n_w):
            for l in range(DEPTH):
                mine = bufs[t].at[l, _half(c, bufs[t].shape[1])]
                cp = pltpu.make_async_remote_copy(src_ref=mine, dst_ref=mine, send_sem=ssem.at[t * DEPTH + l],
                                                  recv_sem=rsem.at[t * DEPTH + l], device_id=(x, y, 1 - c),
                                                  device_id_type=MESH)
                cp.start()
                cps.append(cp)
        for t in range(n_w):
            for l in range(DEPTH):
                theirs = bufs[t].at[l, _half(1 - c, bufs[t].shape[1])]
                pltpu.make_async_remote_copy(src_ref=theirs, dst_ref=theirs, send_sem=ssem.at[t * DEPTH + l],
                                             recv_sem=rsem.at[t * DEPTH + l], device_id=(x, y, 1 - c),
                                             device_id_type=MESH).wait_recv()
        for cp in cps:
            cp.wait_send()

    return pl.pallas_call(
        body, name="grad_pair_share", in_specs=[ANY] * n_w, out_specs=[ANY] * n_w,
        out_shape=[jax.ShapeDtypeStruct(a.shape, F32) for a in halves],
        input_output_aliases={t: t for t in range(n_w)},
        scratch_shapes=[pltpu.SemaphoreType.DMA((n,)), pltpu.SemaphoreType.DMA((n,))],
    )(*halves)


SMALL_ROWS = 240


def small_allreduce(v):
    def body(v_ref, o_ref, buf, ssem, rsem):
        x, y, c, _, _ = _place()
        me = 4 * x + 2 * y + c
        buf[me] = v_ref[...]
        for d in range(N_DEV):
            @pl.when(me != d)
            def _():
                pltpu.make_async_remote_copy(
                    src_ref=v_ref, dst_ref=buf.at[me], send_sem=ssem.at[d], recv_sem=rsem.at[me],
                    device_id=(d // 4, (d // 2) % 2, d % 2), device_id_type=MESH).start()
        for d in range(N_DEV):
            @pl.when(me != d)
            def _():
                cp = pltpu.make_async_remote_copy(
                    src_ref=v_ref, dst_ref=buf.at[d], send_sem=ssem.at[d], recv_sem=rsem.at[d],
                    device_id=(d // 4, (d // 2) % 2, d % 2), device_id_type=MESH)
                cp.wait_recv()
                cp.wait_send()
        acc = buf[0]
        for d in range(1, N_DEV):
            acc = acc + buf[d]
        o_ref[...] = acc

    vm = pl.BlockSpec(memory_space=pltpu.VMEM)
    return pl.pallas_call(
        body, name="small_allreduce", in_specs=[vm], out_specs=vm,
        out_shape=jax.ShapeDtypeStruct(v.shape, F32),
        scratch_shapes=[pltpu.VMEM((N_DEV,) + v.shape, F32), pltpu.SemaphoreType.DMA((N_DEV,)),
                        pltpu.SemaphoreType.DMA((N_DEV,))],
    )(v)


class GradReducer:
    def __init__(self):
        self.c_arr = lax.axis_index("c").astype(jnp.int32).reshape(1)
        self.pc_arr = jnp.stack([2 * lax.axis_index("x") + lax.axis_index("y"), lax.axis_index("c")]).astype(jnp.int32)
        self.fresh = None
        self.paired = None
        self.acc = [None] * 4

    def push(self, l, gs):
        self.fresh = (l, list(gs))

    def pair_rider(self):
        return pair_exchange_rider(self.fresh[1]) if self.fresh else None

    def pair_done(self, got):
        l, gs = self.fresh
        self.fresh = None
        self.paired = (l, [pair_add(f"pair_add{t}", self.c_arr, g, r) for t, (g, r) in enumerate(zip(gs, got))])

    PARTS = ((1, 2, 3), (0,))

    def chip_rider(self, part):
        return chip_exchange_rider([self.paired[1][t][1] for t in self.PARTS[part]]) if self.paired else None

    def chip_done(self, part, arrived):
        l, cs = self.paired
        for t, got in zip(self.PARTS[part], arrived):
            self.acc[t] = chip_sum(f"chip_sum{t}", self.pc_arr, l, cs[t][0], got, self.acc[t])
        if part == len(self.PARTS) - 1:
            self.paired = None

    def finish(self):
        self.pair_done(run_alone("grad_pair_exchange", self.pair_rider()))
        for part in range(len(self.PARTS)):
            self.chip_done(part, run_alone(f"grad_chip_exchange{part}", self.chip_rider(part)))
        return grad_pair_share(self.acc)


def adamw(name, w, g, m, v):
    shape = w.shape
    cols = shape[-1]
    flat = [t.reshape(-1, cols) for t in (w, g, m, v)]
    rows = flat[0].shape[0]
    tr = 128 if rows % 128 == 0 else rows

    def body(w_ref, g_ref, m_ref, v_ref, d_ref, mo_ref, vo_ref):
        gv = g_ref[...]
        mn = ADAM_B1 * m_ref[...] + (1.0 - ADAM_B1) * gv
        vn = ADAM_B2 * v_ref[...] + (1.0 - ADAM_B2) * jnp.square(gv)
        m_hat = mn / (1.0 - ADAM_B1 ** ADAM_STEP)
        v_hat = vn / (1.0 - ADAM_B2 ** ADAM_STEP)
        d_ref[...] = -ADAM_LR * (m_hat / (jnp.sqrt(v_hat) + ADAM_EPS) + ADAM_WD * w_ref[...])
        mo_ref[...] = mn
        vo_ref[...] = vn

    blk = pl.BlockSpec((tr, cols), lambda i: (i, 0))
    out = jax.ShapeDtypeStruct((rows, cols), F32)
    d, mo, vo = pl.pallas_call(
        body, name=name, grid=(rows // tr,), in_specs=[blk] * 4, out_specs=(blk, blk, blk),
        out_shape=(out, out, out), compiler_params=_cp("parallel"),
    )(*flat)
    return d.reshape(shape), mo.reshape(shape), vo.reshape(shape)


SMALL_NAMES = ("norm_mix", "norm_mlp", "hg_lb_fwd", "hg_lb_bwd", "hg_norm", "ret_norm", "q_norm", "k_norm", "rel_bias")


def _pack_small(d):
    flat = jnp.concatenate([d[k].reshape(-1) for k in SMALL_NAMES])
    return jnp.pad(flat, (0, SMALL_ROWS * 128 - flat.shape[0])).reshape(SMALL_ROWS, 128)


def _unpack_small(v, like):
    flat = v.reshape(-1)
    out, off = {}, 0
    for k in SMALL_NAMES:
        n = like[k].size
        out[k] = flat[off:off + n].reshape(like[k].shape)
        off += n
    return out


def kernel(x, w_in, w_out, w_up, w_down, norm_mix, norm_mlp, hg_lb_fwd, hg_lb_bwd, hg_norm, ret_norm, q_norm, k_norm, rel_bias, loss_target, m_w_in, m_w_out, m_w_up, m_w_down, m_norm_mix, m_norm_mlp, m_hg_lb_fwd, m_hg_lb_bwd, m_hg_norm, m_ret_norm, m_q_norm, m_k_norm, m_rel_bias, v_w_in, v_w_out, v_w_up, v_w_down, v_norm_mix, v_norm_mlp, v_hg_lb_fwd, v_hg_lb_bwd, v_hg_norm, v_ret_norm, v_q_norm, v_k_norm, v_rel_bias):
    big_w = (w_in, w_out, w_up, w_down)
    big_m = (m_w_in, m_w_out, m_w_up, m_w_down)
    big_v = (v_w_in, v_w_out, v_w_up, v_w_down)
    small_w = dict(zip(SMALL_NAMES, (norm_mix, norm_mlp, hg_lb_fwd, hg_lb_bwd, hg_norm, ret_norm, q_norm, k_norm, rel_bias)))
    small_m = dict(zip(SMALL_NAMES, (m_norm_mix, m_norm_mlp, m_hg_lb_fwd, m_hg_lb_bwd, m_hg_norm, m_ret_norm, m_q_norm,
                                     m_k_norm, m_rel_bias)))
    small_v = dict(zip(SMALL_NAMES, (v_norm_mix, v_norm_mlp, v_hg_lb_fwd, v_hg_lb_bwd, v_hg_norm, v_ret_norm, v_q_norm,
                                     v_k_norm, v_rel_bias)))

    gatherer = WeightGatherer(big_w)
    loss_part, dx, grads_big, sg, (lbf_vjp, lbb_vjp) = local_step(x[0], loss_target[0], gatherer.w, small_w,
                                                                  GradReducer(), gatherer)
    loss = lax.psum(loss_part, ("x", "y", "c"))

    sg = dict(sg)
    sg["hg_lb_fwd"], sg["hg_lb_bwd"] = sg.pop("lbf"), sg.pop("lbb")
    tot = _unpack_small(small_allreduce(_pack_small(sg)), small_w)
    tot["hg_lb_fwd"] = lbf_vjp(tot["hg_lb_fwd"])[0]
    tot["hg_lb_bwd"] = lbb_vjp(tot["hg_lb_bwd"])[0]
    grads_small = [tot[k] for k in SMALL_NAMES]

    upd_big = [adamw(f"adamw_big{t}", big_w[t], grads_big[t], big_m[t], big_v[t]) for t in range(4)]
    d_s, m_s, v_s = adamw("adamw_small", _pack_small(small_w), _pack_small(tot), _pack_small(small_m), _pack_small(small_v))
    upd_small = [_unpack_small(t, small_w) for t in (d_s, m_s, v_s)]

    outs = [loss, dx[None]] + list(grads_big) + grads_small
    for j in range(3):
        outs += [u[j] for u in upd_big] + [upd_small[j][k] for k in SMALL_NAMES]
    return tuple(outs)
```

```python
import functools
import math

import jax
import jax.numpy as jnp
from jax import lax
from jax.experimental import pallas as pl
from jax.experimental.pallas import tpu as pltpu

F32 = jnp.float32
BF16 = jnp.bfloat16
EPS = 1e-6

D_MODEL = 2048
DEPTH = 4
HG_HEADS = 6
HG_W = 768
RET_HEADS = 6
RET_DK = 64
RET_W = 768
RET_QK_W = RET_HEADS * RET_DK
RET_CHUNK = 128
ROPE_BASE = 10000.0
DIL_SLOTS = 4
DIL_HD = 128
DIL_GROUPS = ((128, 1), (512, 4), (2048, 16))
DIL_HALF = 64
DIL_W = 512
D_FF = 4 * D_MODEL
IN_W = 10752
REL_BUCKETS = 32
REL_MAX_DIST = 1024

OFF_HG_Q, OFF_HG_V, OFF_HG_ZF, OFF_HG_ZB, OFF_HG_GATE = 0, 768, 1536, 2304, 3072
OFF_RET_Q, OFF_RET_K, OFF_RET_V, OFF_RET_GATE = 3840, 4224, 4608, 5376
OFF_DIL = 6144

N_CHIPS = 4
N_DEV = 8
IN_SHARD = IN_W // N_CHIPS
FF_SHARD = D_FF // N_CHIPS

ADAM_LR, ADAM_B1, ADAM_B2, ADAM_EPS, ADAM_WD, ADAM_STEP = 0.001, 0.9, 0.999, 1e-08, 0.01, 10

VMEM_LIMIT = 56 * 1024 * 1024
HG_T = 512
HG_C = 64
RET_T = 256
DIL_TQ = 256
NEG = -1e30

NN = (((1,), (0,)), ((), ()))
NT = (((1,), (1,)), ((), ()))
TN = (((0,), (0,)), ((), ()))
MESH = pl.DeviceIdType.MESH
ANY = pl.BlockSpec(memory_space=pl.ANY)


def _cp(*sem):
    return pltpu.CompilerParams(dimension_semantics=sem, vmem_limit_bytes=VMEM_LIMIT)


def _mxu(a, b, dn):
    return lax.dot_general(a.astype(BF16), b.astype(BF16), dn, preferred_element_type=F32)


@jax.custom_vjp
def dot_nn(a, b):
    return _mxu(a, b, NN)


dot_nn.defvjp(lambda a, b: (_mxu(a, b, NN), (a, b)),
              lambda r, g: (_mxu(g, r[1], NT), _mxu(r[0], g, TN)))


@jax.custom_vjp
def dot_nt(a, b):
    return _mxu(a, b, NT)


dot_nt.defvjp(lambda a, b: (_mxu(a, b, NT), (a, b)),
              lambda r, g: (_mxu(g, r[1], NN), _mxu(g, r[0], TN)))


@jax.custom_vjp
def dot_tn(a, b):
    return _mxu(a, b, TN)


dot_tn.defvjp(lambda a, b: (_mxu(a, b, TN), (a, b)),
              lambda r, g: (_mxu(r[1], g, NT), _mxu(r[0], g, NN)))


def _split3(v):
    hi = v.astype(BF16)
    r1 = v - hi.astype(F32)
    mid = r1.astype(BF16)
    lo = (r1 - mid.astype(F32)).astype(BF16)
    return hi, mid, lo


def _exact_mask_dot(m, v, dn):
    mb = m.astype(BF16)
    hi, mid, lo = _split3(v)
    f = lambda p: lax.dot_general(mb, p, dn, preferred_element_type=F32)
    return (f(lo) + f(mid)) + f(hi)


@jax.custom_vjp
def cumdot(m, v):
    return _exact_mask_dot(m, v, NN)


cumdot.defvjp(lambda m, v: (_exact_mask_dot(m, v, NN), m),
              lambda m, g: (jnp.zeros_like(m), _exact_mask_dot(m, g, TN)))


def _sigmoid(z):
    return 1.0 / (1.0 + jnp.exp(-z))


def _head_rms(t, g):
    return t * lax.rsqrt(jnp.mean(t * t, axis=-1, keepdims=True) + EPS) * g


class Rider:
    def __init__(self, arrays, out_shapes, sems, ops):
        self.arrays, self.sems, self.ops = list(arrays), list(sems), ops
        self.in_place = out_shapes is None
        self.out_shapes = [jax.ShapeDtypeStruct(a.shape, a.dtype) for a in arrays] if self.in_place else list(out_shapes)

    def aliases(self, n_in, n_out):
        return {n_in + i: n_out + i for i in range(len(self.arrays))} if self.in_place else {}

    def start(self, ins, outs, sems):
        for send, _ in self.ops(ins, outs, *sems):
            send.start()

    def finish(self, ins, outs, sems):
        cps = self.ops(ins, outs, *sems)
        for _, arrive in cps:
            arrive.wait_recv()
        for send, _ in cps:
            send.wait_send()

    def __add__(self, other):
        assert self.in_place and other.in_place
        na, sa = len(self.arrays), len(self.sems)
        ops = lambda ins, outs, *sems: (self.ops(ins[:na], outs[:na], *sems[:sa])
                                        + other.ops(ins[na:], outs[na:], *sems[sa:]))
        return Rider(self.arrays + other.arrays, None, self.sems + other.sems, ops)


def _hosted(name, body, rider, *, grid, in_specs, out_specs, out_shape, scratch_shapes, sem, operands):
    n_in, n_out, n_scr = len(in_specs), len(out_specs), len(scratch_shapes)
    r_in = len(rider.arrays) if rider else 0
    r_out = len(rider.out_shapes) if rider else 0
    last = tuple(g - 1 for g in grid)

    def kernel_body(*refs):
        ins, refs = refs[:n_in], refs[n_in:]
        rins, refs = refs[:r_in], refs[r_in:]
        outs, refs = refs[:n_out], refs[n_out:]
        routs, refs = refs[:r_out], refs[r_out:]
        scr, rsems = refs[:n_scr], refs[n_scr:]
        if rider:
            ids = [pl.program_id(d) for d in range(len(grid))]
            first = functools.reduce(lambda p, q: p & q, [i == 0 for i in ids])
            done = functools.reduce(lambda p, q: p & q, [i == e for i, e in zip(ids, last)])
            pl.when(first)(lambda: rider.start(rins, routs, rsems))
        body(ins, outs, scr)
        if rider:
            pl.when(done)(lambda: rider.finish(rins, routs, rsems))

    res = pl.pallas_call(
        kernel_body, name=name, grid=grid,
        in_specs=list(in_specs) + [ANY] * r_in,
        out_specs=list(out_specs) + [ANY] * r_out,
        out_shape=list(out_shape) + (rider.out_shapes if rider else []),
        scratch_shapes=list(scratch_shapes) + ([pltpu.SemaphoreType.DMA((n,)) for n in rider.sems] if rider else []),
        input_output_aliases=rider.aliases(n_in, n_out) if rider else {},
        compiler_params=_cp(*(("arbitrary",) * len(grid) if rider else sem)),
    )(*operands, *(rider.arrays if rider else []))
    return res[:n_out], res[n_out:]


def _mm(name, a, b, *, mode, grid, a_spec, b_spec, tm, tn, extras=(), extra_specs=(), epi, out_shape, out_specs,
        rider=None):
    nk = grid[2]
    single = not isinstance(out_shape, (tuple, list))
    if single:
        out_shape, out_specs = [out_shape], [out_specs]

    def body(ins, outs, scr):
        a_ref, b_ref, ex = ins[0], ins[1], ins[2:]
        part = _mxu(a_ref[...], b_ref[...], {"nn": NN, "nt": NT, "tn": TN}[mode])
        if nk == 1:
            epi(part, ex, outs)
            return
        acc = scr[0]
        k = pl.program_id(2)

        @pl.when(k == 0)
        def _():
            acc[...] = jnp.zeros_like(acc)

        acc[...] += part

        @pl.when(k == nk - 1)
        def _():
            epi(acc[...], ex, outs)

    outs, carried = _hosted(name, body, rider, grid=grid, in_specs=[a_spec, b_spec, *extra_specs],
                            out_specs=out_specs, out_shape=out_shape,
                            scratch_shapes=[] if nk == 1 else [pltpu.VMEM((tm, tn), F32)],
                            sem=("parallel", "parallel", "arbitrary"), operands=(a, b, *extras))
    res = outs[0] if single else tuple(outs)
    return (res, carried) if rider else res


def _epi_store(acc, ex, outs):
    outs[0][...] = acc.astype(outs[0].dtype)


def _epi_residual(acc, ex, outs):
    outs[0][...] = ex[0][...] + acc


def _epi_up(acc, ex, outs):
    outs[0][...] = acc
    outs[1][...] = jnp.square(jnp.maximum(acc, 0.0)).astype(BF16)


def _epi_dact(acc, ex, outs):
    outs[0][...] = (acc * (2.0 * jnp.maximum(ex[0][...], 0.0))).astype(BF16)


def _ij(i, j, k):
    return (i, j)


def proj_in(xn, win_g, rider=None):
    S = xn.shape[0]
    tm, tn = 512, IN_SHARD
    return _mm("proj_in", xn, win_g, mode="nn", grid=(IN_W // tn, S // tm, 1), tm=tm, tn=tn,
               a_spec=pl.BlockSpec((tm, D_MODEL), lambda j, i, k: (i, 0)),
               b_spec=pl.BlockSpec((None, D_MODEL, tn), lambda j, i, k: (j, 0, 0)),
               epi=_epi_store, out_shape=jax.ShapeDtypeStruct((S, IN_W), F32),
               out_specs=pl.BlockSpec((tm, tn), lambda j, i, k: (i, j)), rider=rider)


def proj_out(y, wout_g, x):
    S = y.shape[0]
    tm, tn = 1024, 1024
    return _mm("proj_out", y, wout_g, mode="nn", grid=(S // tm, D_MODEL // tn, 1), tm=tm, tn=tn,
               a_spec=pl.BlockSpec((tm, D_MODEL), lambda i, j, k: (i, 0)),
               b_spec=pl.BlockSpec((D_MODEL, tn), lambda i, j, k: (0, j)),
               extras=(x,), extra_specs=(pl.BlockSpec((tm, tn), _ij),),
               epi=_epi_residual, out_shape=jax.ShapeDtypeStruct((S, D_MODEL), F32),
               out_specs=pl.BlockSpec((tm, tn), _ij))


def proj_up(hm, wup_g, rider=None):
    S = hm.shape[0]
    tm, tn = 1024, 1024
    return _mm("proj_up", hm, wup_g, mode="nn", grid=(S // tm, D_FF // tn, 1), tm=tm, tn=tn,
               a_spec=pl.BlockSpec((tm, D_MODEL), lambda i, j, k: (i, 0)),
               b_spec=pl.BlockSpec((None, D_MODEL, tn), lambda i, j, k: (j // 2, 0, j % 2)),
               epi=_epi_up,
               out_shape=(jax.ShapeDtypeStruct((S, D_FF), F32), jax.ShapeDtypeStruct((S, D_FF), BF16)),
               out_specs=(pl.BlockSpec((tm, tn), _ij), pl.BlockSpec((tm, tn), _ij)), rider=rider)


def proj_down(a, wdown_g, x, rider=None):
    S = a.shape[0]
    tm, tn, tk = 1024, 1024, 2048
    return _mm("proj_down", a, wdown_g, mode="nn", grid=(S // tm, D_MODEL // tn, D_FF // tk), tm=tm, tn=tn,
               a_spec=pl.BlockSpec((tm, tk), lambda i, j, k: (i, k)),
               b_spec=pl.BlockSpec((tk, tn), lambda i, j, k: (k, j)),
               extras=(x,), extra_specs=(pl.BlockSpec((tm, tn), _ij),),
               epi=_epi_residual, out_shape=jax.ShapeDtypeStruct((S, D_MODEL), F32),
               out_specs=pl.BlockSpec((tm, tn), _ij), rider=rider)


def bwd_down_act(dx, wdown_g, u, rider=None):
    S = dx.shape[0]
    tm, tn = 1024, 1024
    return _mm("bwd_down_act", dx, wdown_g, mode="nt", grid=(S // tm, D_FF // tn, 1), tm=tm, tn=tn,
               a_spec=pl.BlockSpec((tm, D_MODEL), lambda i, j, k: (i, 0)),
               b_spec=pl.BlockSpec((tn, D_MODEL), lambda i, j, k: (j, 0)),
               extras=(u,), extra_specs=(pl.BlockSpec((tm, tn), _ij),),
               epi=_epi_dact, out_shape=jax.ShapeDtypeStruct((S, D_FF), BF16),
               out_specs=pl.BlockSpec((tm, tn), _ij), rider=rider)


def bwd_up(du, wup_g):
    S = du.shape[0]
    tm, tn, tk = 1024, 1024, FF_SHARD
    return _mm("bwd_up", du, wup_g, mode="nt", grid=(S // tm, D_MODEL // tn, D_FF // tk), tm=tm, tn=tn,
               a_spec=pl.BlockSpec((tm, tk), lambda i, j, k: (i, k)),
               b_spec=pl.BlockSpec((None, tn, tk), lambda i, j, k: (k, j, 0)),
               epi=_epi_store, out_shape=jax.ShapeDtypeStruct((S, D_MODEL), F32),
               out_specs=pl.BlockSpec((tm, tn), _ij))


def bwd_out(dx, wout_g):
    S = dx.shape[0]
    tm, tn = 1024, 1024
    return _mm("bwd_out", dx, wout_g, mode="nt", grid=(S // tm, D_MODEL // tn, 1), tm=tm, tn=tn,
               a_spec=pl.BlockSpec((tm, D_MODEL), lambda i, j, k: (i, 0)),
               b_spec=pl.BlockSpec((tn, D_MODEL), lambda i, j, k: (j, 0)),
               epi=_epi_store, out_shape=jax.ShapeDtypeStruct((S, D_MODEL), F32),
               out_specs=pl.BlockSpec((tm, tn), _ij))


def bwd_in(dh, win_g, rider=None):
    S = dh.shape[0]
    tm, tn, tk = 1024, 1024, IN_SHARD
    return _mm("bwd_in", dh, win_g, mode="nt", grid=(S // tm, D_MODEL // tn, IN_W // tk), tm=tm, tn=tn,
               a_spec=pl.BlockSpec((tm, tk), lambda i, j, k: (i, k)),
               b_spec=pl.BlockSpec((None, tn, tk), lambda i, j, k: (k, j, 0)),
               epi=_epi_store, out_shape=jax.ShapeDtypeStruct((S, D_MODEL), F32),
               out_specs=pl.BlockSpec((tm, tn), _ij), rider=rider)


def wgrad(name, a, g, *, m, n, n_shard):
    S = a.shape[0]
    tm, tn, tk = (512, IN_SHARD, 1024) if n_shard == IN_SHARD else (1024, 1024, 1024)
    per = n_shard // tn
    if n_shard == n:
        out_shape = jax.ShapeDtypeStruct((m, n), F32)
        out_spec = pl.BlockSpec((tm, tn), _ij)
    else:
        out_shape = jax.ShapeDtypeStruct((N_CHIPS, m, n_shard), F32)
        out_spec = pl.BlockSpec((None, tm, tn), lambda i, j, k: (j // per, i, j % per))
    return _mm(name, a, g, mode="tn", grid=(m // tm, n // tn, S // tk), tm=tm, tn=tn,
               a_spec=pl.BlockSpec((tk, tm), lambda i, j, k: (k, i)),
               b_spec=pl.BlockSpec((tk, tn), lambda i, j, k: (k, j)),
               epi=_epi_store, out_shape=out_shape, out_specs=out_spec)


NORM_T = 256


def rmsnorm_fwd(x, g):
    S = x.shape[0]

    def body(x_ref, g_ref, o_ref):
        xv = x_ref[...]
        r = lax.rsqrt(jnp.mean(xv * xv, axis=-1, keepdims=True) + EPS)
        o_ref[...] = ((xv * r) * g_ref[...]).astype(BF16)

    return pl.pallas_call(
        body, name="rmsnorm_fwd", grid=(S // NORM_T,),
        in_specs=[pl.BlockSpec((NORM_T, D_MODEL), lambda i: (i, 0)), pl.BlockSpec((1, D_MODEL), lambda i: (0, 0))],
        out_specs=pl.BlockSpec((NORM_T, D_MODEL), lambda i: (i, 0)),
        out_shape=jax.ShapeDtypeStruct((S, D_MODEL), BF16), compiler_params=_cp("parallel"),
    )(x, g)


def rmsnorm_bwd(x, g, dxn, dres):
    S = x.shape[0]

    def body(x_ref, g_ref, dxn_ref, dres_ref, dx_ref, dg_ref):
        @pl.when(pl.program_id(0) == 0)
        def _():
            dg_ref[...] = jnp.zeros_like(dg_ref)

        xv, gv, d = x_ref[...], g_ref[...], dxn_ref[...]
        r = lax.rsqrt(jnp.mean(xv * xv, axis=-1, keepdims=True) + EPS)
        gd = gv * d
        dx_ref[...] = dres_ref[...] + r * gd - xv * ((r * r * r) * jnp.mean(xv * gd, axis=-1, keepdims=True))
        dg_ref[...] += jnp.sum(d * (xv * r), axis=0, keepdims=True)

    row = pl.BlockSpec((NORM_T, D_MODEL), lambda i: (i, 0))
    vec = pl.BlockSpec((1, D_MODEL), lambda i: (0, 0))
    return pl.pallas_call(
        body, name="rmsnorm_bwd", grid=(S // NORM_T,),
        in_specs=[row, vec, row, row], out_specs=(row, vec),
        out_shape=(jax.ShapeDtypeStruct((S, D_MODEL), F32), jax.ShapeDtypeStruct((1, D_MODEL), F32)),
        compiler_params=_cp("arbitrary"),
    )(x, g, dxn, dres)


def loss_head(y, tgt):
    S = y.shape[0]

    def body(y_ref, t_ref, dy_ref, l_ref):
        @pl.when(pl.program_id(0) == 0)
        def _():
            l_ref[...] = jnp.zeros_like(l_ref)

        e = y_ref[...] - t_ref[...]
        dy_ref[...] = e * (1.0 / D_MODEL)
        l_ref[...] += jnp.sum(e * e) * (0.5 / D_MODEL)

    row = pl.BlockSpec((NORM_T, D_MODEL), lambda i: (i, 0))
    return pl.pallas_call(
        body, name="loss_head", grid=(S // NORM_T,),
        in_specs=[row, row], out_specs=(row, pl.BlockSpec((1, 128), lambda i: (0, 0))),
        out_shape=(jax.ShapeDtypeStruct((S, D_MODEL), F32), jax.ShapeDtypeStruct((1, 128), F32)),
        compiler_params=_cp("arbitrary"),
    )(y, tgt)


def _hg_block(qs, vs, zs, lb, sT, reverse):
    n = len(qs)
    row = lax.broadcasted_iota(jnp.int32, (HG_C, HG_C), 0)
    col = lax.broadcasted_iota(jnp.int32, (HG_C, HG_C), 1)
    tri = (row <= col) if reverse else (row >= col)
    m = tri.astype(F32)
    rsel = lax.broadcasted_iota(jnp.int32, (HG_C, 128), 0)
    ref_rows = ((rsel >= HG_C // 2) if reverse else (rsel <= HG_C // 2)).astype(F32)
    att, qdec, upd, keep = [None] * n, [None] * n, [None] * n, [None] * n
    for c in range(n):
        f = lb + (1.0 - lb) * _sigmoid(zs[c])
        kc = 1.0 - f
        lc = jnp.log(f)
        b = cumdot(m, lc)
        btot = jnp.sum(lc, axis=0, keepdims=True)
        bref = lax.stop_gradient(jnp.sum(lc * ref_rows, axis=0, keepdims=True))
        qe = qs[c] * jnp.exp(jnp.minimum(b - bref, 80.0))
        ke = kc * jnp.exp(jnp.minimum(bref - b, 80.0))
        att[c] = jnp.where(tri, dot_nt(qe, ke), 0.0)
        qdec[c] = qs[c] * jnp.exp(b)
        upd[c] = dot_tn(vs[c], kc * jnp.exp(btot - b))
        keep[c] = jnp.exp(btot)
    states = [None] * n
    for c in (range(n - 1, -1, -1) if reverse else range(n)):
        states[c] = sT
        sT = sT * keep[c] + upd[c]
    outs = [dot_nn(att[c], vs[c]) + dot_nt(qdec[c], states[c]) for c in range(n)]
    return outs, sT


def _chunks(ref, c, n):
    return [ref[i * c:(i + 1) * c, :] for i in range(n)]


def hg_scan_fwd(h, lbf, lbb):
    S = h.shape[0]
    nb = S // HG_T
    n = HG_T // HG_C

    def body(ins, outs, scr):
        qf, vf, zf, qb, vb, zb, lbf_ref, lbb_ref = ins
        of_ref, ob_ref, sf_ref, sb_ref = outs
        stf, stb = scr

        @pl.when(pl.program_id(1) == 0)
        def _():
            stf[...] = jnp.zeros_like(stf)
            stb[...] = jnp.zeros_like(stb)

        for (q, v, z, lb_ref, o_ref, s_ref, st, rev) in ((qf, vf, zf, lbf_ref, of_ref, sf_ref, stf, False),
                                                         (qb, vb, zb, lbb_ref, ob_ref, sb_ref, stb, True)):
            s_ref[0, 0] = st[...]
            outs, s_new = _hg_block(_chunks(q, HG_C, n), _chunks(v, HG_C, n), _chunks(z, HG_C, n),
                                    lb_ref[...], st[...], rev)
            for c in range(n):
                o_ref[c * HG_C:(c + 1) * HG_C, :] = outs[c]
            st[...] = s_new

    def col(off, rev):
        return pl.BlockSpec((HG_T, 128), (lambda hh, t: (nb - 1 - t, off // 128 + hh)) if rev
                            else (lambda hh, t: (t, off // 128 + hh)))

    lb_spec = pl.BlockSpec((1, 128), lambda hh, t: (0, hh))
    st_f = pl.BlockSpec((1, 1, 128, 128), lambda hh, t: (hh, t, 0, 0))
    st_b = pl.BlockSpec((1, 1, 128, 128), lambda hh, t: (hh, nb - 1 - t, 0, 0))
    outs, carried = _hosted(
        "hg_scan_fwd", body, None, grid=(HG_HEADS, nb),
        in_specs=[col(OFF_HG_Q, False), col(OFF_HG_V, False), col(OFF_HG_ZF, False),
                  col(OFF_HG_Q, True), col(OFF_HG_V, True), col(OFF_HG_ZB, True), lb_spec, lb_spec],
        out_specs=[col(0, False), col(0, True), st_f, st_b],
        out_shape=[jax.ShapeDtypeStruct((S, HG_W), F32), jax.ShapeDtypeStruct((S, HG_W), F32),
                   jax.ShapeDtypeStruct((HG_HEADS, nb, 128, 128), F32),
                   jax.ShapeDtypeStruct((HG_HEADS, nb, 128, 128), F32)],
        scratch_shapes=[pltpu.VMEM((128, 128), F32), pltpu.VMEM((128, 128), F32)],
        sem=("arbitrary", "arbitrary"), operands=(h, h, h, h, h, h, lbf, lbb))
    return tuple(outs)


def hg_scan_bwd(h, lbf, lbb, do, sf, sb, rider=None):
    S = h.shape[0]
    nb = S // HG_T
    n = HG_T // HG_C

    def body(ins, outs, scr):
        qf, vf, zf, dof, sfin, qb, vb, zb, dob, sbin, lbf_ref, lbb_ref = ins
        dqf, dvf, dzf, dqb, dvb, dzb, dlbf, dlbb = outs
        dsf, dsb = scr

        @pl.when(pl.program_id(1) == 0)
        def _():
            for r in (dsf, dsb, dlbf, dlbb):
                r[...] = jnp.zeros_like(r)

        for (q, v, z, dor, sin, lb_ref, dq, dv, dz, dlb, ds, rev) in (
                (qf, vf, zf, dof, sfin, lbf_ref, dqf, dvf, dzf, dlbf, dsf, False),
                (qb, vb, zb, dob, sbin, lbb_ref, dqb, dvb, dzb, dlbb, dsb, True)):
            fn = functools.partial(_hg_block, reverse=rev)
            _, vjp = jax.vjp(fn, _chunks(q, HG_C, n), _chunks(v, HG_C, n), _chunks(z, HG_C, n), lb_ref[...], sin[0, 0])
            dqs, dvs, dzs, dlb_v, ds_in = vjp((_chunks(dor, HG_C, n), ds[...]))
            for c in range(n):
                sl = slice(c * HG_C, (c + 1) * HG_C)
                dq[sl, :] = dqs[c]
                dv[sl, :] = dvs[c]
                dz[sl, :] = dzs[c]
            dlb[...] += dlb_v
            ds[...] = ds_in

    def col(off, fwd_scan):
        return pl.BlockSpec((HG_T, 128), (lambda hh, t: (nb - 1 - t, off // 128 + hh)) if fwd_scan
                            else (lambda hh, t: (t, off // 128 + hh)))

    lb_spec = pl.BlockSpec((1, 128), lambda hh, t: (0, hh))
    st_f = pl.BlockSpec((1, 1, 128, 128), lambda hh, t: (hh, nb - 1 - t, 0, 0))
    st_b = pl.BlockSpec((1, 1, 128, 128), lambda hh, t: (hh, t, 0, 0))
    full = jax.ShapeDtypeStruct((S, HG_W), F32)
    vec = jax.ShapeDtypeStruct((1, HG_W), F32)
    outs, carried = _hosted(
        "hg_scan_bwd", body, rider, grid=(HG_HEADS, nb),
        in_specs=[col(OFF_HG_Q, True), col(OFF_HG_V, True), col(OFF_HG_ZF, True), col(0, True), st_f,
                  col(OFF_HG_Q, False), col(OFF_HG_V, False), col(OFF_HG_ZB, False), col(0, False), st_b,
                  lb_spec, lb_spec],
        out_specs=[col(0, True), col(0, True), col(0, True), col(0, False), col(0, False), col(0, False),
                   lb_spec, lb_spec],
        out_shape=[full, full, full, full, full, full, vec, vec],
        scratch_shapes=[pltpu.VMEM((128, 128), F32), pltpu.VMEM((128, 128), F32)],
        sem=("arbitrary", "arbitrary"), operands=(h, h, h, do, sf, h, h, h, do, sb, lbf, lbb))
    return (tuple(outs), carried) if rider else tuple(outs)


GN_T = 1024


def _gated_norm(o, gate, g, center):
    if center:
        o = o - jnp.mean(o, axis=-1, keepdims=True)
    o = o * lax.rsqrt(jnp.mean(o * o, axis=-1, keepdims=True) + EPS)
    return (o * g) * (gate * _sigmoid(gate))


def gated_norm_fwd(name, of, ob, h, gate_off, g, center):
    S = of.shape[0]

    def body(of_ref, ob_ref, gate_ref, g_ref, y_ref):
        y_ref[...] = _gated_norm(of_ref[...] + ob_ref[...], gate_ref[...], g_ref[...], center)

    blk = pl.BlockSpec((GN_T, 128), lambda hh, i: (i, hh))
    return pl.pallas_call(
        body, name=name, grid=(6, S // GN_T),
        in_specs=[blk, blk, pl.BlockSpec((GN_T, 128), lambda hh, i: (i, gate_off // 128 + hh)),
                  pl.BlockSpec((1, 128), lambda hh, i: (0, hh))],
        out_specs=blk, out_shape=jax.ShapeDtypeStruct((S, 768), F32),
        compiler_params=_cp("parallel", "parallel"),
    )(of, ob, h, g)


def gated_norm_bwd(name, of, ob, h, gate_off, g, dy, dy_off, center):
    S = of.shape[0]

    def body(of_ref, ob_ref, gate_ref, g_ref, dy_ref, do_ref, dgate_ref, dg_ref):
        @pl.when(pl.program_id(1) == 0)
        def _():
            dg_ref[...] = jnp.zeros_like(dg_ref)

        fn = functools.partial(_gated_norm, center=center)
        _, vjp = jax.vjp(fn, of_ref[...] + ob_ref[...], gate_ref[...], g_ref[...])
        do, dgate, dg = vjp(dy_ref[...])
        do_ref[...] = do
        dgate_ref[...] = dgate
        dg_ref[...] += dg

    blk = pl.BlockSpec((GN_T, 128), lambda hh, i: (i, hh))
    vec = pl.BlockSpec((1, 128), lambda hh, i: (0, hh))
    return pl.pallas_call(
        body, name=name, grid=(6, S // GN_T),
        in_specs=[blk, blk, pl.BlockSpec((GN_T, 128), lambda hh, i: (i, gate_off // 128 + hh)), vec,
                  pl.BlockSpec((GN_T, 128), lambda hh, i: (i, dy_off // 128 + hh))],
        out_specs=(blk, blk, vec),
        out_shape=(jax.ShapeDtypeStruct((S, 768), F32), jax.ShapeDtypeStruct((S, 768), F32),
                   jax.ShapeDtypeStruct((1, 768), F32)),
        compiler_params=_cp("arbitrary", "arbitrary"),
    )(of, ob, h, g, dy)


def _ret_consts(S):
    half = RET_DK // 2
    inv = ROPE_BASE ** (-jnp.arange(half, dtype=F32) / half)
    ang = jnp.arange(S, dtype=F32)[:, None] * inv[None, :]
    cos, sin = jnp.cos(ang), jnp.sin(ang)
    cos_t = jnp.tile(jnp.concatenate([cos, cos], axis=1), (1, RET_HEADS))
    sin_t = jnp.tile(jnp.concatenate([-sin, sin], axis=1), (1, RET_HEADS))
    hidx = jnp.arange(RET_HEADS, dtype=F32)
    lg_f = jnp.log1p(-jnp.exp2(-5.0 - hidx))
    C = RET_CHUNK
    idx = jnp.arange(C, dtype=F32)
    rel = idx[:, None] - idx[None, :]

    def one(lg, reverse):
        lgc = lg[:, None]
        decay = jnp.where(rel >= 0, jnp.exp(lgc[:, :, None] * jnp.maximum(rel, 0.0)), 0.0)
        zeta = jnp.exp(lgc * (C - 1 - idx))
        xi = jnp.exp(lgc * (idx + 1))
        if reverse:
            decay = decay[:, ::-1, ::-1]
            zeta, xi = zeta[:, ::-1], xi[:, ::-1]
        wide = lambda t: jnp.repeat(t.T, RET_DK, axis=1)
        gam_w = jnp.broadcast_to(jnp.repeat(jnp.exp(lg * C), 128)[None, :], (8, RET_W))
        return decay, wide(xi), wide(zeta), gam_w

    hm = (jnp.arange(RET_QK_W)[None, :] // RET_DK == jnp.arange(8)[:, None]).astype(F32)
    return (cos_t, sin_t, hm) + one(lg_f, False) + one(lg_f[::-1], True)


def _rope(t, cos, sin_signed):
    lane = lax.broadcasted_iota(jnp.int32, t.shape, 1)
    first = (lane & (RET_DK - 1)) < RET_DK // 2
    partner = jnp.where(first, pltpu.roll(t, RET_QK_W - RET_DK // 2, 1), pltpu.roll(t, RET_DK // 2, 1))
    return t * cos + partner * sin_signed


def _ret_block(qs, ks, vs, st, dec, xi, zeta, gam, hms, reverse):
    n = len(qs)
    heads = range(RET_HEADS)
    qx = [q * xi for q in qs]
    kz = [k * zeta for k in ks]
    sc = [[dot_nt(qs[c] * hms[hh], ks[c]) * dec[hh] for hh in heads] for c in range(n)]
    upd = [[dot_tn(kz[c] * hms[hh], vs[c][hh]) for hh in heads] for c in range(n)]
    st = list(st)
    seen = [None] * n
    for c in (range(n - 1, -1, -1) if reverse else range(n)):
        seen[c] = list(st)
        st = [st[hh] * gam[hh] + upd[c][hh] for hh in heads]
    outs = [[dot_nn(sc[c][hh], vs[c][hh]) + dot_nn(qx[c], seen[c][hh]) for hh in heads] for c in range(n)]
    return outs, st


def _ret_inputs(q_ref, k_ref, v_ref, cos_ref, sin_ref):
    n = RET_T // RET_CHUNK
    qr = _rope(q_ref[...], cos_ref[...], sin_ref[...])
    kr = _rope(k_ref[...], cos_ref[...], sin_ref[...]) * (RET_DK ** -0.5)
    qs = [qr[c * RET_CHUNK:(c + 1) * RET_CHUNK] for c in range(n)]
    ks = [kr[c * RET_CHUNK:(c + 1) * RET_CHUNK] for c in range(n)]
    vs = [[v_ref[c * RET_CHUNK:(c + 1) * RET_CHUNK, hh * 128:(hh + 1) * 128] for hh in range(RET_HEADS)]
          for c in range(n)]
    return qs, ks, vs


def _ret_dir_consts(dec_ref, xi_ref, zeta_ref, gam_ref, hm_ref):
    dec = [dec_ref[hh] for hh in range(RET_HEADS)]
    gam = [gam_ref[0:1, hh * 128:(hh + 1) * 128] for hh in range(RET_HEADS)]
    hms = [hm_ref[hh:hh + 1, :] for hh in range(RET_HEADS)]
    return dec, xi_ref[...], zeta_ref[...], gam, hms


def _ret_rows(nb, rev):
    def rows(width, colblk):
        return pl.BlockSpec((RET_T, width), (lambda t: (nb - 1 - t, colblk)) if rev else (lambda t: (t, colblk)))
    return rows


def _const_spec(shape):
    nd = len(shape)
    return pl.BlockSpec(shape, lambda t: (0,) * nd)


def ret_scan_fwd(h, consts):
    S = h.shape[0]
    nb = S // RET_T
    n = RET_T // RET_CHUNK
    cos_t, sin_t, hm, dec_f, xi_f, zeta_f, gam_f, dec_b, xi_b, zeta_b, gam_b = consts

    def body(ins, outs, scr):
        qf, kf, vf, cf, sf, qb, kb, vb, cb, sb_, hm_ref, decf, xif, zetaf, gamf, decb, xib, zetab, gamb = ins
        of_ref, ob_ref, sfo, sbo = outs
        stf, stb = scr

        @pl.when(pl.program_id(0) == 0)
        def _():
            stf[...] = jnp.zeros_like(stf)
            stb[...] = jnp.zeros_like(stb)

        for (q, k, v, cs, sn, dr, xr, zr, gr, o_ref, so, st, rev) in (
                (qf, kf, vf, cf, sf, decf, xif, zetaf, gamf, of_ref, sfo, stf, False),
                (qb, kb, vb, cb, sb_, decb, xib, zetab, gamb, ob_ref, sbo, stb, True)):
            so[0] = st[...]
            qs, ks, vs = _ret_inputs(q, k, v, cs, sn)
            dec, xi, zeta, gam, hms = _ret_dir_consts(dr, xr, zr, gr, hm_ref)
            st_in = [st[:, hh * 128:(hh + 1) * 128] for hh in range(RET_HEADS)]
            outs, st_new = _ret_block(qs, ks, vs, st_in, dec, xi, zeta, gam, hms, rev)
            for c in range(n):
                for hh in range(RET_HEADS):
                    o_ref[c * RET_CHUNK:(c + 1) * RET_CHUNK, hh * 128:(hh + 1) * 128] = outs[c][hh]
            for hh in range(RET_HEADS):
                st[:, hh * 128:(hh + 1) * 128] = st_new[hh]

    rf, rb = _ret_rows(nb, False), _ret_rows(nb, True)
    cspecs = [_const_spec(a.shape) for a in (hm, dec_f, xi_f, zeta_f, gam_f, dec_b, xi_b, zeta_b, gam_b)]
    st_shape = jax.ShapeDtypeStruct((nb, RET_QK_W, RET_W), F32)
    qc, kc, vc = OFF_RET_Q // RET_QK_W, OFF_RET_K // RET_QK_W, OFF_RET_V // RET_W
    outs, carried = _hosted(
        "ret_scan_fwd", body, None, grid=(nb,),
        in_specs=[rf(RET_QK_W, qc), rf(RET_QK_W, kc), rf(RET_W, vc), rf(RET_QK_W, 0), rf(RET_QK_W, 0),
                  rb(RET_QK_W, qc), rb(RET_QK_W, kc), rb(RET_W, vc), rb(RET_QK_W, 0), rb(RET_QK_W, 0)] + cspecs,
        out_specs=[rf(RET_W, 0), rb(RET_W, 0),
                   pl.BlockSpec((1, RET_QK_W, RET_W), lambda t: (t, 0, 0)),
                   pl.BlockSpec((1, RET_QK_W, RET_W), lambda t: (nb - 1 - t, 0, 0))],
        out_shape=[jax.ShapeDtypeStruct((S, RET_W), F32), jax.ShapeDtypeStruct((S, RET_W), F32), st_shape, st_shape],
        scratch_shapes=[pltpu.VMEM((RET_QK_W, RET_W), F32), pltpu.VMEM((RET_QK_W, RET_W), F32)],
        sem=("arbitrary",),
        operands=(h, h, h, cos_t, sin_t, h, h, h, cos_t, sin_t, hm, dec_f, xi_f, zeta_f, gam_f, dec_b, xi_b, zeta_b,
                  gam_b))
    return tuple(outs)


def ret_scan_bwd(h, consts, do, sf, sb, rider=None):
    S = h.shape[0]
    nb = S // RET_T
    n = RET_T // RET_CHUNK
    cos_t, sin_t, hm, dec_f, xi_f, zeta_f, gam_f, dec_b, xi_b, zeta_b, gam_b = consts

    def body(ins, outs, scr):
        (qf, kf, vf, cf, sf_, dof, sfin, qb, kb, vb, cb, sb_, dob, sbin,
         hm_ref, decf, xif, zetaf, gamf, decb, xib, zetab, gamb) = ins
        dqf, dkf, dvf, dqb, dkb, dvb = outs
        dsf, dsb = scr

        @pl.when(pl.program_id(0) == 0)
        def _():
            dsf[...] = jnp.zeros_like(dsf)
            dsb[...] = jnp.zeros_like(dsb)

        for (q, k, v, cs, sn, dor, sin, dr, xr, zr, gr, dq, dk, dv, ds, rev) in (
                (qf, kf, vf, cf, sf_, dof, sfin, decf, xif, zetaf, gamf, dqf, dkf, dvf, dsf, False),
                (qb, kb, vb, cb, sb_, dob, sbin, decb, xib, zetab, gamb, dqb, dkb, dvb, dsb, True)):
            qs, ks, vs = _ret_inputs(q, k, v, cs, sn)
            dec, xi, zeta, gam, hms = _ret_dir_consts(dr, xr, zr, gr, hm_ref)
            st_in = [sin[0, :, hh * 128:(hh + 1) * 128] for hh in range(RET_HEADS)]
            fn = lambda a, b_, c_, d_: _ret_block(a, b_, c_, d_, dec, xi, zeta, gam, hms, rev)
            _, vjp = jax.vjp(fn, qs, ks, vs, st_in)
            dos = [[dor[c * RET_CHUNK:(c + 1) * RET_CHUNK, hh * 128:(hh + 1) * 128] for hh in range(RET_HEADS)]
                   for c in range(n)]
            dst = [ds[:, hh * 128:(hh + 1) * 128] for hh in range(RET_HEADS)]
            dqs, dks, dvs, dst_in = vjp((dos, dst))
            cosv, sinv = cs[...], sn[...]
            dq[...] = _rope(jnp.concatenate(dqs, axis=0), cosv, -sinv)
            dk[...] = _rope(jnp.concatenate(dks, axis=0) * (RET_DK ** -0.5), cosv, -sinv)
            for c in range(n):
                for hh in range(RET_HEADS):
                    dv[c * RET_CHUNK:(c + 1) * RET_CHUNK, hh * 128:(hh + 1) * 128] = dvs[c][hh]
            for hh in range(RET_HEADS):
                ds[:, hh * 128:(hh + 1) * 128] = dst_in[hh]

    rf, rb = _ret_rows(nb, True), _ret_rows(nb, False)
    cspecs = [_const_spec(a.shape) for a in (hm, dec_f, xi_f, zeta_f, gam_f, dec_b, xi_b, zeta_b, gam_b)]
    qk = jax.ShapeDtypeStruct((S, RET_QK_W), F32)
    vv = jax.ShapeDtypeStruct((S, RET_W), F32)
    qc, kc, vc = OFF_RET_Q // RET_QK_W, OFF_RET_K // RET_QK_W, OFF_RET_V // RET_W
    outs, carried = _hosted(
        "ret_scan_bwd", body, rider, grid=(nb,),
        in_specs=[rf(RET_QK_W, qc), rf(RET_QK_W, kc), rf(RET_W, vc), rf(RET_QK_W, 0), rf(RET_QK_W, 0), rf(RET_W, 0),
                  pl.BlockSpec((1, RET_QK_W, RET_W), lambda t: (nb - 1 - t, 0, 0)),
                  rb(RET_QK_W, qc), rb(RET_QK_W, kc), rb(RET_W, vc), rb(RET_QK_W, 0), rb(RET_QK_W, 0), rb(RET_W, 0),
                  pl.BlockSpec((1, RET_QK_W, RET_W), lambda t: (t, 0, 0))] + cspecs,
        out_specs=[rf(RET_QK_W, 0), rf(RET_QK_W, 0), rf(RET_W, 0), rb(RET_QK_W, 0), rb(RET_QK_W, 0), rb(RET_W, 0)],
        out_shape=[qk, qk, vv, qk, qk, vv],
        scratch_shapes=[pltpu.VMEM((RET_QK_W, RET_W), F32), pltpu.VMEM((RET_QK_W, RET_W), F32)],
        sem=("arbitrary",),
        operands=(h, h, h, cos_t, sin_t, do, sf, h, h, h, cos_t, sin_t, do, sb,
                  hm, dec_f, xi_f, zeta_f, gam_f, dec_b, xi_b, zeta_b, gam_b))
    return (tuple(outs), carried) if rider else tuple(outs)


def _t5_bucket(rel):
    nb = REL_BUCKETS // 2
    max_exact = nb // 2
    sign_off = jnp.where(rel > 0, nb, 0)
    n = jnp.abs(rel)
    nf = jnp.maximum(n, 1).astype(F32)
    large = max_exact + (jnp.log(nf / max_exact) / math.log(REL_MAX_DIST / max_exact)
                         * (nb - max_exact)).astype(jnp.int32)
    large = jnp.minimum(large, nb - 1)
    return sign_off + jnp.where(n < max_exact, n, large)


def _dil_buckets(dil):
    tq, tb = DIL_TQ, DIL_TQ + 2 * DIL_HALF
    rel_q = jnp.arange(tb)[None, :] - DIL_HALF - jnp.arange(tq)[:, None]
    rel_k = jnp.arange(tq)[None, :] + DIL_HALF - jnp.arange(tb)[:, None]
    return _t5_bucket(rel_q * dil), _t5_bucket(rel_k * dil)


def dil_view(h, g, dil):
    base = OFF_DIL + 3 * g * DIL_W
    if dil == 1:
        return h, IN_W, base
    return h[:, base:base + 3 * DIL_W].reshape(h.shape[0] // dil, dil * 3 * DIL_W), 3 * DIL_W, 0


def _dil_col(view, j):
    _, width, base = view
    return lambda r: (r * width + base + j * DIL_W) // DIL_W


def _dil_specs(L):
    nq = DIL_TQ // DIL_HALF
    last = L // DIL_HALF - 1

    def cur(colfn):
        return pl.BlockSpec((DIL_TQ, DIL_W), lambda r, n: (n, colfn(r)))

    def prev(colfn):
        return pl.BlockSpec((DIL_HALF, DIL_W), lambda r, n: (jnp.maximum(n * nq - 1, 0), colfn(r)))

    def nxt(colfn):
        return pl.BlockSpec((DIL_HALF, DIL_W), lambda r, n: (jnp.minimum((n + 1) * nq, last), colfn(r)))

    return prev, cur, nxt


def _slot(s):
    return slice(s * DIL_HD, (s + 1) * DIL_HD)


def _cat3(a, b, c, s):
    return jnp.concatenate([a[:, _slot(s)], b[:, _slot(s)], c[:, _slot(s)]], axis=0)


def dil_attn_fwd(view, S, g, dil, bias, qg, kg):
    L = S // dil
    hv = view[0]
    tb = DIL_TQ + 2 * DIL_HALF

    def body(q_ref, kp, kc, kn, vp, vc, vn, bias_ref, qg_ref, kg_ref, o_ref, lse_ref):
        n = pl.program_id(1)
        ii = lax.broadcasted_iota(jnp.int32, (DIL_TQ, tb), 0)
        jj = lax.broadcasted_iota(jnp.int32, (DIL_TQ, tb), 1)
        kabs = n * DIL_TQ - DIL_HALF + jj
        valid = (jnp.abs(jj - DIL_HALF - ii) <= DIL_HALF) & (kabs >= 0) & (kabs < L)
        for s in range(DIL_SLOTS):
            q = _head_rms(q_ref[:, _slot(s)], qg_ref[...]) * (DIL_HD ** -0.5)
            kb = _head_rms(_cat3(kp, kc, kn, s), kg_ref[...])
            sc = jnp.where(valid, _mxu(q, kb, NT) + bias_ref[s], NEG)
            m = jnp.max(sc, axis=-1, keepdims=True)
            p = jnp.exp(sc - m)
            den = jnp.sum(p, axis=-1, keepdims=True)
            o_ref[:, _slot(s)] = _mxu(p, _cat3(vp, vc, vn, s), NN) / den
            lse_ref[:, _slot(s)] = jnp.broadcast_to(m + jnp.log(den), (DIL_TQ, DIL_HD))

    prev, cur, nxt = _dil_specs(L)
    qc, kc_, vc_ = (_dil_col(view, j) for j in range(3))
    oc = lambda r: r
    vec = pl.BlockSpec((1, 128), lambda r, n: (0, 0))
    out = jax.ShapeDtypeStruct((L, dil * DIL_W), F32)
    o, lse = pl.pallas_call(
        body, name=f"dil_attn_fwd{g}", grid=(dil, L // DIL_TQ),
        in_specs=[cur(qc), prev(kc_), cur(kc_), nxt(kc_), prev(vc_), cur(vc_), nxt(vc_),
                  pl.BlockSpec((DIL_SLOTS, DIL_TQ, tb), lambda r, n: (0, 0, 0)), vec, vec],
        out_specs=(cur(oc), cur(oc)), out_shape=(out, out),
        compiler_params=_cp("parallel", "parallel"),
    )(hv, hv, hv, hv, hv, hv, hv, bias, qg, kg)
    return o.reshape(S, DIL_W), lse.reshape(S, DIL_W)


def dil_combine(os_, lses):
    S = os_[0].shape[0]

    def body(o1, o2, o3, l1, l2, l3, y_ref, lt_ref):
        a, b, c = l1[...], l2[...], l3[...]
        m = jnp.maximum(jnp.maximum(a, b), c)
        ea, eb, ec = jnp.exp(a - m), jnp.exp(b - m), jnp.exp(c - m)
        den = ea + eb + ec
        y_ref[...] = (ea * o1[...] + eb * o2[...] + ec * o3[...]) / den
        lt_ref[...] = m + jnp.log(den)

    blk = pl.BlockSpec((GN_T, DIL_W), lambda i: (i, 0))
    out = jax.ShapeDtypeStruct((S, DIL_W), F32)
    return pl.pallas_call(
        body, name="dil_combine", grid=(S // GN_T,), in_specs=[blk] * 6, out_specs=(blk, blk),
        out_shape=(out, out), compiler_params=_cp("parallel"),
    )(*os_, *lses)


def dil_delta(dy, yc):
    S = yc.shape[0]

    def body(dy_ref, y_ref, d_ref):
        d_ref[...] = jnp.broadcast_to(jnp.sum(dy_ref[...] * y_ref[...], axis=-1, keepdims=True), (GN_T, 128))

    return pl.pallas_call(
        body, name="dil_delta", grid=(S // GN_T, DIL_SLOTS),
        in_specs=[pl.BlockSpec((GN_T, 128), lambda i, s: (i, (HG_W + RET_W) // 128 + s)),
                  pl.BlockSpec((GN_T, 128), lambda i, s: (i, s))],
        out_specs=pl.BlockSpec((GN_T, 128), lambda i, s: (i, s)),
        out_shape=jax.ShapeDtypeStruct((S, DIL_W), F32), compiler_params=_cp("parallel", "parallel"),
    )(dy, yc)


def dil_attn_bwd(view, S, g, dil, bias_q, bias_k, qg, kg, dy, lse_t, delta):
    L = S // dil
    hv = view[0]
    if dil == 1:
        dyv, dyc = dy, lambda r: (HG_W + RET_W) // DIL_W
    else:
        dyv, dyc = dy[:, HG_W + RET_W:].reshape(L, dil * DIL_W), lambda r: r
    lv = lse_t.reshape(L, dil * DIL_W)
    dv_ = delta.reshape(L, dil * DIL_W)
    tq, tb = DIL_TQ, DIL_TQ + 2 * DIL_HALF
    scale = DIL_HD ** -0.5

    def body(qp, qc, qn, kp, kc, kn, vp, vc, vn, dp_, dc, dn, lp, lc, ln, ep, ec, en, bq_ref, bk_ref, qg_ref, kg_ref,
             dq_ref, dk_ref, dv_ref, dbias_ref, dqg_ref, dkg_ref):
        r, n = pl.program_id(0), pl.program_id(1)

        @pl.when((r == 0) & (n == 0))
        def _():
            for ref in (dbias_ref, dqg_ref, dkg_ref):
                ref[...] = jnp.zeros_like(ref)

        qgv, kgv = qg_ref[...], kg_ref[...]
        qfn = lambda t, gg: _head_rms(t, gg) * scale
        ii = lax.broadcasted_iota(jnp.int32, (tq, tb), 0)
        jj = lax.broadcasted_iota(jnp.int32, (tq, tb), 1)
        kabs = n * tq - DIL_HALF + jj
        valid = (jnp.abs(jj - DIL_HALF - ii) <= DIL_HALF) & (kabs >= 0) & (kabs < L)
        i2 = lax.broadcasted_iota(jnp.int32, (tb, tq), 0)
        j2 = lax.broadcasted_iota(jnp.int32, (tb, tq), 1)
        qabs = n * tq - DIL_HALF + i2
        valid2 = (jnp.abs(j2 + DIL_HALF - i2) <= DIL_HALF) & (qabs >= 0) & (qabs < L)
        for s in range(DIL_SLOTS):
            sl = _slot(s)
            one = slice(s * DIL_HD, s * DIL_HD + 1)
            qn_c, q_vjp = jax.vjp(qfn, qc[:, sl], qgv)
            k_band = _head_rms(_cat3(kp, kc, kn, s), kgv)
            sc = _mxu(qn_c, k_band, NT) + bq_ref[s]
            p = jnp.where(valid, jnp.exp(jnp.where(valid, sc, NEG) - lc[:, one]), 0.0)
            ds = p * (_mxu(dc[:, sl], _cat3(vp, vc, vn, s), NT) - ec[:, one])
            dbias_ref[s] += ds
            dq, dqg = q_vjp(_mxu(ds, k_band, NN))
            dq_ref[:, sl] = dq
            dqg_ref[s] += dqg
            kn_c, k_vjp = jax.vjp(_head_rms, kc[:, sl], kgv)
            q_band = qfn(_cat3(qp, qc, qn, s), qgv)
            do_band = _cat3(dp_, dc, dn, s)
            s2 = _mxu(q_band, kn_c, NT) + bk_ref[s]
            lse_band = jnp.concatenate([lp[:, one], lc[:, one], ln[:, one]], axis=0)
            delta_band = jnp.concatenate([ep[:, one], ec[:, one], en[:, one]], axis=0)
            p2 = jnp.where(valid2, jnp.exp(jnp.where(valid2, s2, NEG) - lse_band), 0.0)
            dv_ref[:, sl] = _mxu(p2, do_band, TN)
            ds2 = p2 * (_mxu(do_band, vc[:, sl], NT) - delta_band)
            dk, dkg = k_vjp(_mxu(ds2, q_band, TN))
            dk_ref[:, sl] = dk
            dkg_ref[s] += dkg

    prev, cur, nxt = _dil_specs(L)
    three = lambda colfn: [prev(colfn), cur(colfn), nxt(colfn)]
    qc_, kc_, vc_ = (_dil_col(view, j) for j in range(3))
    oc = lambda r: r
    vec = pl.BlockSpec((1, 128), lambda r, n: (0, 0))
    acc_vec = pl.BlockSpec((DIL_SLOTS, 1, 128), lambda r, n: (0, 0, 0))
    out = jax.ShapeDtypeStruct((L, dil * DIL_W), F32)
    dq, dk, dv, dbias, dqg, dkg = pl.pallas_call(
        body, name=f"dil_attn_bwd{g}", grid=(dil, L // tq),
        in_specs=three(qc_) + three(kc_) + three(vc_) + three(dyc) + three(oc) + three(oc)
        + [pl.BlockSpec((DIL_SLOTS, tq, tb), lambda r, n: (0, 0, 0)),
           pl.BlockSpec((DIL_SLOTS, tb, tq), lambda r, n: (0, 0, 0)), vec, vec],
        out_specs=(cur(oc), cur(oc), cur(oc), pl.BlockSpec((DIL_SLOTS, tq, tb), lambda r, n: (0, 0, 0)),
                   acc_vec, acc_vec),
        out_shape=(out, out, out, jax.ShapeDtypeStruct((DIL_SLOTS, tq, tb), F32),
                   jax.ShapeDtypeStruct((DIL_SLOTS, 1, 128), F32), jax.ShapeDtypeStruct((DIL_SLOTS, 1, 128), F32)),
        compiler_params=_cp("arbitrary", "arbitrary"),
    )(hv, hv, hv, hv, hv, hv, hv, hv, hv, dyv, dyv, dyv, lv, lv, lv, dv_, dv_, dv_, bias_q, bias_k, qg, kg)
    return dq.reshape(S, DIL_W), dk.reshape(S, DIL_W), dv.reshape(S, DIL_W), dbias, dqg, dkg


def _lb_eff(p):
    a = jnp.cumsum(jax.nn.softmax(p.astype(F32), axis=0), axis=0)
    return a - a[0:1]


def _dil_bias(rel_bias, g, dil):
    tbl = rel_bias[:, g * DIL_SLOTS:(g + 1) * DIL_SLOTS]
    return tuple(jnp.einsum("ijb,bs->sij", jax.nn.one_hot(b, REL_BUCKETS, dtype=F32), tbl,
                            precision=lax.Precision.HIGHEST) for b in _dil_buckets(dil))


def _big_weights(w):
    return w[0], w[1].reshape(D_MODEL, D_MODEL), w[2], w[3].reshape(D_FF, D_MODEL)


def _layer_fwd(x, l, prm, wts, rc, biases, gatherer=None):
    def carrying(name, call, *args):
        rider = gatherer.rider(l, name) if gatherer else None
        if rider is None:
            return call(*args)
        res, arrays = call(*args, rider)
        gatherer.done(l, name, arrays)
        return res

    weight = lambda t: _big_weights(wts[l])[t]
    row = lambda a: a[l][None]
    xn = rmsnorm_fwd(x, row(prm["norm_mix"]))
    h = carrying("in", proj_in, xn, weight(0))
    hof, hob, hsf, hsb = hg_scan_fwd(h, row(prm["lbf"]), row(prm["lbb"]))
    ya = gated_norm_fwd("hg_out", hof, hob, h, OFF_HG_GATE, row(prm["hg_norm"]), False)
    rof, rob, rsf, rsb = ret_scan_fwd(h, rc)
    yb = gated_norm_fwd("ret_out", rof, rob, h, OFF_RET_GATE, row(prm["ret_norm"]), True)
    os_, lses, views = [], [], []
    for g, (_, dil) in enumerate(DIL_GROUPS):
        views.append(dil_view(h, g, dil))
        o, lse = dil_attn_fwd(views[g], h.shape[0], g, dil, biases[g][0], row(prm["q_norm"]), row(prm["k_norm"]))
        os_.append(o)
        lses.append(lse)
    yc, lse_t = dil_combine(os_, lses)
    y = jnp.concatenate([ya, yb, yc], axis=1).astype(BF16)
    if gatherer:
        gatherer.alone(l, "mid")
    x2 = proj_out(y, weight(1), x)
    hm = rmsnorm_fwd(x2, row(prm["norm_mlp"]))
    u, act = carrying("up", proj_up, hm, weight(2))
    x3 = carrying("down", proj_down, act, weight(3), x2)
    saved = dict(x=x, xn=xn, h=h, hof=hof, hob=hob, hsf=hsf, hsb=hsb, rof=rof, rob=rob, rsf=rsf, rsb=rsb,
                 yc=yc, lse_t=lse_t, y=y, x2=x2, hm=hm, u=u, act=act, views=views)
    return x3, saved


def _layer_bwd(dx3, l, prm, wts, rc, biases, sv, reducer=None):
    def carrying(stage, group, call, *args):
        rider = getattr(reducer, stage + "_rider")(group) if reducer else None
        if rider is None:
            return call(*args)
        res, arrived = call(*args, rider)
        getattr(reducer, stage + "_done")(group, arrived)
        return res

    early, late = GradReducer.EARLY, GradReducer.LATE
    win_g, wout_g, wup_g, wdown_g = _big_weights(wts[l])
    row = lambda a: a[l][None]
    h = sv["h"]
    du = carrying("pair", late, bwd_down_act, dx3, wdown_g, sv["u"])
    g_down = wgrad("wgrad_down", sv["act"], dx3, m=D_FF, n=D_MODEL, n_shard=D_MODEL)
    dhm = bwd_up(du, wup_g)
    g_up = wgrad("wgrad_up", sv["hm"], du, m=D_MODEL, n=D_FF, n_shard=FF_SHARD)
    dx2, dg_mlp = rmsnorm_bwd(sv["x2"], row(prm["norm_mlp"]), dhm, dx3)
    dy = bwd_out(dx2, wout_g)
    g_out = wgrad("wgrad_out", sv["y"], dx2, m=D_MODEL, n=D_MODEL, n_shard=D_MODEL)
    g_out, g_down = g_out.reshape(N_CHIPS, D_MODEL // N_CHIPS, D_MODEL), g_down.reshape(N_CHIPS, D_FF // N_CHIPS, D_MODEL)
    if reducer:
        reducer.push(early, l, (g_out, g_up, g_down))
    hdo, hdgate, dg_hg = gated_norm_bwd("hg_out_bwd", sv["hof"], sv["hob"], h, OFF_HG_GATE, row(prm["hg_norm"]),
                                        dy, 0, False)
    hdqf, hdvf, hdzf, hdqb, hdvb, hdzb, dlbf, dlbb = carrying(
        "chip", late, hg_scan_bwd, h, row(prm["lbf"]), row(prm["lbb"]), hdo, sv["hsf"], sv["hsb"])
    rdo, rdgate, dg_ret = gated_norm_bwd("ret_out_bwd", sv["rof"], sv["rob"], h, OFF_RET_GATE, row(prm["ret_norm"]),
                                         dy, HG_W, True)
    rdqf, rdkf, rdvf, rdqb, rdkb, rdvb = carrying("pair", early, ret_scan_bwd, h, rc, rdo, sv["rsf"], sv["rsb"])
    delta = dil_delta(dy, sv["yc"])
    dil_parts, dbiases = [], []
    dqg = jnp.zeros((1, DIL_HD), F32)
    dkg = jnp.zeros((1, DIL_HD), F32)
    for g, (_, dil) in enumerate(DIL_GROUPS):
        dq, dk, dv, dbias, dqg_g, dkg_g = dil_attn_bwd(sv["views"][g], h.shape[0], g, dil, biases[g][0], biases[g][1],
                                                       row(prm["q_norm"]), row(prm["k_norm"]), dy, sv["lse_t"], delta)
        dil_parts += [dq, dk, dv]
        dbiases.append(dbias)
        dqg = dqg + jnp.sum(dqg_g, axis=0)
        dkg = dkg + jnp.sum(dkg_g, axis=0)
    dh = jnp.concatenate([hdqf + hdqb, hdvf + hdvb, hdzf, hdzb, hdgate,
                          rdqf + rdqb, rdkf + rdkb, rdvf + rdvb, rdgate] + dil_parts, axis=1).astype(BF16)
    dxn = carrying("chip", early, bwd_in, dh, win_g)
    g_in = wgrad("wgrad_in", sv["xn"], dh, m=D_MODEL, n=IN_W, n_shard=IN_SHARD)
    dx, dg_mix = rmsnorm_bwd(sv["x"], row(prm["norm_mix"]), dxn, dx2)
    small = dict(norm_mix=dg_mix, norm_mlp=dg_mlp, lbf=dlbf, lbb=dlbb, hg_norm=dg_hg, ret_norm=dg_ret,
                 q_norm=dqg, k_norm=dkg)
    if reducer:
        reducer.push(late, l, (g_in,))
    return dx, (g_in, g_out, g_up, g_down), small, dbiases


def _rel_bias_grad(dbias_layers):
    cols = []
    for g, (_, dil) in enumerate(DIL_GROUPS):
        bq, _ = _dil_buckets(dil)
        onehot = jax.nn.one_hot(bq, REL_BUCKETS, dtype=F32)
        tot = dbias_layers[0][g]
        for d in dbias_layers[1:]:
            tot = tot + d[g]
        cols.append(jnp.einsum("sij,ijb->bs", tot, onehot, precision=lax.Precision.HIGHEST))
    return jnp.concatenate(cols, axis=1)


def local_step(x, tgt, wts, prm_in, reducer=None, gatherer=None):
    S = x.shape[0]
    prm = dict(prm_in)
    prm["lbf"], lbf_vjp = jax.vjp(_lb_eff, prm_in["hg_lb_fwd"])
    prm["lbb"], lbb_vjp = jax.vjp(_lb_eff, prm_in["hg_lb_bwd"])
    rc = _ret_consts(S)
    biases = [_dil_bias(prm["rel_bias"], g, dil) for g, (_, dil) in enumerate(DIL_GROUPS)]
    saved = []
    for l in range(DEPTH):
        x, sv = _layer_fwd(x, l, prm, wts, rc, biases, gatherer)
        saved.append(sv)
    dx, loss_row = loss_head(x, tgt)
    big, small, dbias_layers = [None] * DEPTH, [None] * DEPTH, [None] * DEPTH
    for l in range(DEPTH - 1, -1, -1):
        dx, big[l], small[l], dbias_layers[l] = _layer_bwd(dx, l, prm, wts, rc, biases, saved[l], reducer)
    sg = {k: jnp.concatenate([small[l][k] for l in range(DEPTH)], axis=0) for k in small[0]}
    sg["rel_bias"] = _rel_bias_grad(dbias_layers)
    return loss_row[0, 0], dx, (reducer.finish() if reducer else big), sg, (lbf_vjp, lbb_vjp)


def _place():
    x, y, c = lax.axis_index("x"), lax.axis_index("y"), lax.axis_index("c")
    rels = [(1 - x, y), (x, 1 - y), (1 - x, 1 - y)]
    return x, y, c, 2 * x + y, rels


def _half(c, rows):
    return pl.ds(pl.multiple_of(c * (rows // 2), 16), rows // 2)


def place_own(name, p_arr, w, l):
    _, rows, cols = w.shape
    tr = 512

    def body(p_ref, w_ref, o_ref):
        o_ref[...] = w_ref[...].astype(BF16)

    return pl.pallas_call(
        body, name=name,
        grid_spec=pltpu.PrefetchScalarGridSpec(
            num_scalar_prefetch=1, grid=(rows // tr,),
            in_specs=[pl.BlockSpec((1, tr, cols), lambda i, p: (l, i, 0))],
            out_specs=pl.BlockSpec((1, tr, cols), lambda i, p: (p[0], i, 0))),
        out_shape=jax.ShapeDtypeStruct((N_CHIPS, rows, cols), BF16),
        compiler_params=_cp("parallel"),
    )(p_arr, w)


class WeightGatherer:
    PLAN = {
        (0, "in"): [("ici", 0, (1, 2, 3))],
        (0, "mid"): [("pass", 0, (1, 2, 3))],
        (0, "up"): [("ici", 1, (0, 1))],
        (0, "down"): [("pass", 1, (0, 1)), ("ici", 1, (2, 3))],
        (1, "in"): [("pass", 1, (2, 3)), ("ici", 2, (0, 1))],
        (1, "up"): [("pass", 2, (0, 1)), ("ici", 2, (2, 3))],
        (1, "down"): [("pass", 2, (2, 3))],
        (2, "in"): [("ici", 3, (0, 1))],
        (2, "up"): [("pass", 3, (0, 1)), ("ici", 3, (2, 3))],
        (2, "down"): [("pass", 3, (2, 3))],
    }

    def __init__(self, big_w):
        assert DEPTH == 4
        p_arr = (2 * lax.axis_index("x") + lax.axis_index("y")).astype(jnp.int32).reshape(1)
        self.w = [[place_own(f"place_own{t}", p_arr, w, l) for t, w in enumerate(big_w)] for l in range(DEPTH)]
        self.w[0][0:1] = run_alone("gather_first_chips", gather_ici_rider(self.w[0][0:1]))
        self.w[0][0:1] = run_alone("gather_first_cores", gather_pass_rider(self.w[0][0:1]))

    def rider(self, l, call):
        parts = [(gather_ici_rider if kind == "ici" else gather_pass_rider)([self.w[wl][t] for t in ts])
                 for kind, wl, ts in self.PLAN.get((l, call), ())]
        return functools.reduce(lambda a, b: a + b, parts) if parts else None

    def done(self, l, call, arrays):
        arrays = list(arrays)
        for _, wl, ts in self.PLAN[(l, call)]:
            for t in ts:
                self.w[wl][t] = arrays.pop(0)

    def alone(self, l, call):
        rider = self.rider(l, call)
        if rider:
            self.done(l, call, run_alone(f"gather_{call}", rider))


def run_alone(name, rider):
    n_in, n_out = len(rider.arrays), len(rider.out_shapes)

    def body(*refs):
        ins, outs, sems = refs[:n_in], refs[n_in:n_in + n_out], refs[n_in + n_out:]
        rider.start(ins, outs, sems)
        rider.finish(ins, outs, sems)

    return pl.pallas_call(
        body, name=name, in_specs=[ANY] * n_in, out_specs=[ANY] * n_out, out_shape=rider.out_shapes,
        scratch_shapes=[pltpu.SemaphoreType.DMA((n,)) for n in rider.sems],
        input_output_aliases=rider.aliases(0, 0),
    )(*rider.arrays)


def _both(cp):
    return (cp, cp)


def pair_exchange_rider(gs):
    n = len(gs)

    def ops(ins, outs, ssem, rsem):
        x, y, c, _, _ = _place()
        return [_both(pltpu.make_async_remote_copy(
            src_ref=ins[i].at[:, _half(1 - c, ins[i].shape[1]), :], dst_ref=outs[i],
            send_sem=ssem.at[i], recv_sem=rsem.at[i], device_id=(x, y, 1 - c), device_id_type=MESH)) for i in range(n)]

    return Rider(gs, [jax.ShapeDtypeStruct((N_CHIPS, g.shape[1] // 2, g.shape[2]), F32) for g in gs], [n, n], ops)


def gather_ici_rider(bufs):
    n = len(bufs)

    def ops(ins, outs, ssem, rsem):
        x, y, c, p, rels = _place()
        cps = []
        for i in range(n):
            mine = _half(c, outs[i].shape[1])
            for r, (rx, ry) in enumerate(rels):
                k = i * 3 + r
                peer = dict(device_id=(rx, ry, c), device_id_type=MESH, send_sem=ssem.at[k], recv_sem=rsem.at[k])
                own, landing = outs[i].at[p, mine], outs[i].at[2 * rx + ry, mine]
                cps.append((pltpu.make_async_remote_copy(src_ref=own, dst_ref=own, **peer),
                            pltpu.make_async_remote_copy(src_ref=landing, dst_ref=landing, **peer)))
        return cps

    return Rider(bufs, None, [3 * n, 3 * n], ops)


def gather_pass_rider(bufs):
    n = len(bufs)

    def ops(ins, outs, ssem, rsem):
        x, y, c, p, rels = _place()
        cps = []
        for i in range(n):
            rows = outs[i].shape[1]
            for r, (rx, ry) in enumerate(rels):
                k = i * 3 + r
                peer = dict(device_id=(x, y, 1 - c), device_id_type=MESH, send_sem=ssem.at[k], recv_sem=rsem.at[k])
                landed, theirs = outs[i].at[2 * rx + ry, _half(c, rows)], outs[i].at[2 * rx + ry, _half(1 - c, rows)]
                cps.append((pltpu.make_async_remote_copy(src_ref=landed, dst_ref=landed, **peer),
                            pltpu.make_async_remote_copy(src_ref=theirs, dst_ref=theirs, **peer)))
        return cps

    return Rider(bufs, None, [3 * n, 3 * n], ops)


def pair_add(name, c_arr, g, got):
    _, rows, cols = g.shape
    hr = rows // 2
    tr = 256
    nblk = hr // tr

    def body(c_ref, g_ref, r_ref, o32, o16):
        s = g_ref[...] + r_ref[...]
        o32[...] = s
        o16[...] = s.astype(BF16)

    blk = pl.BlockSpec((1, tr, cols), lambda pp, i, c_ref: (pp, i, 0))
    return pl.pallas_call(
        body, name=name,
        grid_spec=pltpu.PrefetchScalarGridSpec(
            num_scalar_prefetch=1, grid=(N_CHIPS, nblk),
            in_specs=[pl.BlockSpec((1, tr, cols), lambda pp, i, c_ref: (pp, c_ref[0] * nblk + i, 0)), blk],
            out_specs=(blk, blk)),
        out_shape=(jax.ShapeDtypeStruct((N_CHIPS, hr, cols), F32), jax.ShapeDtypeStruct((N_CHIPS, hr, cols), BF16)),
        compiler_params=_cp("parallel", "parallel"),
    )(c_arr, g, got)


def chip_exchange_rider(cs16):
    n = len(cs16)

    def ops(ins, outs, ssem, rsem):
        x, y, c, p, rels = _place()
        return [_both(pltpu.make_async_remote_copy(
            src_ref=ins[i].at[2 * rx + ry], dst_ref=outs[i].at[r], send_sem=ssem.at[i * 3 + r],
            recv_sem=rsem.at[i * 3 + r], device_id=(rx, ry, c), device_id_type=MESH))
            for i in range(n) for r, (rx, ry) in enumerate(rels)]

    return Rider(cs16, [jax.ShapeDtypeStruct((3,) + a.shape[1:], BF16) for a in cs16], [3 * n, 3 * n], ops)


def chip_sum(name, pc_arr, l, cs32, got, prev):
    _, hr, cols = cs32.shape
    tr = 256
    nblk = hr // tr

    def body(pc_ref, o_ref, g_ref, *rest):
        rest[-1][0] = ((o_ref[0] + g_ref[0].astype(F32)) + g_ref[1].astype(F32)) + g_ref[2].astype(F32)

    return pl.pallas_call(
        body, name=name,
        grid_spec=pltpu.PrefetchScalarGridSpec(
            num_scalar_prefetch=1, grid=(nblk,),
            in_specs=[pl.BlockSpec((1, tr, cols), lambda i, pc: (pc[0], i, 0)),
                      pl.BlockSpec((3, tr, cols), lambda i, pc: (0, i, 0))] + ([] if prev is None else [ANY]),
            out_specs=pl.BlockSpec((1, tr, cols), lambda i, pc: (l, pc[1] * nblk + i, 0))),
        out_shape=jax.ShapeDtypeStruct((DEPTH, 2 * hr, cols), F32),
        input_output_aliases={} if prev is None else {3: 0},
        compiler_params=_cp("arbitrary"),
    )(*((pc_arr, cs32, got) + (() if prev is None else (prev,))))


def grad_pair_share(halves):
    n_w = len(halves)
    n = n_w * DEPTH

    def body(*refs):
        bufs = refs[n_w:2 * n_w]
        ssem, rsem = refs[2 * n_w:]
        x, y, c, _, _ = _place()
        cps = []
        for t in range(n_w):
            for l in range(DEPTH):
                mine = bufs[t].at[l, _half(c, bufs[t].shape[1])]
                cp = pltpu.make_async_remote_copy(src_ref=mine, dst_ref=mine, send_sem=ssem.at[t * DEPTH + l],
                                                  recv_sem=rsem.at[t * DEPTH + l], device_id=(x, y, 1 - c),
                                                  device_id_type=MESH)
                cp.start()
                cps.append(cp)
        for t in range(n_w):
            for l in range(DEPTH):
                theirs = bufs[t].at[l, _half(1 - c, bufs[t].shape[1])]
                pltpu.make_async_remote_copy(src_ref=theirs, dst_ref=theirs, send_sem=ssem.at[t * DEPTH + l],
                                             recv_sem=rsem.at[t * DEPTH + l], device_id=(x, y, 1 - c),
                                             device_id_type=MESH).wait_recv()
        for cp in cps:
            cp.wait_send()

    return pl.pallas_call(
        body, name="grad_pair_share", in_specs=[ANY] * n_w, out_specs=[ANY] * n_w,
        out_shape=[jax.ShapeDtypeStruct(a.shape, F32) for a in halves],
        input_output_aliases={t: t for t in range(n_w)},
        scratch_shapes=[pltpu.SemaphoreType.DMA((n,)), pltpu.SemaphoreType.DMA((n,))],
    )(*halves)


SMALL_ROWS = 240


def small_allreduce(v):
    def body(v_ref, o_ref, buf, ssem, rsem):
        x, y, c, _, _ = _place()
        me = 4 * x + 2 * y + c
        buf[me] = v_ref[...]
        for d in range(N_DEV):
            @pl.when(me != d)
            def _():
                pltpu.make_async_remote_copy(
                    src_ref=v_ref, dst_ref=buf.at[me], send_sem=ssem.at[d], recv_sem=rsem.at[me],
                    device_id=(d // 4, (d // 2) % 2, d % 2), device_id_type=MESH).start()
        for d in range(N_DEV):
            @pl.when(me != d)
            def _():
                cp = pltpu.make_async_remote_copy(
                    src_ref=v_ref, dst_ref=buf.at[d], send_sem=ssem.at[d], recv_sem=rsem.at[d],
                    device_id=(d // 4, (d // 2) % 2, d % 2), device_id_type=MESH)
                cp.wait_recv()
                cp.wait_send()
        acc = buf[0]
        for d in range(1, N_DEV):
            acc = acc + buf[d]
        o_ref[...] = acc

    vm = pl.BlockSpec(memory_space=pltpu.VMEM)
    return pl.pallas_call(
        body, name="small_allreduce", in_specs=[vm], out_specs=vm,
        out_shape=jax.ShapeDtypeStruct(v.shape, F32),
        scratch_shapes=[pltpu.VMEM((N_DEV,) + v.shape, F32), pltpu.SemaphoreType.DMA((N_DEV,)),
                        pltpu.SemaphoreType.DMA((N_DEV,))],
    )(v)


class GradReducer:
    EARLY, LATE = (1, 2, 3), (0,)

    def __init__(self):
        self.c_arr = lax.axis_index("c").astype(jnp.int32).reshape(1)
        self.pc_arr = jnp.stack([2 * lax.axis_index("x") + lax.axis_index("y"), lax.axis_index("c")]).astype(jnp.int32)
        self.fresh = {}
        self.paired = {}
        self.acc = [None] * 4

    def push(self, group, l, gs):
        self.fresh[group] = (l, list(gs))

    def pair_rider(self, group):
        return pair_exchange_rider(self.fresh[group][1]) if group in self.fresh else None

    def pair_done(self, group, got):
        l, gs = self.fresh.pop(group)
        self.paired[group] = (l, [pair_add(f"pair_add{t}", self.c_arr, g, r) for t, g, r in zip(group, gs, got)])

    def chip_rider(self, group):
        return chip_exchange_rider([s16 for _, s16 in self.paired[group][1]]) if group in self.paired else None

    def chip_done(self, group, arrived):
        l, cs = self.paired.pop(group)
        for t, (s32, _), got in zip(group, cs, arrived):
            self.acc[t] = chip_sum(f"chip_sum{t}", self.pc_arr, l, s32, got, self.acc[t])

    def finish(self):
        self.pair_done(self.LATE, run_alone("grad_pair_exchange", self.pair_rider(self.LATE)))
        self.chip_done(self.LATE, run_alone("grad_chip_exchange", self.chip_rider(self.LATE)))
        return grad_pair_share(self.acc)


def adamw(name, w, g, m, v):
    shape = w.shape
    cols = shape[-1]
    flat = [t.reshape(-1, cols) for t in (w, g, m, v)]
    rows = flat[0].shape[0]
    tr = 128 if rows % 128 == 0 else rows

    def body(w_ref, g_ref, m_ref, v_ref, d_ref, mo_ref, vo_ref):
        gv = g_ref[...]
        mn = ADAM_B1 * m_ref[...] + (1.0 - ADAM_B1) * gv
        vn = ADAM_B2 * v_ref[...] + (1.0 - ADAM_B2) * jnp.square(gv)
        m_hat = mn / (1.0 - ADAM_B1 ** ADAM_STEP)
        v_hat = vn / (1.0 - ADAM_B2 ** ADAM_STEP)
        d_ref[...] = -ADAM_LR * (m_hat / (jnp.sqrt(v_hat) + ADAM_EPS) + ADAM_WD * w_ref[...])
        mo_ref[...] = mn
        vo_ref[...] = vn

    blk = pl.BlockSpec((tr, cols), lambda i: (i, 0))
    out = jax.ShapeDtypeStruct((rows, cols), F32)
    d, mo, vo = pl.pallas_call(
        body, name=name, grid=(rows // tr,), in_specs=[blk] * 4, out_specs=(blk, blk, blk),
        out_shape=(out, out, out), compiler_params=_cp("parallel"),
    )(*flat)
    return d.reshape(shape), mo.reshape(shape), vo.reshape(shape)


SMALL_NAMES = ("norm_mix", "norm_mlp", "hg_lb_fwd", "hg_lb_bwd", "hg_norm", "ret_norm", "q_norm", "k_norm", "rel_bias")


def _pack_small(d):
    flat = jnp.concatenate([d[k].reshape(-1) for k in SMALL_NAMES])
    return jnp.pad(flat, (0, SMALL_ROWS * 128 - flat.shape[0])).reshape(SMALL_ROWS, 128)


def _unpack_small(v, like):
    flat = v.reshape(-1)
    out, off = {}, 0
    for k in SMALL_NAMES:
        n = like[k].size
        out[k] = flat[off:off + n].reshape(like[k].shape)
        off += n
    return out


def kernel(x, w_in, w_out, w_up, w_down, norm_mix, norm_mlp, hg_lb_fwd, hg_lb_bwd, hg_norm, ret_norm, q_norm, k_norm, rel_bias, loss_target, m_w_in, m_w_out, m_w_up, m_w_down, m_norm_mix, m_norm_mlp, m_hg_lb_fwd, m_hg_lb_bwd, m_hg_norm, m_ret_norm, m_q_norm, m_k_norm, m_rel_bias, v_w_in, v_w_out, v_w_up, v_w_down, v_norm_mix, v_norm_mlp, v_hg_lb_fwd, v_hg_lb_bwd, v_hg_norm, v_ret_norm, v_q_norm, v_k_norm, v_rel_bias):
    big_w = (w_in, w_out, w_up, w_down)
    big_m = (m_w_in, m_w_out, m_w_up, m_w_down)
    big_v = (v_w_in, v_w_out, v_w_up, v_w_down)
    small_w = dict(zip(SMALL_NAMES, (norm_mix, norm_mlp, hg_lb_fwd, hg_lb_bwd, hg_norm, ret_norm, q_norm, k_norm, rel_bias)))
    small_m = dict(zip(SMALL_NAMES, (m_norm_mix, m_norm_mlp, m_hg_lb_fwd, m_hg_lb_bwd, m_hg_norm, m_ret_norm, m_q_norm,
                                     m_k_norm, m_rel_bias)))
    small_v = dict(zip(SMALL_NAMES, (v_norm_mix, v_norm_mlp, v_hg_lb_fwd, v_hg_lb_bwd, v_hg_norm, v_ret_norm, v_q_norm,
                                     v_k_norm, v_rel_bias)))

    gatherer = WeightGatherer(big_w)
    loss_part, dx, grads_big, sg, (lbf_vjp, lbb_vjp) = local_step(x[0], loss_target[0], gatherer.w, small_w,
                                                                  GradReducer(), gatherer)
    loss = lax.psum(loss_part, ("x", "y", "c"))

    sg = dict(sg)
    sg["hg_lb_fwd"], sg["hg_lb_bwd"] = sg.pop("lbf"), sg.pop("lbb")
    tot = _unpack_small(small_allreduce(_pack_small(sg)), small_w)
    tot["hg_lb_fwd"] = lbf_vjp(tot["hg_lb_fwd"])[0]
    tot["hg_lb_bwd"] = lbb_vjp(tot["hg_lb_bwd"])[0]
    grads_small = [tot[k] for k in SMALL_NAMES]

    upd_big = [adamw(f"adamw_big{t}", big_w[t], grads_big[t], big_m[t], big_v[t]) for t in range(4)]
    d_s, m_s, v_s = adamw("adamw_small", _pack_small(small_w), _pack_small(tot), _pack_small(small_m), _pack_small(small_v))
    upd_small = [_unpack_small(t, small_w) for t in (d_s, m_s, v_s)]

    outs = [loss, dx[None]] + list(grads_big) + grads_small
    for j in range(3):
        outs += [u[j] for u in upd_big] + [upd_small[j][k] for k in SMALL_NAMES]
    return tuple(outs)
```

```python
import functools
import math

import jax
import jax.numpy as jnp
from jax import lax
from jax.experimental import pallas as pl
from jax.experimental.pallas import tpu as pltpu

F32 = jnp.float32
BF16 = jnp.bfloat16
EPS = 1e-6

D_MODEL = 2048
DEPTH = 4
HG_HEADS = 6
HG_W = 768
RET_HEADS = 6
RET_DK = 64
RET_W = 768
RET_QK_W = RET_HEADS * RET_DK
RET_CHUNK = 128
ROPE_BASE = 10000.0
DIL_SLOTS = 4
DIL_HD = 128
DIL_GROUPS = ((128, 1), (512, 4), (2048, 16))
DIL_HALF = 64
DIL_W = 512
D_FF = 4 * D_MODEL
IN_W = 10752
REL_BUCKETS = 32
REL_MAX_DIST = 1024

OFF_HG_Q, OFF_HG_V, OFF_HG_ZF, OFF_HG_ZB, OFF_HG_GATE = 0, 768, 1536, 2304, 3072
OFF_RET_Q, OFF_RET_K, OFF_RET_V, OFF_RET_GATE = 3840, 4224, 4608, 5376
OFF_DIL = 6144

N_CHIPS = 4
N_DEV = 8
IN_SHARD = IN_W // N_CHIPS
FF_SHARD = D_FF // N_CHIPS

ADAM_LR, ADAM_B1, ADAM_B2, ADAM_EPS, ADAM_WD, ADAM_STEP = 0.001, 0.9, 0.999, 1e-08, 0.01, 10

VMEM_LIMIT = 56 * 1024 * 1024
HG_T = 512
HG_C = 64
RET_T = 256
DIL_TQ = 256
NEG = -1e30

NN = (((1,), (0,)), ((), ()))
NT = (((1,), (1,)), ((), ()))
TN = (((0,), (0,)), ((), ()))
MESH = pl.DeviceIdType.MESH
ANY = pl.BlockSpec(memory_space=pl.ANY)


def _cp(*sem):
    return pltpu.CompilerParams(dimension_semantics=sem, vmem_limit_bytes=VMEM_LIMIT)


def _mxu(a, b, dn):
    return lax.dot_general(a.astype(BF16), b.astype(BF16), dn, preferred_element_type=F32)


@jax.custom_vjp
def dot_nn(a, b):
    return _mxu(a, b, NN)


dot_nn.defvjp(lambda a, b: (_mxu(a, b, NN), (a, b)),
              lambda r, g: (_mxu(g, r[1], NT), _mxu(r[0], g, TN)))


@jax.custom_vjp
def dot_nt(a, b):
    return _mxu(a, b, NT)


dot_nt.defvjp(lambda a, b: (_mxu(a, b, NT), (a, b)),
              lambda r, g: (_mxu(g, r[1], NN), _mxu(g, r[0], TN)))


@jax.custom_vjp
def dot_tn(a, b):
    return _mxu(a, b, TN)


dot_tn.defvjp(lambda a, b: (_mxu(a, b, TN), (a, b)),
              lambda r, g: (_mxu(r[1], g, NT), _mxu(r[0], g, NN)))


def _split3(v):
    hi = v.astype(BF16)
    r1 = v - hi.astype(F32)
    mid = r1.astype(BF16)
    lo = (r1 - mid.astype(F32)).astype(BF16)
    return hi, mid, lo


def _exact_mask_dot(m, v, dn):
    mb = m.astype(BF16)
    hi, mid, lo = _split3(v)
    f = lambda p: lax.dot_general(mb, p, dn, preferred_element_type=F32)
    return (f(lo) + f(mid)) + f(hi)


@jax.custom_vjp
def cumdot(m, v):
    return _exact_mask_dot(m, v, NN)


cumdot.defvjp(lambda m, v: (_exact_mask_dot(m, v, NN), m),
              lambda m, g: (jnp.zeros_like(m), _exact_mask_dot(m, g, TN)))


def _sigmoid(z):
    return 1.0 / (1.0 + jnp.exp(-z))


def _head_rms(t, g):
    return t * lax.rsqrt(jnp.mean(t * t, axis=-1, keepdims=True) + EPS) * g


class Rider:
    def __init__(self, arrays, out_shapes, sems, ops):
        self.arrays, self.sems, self.ops = list(arrays), list(sems), ops
        self.in_place = out_shapes is None
        self.out_shapes = [jax.ShapeDtypeStruct(a.shape, a.dtype) for a in arrays] if self.in_place else list(out_shapes)

    def aliases(self, n_in, n_out):
        return {n_in + i: n_out + i for i in range(len(self.arrays))} if self.in_place else {}

    def start(self, ins, outs, sems):
        for send, _ in self.ops(ins, outs, *sems):
            send.start()

    def finish(self, ins, outs, sems):
        cps = self.ops(ins, outs, *sems)
        for _, arrive in cps:
            arrive.wait_recv()
        for send, _ in cps:
            send.wait_send()

    def __add__(self, other):
        assert self.in_place and other.in_place
        na, sa = len(self.arrays), len(self.sems)
        ops = lambda ins, outs, *sems: (self.ops(ins[:na], outs[:na], *sems[:sa])
                                        + other.ops(ins[na:], outs[na:], *sems[sa:]))
        return Rider(self.arrays + other.arrays, None, self.sems + other.sems, ops)


def _hosted(name, body, rider, *, grid, in_specs, out_specs, out_shape, scratch_shapes, sem, operands):
    n_in, n_out, n_scr = len(in_specs), len(out_specs), len(scratch_shapes)
    r_in = len(rider.arrays) if rider else 0
    r_out = len(rider.out_shapes) if rider else 0
    last = tuple(g - 1 for g in grid)

    def kernel_body(*refs):
        ins, refs = refs[:n_in], refs[n_in:]
        rins, refs = refs[:r_in], refs[r_in:]
        outs, refs = refs[:n_out], refs[n_out:]
        routs, refs = refs[:r_out], refs[r_out:]
        scr, rsems = refs[:n_scr], refs[n_scr:]
        if rider:
            ids = [pl.program_id(d) for d in range(len(grid))]
            first = functools.reduce(lambda p, q: p & q, [i == 0 for i in ids])
            done = functools.reduce(lambda p, q: p & q, [i == e for i, e in zip(ids, last)])
            pl.when(first)(lambda: rider.start(rins, routs, rsems))
        body(ins, outs, scr)
        if rider:
            pl.when(done)(lambda: rider.finish(rins, routs, rsems))

    res = pl.pallas_call(
        kernel_body, name=name, grid=grid,
        in_specs=list(in_specs) + [ANY] * r_in,
        out_specs=list(out_specs) + [ANY] * r_out,
        out_shape=list(out_shape) + (rider.out_shapes if rider else []),
        scratch_shapes=list(scratch_shapes) + ([pltpu.SemaphoreType.DMA((n,)) for n in rider.sems] if rider else []),
        input_output_aliases=rider.aliases(n_in, n_out) if rider else {},
        compiler_params=_cp(*(("arbitrary",) * len(grid) if rider else sem)),
    )(*operands, *(rider.arrays if rider else []))
    return res[:n_out], res[n_out:]


def _mm(name, a, b, *, mode, grid, a_spec, b_spec, tm, tn, extras=(), extra_specs=(), epi, out_shape, out_specs,
        rider=None):
    nk = grid[2]
    single = not isinstance(out_shape, (tuple, list))
    if single:
        out_shape, out_specs = [out_shape], [out_specs]

    def body(ins, outs, scr):
        a_ref, b_ref, ex = ins[0], ins[1], ins[2:]
        product = lambda: _mxu(a_ref[...], b_ref[...], {"nn": NN, "nt": NT, "tn": TN}[mode])
        if nk == 1:
            epi(product(), ex, outs)
            return
        acc = scr[0]
        k = pl.program_id(2)

        @pl.when(k == 0)
        def _():
            acc[...] = jnp.zeros_like(acc)

        acc[...] += product()

        @pl.when(k == nk - 1)
        def _():
            epi(acc[...], ex, outs)

    outs, carried = _hosted(name, body, rider, grid=grid, in_specs=[a_spec, b_spec, *extra_specs],
                            out_specs=out_specs, out_shape=out_shape,
                            scratch_shapes=[] if nk == 1 else [pltpu.VMEM((tm, tn), F32)],
                            sem=("parallel", "parallel", "arbitrary"), operands=(a, b, *extras))
    res = outs[0] if single else tuple(outs)
    return (res, carried) if rider else res


def _epi_store(acc, ex, outs):
    outs[0][...] = acc.astype(outs[0].dtype)


def _epi_residual(acc, ex, outs):
    outs[0][...] = ex[0][...] + acc


def _epi_up(acc, ex, outs):
    outs[0][...] = acc
    outs[1][...] = jnp.square(jnp.maximum(acc, 0.0)).astype(BF16)


def _epi_dact(acc, ex, outs):
    outs[0][...] = (acc * (2.0 * jnp.maximum(ex[0][...], 0.0))).astype(BF16)


def _ij(i, j, k):
    return (i, j)


def proj_in(xn, win_g, rider=None):
    S = xn.shape[0]
    tm, tn = 512, IN_SHARD
    return _mm("proj_in", xn, win_g, mode="nn", grid=(IN_W // tn, S // tm, 1), tm=tm, tn=tn,
               a_spec=pl.BlockSpec((tm, D_MODEL), lambda j, i, k: (i, 0)),
               b_spec=pl.BlockSpec((None, D_MODEL, tn), lambda j, i, k: (j, 0, 0)),
               epi=_epi_store, out_shape=jax.ShapeDtypeStruct((S, IN_W), F32),
               out_specs=pl.BlockSpec((tm, tn), lambda j, i, k: (i, j)), rider=rider)


def proj_out(y, wout_g, x):
    S = y.shape[0]
    tm, tn = 1024, 1024
    return _mm("proj_out", y, wout_g, mode="nn", grid=(S // tm, D_MODEL // tn, 1), tm=tm, tn=tn,
               a_spec=pl.BlockSpec((tm, D_MODEL), lambda i, j, k: (i, 0)),
               b_spec=pl.BlockSpec((D_MODEL, tn), lambda i, j, k: (0, j)),
               extras=(x,), extra_specs=(pl.BlockSpec((tm, tn), _ij),),
               epi=_epi_residual, out_shape=jax.ShapeDtypeStruct((S, D_MODEL), F32),
               out_specs=pl.BlockSpec((tm, tn), _ij))


def proj_up(hm, wup_g, rider=None):
    S = hm.shape[0]
    tm, tn = 1024, 1024
    return _mm("proj_up", hm, wup_g, mode="nn", grid=(S // tm, D_FF // tn, 1), tm=tm, tn=tn,
               a_spec=pl.BlockSpec((tm, D_MODEL), lambda i, j, k: (i, 0)),
               b_spec=pl.BlockSpec((None, D_MODEL, tn), lambda i, j, k: (j // 2, 0, j % 2)),
               epi=_epi_up,
               out_shape=(jax.ShapeDtypeStruct((S, D_FF), F32), jax.ShapeDtypeStruct((S, D_FF), BF16)),
               out_specs=(pl.BlockSpec((tm, tn), _ij), pl.BlockSpec((tm, tn), _ij)), rider=rider)


def proj_down(a, wdown_g, x, rider=None):
    S = a.shape[0]
    tm, tn, tk = 1024, 1024, 2048
    return _mm("proj_down", a, wdown_g, mode="nn", grid=(S // tm, D_MODEL // tn, D_FF // tk), tm=tm, tn=tn,
               a_spec=pl.BlockSpec((tm, tk), lambda i, j, k: (i, k)),
               b_spec=pl.BlockSpec((tk, tn), lambda i, j, k: (k, j)),
               extras=(x,), extra_specs=(pl.BlockSpec((tm, tn), _ij),),
               epi=_epi_residual, out_shape=jax.ShapeDtypeStruct((S, D_MODEL), F32),
               out_specs=pl.BlockSpec((tm, tn), _ij), rider=rider)


def bwd_down_act(dx, wdown_g, u, rider=None):
    S = dx.shape[0]
    tm, tn = 1024, 1024
    return _mm("bwd_down_act", dx, wdown_g, mode="nt", grid=(S // tm, D_FF // tn, 1), tm=tm, tn=tn,
               a_spec=pl.BlockSpec((tm, D_MODEL), lambda i, j, k: (i, 0)),
               b_spec=pl.BlockSpec((tn, D_MODEL), lambda i, j, k: (j, 0)),
               extras=(u,), extra_specs=(pl.BlockSpec((tm, tn), _ij),),
               epi=_epi_dact, out_shape=jax.ShapeDtypeStruct((S, D_FF), BF16),
               out_specs=pl.BlockSpec((tm, tn), _ij), rider=rider)


def bwd_up(du, wup_g):
    S = du.shape[0]
    tm, tn, tk = 1024, 1024, FF_SHARD
    return _mm("bwd_up", du, wup_g, mode="nt", grid=(S // tm, D_MODEL // tn, D_FF // tk), tm=tm, tn=tn,
               a_spec=pl.BlockSpec((tm, tk), lambda i, j, k: (i, k)),
               b_spec=pl.BlockSpec((None, tn, tk), lambda i, j, k: (k, j, 0)),
               epi=_epi_store, out_shape=jax.ShapeDtypeStruct((S, D_MODEL), F32),
               out_specs=pl.BlockSpec((tm, tn), _ij))


def bwd_out(dx, wout_g):
    S = dx.shape[0]
    tm, tn = 1024, 1024
    return _mm("bwd_out", dx, wout_g, mode="nt", grid=(S // tm, D_MODEL // tn, 1), tm=tm, tn=tn,
               a_spec=pl.BlockSpec((tm, D_MODEL), lambda i, j, k: (i, 0)),
               b_spec=pl.BlockSpec((tn, D_MODEL), lambda i, j, k: (j, 0)),
               epi=_epi_store, out_shape=jax.ShapeDtypeStruct((S, D_MODEL), F32),
               out_specs=pl.BlockSpec((tm, tn), _ij))


def bwd_in(dh, win_g, rider=None):
    S = dh.shape[0]
    tm, tn, tk = 1024, 1024, IN_SHARD
    return _mm("bwd_in", dh, win_g, mode="nt", grid=(S // tm, D_MODEL // tn, IN_W // tk), tm=tm, tn=tn,
               a_spec=pl.BlockSpec((tm, tk), lambda i, j, k: (i, k)),
               b_spec=pl.BlockSpec((None, tn, tk), lambda i, j, k: (k, j, 0)),
               epi=_epi_store, out_shape=jax.ShapeDtypeStruct((S, D_MODEL), F32),
               out_specs=pl.BlockSpec((tm, tn), _ij), rider=rider)


def wgrad(name, a, g, *, m, n, n_shard):
    S = a.shape[0]
    tm, tn, tk = (512, IN_SHARD, 1024) if n_shard == IN_SHARD else (1024, 1024, 1024)
    per = n_shard // tn
    if n_shard == n:
        out_shape = jax.ShapeDtypeStruct((m, n), F32)
        out_spec = pl.BlockSpec((tm, tn), _ij)
    else:
        out_shape = jax.ShapeDtypeStruct((N_CHIPS, m, n_shard), F32)
        out_spec = pl.BlockSpec((None, tm, tn), lambda i, j, k: (j // per, i, j % per))
    return _mm(name, a, g, mode="tn", grid=(m // tm, n // tn, S // tk), tm=tm, tn=tn,
               a_spec=pl.BlockSpec((tk, tm), lambda i, j, k: (k, i)),
               b_spec=pl.BlockSpec((tk, tn), lambda i, j, k: (k, j)),
               epi=_epi_store, out_shape=out_shape, out_specs=out_spec)


NORM_T = 256


def rmsnorm_fwd(x, g):
    S = x.shape[0]

    def body(x_ref, g_ref, o_ref):
        xv = x_ref[...]
        r = lax.rsqrt(jnp.mean(xv * xv, axis=-1, keepdims=True) + EPS)
        o_ref[...] = ((xv * r) * g_ref[...]).astype(BF16)

    return pl.pallas_call(
        body, name="rmsnorm_fwd", grid=(S // NORM_T,),
        in_specs=[pl.BlockSpec((NORM_T, D_MODEL), lambda i: (i, 0)), pl.BlockSpec((1, D_MODEL), lambda i: (0, 0))],
        out_specs=pl.BlockSpec((NORM_T, D_MODEL), lambda i: (i, 0)),
        out_shape=jax.ShapeDtypeStruct((S, D_MODEL), BF16), compiler_params=_cp("parallel"),
    )(x, g)


def rmsnorm_bwd(x, g, dxn, dres):
    S = x.shape[0]

    def body(x_ref, g_ref, dxn_ref, dres_ref, dx_ref, dg_ref):
        @pl.when(pl.program_id(0) == 0)
        def _():
            dg_ref[...] = jnp.zeros_like(dg_ref)

        xv, gv, d = x_ref[...], g_ref[...], dxn_ref[...]
        r = lax.rsqrt(jnp.mean(xv * xv, axis=-1, keepdims=True) + EPS)
        gd = gv * d
        dx_ref[...] = dres_ref[...] + r * gd - xv * ((r * r * r) * jnp.mean(xv * gd, axis=-1, keepdims=True))
        dg_ref[...] += jnp.sum(d * (xv * r), axis=0, keepdims=True)

    row = pl.BlockSpec((NORM_T, D_MODEL), lambda i: (i, 0))
    vec = pl.BlockSpec((1, D_MODEL), lambda i: (0, 0))
    return pl.pallas_call(
        body, name="rmsnorm_bwd", grid=(S // NORM_T,),
        in_specs=[row, vec, row, row], out_specs=(row, vec),
        out_shape=(jax.ShapeDtypeStruct((S, D_MODEL), F32), jax.ShapeDtypeStruct((1, D_MODEL), F32)),
        compiler_params=_cp("arbitrary"),
    )(x, g, dxn, dres)


def loss_head(y, tgt):
    S = y.shape[0]

    def body(y_ref, t_ref, dy_ref, l_ref):
        @pl.when(pl.program_id(0) == 0)
        def _():
            l_ref[...] = jnp.zeros_like(l_ref)

        e = y_ref[...] - t_ref[...]
        dy_ref[...] = e * (1.0 / D_MODEL)
        l_ref[...] += jnp.sum(e * e) * (0.5 / D_MODEL)

    row = pl.BlockSpec((NORM_T, D_MODEL), lambda i: (i, 0))
    return pl.pallas_call(
        body, name="loss_head", grid=(S // NORM_T,),
        in_specs=[row, row], out_specs=(row, pl.BlockSpec((1, 128), lambda i: (0, 0))),
        out_shape=(jax.ShapeDtypeStruct((S, D_MODEL), F32), jax.ShapeDtypeStruct((1, 128), F32)),
        compiler_params=_cp("arbitrary"),
    )(y, tgt)


def _hg_block(qs, vs, zs, lb, sT, reverse):
    n = len(qs)
    row = lax.broadcasted_iota(jnp.int32, (HG_C, HG_C), 0)
    col = lax.broadcasted_iota(jnp.int32, (HG_C, HG_C), 1)
    tri = (row <= col) if reverse else (row >= col)
    m = tri.astype(F32)
    rsel = lax.broadcasted_iota(jnp.int32, (HG_C, 128), 0)
    ref_rows = ((rsel >= HG_C // 2) if reverse else (rsel <= HG_C // 2)).astype(F32)
    att, qdec, upd, keep = [None] * n, [None] * n, [None] * n, [None] * n
    for c in range(n):
        f = lb + (1.0 - lb) * _sigmoid(zs[c])
        kc = 1.0 - f
        lc = jnp.log(f)
        b = cumdot(m, lc)
        btot = jnp.sum(lc, axis=0, keepdims=True)
        bref = lax.stop_gradient(jnp.sum(lc * ref_rows, axis=0, keepdims=True))
        qe = qs[c] * jnp.exp(jnp.minimum(b - bref, 80.0))
        ke = kc * jnp.exp(jnp.minimum(bref - b, 80.0))
        att[c] = jnp.where(tri, dot_nt(qe, ke), 0.0)
        qdec[c] = qs[c] * jnp.exp(b)
        upd[c] = dot_tn(vs[c], kc * jnp.exp(btot - b))
        keep[c] = jnp.exp(btot)
    states = [None] * n
    for c in (range(n - 1, -1, -1) if reverse else range(n)):
        states[c] = sT
        sT = sT * keep[c] + upd[c]
    outs = [dot_nn(att[c], vs[c]) + dot_nt(qdec[c], states[c]) for c in range(n)]
    return outs, sT


def _chunks(ref, c, n):
    return [ref[i * c:(i + 1) * c, :] for i in range(n)]


def hg_scan_fwd(h, lbf, lbb):
    S = h.shape[0]
    nb = S // HG_T
    n = HG_T // HG_C

    def body(ins, outs, scr):
        qf, vf, zf, qb, vb, zb, lbf_ref, lbb_ref = ins
        of_ref, ob_ref, sf_ref, sb_ref = outs
        stf, stb = scr

        @pl.when(pl.program_id(1) == 0)
        def _():
            stf[...] = jnp.zeros_like(stf)
            stb[...] = jnp.zeros_like(stb)

        for (q, v, z, lb_ref, o_ref, s_ref, st, rev) in ((qf, vf, zf, lbf_ref, of_ref, sf_ref, stf, False),
                                                         (qb, vb, zb, lbb_ref, ob_ref, sb_ref, stb, True)):
            s_ref[0, 0] = st[...]
            outs, s_new = _hg_block(_chunks(q, HG_C, n), _chunks(v, HG_C, n), _chunks(z, HG_C, n),
                                    lb_ref[...], st[...], rev)
            for c in range(n):
                o_ref[c * HG_C:(c + 1) * HG_C, :] = outs[c]
            st[...] = s_new

    def col(off, rev):
        return pl.BlockSpec((HG_T, 128), (lambda hh, t: (nb - 1 - t, off // 128 + hh)) if rev
                            else (lambda hh, t: (t, off // 128 + hh)))

    lb_spec = pl.BlockSpec((1, 128), lambda hh, t: (0, hh))
    st_f = pl.BlockSpec((1, 1, 128, 128), lambda hh, t: (hh, t, 0, 0))
    st_b = pl.BlockSpec((1, 1, 128, 128), lambda hh, t: (hh, nb - 1 - t, 0, 0))
    outs, carried = _hosted(
        "hg_scan_fwd", body, None, grid=(HG_HEADS, nb),
        in_specs=[col(OFF_HG_Q, False), col(OFF_HG_V, False), col(OFF_HG_ZF, False),
                  col(OFF_HG_Q, True), col(OFF_HG_V, True), col(OFF_HG_ZB, True), lb_spec, lb_spec],
        out_specs=[col(0, False), col(0, True), st_f, st_b],
        out_shape=[jax.ShapeDtypeStruct((S, HG_W), F32), jax.ShapeDtypeStruct((S, HG_W), F32),
                   jax.ShapeDtypeStruct((HG_HEADS, nb, 128, 128), F32),
                   jax.ShapeDtypeStruct((HG_HEADS, nb, 128, 128), F32)],
        scratch_shapes=[pltpu.VMEM((128, 128), F32), pltpu.VMEM((128, 128), F32)],
        sem=("arbitrary", "arbitrary"), operands=(h, h, h, h, h, h, lbf, lbb))
    return tuple(outs)


def hg_scan_bwd(h, lbf, lbb, do, sf, sb, rider=None):
    S = h.shape[0]
    nb = S // HG_T
    n = HG_T // HG_C

    def body(ins, outs, scr):
        qf, vf, zf, dof, sfin, qb, vb, zb, dob, sbin, lbf_ref, lbb_ref = ins
        dqf, dvf, dzf, dqb, dvb, dzb, dlbf, dlbb = outs
        dsf, dsb = scr

        @pl.when(pl.program_id(1) == 0)
        def _():
            for r in (dsf, dsb, dlbf, dlbb):
                r[...] = jnp.zeros_like(r)

        for (q, v, z, dor, sin, lb_ref, dq, dv, dz, dlb, ds, rev) in (
                (qf, vf, zf, dof, sfin, lbf_ref, dqf, dvf, dzf, dlbf, dsf, False),
                (qb, vb, zb, dob, sbin, lbb_ref, dqb, dvb, dzb, dlbb, dsb, True)):
            fn = functools.partial(_hg_block, reverse=rev)
            _, vjp = jax.vjp(fn, _chunks(q, HG_C, n), _chunks(v, HG_C, n), _chunks(z, HG_C, n), lb_ref[...], sin[0, 0])
            dqs, dvs, dzs, dlb_v, ds_in = vjp((_chunks(dor, HG_C, n), ds[...]))
            for c in range(n):
                sl = slice(c * HG_C, (c + 1) * HG_C)
                dq[sl, :] = dqs[c]
                dv[sl, :] = dvs[c]
                dz[sl, :] = dzs[c]
            dlb[...] += dlb_v
            ds[...] = ds_in

    def col(off, fwd_scan):
        return pl.BlockSpec((HG_T, 128), (lambda hh, t: (nb - 1 - t, off // 128 + hh)) if fwd_scan
                            else (lambda hh, t: (t, off // 128 + hh)))

    lb_spec = pl.BlockSpec((1, 128), lambda hh, t: (0, hh))
    st_f = pl.BlockSpec((1, 1, 128, 128), lambda hh, t: (hh, nb - 1 - t, 0, 0))
    st_b = pl.BlockSpec((1, 1, 128, 128), lambda hh, t: (hh, t, 0, 0))
    full = jax.ShapeDtypeStruct((S, HG_W), F32)
    vec = jax.ShapeDtypeStruct((1, HG_W), F32)
    outs, carried = _hosted(
        "hg_scan_bwd", body, rider, grid=(HG_HEADS, nb),
        in_specs=[col(OFF_HG_Q, True), col(OFF_HG_V, True), col(OFF_HG_ZF, True), col(0, True), st_f,
                  col(OFF_HG_Q, False), col(OFF_HG_V, False), col(OFF_HG_ZB, False), col(0, False), st_b,
                  lb_spec, lb_spec],
        out_specs=[col(0, True), col(0, True), col(0, True), col(0, False), col(0, False), col(0, False),
                   lb_spec, lb_spec],
        out_shape=[full, full, full, full, full, full, vec, vec],
        scratch_shapes=[pltpu.VMEM((128, 128), F32), pltpu.VMEM((128, 128), F32)],
        sem=("arbitrary", "arbitrary"), operands=(h, h, h, do, sf, h, h, h, do, sb, lbf, lbb))
    return (tuple(outs), carried) if rider else tuple(outs)


GN_T = 1024


def _gated_norm(o, gate, g, center):
    if center:
        o = o - jnp.mean(o, axis=-1, keepdims=True)
    o = o * lax.rsqrt(jnp.mean(o * o, axis=-1, keepdims=True) + EPS)
    return (o * g) * (gate * _sigmoid(gate))


def gated_norm_fwd(name, of, ob, h, gate_off, g, center):
    S = of.shape[0]

    def body(of_ref, ob_ref, gate_ref, g_ref, y_ref):
        y_ref[...] = _gated_norm(of_ref[...] + ob_ref[...], gate_ref[...], g_ref[...], center)

    blk = pl.BlockSpec((GN_T, 128), lambda hh, i: (i, hh))
    return pl.pallas_call(
        body, name=name, grid=(6, S // GN_T),
        in_specs=[blk, blk, pl.BlockSpec((GN_T, 128), lambda hh, i: (i, gate_off // 128 + hh)),
                  pl.BlockSpec((1, 128), lambda hh, i: (0, hh))],
        out_specs=blk, out_shape=jax.ShapeDtypeStruct((S, 768), F32),
        compiler_params=_cp("parallel", "parallel"),
    )(of, ob, h, g)


def gated_norm_bwd(name, of, ob, h, gate_off, g, dy, dy_off, center):
    S = of.shape[0]

    def body(of_ref, ob_ref, gate_ref, g_ref, dy_ref, do_ref, dgate_ref, dg_ref):
        @pl.when(pl.program_id(1) == 0)
        def _():
            dg_ref[...] = jnp.zeros_like(dg_ref)

        fn = functools.partial(_gated_norm, center=center)
        _, vjp = jax.vjp(fn, of_ref[...] + ob_ref[...], gate_ref[...], g_ref[...])
        do, dgate, dg = vjp(dy_ref[...])
        do_ref[...] = do
        dgate_ref[...] = dgate
        dg_ref[...] += dg

    blk = pl.BlockSpec((GN_T, 128), lambda hh, i: (i, hh))
    vec = pl.BlockSpec((1, 128), lambda hh, i: (0, hh))
    return pl.pallas_call(
        body, name=name, grid=(6, S // GN_T),
        in_specs=[blk, blk, pl.BlockSpec((GN_T, 128), lambda hh, i: (i, gate_off // 128 + hh)), vec,
                  pl.BlockSpec((GN_T, 128), lambda hh, i: (i, dy_off // 128 + hh))],
        out_specs=(blk, blk, vec),
        out_shape=(jax.ShapeDtypeStruct((S, 768), F32), jax.ShapeDtypeStruct((S, 768), F32),
                   jax.ShapeDtypeStruct((1, 768), F32)),
        compiler_params=_cp("arbitrary", "arbitrary"),
    )(of, ob, h, g, dy)


def _ret_consts(S):
    half = RET_DK // 2
    inv = ROPE_BASE ** (-jnp.arange(half, dtype=F32) / half)
    ang = jnp.arange(S, dtype=F32)[:, None] * inv[None, :]
    cos, sin = jnp.cos(ang), jnp.sin(ang)
    cos_t = jnp.tile(jnp.concatenate([cos, cos], axis=1), (1, RET_HEADS))
    sin_t = jnp.tile(jnp.concatenate([-sin, sin], axis=1), (1, RET_HEADS))
    hidx = jnp.arange(RET_HEADS, dtype=F32)
    lg_f = jnp.log1p(-jnp.exp2(-5.0 - hidx))
    C = RET_CHUNK
    idx = jnp.arange(C, dtype=F32)
    rel = idx[:, None] - idx[None, :]

    def one(lg, reverse):
        lgc = lg[:, None]
        decay = jnp.where(rel >= 0, jnp.exp(lgc[:, :, None] * jnp.maximum(rel, 0.0)), 0.0)
        zeta = jnp.exp(lgc * (C - 1 - idx))
        xi = jnp.exp(lgc * (idx + 1))
        if reverse:
            decay = decay[:, ::-1, ::-1]
            zeta, xi = zeta[:, ::-1], xi[:, ::-1]
        wide = lambda t: jnp.repeat(t.T, RET_DK, axis=1)
        gam_w = jnp.broadcast_to(jnp.repeat(jnp.exp(lg * C), 128)[None, :], (8, RET_W))
        return decay, wide(xi), wide(zeta), gam_w

    hm = (jnp.arange(RET_QK_W)[None, :] // RET_DK == jnp.arange(8)[:, None]).astype(F32)
    return (cos_t, sin_t, hm) + one(lg_f, False) + one(lg_f[::-1], True)


def _rope(t, cos, sin_signed):
    lane = lax.broadcasted_iota(jnp.int32, t.shape, 1)
    first = (lane & (RET_DK - 1)) < RET_DK // 2
    partner = jnp.where(first, pltpu.roll(t, RET_QK_W - RET_DK // 2, 1), pltpu.roll(t, RET_DK // 2, 1))
    return t * cos + partner * sin_signed


def _ret_block(qs, ks, vs, st, dec, xi, zeta, gam, hms, reverse):
    n = len(qs)
    heads = range(RET_HEADS)
    qx = [q * xi for q in qs]
    kz = [k * zeta for k in ks]
    sc = [[dot_nt(qs[c] * hms[hh], ks[c]) * dec[hh] for hh in heads] for c in range(n)]
    upd = [[dot_tn(kz[c] * hms[hh], vs[c][hh]) for hh in heads] for c in range(n)]
    st = list(st)
    seen = [None] * n
    for c in (range(n - 1, -1, -1) if reverse else range(n)):
        seen[c] = list(st)
        st = [st[hh] * gam[hh] + upd[c][hh] for hh in heads]
    outs = [[dot_nn(sc[c][hh], vs[c][hh]) + dot_nn(qx[c], seen[c][hh]) for hh in heads] for c in range(n)]
    return outs, st


def _ret_inputs(q_ref, k_ref, v_ref, cos_ref, sin_ref):
    n = RET_T // RET_CHUNK
    qr = _rope(q_ref[...], cos_ref[...], sin_ref[...])
    kr = _rope(k_ref[...], cos_ref[...], sin_ref[...]) * (RET_DK ** -0.5)
    qs = [qr[c * RET_CHUNK:(c + 1) * RET_CHUNK] for c in range(n)]
    ks = [kr[c * RET_CHUNK:(c + 1) * RET_CHUNK] for c in range(n)]
    vs = [[v_ref[c * RET_CHUNK:(c + 1) * RET_CHUNK, hh * 128:(hh + 1) * 128] for hh in range(RET_HEADS)]
          for c in range(n)]
    return qs, ks, vs


def _ret_dir_consts(dec_ref, xi_ref, zeta_ref, gam_ref, hm_ref):
    dec = [dec_ref[hh] for hh in range(RET_HEADS)]
    gam = [gam_ref[0:1, hh * 128:(hh + 1) * 128] for hh in range(RET_HEADS)]
    hms = [hm_ref[hh:hh + 1, :] for hh in range(RET_HEADS)]
    return dec, xi_ref[...], zeta_ref[...], gam, hms


def _ret_rows(nb, rev):
    def rows(width, colblk):
        return pl.BlockSpec((RET_T, width), (lambda t: (nb - 1 - t, colblk)) if rev else (lambda t: (t, colblk)))
    return rows


def _const_spec(shape):
    nd = len(shape)
    return pl.BlockSpec(shape, lambda t: (0,) * nd)


def ret_scan_fwd(h, consts):
    S = h.shape[0]
    nb = S // RET_T
    n = RET_T // RET_CHUNK
    cos_t, sin_t, hm, dec_f, xi_f, zeta_f, gam_f, dec_b, xi_b, zeta_b, gam_b = consts

    def body(ins, outs, scr):
        qf, kf, vf, cf, sf, qb, kb, vb, cb, sb_, hm_ref, decf, xif, zetaf, gamf, decb, xib, zetab, gamb = ins
        of_ref, ob_ref, sfo, sbo = outs
        stf, stb = scr

        @pl.when(pl.program_id(0) == 0)
        def _():
            stf[...] = jnp.zeros_like(stf)
            stb[...] = jnp.zeros_like(stb)

        for (q, k, v, cs, sn, dr, xr, zr, gr, o_ref, so, st, rev) in (
                (qf, kf, vf, cf, sf, decf, xif, zetaf, gamf, of_ref, sfo, stf, False),
                (qb, kb, vb, cb, sb_, decb, xib, zetab, gamb, ob_ref, sbo, stb, True)):
            so[0] = st[...]
            qs, ks, vs = _ret_inputs(q, k, v, cs, sn)
            dec, xi, zeta, gam, hms = _ret_dir_consts(dr, xr, zr, gr, hm_ref)
            st_in = [st[:, hh * 128:(hh + 1) * 128] for hh in range(RET_HEADS)]
            outs, st_new = _ret_block(qs, ks, vs, st_in, dec, xi, zeta, gam, hms, rev)
            for c in range(n):
                for hh in range(RET_HEADS):
                    o_ref[c * RET_CHUNK:(c + 1) * RET_CHUNK, hh * 128:(hh + 1) * 128] = outs[c][hh]
            for hh in range(RET_HEADS):
                st[:, hh * 128:(hh + 1) * 128] = st_new[hh]

    rf, rb = _ret_rows(nb, False), _ret_rows(nb, True)
    cspecs = [_const_spec(a.shape) for a in (hm, dec_f, xi_f, zeta_f, gam_f, dec_b, xi_b, zeta_b, gam_b)]
    st_shape = jax.ShapeDtypeStruct((nb, RET_QK_W, RET_W), F32)
    qc, kc, vc = OFF_RET_Q // RET_QK_W, OFF_RET_K // RET_QK_W, OFF_RET_V // RET_W
    outs, carried = _hosted(
        "ret_scan_fwd", body, None, grid=(nb,),
        in_specs=[rf(RET_QK_W, qc), rf(RET_QK_W, kc), rf(RET_W, vc), rf(RET_QK_W, 0), rf(RET_QK_W, 0),
                  rb(RET_QK_W, qc), rb(RET_QK_W, kc), rb(RET_W, vc), rb(RET_QK_W, 0), rb(RET_QK_W, 0)] + cspecs,
        out_specs=[rf(RET_W, 0), rb(RET_W, 0),
                   pl.BlockSpec((1, RET_QK_W, RET_W), lambda t: (t, 0, 0)),
                   pl.BlockSpec((1, RET_QK_W, RET_W), lambda t: (nb - 1 - t, 0, 0))],
        out_shape=[jax.ShapeDtypeStruct((S, RET_W), F32), jax.ShapeDtypeStruct((S, RET_W), F32), st_shape, st_shape],
        scratch_shapes=[pltpu.VMEM((RET_QK_W, RET_W), F32), pltpu.VMEM((RET_QK_W, RET_W), F32)],
        sem=("arbitrary",),
        operands=(h, h, h, cos_t, sin_t, h, h, h, cos_t, sin_t, hm, dec_f, xi_f, zeta_f, gam_f, dec_b, xi_b, zeta_b,
                  gam_b))
    return tuple(outs)


def ret_scan_bwd(h, consts, do, sf, sb, rider=None):
    S = h.shape[0]
    nb = S // RET_T
    n = RET_T // RET_CHUNK
    cos_t, sin_t, hm, dec_f, xi_f, zeta_f, gam_f, dec_b, xi_b, zeta_b, gam_b = consts

    def body(ins, outs, scr):
        (qf, kf, vf, cf, sf_, dof, sfin, qb, kb, vb, cb, sb_, dob, sbin,
         hm_ref, decf, xif, zetaf, gamf, decb, xib, zetab, gamb) = ins
        dqf, dkf, dvf, dqb, dkb, dvb = outs
        dsf, dsb = scr

        @pl.when(pl.program_id(0) == 0)
        def _():
            dsf[...] = jnp.zeros_like(dsf)
            dsb[...] = jnp.zeros_like(dsb)

        for (q, k, v, cs, sn, dor, sin, dr, xr, zr, gr, dq, dk, dv, ds, rev) in (
                (qf, kf, vf, cf, sf_, dof, sfin, decf, xif, zetaf, gamf, dqf, dkf, dvf, dsf, False),
                (qb, kb, vb, cb, sb_, dob, sbin, decb, xib, zetab, gamb, dqb, dkb, dvb, dsb, True)):
            qs, ks, vs = _ret_inputs(q, k, v, cs, sn)
            dec, xi, zeta, gam, hms = _ret_dir_consts(dr, xr, zr, gr, hm_ref)
            st_in = [sin[0, :, hh * 128:(hh + 1) * 128] for hh in range(RET_HEADS)]
            fn = lambda a, b_, c_, d_: _ret_block(a, b_, c_, d_, dec, xi, zeta, gam, hms, rev)
            _, vjp = jax.vjp(fn, qs, ks, vs, st_in)
            dos = [[dor[c * RET_CHUNK:(c + 1) * RET_CHUNK, hh * 128:(hh + 1) * 128] for hh in range(RET_HEADS)]
                   for c in range(n)]
            dst = [ds[:, hh * 128:(hh + 1) * 128] for hh in range(RET_HEADS)]
            dqs, dks, dvs, dst_in = vjp((dos, dst))
            cosv, sinv = cs[...], sn[...]
            dq[...] = _rope(jnp.concatenate(dqs, axis=0), cosv, -sinv)
            dk[...] = _rope(jnp.concatenate(dks, axis=0) * (RET_DK ** -0.5), cosv, -sinv)
            for c in range(n):
                for hh in range(RET_HEADS):
                    dv[c * RET_CHUNK:(c + 1) * RET_CHUNK, hh * 128:(hh + 1) * 128] = dvs[c][hh]
            for hh in range(RET_HEADS):
                ds[:, hh * 128:(hh + 1) * 128] = dst_in[hh]

    rf, rb = _ret_rows(nb, True), _ret_rows(nb, False)
    cspecs = [_const_spec(a.shape) for a in (hm, dec_f, xi_f, zeta_f, gam_f, dec_b, xi_b, zeta_b, gam_b)]
    qk = jax.ShapeDtypeStruct((S, RET_QK_W), F32)
    vv = jax.ShapeDtypeStruct((S, RET_W), F32)
    qc, kc, vc = OFF_RET_Q // RET_QK_W, OFF_RET_K // RET_QK_W, OFF_RET_V // RET_W
    outs, carried = _hosted(
        "ret_scan_bwd", body, rider, grid=(nb,),
        in_specs=[rf(RET_QK_W, qc), rf(RET_QK_W, kc), rf(RET_W, vc), rf(RET_QK_W, 0), rf(RET_QK_W, 0), rf(RET_W, 0),
                  pl.BlockSpec((1, RET_QK_W, RET_W), lambda t: (nb - 1 - t, 0, 0)),
                  rb(RET_QK_W, qc), rb(RET_QK_W, kc), rb(RET_W, vc), rb(RET_QK_W, 0), rb(RET_QK_W, 0), rb(RET_W, 0),
                  pl.BlockSpec((1, RET_QK_W, RET_W), lambda t: (t, 0, 0))] + cspecs,
        out_specs=[rf(RET_QK_W, 0), rf(RET_QK_W, 0), rf(RET_W, 0), rb(RET_QK_W, 0), rb(RET_QK_W, 0), rb(RET_W, 0)],
        out_shape=[qk, qk, vv, qk, qk, vv],
        scratch_shapes=[pltpu.VMEM((RET_QK_W, RET_W), F32), pltpu.VMEM((RET_QK_W, RET_W), F32)],
        sem=("arbitrary",),
        operands=(h, h, h, cos_t, sin_t, do, sf, h, h, h, cos_t, sin_t, do, sb,
                  hm, dec_f, xi_f, zeta_f, gam_f, dec_b, xi_b, zeta_b, gam_b))
    return (tuple(outs), carried) if rider else tuple(outs)


def _t5_bucket(rel):
    nb = REL_BUCKETS // 2
    max_exact = nb // 2
    sign_off = jnp.where(rel > 0, nb, 0)
    n = jnp.abs(rel)
    nf = jnp.maximum(n, 1).astype(F32)
    large = max_exact + (jnp.log(nf / max_exact) / math.log(REL_MAX_DIST / max_exact)
                         * (nb - max_exact)).astype(jnp.int32)
    large = jnp.minimum(large, nb - 1)
    return sign_off + jnp.where(n < max_exact, n, large)


def _dil_buckets(dil):
    tq, tb = DIL_TQ, DIL_TQ + 2 * DIL_HALF
    rel_q = jnp.arange(tb)[None, :] - DIL_HALF - jnp.arange(tq)[:, None]
    rel_k = jnp.arange(tq)[None, :] + DIL_HALF - jnp.arange(tb)[:, None]
    return _t5_bucket(rel_q * dil), _t5_bucket(rel_k * dil)


def dil_view(h, g, dil):
    base = OFF_DIL + 3 * g * DIL_W
    if dil == 1:
        return h, IN_W, base
    return h[:, base:base + 3 * DIL_W].reshape(h.shape[0] // dil, dil * 3 * DIL_W), 3 * DIL_W, 0


def _dil_col(view, j):
    _, width, base = view
    return lambda r: (r * width + base + j * DIL_W) // DIL_W


def _dil_specs(L):
    nq = DIL_TQ // DIL_HALF
    last = L // DIL_HALF - 1

    def cur(colfn):
        return pl.BlockSpec((DIL_TQ, DIL_W), lambda r, n: (n, colfn(r)))

    def prev(colfn):
        return pl.BlockSpec((DIL_HALF, DIL_W), lambda r, n: (jnp.maximum(n * nq - 1, 0), colfn(r)))

    def nxt(colfn):
        return pl.BlockSpec((DIL_HALF, DIL_W), lambda r, n: (jnp.minimum((n + 1) * nq, last), colfn(r)))

    return prev, cur, nxt


def _slot(s):
    return slice(s * DIL_HD, (s + 1) * DIL_HD)


def _cat3(a, b, c, s):
    return jnp.concatenate([a[:, _slot(s)], b[:, _slot(s)], c[:, _slot(s)]], axis=0)


def dil_attn_fwd(view, S, g, dil, bias, qg, kg):
    L = S // dil
    hv = view[0]
    tb = DIL_TQ + 2 * DIL_HALF

    def body(q_ref, kp, kc, kn, vp, vc, vn, bias_ref, qg_ref, kg_ref, o_ref, lse_ref):
        n = pl.program_id(1)
        ii = lax.broadcasted_iota(jnp.int32, (DIL_TQ, tb), 0)
        jj = lax.broadcasted_iota(jnp.int32, (DIL_TQ, tb), 1)
        kabs = n * DIL_TQ - DIL_HALF + jj
        valid = (jnp.abs(jj - DIL_HALF - ii) <= DIL_HALF) & (kabs >= 0) & (kabs < L)
        for s in range(DIL_SLOTS):
            q = _head_rms(q_ref[:, _slot(s)], qg_ref[...]) * (DIL_HD ** -0.5)
            kb = _head_rms(_cat3(kp, kc, kn, s), kg_ref[...])
            sc = jnp.where(valid, _mxu(q, kb, NT) + bias_ref[s], NEG)
            m = jnp.max(sc, axis=-1, keepdims=True)
            p = jnp.exp(sc - m)
            den = jnp.sum(p, axis=-1, keepdims=True)
            o_ref[:, _slot(s)] = _mxu(p, _cat3(vp, vc, vn, s), NN) / den
            lse_ref[:, _slot(s)] = jnp.broadcast_to(m + jnp.log(den), (DIL_TQ, DIL_HD))

    prev, cur, nxt = _dil_specs(L)
    qc, kc_, vc_ = (_dil_col(view, j) for j in range(3))
    oc = lambda r: r
    vec = pl.BlockSpec((1, 128), lambda r, n: (0, 0))
    out = jax.ShapeDtypeStruct((L, dil * DIL_W), F32)
    o, lse = pl.pallas_call(
        body, name=f"dil_attn_fwd{g}", grid=(dil, L // DIL_TQ),
        in_specs=[cur(qc), prev(kc_), cur(kc_), nxt(kc_), prev(vc_), cur(vc_), nxt(vc_),
                  pl.BlockSpec((DIL_SLOTS, DIL_TQ, tb), lambda r, n: (0, 0, 0)), vec, vec],
        out_specs=(cur(oc), cur(oc)), out_shape=(out, out),
        compiler_params=_cp("parallel", "parallel"),
    )(hv, hv, hv, hv, hv, hv, hv, bias, qg, kg)
    return o.reshape(S, DIL_W), lse.reshape(S, DIL_W)


def dil_combine(os_, lses):
    S = os_[0].shape[0]

    def body(o1, o2, o3, l1, l2, l3, y_ref, lt_ref):
        a, b, c = l1[...], l2[...], l3[...]
        m = jnp.maximum(jnp.maximum(a, b), c)
        ea, eb, ec = jnp.exp(a - m), jnp.exp(b - m), jnp.exp(c - m)
        den = ea + eb + ec
        y_ref[...] = (ea * o1[...] + eb * o2[...] + ec * o3[...]) / den
        lt_ref[...] = m + jnp.log(den)

    blk = pl.BlockSpec((GN_T, DIL_W), lambda i: (i, 0))
    out = jax.ShapeDtypeStruct((S, DIL_W), F32)
    return pl.pallas_call(
        body, name="dil_combine", grid=(S // GN_T,), in_specs=[blk] * 6, out_specs=(blk, blk),
        out_shape=(out, out), compiler_params=_cp("parallel"),
    )(*os_, *lses)


def dil_delta(dy, yc):
    S = yc.shape[0]

    def body(dy_ref, y_ref, d_ref):
        d_ref[...] = jnp.broadcast_to(jnp.sum(dy_ref[...] * y_ref[...], axis=-1, keepdims=True), (GN_T, 128))

    return pl.pallas_call(
        body, name="dil_delta", grid=(S // GN_T, DIL_SLOTS),
        in_specs=[pl.BlockSpec((GN_T, 128), lambda i, s: (i, (HG_W + RET_W) // 128 + s)),
                  pl.BlockSpec((GN_T, 128), lambda i, s: (i, s))],
        out_specs=pl.BlockSpec((GN_T, 128), lambda i, s: (i, s)),
        out_shape=jax.ShapeDtypeStruct((S, DIL_W), F32), compiler_params=_cp("parallel", "parallel"),
    )(dy, yc)


def dil_attn_bwd(view, S, g, dil, bias_q, bias_k, qg, kg, dy, lse_t, delta):
    L = S // dil
    hv = view[0]
    if dil == 1:
        dyv, dyc = dy, lambda r: (HG_W + RET_W) // DIL_W
    else:
        dyv, dyc = dy[:, HG_W + RET_W:].reshape(L, dil * DIL_W), lambda r: r
    lv = lse_t.reshape(L, dil * DIL_W)
    dv_ = delta.reshape(L, dil * DIL_W)
    tq, tb = DIL_TQ, DIL_TQ + 2 * DIL_HALF
    scale = DIL_HD ** -0.5

    def body(qp, qc, qn, kp, kc, kn, vp, vc, vn, dp_, dc, dn, lp, lc, ln, ep, ec, en, bq_ref, bk_ref, qg_ref, kg_ref,
             dq_ref, dk_ref, dv_ref, dbias_ref, dqg_ref, dkg_ref):
        r, n = pl.program_id(0), pl.program_id(1)

        @pl.when((r == 0) & (n == 0))
        def _():
            for ref in (dbias_ref, dqg_ref, dkg_ref):
                ref[...] = jnp.zeros_like(ref)

        qgv, kgv = qg_ref[...], kg_ref[...]
        qfn = lambda t, gg: _head_rms(t, gg) * scale
        ii = lax.broadcasted_iota(jnp.int32, (tq, tb), 0)
        jj = lax.broadcasted_iota(jnp.int32, (tq, tb), 1)
        kabs = n * tq - DIL_HALF + jj
        valid = (jnp.abs(jj - DIL_HALF - ii) <= DIL_HALF) & (kabs >= 0) & (kabs < L)
        i2 = lax.broadcasted_iota(jnp.int32, (tb, tq), 0)
        j2 = lax.broadcasted_iota(jnp.int32, (tb, tq), 1)
        qabs = n * tq - DIL_HALF + i2
        valid2 = (jnp.abs(j2 + DIL_HALF - i2) <= DIL_HALF) & (qabs >= 0) & (qabs < L)
        for s in range(DIL_SLOTS):
            sl = _slot(s)
            one = slice(s * DIL_HD, s * DIL_HD + 1)
            qn_c, q_vjp = jax.vjp(qfn, qc[:, sl], qgv)
            k_band = _head_rms(_cat3(kp, kc, kn, s), kgv)
            sc = _mxu(qn_c, k_band, NT) + bq_ref[s]
            p = jnp.where(valid, jnp.exp(jnp.where(valid, sc, NEG) - lc[:, one]), 0.0)
            ds = p * (_mxu(dc[:, sl], _cat3(vp, vc, vn, s), NT) - ec[:, one])
            dbias_ref[s] += ds
            dq, dqg = q_vjp(_mxu(ds, k_band, NN))
            dq_ref[:, sl] = dq
            dqg_ref[s] += dqg
            kn_c, k_vjp = jax.vjp(_head_rms, kc[:, sl], kgv)
            q_band = qfn(_cat3(qp, qc, qn, s), qgv)
            do_band = _cat3(dp_, dc, dn, s)
            s2 = _mxu(q_band, kn_c, NT) + bk_ref[s]
            lse_band = jnp.concatenate([lp[:, one], lc[:, one], ln[:, one]], axis=0)
            delta_band = jnp.concatenate([ep[:, one], ec[:, one], en[:, one]], axis=0)
            p2 = jnp.where(valid2, jnp.exp(jnp.where(valid2, s2, NEG) - lse_band), 0.0)
            dv_ref[:, sl] = _mxu(p2, do_band, TN)
            ds2 = p2 * (_mxu(do_band, vc[:, sl], NT) - delta_band)
            dk, dkg = k_vjp(_mxu(ds2, q_band, TN))
            dk_ref[:, sl] = dk
            dkg_ref[s] += dkg

    prev, cur, nxt = _dil_specs(L)
    three = lambda colfn: [prev(colfn), cur(colfn), nxt(colfn)]
    qc_, kc_, vc_ = (_dil_col(view, j) for j in range(3))
    oc = lambda r: r
    vec = pl.BlockSpec((1, 128), lambda r, n: (0, 0))
    acc_vec = pl.BlockSpec((DIL_SLOTS, 1, 128), lambda r, n: (0, 0, 0))
    out = jax.ShapeDtypeStruct((L, dil * DIL_W), F32)
    dq, dk, dv, dbias, dqg, dkg = pl.pallas_call(
        body, name=f"dil_attn_bwd{g}", grid=(dil, L // tq),
        in_specs=three(qc_) + three(kc_) + three(vc_) + three(dyc) + three(oc) + three(oc)
        + [pl.BlockSpec((DIL_SLOTS, tq, tb), lambda r, n: (0, 0, 0)),
           pl.BlockSpec((DIL_SLOTS, tb, tq), lambda r, n: (0, 0, 0)), vec, vec],
        out_specs=(cur(oc), cur(oc), cur(oc), pl.BlockSpec((DIL_SLOTS, tq, tb), lambda r, n: (0, 0, 0)),
                   acc_vec, acc_vec),
        out_shape=(out, out, out, jax.ShapeDtypeStruct((DIL_SLOTS, tq, tb), F32),
                   jax.ShapeDtypeStruct((DIL_SLOTS, 1, 128), F32), jax.ShapeDtypeStruct((DIL_SLOTS, 1, 128), F32)),
        compiler_params=_cp("arbitrary", "arbitrary"),
    )(hv, hv, hv, hv, hv, hv, hv, hv, hv, dyv, dyv, dyv, lv, lv, lv, dv_, dv_, dv_, bias_q, bias_k, qg, kg)
    return dq.reshape(S, DIL_W), dk.reshape(S, DIL_W), dv.reshape(S, DIL_W), dbias, dqg, dkg


def _lb_eff(p):
    a = jnp.cumsum(jax.nn.softmax(p.astype(F32), axis=0), axis=0)
    return a - a[0:1]


def _dil_bias(rel_bias, g, dil):
    tbl = rel_bias[:, g * DIL_SLOTS:(g + 1) * DIL_SLOTS]
    return tuple(jnp.einsum("ijb,bs->sij", jax.nn.one_hot(b, REL_BUCKETS, dtype=F32), tbl,
                            precision=lax.Precision.HIGHEST) for b in _dil_buckets(dil))


def _big_weights(w):
    return w[0], w[1].reshape(D_MODEL, D_MODEL), w[2], w[3].reshape(D_FF, D_MODEL)


def _layer_fwd(x, l, prm, wts, rc, biases, gatherer=None):
    def carrying(name, call, *args):
        rider = gatherer.rider(l, name) if gatherer else None
        if rider is None:
            return call(*args)
        res, arrays = call(*args, rider)
        gatherer.done(l, name, arrays)
        return res

    weight = lambda t: _big_weights(wts[l])[t]
    row = lambda a: a[l][None]
    xn = rmsnorm_fwd(x, row(prm["norm_mix"]))
    h = carrying("in", proj_in, xn, weight(0))
    hof, hob, hsf, hsb = hg_scan_fwd(h, row(prm["lbf"]), row(prm["lbb"]))
    ya = gated_norm_fwd("hg_out", hof, hob, h, OFF_HG_GATE, row(prm["hg_norm"]), False)
    rof, rob, rsf, rsb = ret_scan_fwd(h, rc)
    yb = gated_norm_fwd("ret_out", rof, rob, h, OFF_RET_GATE, row(prm["ret_norm"]), True)
    os_, lses, views = [], [], []
    for g, (_, dil) in enumerate(DIL_GROUPS):
        views.append(dil_view(h, g, dil))
        o, lse = dil_attn_fwd(views[g], h.shape[0], g, dil, biases[g][0], row(prm["q_norm"]), row(prm["k_norm"]))
        os_.append(o)
        lses.append(lse)
    yc, lse_t = dil_combine(os_, lses)
    y = jnp.concatenate([ya, yb, yc], axis=1).astype(BF16)
    if gatherer:
        gatherer.alone(l, "mid")
    x2 = proj_out(y, weight(1), x)
    hm = rmsnorm_fwd(x2, row(prm["norm_mlp"]))
    u, act = carrying("up", proj_up, hm, weight(2))
    x3 = carrying("down", proj_down, act, weight(3), x2)
    saved = dict(x=x, xn=xn, h=h, hof=hof, hob=hob, hsf=hsf, hsb=hsb, rof=rof, rob=rob, rsf=rsf, rsb=rsb,
                 yc=yc, lse_t=lse_t, y=y, x2=x2, hm=hm, u=u, act=act, views=views)
    return x3, saved


def _layer_bwd(dx3, l, prm, wts, rc, biases, sv, reducer=None):
    def carrying(stage, group, call, *args):
        rider = getattr(reducer, stage + "_rider")(group) if reducer else None
        if rider is None:
            return call(*args)
        res, arrived = call(*args, rider)
        getattr(reducer, stage + "_done")(group, arrived)
        return res

    early, late = GradReducer.EARLY, GradReducer.LATE
    win_g, wout_g, wup_g, wdown_g = _big_weights(wts[l])
    row = lambda a: a[l][None]
    h = sv["h"]
    du = carrying("pair", late, bwd_down_act, dx3, wdown_g, sv["u"])
    g_down = wgrad("wgrad_down", sv["act"], dx3, m=D_FF, n=D_MODEL, n_shard=D_MODEL)
    dhm = bwd_up(du, wup_g)
    g_up = wgrad("wgrad_up", sv["hm"], du, m=D_MODEL, n=D_FF, n_shard=FF_SHARD)
    dx2, dg_mlp = rmsnorm_bwd(sv["x2"], row(prm["norm_mlp"]), dhm, dx3)
    dy = bwd_out(dx2, wout_g)
    g_out = wgrad("wgrad_out", sv["y"], dx2, m=D_MODEL, n=D_MODEL, n_shard=D_MODEL)
    g_out, g_down = g_out.reshape(N_CHIPS, D_MODEL // N_CHIPS, D_MODEL), g_down.reshape(N_CHIPS, D_FF // N_CHIPS, D_MODEL)
    if reducer:
        reducer.push(early, l, (g_out, g_up, g_down))
    hdo, hdgate, dg_hg = gated_norm_bwd("hg_out_bwd", sv["hof"], sv["hob"], h, OFF_HG_GATE, row(prm["hg_norm"]),
                                        dy, 0, False)
    hdqf, hdvf, hdzf, hdqb, hdvb, hdzb, dlbf, dlbb = carrying(
        "chip", late, hg_scan_bwd, h, row(prm["lbf"]), row(prm["lbb"]), hdo, sv["hsf"], sv["hsb"])
    rdo, rdgate, dg_ret = gated_norm_bwd("ret_out_bwd", sv["rof"], sv["rob"], h, OFF_RET_GATE, row(prm["ret_norm"]),
                                         dy, HG_W, True)
    rdqf, rdkf, rdvf, rdqb, rdkb, rdvb = carrying("pair", early, ret_scan_bwd, h, rc, rdo, sv["rsf"], sv["rsb"])
    delta = dil_delta(dy, sv["yc"])
    dil_parts, dbiases = [], []
    dqg = jnp.zeros((1, DIL_HD), F32)
    dkg = jnp.zeros((1, DIL_HD), F32)
    for g, (_, dil) in enumerate(DIL_GROUPS):
        dq, dk, dv, dbias, dqg_g, dkg_g = dil_attn_bwd(sv["views"][g], h.shape[0], g, dil, biases[g][0], biases[g][1],
                                                       row(prm["q_norm"]), row(prm["k_norm"]), dy, sv["lse_t"], delta)
        dil_parts += [dq, dk, dv]
        dbiases.append(dbias)
        dqg = dqg + jnp.sum(dqg_g, axis=0)
        dkg = dkg + jnp.sum(dkg_g, axis=0)
    dh = jnp.concatenate([hdqf + hdqb, hdvf + hdvb, hdzf, hdzb, hdgate,
                          rdqf + rdqb, rdkf + rdkb, rdvf + rdvb, rdgate] + dil_parts, axis=1).astype(BF16)
    dxn = carrying("chip", early, bwd_in, dh, win_g)
    g_in = wgrad("wgrad_in", sv["xn"], dh, m=D_MODEL, n=IN_W, n_shard=IN_SHARD)
    dx, dg_mix = rmsnorm_bwd(sv["x"], row(prm["norm_mix"]), dxn, dx2)
    small = dict(norm_mix=dg_mix, norm_mlp=dg_mlp, lbf=dlbf, lbb=dlbb, hg_norm=dg_hg, ret_norm=dg_ret,
                 q_norm=dqg, k_norm=dkg)
    if reducer:
        reducer.push(late, l, (g_in,))
    return dx, (g_in, g_out, g_up, g_down), small, dbiases


def _rel_bias_grad(dbias_layers):
    cols = []
    for g, (_, dil) in enumerate(DIL_GROUPS):
        bq, _ = _dil_buckets(dil)
        onehot = jax.nn.one_hot(bq, REL_BUCKETS, dtype=F32)
        tot = dbias_layers[0][g]
        for d in dbias_layers[1:]:
            tot = tot + d[g]
        cols.append(jnp.einsum("sij,ijb->bs", tot, onehot, precision=lax.Precision.HIGHEST))
    return jnp.concatenate(cols, axis=1)


def local_step(x, tgt, wts, prm_in, reducer=None, gatherer=None):
    S = x.shape[0]
    prm = dict(prm_in)
    prm["lbf"], lbf_vjp = jax.vjp(_lb_eff, prm_in["hg_lb_fwd"])
    prm["lbb"], lbb_vjp = jax.vjp(_lb_eff, prm_in["hg_lb_bwd"])
    rc = _ret_consts(S)
    biases = [_dil_bias(prm["rel_bias"], g, dil) for g, (_, dil) in enumerate(DIL_GROUPS)]
    saved = []
    for l in range(DEPTH):
        x, sv = _layer_fwd(x, l, prm, wts, rc, biases, gatherer)
        saved.append(sv)
    dx, loss_row = loss_head(x, tgt)
    big, small, dbias_layers = [None] * DEPTH, [None] * DEPTH, [None] * DEPTH
    for l in range(DEPTH - 1, -1, -1):
        dx, big[l], small[l], dbias_layers[l] = _layer_bwd(dx, l, prm, wts, rc, biases, saved[l], reducer)
    sg = {k: jnp.concatenate([small[l][k] for l in range(DEPTH)], axis=0) for k in small[0]}
    sg["rel_bias"] = _rel_bias_grad(dbias_layers)
    return loss_row[0, 0], dx, (reducer.finish() if reducer else big), sg, (lbf_vjp, lbb_vjp)


def _place():
    x, y, c = lax.axis_index("x"), lax.axis_index("y"), lax.axis_index("c")
    rels = [(1 - x, y), (x, 1 - y), (1 - x, 1 - y)]
    return x, y, c, 2 * x + y, rels


def _half(c, rows):
    return pl.ds(pl.multiple_of(c * (rows // 2), 16), rows // 2)


def place_own(name, p_arr, w, l):
    _, rows, cols = w.shape
    tr = 512

    def body(p_ref, w_ref, o_ref):
        o_ref[...] = w_ref[...].astype(BF16)

    return pl.pallas_call(
        body, name=name,
        grid_spec=pltpu.PrefetchScalarGridSpec(
            num_scalar_prefetch=1, grid=(rows // tr,),
            in_specs=[pl.BlockSpec((1, tr, cols), lambda i, p: (l, i, 0))],
            out_specs=pl.BlockSpec((1, tr, cols), lambda i, p: (p[0], i, 0))),
        out_shape=jax.ShapeDtypeStruct((N_CHIPS, rows, cols), BF16),
        compiler_params=_cp("parallel"),
    )(p_arr, w)


class WeightGatherer:
    PLAN = {
        (0, "in"): [("ici", 0, (1, 2, 3))],
        (0, "mid"): [("pass", 0, (1, 2, 3))],
        (0, "up"): [("ici", 1, (0, 1))],
        (0, "down"): [("pass", 1, (0, 1)), ("ici", 1, (2, 3))],
        (1, "in"): [("pass", 1, (2, 3)), ("ici", 2, (0, 1))],
        (1, "up"): [("pass", 2, (0, 1)), ("ici", 2, (2, 3))],
        (1, "down"): [("pass", 2, (2, 3))],
        (2, "in"): [("ici", 3, (0, 1))],
        (2, "up"): [("pass", 3, (0, 1)), ("ici", 3, (2, 3))],
        (2, "down"): [("pass", 3, (2, 3))],
    }

    def __init__(self, big_w):
        assert DEPTH == 4
        p_arr = (2 * lax.axis_index("x") + lax.axis_index("y")).astype(jnp.int32).reshape(1)
        self.w = [[place_own(f"place_own{t}", p_arr, w, l) for t, w in enumerate(big_w)] for l in range(DEPTH)]
        self.w[0][0:1] = run_alone("gather_first_chips", gather_ici_rider(self.w[0][0:1]))
        self.w[0][0:1] = run_alone("gather_first_cores", gather_pass_rider(self.w[0][0:1]))

    def rider(self, l, call):
        parts = [(gather_ici_rider if kind == "ici" else gather_pass_rider)([self.w[wl][t] for t in ts])
                 for kind, wl, ts in self.PLAN.get((l, call), ())]
        return functools.reduce(lambda a, b: a + b, parts) if parts else None

    def done(self, l, call, arrays):
        arrays = list(arrays)
        for _, wl, ts in self.PLAN[(l, call)]:
            for t in ts:
                self.w[wl][t] = arrays.pop(0)

    def alone(self, l, call):
        rider = self.rider(l, call)
        if rider:
            self.done(l, call, run_alone(f"gather_{call}", rider))


def run_alone(name, rider):
    n_in, n_out = len(rider.arrays), len(rider.out_shapes)

    def body(*refs):
        ins, outs, sems = refs[:n_in], refs[n_in:n_in + n_out], refs[n_in + n_out:]
        rider.start(ins, outs, sems)
        rider.finish(ins, outs, sems)

    return pl.pallas_call(
        body, name=name, in_specs=[ANY] * n_in, out_specs=[ANY] * n_out, out_shape=rider.out_shapes,
        scratch_shapes=[pltpu.SemaphoreType.DMA((n,)) for n in rider.sems],
        input_output_aliases=rider.aliases(0, 0),
    )(*rider.arrays)


def _both(cp):
    return (cp, cp)


def pair_exchange_rider(gs):
    n = len(gs)

    def ops(ins, outs, ssem, rsem):
        x, y, c, _, _ = _place()
        return [_both(pltpu.make_async_remote_copy(
            src_ref=ins[i].at[:, _half(1 - c, ins[i].shape[1]), :], dst_ref=outs[i],
            send_sem=ssem.at[i], recv_sem=rsem.at[i], device_id=(x, y, 1 - c), device_id_type=MESH)) for i in range(n)]

    return Rider(gs, [jax.ShapeDtypeStruct((N_CHIPS, g.shape[1] // 2, g.shape[2]), F32) for g in gs], [n, n], ops)


def gather_ici_rider(bufs):
    n = len(bufs)

    def ops(ins, outs, ssem, rsem):
        x, y, c, p, rels = _place()
        cps = []
        for i in range(n):
            mine = _half(c, outs[i].shape[1])
            for r, (rx, ry) in enumerate(rels):
                k = i * 3 + r
                peer = dict(device_id=(rx, ry, c), device_id_type=MESH, send_sem=ssem.at[k], recv_sem=rsem.at[k])
                own, landing = outs[i].at[p, mine], outs[i].at[2 * rx + ry, mine]
                cps.append((pltpu.make_async_remote_copy(src_ref=own, dst_ref=own, **peer),
                            pltpu.make_async_remote_copy(src_ref=landing, dst_ref=landing, **peer)))
        return cps

    return Rider(bufs, None, [3 * n, 3 * n], ops)


def gather_pass_rider(bufs):
    n = len(bufs)

    def ops(ins, outs, ssem, rsem):
        x, y, c, p, rels = _place()
        cps = []
        for i in range(n):
            rows = outs[i].shape[1]
            for r, (rx, ry) in enumerate(rels):
                k = i * 3 + r
                peer = dict(device_id=(x, y, 1 - c), device_id_type=MESH, send_sem=ssem.at[k], recv_sem=rsem.at[k])
                landed, theirs = outs[i].at[2 * rx + ry, _half(c, rows)], outs[i].at[2 * rx + ry, _half(1 - c, rows)]
                cps.append((pltpu.make_async_remote_copy(src_ref=landed, dst_ref=landed, **peer),
                            pltpu.make_async_remote_copy(src_ref=theirs, dst_ref=theirs, **peer)))
        return cps

    return Rider(bufs, None, [3 * n, 3 * n], ops)


def pair_add(name, c_arr, g, got):
    _, rows, cols = g.shape
    hr = rows // 2
    tr = 256
    nblk = hr // tr

    def body(c_ref, g_ref, r_ref, o32, o16):
        s = g_ref[...] + r_ref[...]
        o32[...] = s
        o16[...] = s.astype(BF16)

    blk = pl.BlockSpec((1, tr, cols), lambda pp, i, c_ref: (pp, i, 0))
    return pl.pallas_call(
        body, name=name,
        grid_spec=pltpu.PrefetchScalarGridSpec(
            num_scalar_prefetch=1, grid=(N_CHIPS, nblk),
            in_specs=[pl.BlockSpec((1, tr, cols), lambda pp, i, c_ref: (pp, c_ref[0] * nblk + i, 0)), blk],
            out_specs=(blk, blk)),
        out_shape=(jax.ShapeDtypeStruct((N_CHIPS, hr, cols), F32), jax.ShapeDtypeStruct((N_CHIPS, hr, cols), BF16)),
        compiler_params=_cp("parallel", "parallel"),
    )(c_arr, g, got)


def chip_exchange_rider(cs16):
    n = len(cs16)

    def ops(ins, outs, ssem, rsem):
        x, y, c, p, rels = _place()
        return [_both(pltpu.make_async_remote_copy(
            src_ref=ins[i].at[2 * rx + ry], dst_ref=outs[i].at[r], send_sem=ssem.at[i * 3 + r],
            recv_sem=rsem.at[i * 3 + r], device_id=(rx, ry, c), device_id_type=MESH))
            for i in range(n) for r, (rx, ry) in enumerate(rels)]

    return Rider(cs16, [jax.ShapeDtypeStruct((3,) + a.shape[1:], BF16) for a in cs16], [3 * n, 3 * n], ops)


def chip_sum(name, pc_arr, l, cs32, got, prev):
    _, hr, cols = cs32.shape
    tr = 256
    nblk = hr // tr

    def body(pc_ref, o_ref, g_ref, *rest):
        rest[-1][0] = ((o_ref[0] + g_ref[0].astype(F32)) + g_ref[1].astype(F32)) + g_ref[2].astype(F32)

    return pl.pallas_call(
        body, name=name,
        grid_spec=pltpu.PrefetchScalarGridSpec(
            num_scalar_prefetch=1, grid=(nblk,),
            in_specs=[pl.BlockSpec((1, tr, cols), lambda i, pc: (pc[0], i, 0)),
                      pl.BlockSpec((3, tr, cols), lambda i, pc: (0, i, 0))] + ([] if prev is None else [ANY]),
            out_specs=pl.BlockSpec((1, tr, cols), lambda i, pc: (l, pc[1] * nblk + i, 0))),
        out_shape=jax.ShapeDtypeStruct((DEPTH, 2 * hr, cols), F32),
        input_output_aliases={} if prev is None else {3: 0},
        compiler_params=_cp("arbitrary"),
    )(*((pc_arr, cs32, got) + (() if prev is None else (prev,))))


def grad_pair_share(halves):
    n_w = len(halves)
    n = n_w * DEPTH

    def body(*refs):
        bufs = refs[n_w:2 * n_w]
        ssem, rsem = refs[2 * n_w:]
        x, y, c, _, _ = _place()
        cps = []
        for t in range(n_w):
            for l in range(DEPTH):
                mine = bufs[t].at[l, _half(c, bufs[t].shape[1])]
                cp = pltpu.make_async_remote_copy(src_ref=mine, dst_ref=mine, send_sem=ssem.at[t * DEPTH + l],
                                                  recv_sem=rsem.at[t * DEPTH + l], device_id=(x, y, 1 - c),
                                                  device_id_type=MESH)
                cp.start()
                cps.append(cp)
        for t in range(n_w):
            for l in range(DEPTH):
                theirs = bufs[t].at[l, _half(1 - c, bufs[t].shape[1])]
                pltpu.make_async_remote_copy(src_ref=theirs, dst_ref=theirs, send_sem=ssem.at[t * DEPTH + l],
                                             recv_sem=rsem.at[t * DEPTH + l], device_id=(x, y, 1 - c),
                                             device_id_type=MESH).wait_recv()
        for cp in cps:
            cp.wait_send()

    return pl.pallas_call(
        body, name="grad_pair_share", in_specs=[ANY] * n_w, out_specs=[ANY] * n_w,
        out_shape=[jax.ShapeDtypeStruct(a.shape, F32) for a in halves],
        input_output_aliases={t: t for t in range(n_w)},
        scratch_shapes=[pltpu.SemaphoreType.DMA((n,)), pltpu.SemaphoreType.DMA((n,))],
    )(*halves)


SMALL_ROWS = 240


def small_allreduce(v):
    def body(v_ref, o_ref, buf, ssem, rsem):
        x, y, c, _, _ = _place()
        me = 4 * x + 2 * y + c
        buf[me] = v_ref[...]
        for d in range(N_DEV):
            @pl.when(me != d)
            def _():
                pltpu.make_async_remote_copy(
                    src_ref=v_ref, dst_ref=buf.at[me], send_sem=ssem.at[d], recv_sem=rsem.at[me],
                    device_id=(d // 4, (d // 2) % 2, d % 2), device_id_type=MESH).start()
        for d in range(N_DEV):
            @pl.when(me != d)
            def _():
                cp = pltpu.make_async_remote_copy(
                    src_ref=v_ref, dst_ref=buf.at[d], send_sem=ssem.at[d], recv_sem=rsem.at[d],
                    device_id=(d // 4, (d // 2) % 2, d % 2), device_id_type=MESH)
                cp.wait_recv()
                cp.wait_send()
        acc = buf[0]
        for d in range(1, N_DEV):
            acc = acc + buf[d]
        o_ref[...] = acc

    vm = pl.BlockSpec(memory_space=pltpu.VMEM)
    return pl.pallas_call(
        body, name="small_allreduce", in_specs=[vm], out_specs=vm,
        out_shape=jax.ShapeDtypeStruct(v.shape, F32),
        scratch_shapes=[pltpu.VMEM((N_DEV,) + v.shape, F32), pltpu.SemaphoreType.DMA((N_DEV,)),
                        pltpu.SemaphoreType.DMA((N_DEV,))],
    )(v)


class GradReducer:
    EARLY, LATE = (1, 2, 3), (0,)

    def __init__(self):
        self.c_arr = lax.axis_index("c").astype(jnp.int32).reshape(1)
        self.pc_arr = jnp.stack([2 * lax.axis_index("x") + lax.axis_index("y"), lax.axis_index("c")]).astype(jnp.int32)
        self.fresh = {}
        self.paired = {}
        self.acc = [None] * 4

    def push(self, group, l, gs):
        self.fresh[group] = (l, list(gs))

    def pair_rider(self, group):
        return pair_exchange_rider(self.fresh[group][1]) if group in self.fresh else None

    def pair_done(self, group, got):
        l, gs = self.fresh.pop(group)
        self.paired[group] = (l, [pair_add(f"pair_add{t}", self.c_arr, g, r) for t, g, r in zip(group, gs, got)])

    def chip_rider(self, group):
        return chip_exchange_rider([s16 for _, s16 in self.paired[group][1]]) if group in self.paired else None

    def chip_done(self, group, arrived):
        l, cs = self.paired.pop(group)
        for t, (s32, _), got in zip(group, cs, arrived):
            self.acc[t] = chip_sum(f"chip_sum{t}", self.pc_arr, l, s32, got, self.acc[t])

    def finish(self):
        self.pair_done(self.LATE, run_alone("grad_pair_exchange", self.pair_rider(self.LATE)))
        self.chip_done(self.LATE, run_alone("grad_chip_exchange", self.chip_rider(self.LATE)))
        return grad_pair_share(self.acc)


def adamw(name, w, g, m, v):
    shape = w.shape
    cols = shape[-1]
    flat = [t.reshape(-1, cols) for t in (w, g, m, v)]
    rows = flat[0].shape[0]
    tr = 128 if rows % 128 == 0 else rows

    def body(w_ref, g_ref, m_ref, v_ref, d_ref, mo_ref, vo_ref):
        gv = g_ref[...]
        mn = ADAM_B1 * m_ref[...] + (1.0 - ADAM_B1) * gv
        vn = ADAM_B2 * v_ref[...] + (1.0 - ADAM_B2) * jnp.square(gv)
        m_hat = mn / (1.0 - ADAM_B1 ** ADAM_STEP)
        v_hat = vn / (1.0 - ADAM_B2 ** ADAM_STEP)
        d_ref[...] = -ADAM_LR * (m_hat / (jnp.sqrt(v_hat) + ADAM_EPS) + ADAM_WD * w_ref[...])
        mo_ref[...] = mn
        vo_ref[...] = vn

    blk = pl.BlockSpec((tr, cols), lambda i: (i, 0))
    out = jax.ShapeDtypeStruct((rows, cols), F32)
    d, mo, vo = pl.pallas_call(
        body, name=name, grid=(rows // tr,), in_specs=[blk] * 4, out_specs=(blk, blk, blk),
        out_shape=(out, out, out), compiler_params=_cp("parallel"),
    )(*flat)
    return d.reshape(shape), mo.reshape(shape), vo.reshape(shape)


SMALL_NAMES = ("norm_mix", "norm_mlp", "hg_lb_fwd", "hg_lb_bwd", "hg_norm", "ret_norm", "q_norm", "k_norm", "rel_bias")


def _pack_small(d):
    flat = jnp.concatenate([d[k].reshape(-1) for k in SMALL_NAMES])
    return jnp.pad(flat, (0, SMALL_ROWS * 128 - flat.shape[0])).reshape(SMALL_ROWS, 128)


def _unpack_small(v, like):
    flat = v.reshape(-1)
    out, off = {}, 0
    for k in SMALL_NAMES:
        n = like[k].size
        out[k] = flat[off:off + n].reshape(like[k].shape)
        off += n
    return out


def kernel(x, w_in, w_out, w_up, w_down, norm_mix, norm_mlp, hg_lb_fwd, hg_lb_bwd, hg_norm, ret_norm, q_norm, k_norm, rel_bias, loss_target, m_w_in, m_w_out, m_w_up, m_w_down, m_norm_mix, m_norm_mlp, m_hg_lb_fwd, m_hg_lb_bwd, m_hg_norm, m_ret_norm, m_q_norm, m_k_norm, m_rel_bias, v_w_in, v_w_out, v_w_up, v_w_down, v_norm_mix, v_norm_mlp, v_hg_lb_fwd, v_hg_lb_bwd, v_hg_norm, v_ret_norm, v_q_norm, v_k_norm, v_rel_bias):
    big_w = (w_in, w_out, w_up, w_down)
    big_m = (m_w_in, m_w_out, m_w_up, m_w_down)
    big_v = (v_w_in, v_w_out, v_w_up, v_w_down)
    small_w = dict(zip(SMALL_NAMES, (norm_mix, norm_mlp, hg_lb_fwd, hg_lb_bwd, hg_norm, ret_norm, q_norm, k_norm, rel_bias)))
    small_m = dict(zip(SMALL_NAMES, (m_norm_mix, m_norm_mlp, m_hg_lb_fwd, m_hg_lb_bwd, m_hg_norm, m_ret_norm, m_q_norm,
                                     m_k_norm, m_rel_bias)))
    small_v = dict(zip(SMALL_NAMES, (v_norm_mix, v_norm_mlp, v_hg_lb_fwd, v_hg_lb_bwd, v_hg_norm, v_ret_norm, v_q_norm,
                                     v_k_norm, v_rel_bias)))

    gatherer = WeightGatherer(big_w)
    loss_part, dx, grads_big, sg, (lbf_vjp, lbb_vjp) = local_step(x[0], loss_target[0], gatherer.w, small_w,
                                                                  GradReducer(), gatherer)
    loss = lax.psum(loss_part, ("x", "y", "c"))

    sg = dict(sg)
    sg["hg_lb_fwd"], sg["hg_lb_bwd"] = sg.pop("lbf"), sg.pop("lbb")
    tot = _unpack_small(small_allreduce(_pack_small(sg)), small_w)
    tot["hg_lb_fwd"] = lbf_vjp(tot["hg_lb_fwd"])[0]
    tot["hg_lb_bwd"] = lbb_vjp(tot["hg_lb_bwd"])[0]
    grads_small = [tot[k] for k in SMALL_NAMES]

    upd_big = [adamw(f"adamw_big{t}", big_w[t], grads_big[t], big_m[t], big_v[t]) for t in range(4)]
    d_s, m_s, v_s = adamw("adamw_small", _pack_small(small_w), _pack_small(tot), _pack_small(small_m), _pack_small(small_v))
    upd_small = [_unpack_small(t, small_w) for t in (d_s, m_s, v_s)]

    outs = [loss, dx[None]] + list(grads_big) + grads_small
    for j in range(3):
        outs += [u[j] for u in upd_big] + [upd_small[j][k] for k in SMALL_NAMES]
    return tuple(outs)
```

```python
import functools
import math

import jax
import jax.numpy as jnp
from jax import lax
from jax.experimental import pallas as pl
from jax.experimental.pallas import tpu as pltpu

F32 = jnp.float32
BF16 = jnp.bfloat16
EPS = 1e-6

D_MODEL = 2048
DEPTH = 4
HG_HEADS = 6
HG_W = 768
RET_HEADS = 6
RET_DK = 64
RET_W = 768
RET_QK_W = RET_HEADS * RET_DK
RET_CHUNK = 128
ROPE_BASE = 10000.0
DIL_SLOTS = 4
DIL_HD = 128
DIL_GROUPS = ((128, 1), (512, 4), (2048, 16))
DIL_HALF = 64
DIL_W = 512
D_FF = 4 * D_MODEL
IN_W = 10752
REL_BUCKETS = 32
REL_MAX_DIST = 1024

OFF_HG_Q, OFF_HG_V, OFF_HG_ZF, OFF_HG_ZB, OFF_HG_GATE = 0, 768, 1536, 2304, 3072
OFF_RET_Q, OFF_RET_K, OFF_RET_V, OFF_RET_GATE = 3840, 4224, 4608, 5376
OFF_DIL = 6144

N_CHIPS = 4
N_DEV = 8
IN_SHARD = IN_W // N_CHIPS
FF_SHARD = D_FF // N_CHIPS

ADAM_LR, ADAM_B1, ADAM_B2, ADAM_EPS, ADAM_WD, ADAM_STEP = 0.001, 0.9, 0.999, 1e-08, 0.01, 10

VMEM_LIMIT = 56 * 1024 * 1024
HG_T = 512
HG_C = 64
RET_T = 256
DIL_TQ = 256
NEG = -1e30

NN = (((1,), (0,)), ((), ()))
NT = (((1,), (1,)), ((), ()))
TN = (((0,), (0,)), ((), ()))
MESH = pl.DeviceIdType.MESH
ANY = pl.BlockSpec(memory_space=pl.ANY)


def _cp(*sem):
    return pltpu.CompilerParams(dimension_semantics=sem, vmem_limit_bytes=VMEM_LIMIT)


def _mxu(a, b, dn):
    return lax.dot_general(a.astype(BF16), b.astype(BF16), dn, preferred_element_type=F32)


@jax.custom_vjp
def dot_nn(a, b):
    return _mxu(a, b, NN)


dot_nn.defvjp(lambda a, b: (_mxu(a, b, NN), (a, b)),
              lambda r, g: (_mxu(g, r[1], NT), _mxu(r[0], g, TN)))


@jax.custom_vjp
def dot_nt(a, b):
    return _mxu(a, b, NT)


dot_nt.defvjp(lambda a, b: (_mxu(a, b, NT), (a, b)),
              lambda r, g: (_mxu(g, r[1], NN), _mxu(g, r[0], TN)))


@jax.custom_vjp
def dot_tn(a, b):
    return _mxu(a, b, TN)


dot_tn.defvjp(lambda a, b: (_mxu(a, b, TN), (a, b)),
              lambda r, g: (_mxu(r[1], g, NT), _mxu(r[0], g, NN)))


def _split3(v):
    hi = v.astype(BF16)
    r1 = v - hi.astype(F32)
    mid = r1.astype(BF16)
    lo = (r1 - mid.astype(F32)).astype(BF16)
    return hi, mid, lo


def _exact_mask_dot(m, v, dn):
    mb = m.astype(BF16)
    hi, mid, lo = _split3(v)
    f = lambda p: lax.dot_general(mb, p, dn, preferred_element_type=F32)
    return (f(lo) + f(mid)) + f(hi)


@jax.custom_vjp
def cumdot(m, v):
    return _exact_mask_dot(m, v, NN)


cumdot.defvjp(lambda m, v: (_exact_mask_dot(m, v, NN), m),
              lambda m, g: (jnp.zeros_like(m), _exact_mask_dot(m, g, TN)))


def _sigmoid(z):
    return 1.0 / (1.0 + jnp.exp(-z))


def _head_rms(t, g):
    return t * lax.rsqrt(jnp.mean(t * t, axis=-1, keepdims=True) + EPS) * g


class Rider:
    def __init__(self, arrays, out_shapes, sems, ops):
        self.arrays, self.sems, self.ops = list(arrays), list(sems), ops
        self.in_place = out_shapes is None
        self.out_shapes = [jax.ShapeDtypeStruct(a.shape, a.dtype) for a in arrays] if self.in_place else list(out_shapes)

    def aliases(self, n_in, n_out):
        return {n_in + i: n_out + i for i in range(len(self.arrays))} if self.in_place else {}

    def start(self, ins, outs, sems):
        for send, _ in self.ops(ins, outs, *sems):
            send.start()

    def finish(self, ins, outs, sems):
        cps = self.ops(ins, outs, *sems)
        for _, arrive in cps:
            arrive.wait_recv()
        for send, _ in cps:
            send.wait_send()

    def __add__(self, other):
        assert self.in_place and other.in_place
        na, sa = len(self.arrays), len(self.sems)
        ops = lambda ins, outs, *sems: (self.ops(ins[:na], outs[:na], *sems[:sa])
                                        + other.ops(ins[na:], outs[na:], *sems[sa:]))
        return Rider(self.arrays + other.arrays, None, self.sems + other.sems, ops)


def _hosted(name, body, rider, *, grid, in_specs, out_specs, out_shape, scratch_shapes, sem, operands):
    n_in, n_out, n_scr = len(in_specs), len(out_specs), len(scratch_shapes)
    r_in = len(rider.arrays) if rider else 0
    r_out = len(rider.out_shapes) if rider else 0
    last = tuple(g - 1 for g in grid)

    def kernel_body(*refs):
        ins, refs = refs[:n_in], refs[n_in:]
        rins, refs = refs[:r_in], refs[r_in:]
        outs, refs = refs[:n_out], refs[n_out:]
        routs, refs = refs[:r_out], refs[r_out:]
        scr, rsems = refs[:n_scr], refs[n_scr:]
        if rider:
            ids = [pl.program_id(d) for d in range(len(grid))]
            first = functools.reduce(lambda p, q: p & q, [i == 0 for i in ids])
            done = functools.reduce(lambda p, q: p & q, [i == e for i, e in zip(ids, last)])
            pl.when(first)(lambda: rider.start(rins, routs, rsems))
        body(ins, outs, scr)
        if rider:
            pl.when(done)(lambda: rider.finish(rins, routs, rsems))

    res = pl.pallas_call(
        kernel_body, name=name, grid=grid,
        in_specs=list(in_specs) + [ANY] * r_in,
        out_specs=list(out_specs) + [ANY] * r_out,
        out_shape=list(out_shape) + (rider.out_shapes if rider else []),
        scratch_shapes=list(scratch_shapes) + ([pltpu.SemaphoreType.DMA((n,)) for n in rider.sems] if rider else []),
        input_output_aliases=rider.aliases(n_in, n_out) if rider else {},
        compiler_params=_cp(*(("arbitrary",) * len(grid) if rider else sem)),
    )(*operands, *(rider.arrays if rider else []))
    return res[:n_out], res[n_out:]


def _mm(name, a, b, *, mode, grid, a_spec, b_spec, tm, tn, extras=(), extra_specs=(), epi, out_shape, out_specs,
        rider=None):
    nk = grid[2]
    single = not isinstance(out_shape, (tuple, list))
    if single:
        out_shape, out_specs = [out_shape], [out_specs]

    def body(ins, outs, scr):
        a_ref, b_ref, ex = ins[0], ins[1], ins[2:]
        product = lambda: _mxu(a_ref[...], b_ref[...], {"nn": NN, "nt": NT, "tn": TN}[mode])
        if nk == 1:
            epi(product(), ex, outs)
            return
        acc = scr[0]
        k = pl.program_id(2)

        @pl.when(k == 0)
        def _():
            acc[...] = jnp.zeros_like(acc)

        acc[...] += product()

        @pl.when(k == nk - 1)
        def _():
            epi(acc[...], ex, outs)

    outs, carried = _hosted(name, body, rider, grid=grid, in_specs=[a_spec, b_spec, *extra_specs],
                            out_specs=out_specs, out_shape=out_shape,
                            scratch_shapes=[] if nk == 1 else [pltpu.VMEM((tm, tn), F32)],
                            sem=("parallel", "parallel", "arbitrary"), operands=(a, b, *extras))
    res = outs[0] if single else tuple(outs)
    return (res, carried) if rider else res


def _epi_store(acc, ex, outs):
    outs[0][...] = acc.astype(outs[0].dtype)


def _epi_residual(acc, ex, outs):
    outs[0][...] = ex[0][...] + acc


def _epi_up(acc, ex, outs):
    outs[0][...] = acc
    outs[1][...] = jnp.square(jnp.maximum(acc, 0.0)).astype(BF16)


def _epi_dact(acc, ex, outs):
    outs[0][...] = (acc * (2.0 * jnp.maximum(ex[0][...], 0.0))).astype(BF16)


def _ij(i, j, k):
    return (i, j)


def proj_in(xn, win_g, rider=None):
    S = xn.shape[0]
    tm, tn = 512, IN_SHARD
    return _mm("proj_in", xn, win_g, mode="nn", grid=(IN_W // tn, S // tm, 1), tm=tm, tn=tn,
               a_spec=pl.BlockSpec((tm, D_MODEL), lambda j, i, k: (i, 0)),
               b_spec=pl.BlockSpec((None, D_MODEL, tn), lambda j, i, k: (j, 0, 0)),
               epi=_epi_store, out_shape=jax.ShapeDtypeStruct((S, IN_W), F32),
               out_specs=pl.BlockSpec((tm, tn), lambda j, i, k: (i, j)), rider=rider)


def proj_out(y, wout_g, x):
    S = y.shape[0]
    tm, tn = 1024, 1024
    return _mm("proj_out", y, wout_g, mode="nn", grid=(S // tm, D_MODEL // tn, 1), tm=tm, tn=tn,
               a_spec=pl.BlockSpec((tm, D_MODEL), lambda i, j, k: (i, 0)),
               b_spec=pl.BlockSpec((D_MODEL, tn), lambda i, j, k: (0, j)),
               extras=(x,), extra_specs=(pl.BlockSpec((tm, tn), _ij),),
               epi=_epi_residual, out_shape=jax.ShapeDtypeStruct((S, D_MODEL), F32),
               out_specs=pl.BlockSpec((tm, tn), _ij))


def proj_up(hm, wup_g, rider=None):
    S = hm.shape[0]
    tm, tn = 1024, 1024
    return _mm("proj_up", hm, wup_g, mode="nn", grid=(S // tm, D_FF // tn, 1), tm=tm, tn=tn,
               a_spec=pl.BlockSpec((tm, D_MODEL), lambda i, j, k: (i, 0)),
               b_spec=pl.BlockSpec((None, D_MODEL, tn), lambda i, j, k: (j // 2, 0, j % 2)),
               epi=_epi_up,
               out_shape=(jax.ShapeDtypeStruct((S, D_FF), F32), jax.ShapeDtypeStruct((S, D_FF), BF16)),
               out_specs=(pl.BlockSpec((tm, tn), _ij), pl.BlockSpec((tm, tn), _ij)), rider=rider)


def proj_down(a, wdown_g, x, rider=None):
    S = a.shape[0]
    tm, tn, tk = 1024, 1024, 2048
    return _mm("proj_down", a, wdown_g, mode="nn", grid=(S // tm, D_MODEL // tn, D_FF // tk), tm=tm, tn=tn,
               a_spec=pl.BlockSpec((tm, tk), lambda i, j, k: (i, k)),
               b_spec=pl.BlockSpec((tk, tn), lambda i, j, k: (k, j)),
               extras=(x,), extra_specs=(pl.BlockSpec((tm, tn), _ij),),
               epi=_epi_residual, out_shape=jax.ShapeDtypeStruct((S, D_MODEL), F32),
               out_specs=pl.BlockSpec((tm, tn), _ij), rider=rider)


def bwd_down_act(dx, wdown_g, u, rider=None):
    S = dx.shape[0]
    tm, tn = 1024, 1024
    return _mm("bwd_down_act", dx, wdown_g, mode="nt", grid=(S // tm, D_FF // tn, 1), tm=tm, tn=tn,
               a_spec=pl.BlockSpec((tm, D_MODEL), lambda i, j, k: (i, 0)),
               b_spec=pl.BlockSpec((tn, D_MODEL), lambda i, j, k: (j, 0)),
               extras=(u,), extra_specs=(pl.BlockSpec((tm, tn), _ij),),
               epi=_epi_dact, out_shape=jax.ShapeDtypeStruct((S, D_FF), BF16),
               out_specs=pl.BlockSpec((tm, tn), _ij), rider=rider)


def bwd_up(du, wup_g):
    S = du.shape[0]
    tm, tn, tk = 1024, 1024, FF_SHARD
    return _mm("bwd_up", du, wup_g, mode="nt", grid=(S // tm, D_MODEL // tn, D_FF // tk), tm=tm, tn=tn,
               a_spec=pl.BlockSpec((tm, tk), lambda i, j, k: (i, k)),
               b_spec=pl.BlockSpec((None, tn, tk), lambda i, j, k: (k, j, 0)),
               epi=_epi_store, out_shape=jax.ShapeDtypeStruct((S, D_MODEL), F32),
               out_specs=pl.BlockSpec((tm, tn), _ij))


def bwd_out(dx, wout_g):
    S = dx.shape[0]
    tm, tn = 1024, 1024
    return _mm("bwd_out", dx, wout_g, mode="nt", grid=(S // tm, D_MODEL // tn, 1), tm=tm, tn=tn,
               a_spec=pl.BlockSpec((tm, D_MODEL), lambda i, j, k: (i, 0)),
               b_spec=pl.BlockSpec((tn, D_MODEL), lambda i, j, k: (j, 0)),
               epi=_epi_store, out_shape=jax.ShapeDtypeStruct((S, D_MODEL), F32),
               out_specs=pl.BlockSpec((tm, tn), _ij))


def bwd_in(dh, win_g, rider=None):
    S = dh.shape[0]
    tm, tn, tk = 1024, 1024, IN_SHARD
    return _mm("bwd_in", dh, win_g, mode="nt", grid=(S // tm, D_MODEL // tn, IN_W // tk), tm=tm, tn=tn,
               a_spec=pl.BlockSpec((tm, tk), lambda i, j, k: (i, k)),
               b_spec=pl.BlockSpec((None, tn, tk), lambda i, j, k: (k, j, 0)),
               epi=_epi_store, out_shape=jax.ShapeDtypeStruct((S, D_MODEL), F32),
               out_specs=pl.BlockSpec((tm, tn), _ij), rider=rider)


def wgrad(name, a, g, *, m, n, n_shard):
    S = a.shape[0]
    tm, tn, tk = (512, IN_SHARD, 1024) if n_shard == IN_SHARD else (1024, 1024, 1024)
    per = n_shard // tn
    if n_shard == n:
        out_shape = jax.ShapeDtypeStruct((m, n), F32)
        out_spec = pl.BlockSpec((tm, tn), _ij)
    else:
        out_shape = jax.ShapeDtypeStruct((N_CHIPS, m, n_shard), F32)
        out_spec = pl.BlockSpec((None, tm, tn), lambda i, j, k: (j // per, i, j % per))
    return _mm(name, a, g, mode="tn", grid=(m // tm, n // tn, S // tk), tm=tm, tn=tn,
               a_spec=pl.BlockSpec((tk, tm), lambda i, j, k: (k, i)),
               b_spec=pl.BlockSpec((tk, tn), lambda i, j, k: (k, j)),
               epi=_epi_store, out_shape=out_shape, out_specs=out_spec)


NORM_T = 512


def rmsnorm_fwd(x, g):
    S = x.shape[0]

    def body(x_ref, g_ref, o_ref):
        xv = x_ref[...]
        r = lax.rsqrt(jnp.mean(xv * xv, axis=-1, keepdims=True) + EPS)
        o_ref[...] = ((xv * r) * g_ref[...]).astype(BF16)

    return pl.pallas_call(
        body, name="rmsnorm_fwd", grid=(S // NORM_T,),
        in_specs=[pl.BlockSpec((NORM_T, D_MODEL), lambda i: (i, 0)), pl.BlockSpec((1, D_MODEL), lambda i: (0, 0))],
        out_specs=pl.BlockSpec((NORM_T, D_MODEL), lambda i: (i, 0)),
        out_shape=jax.ShapeDtypeStruct((S, D_MODEL), BF16), compiler_params=_cp("parallel"),
    )(x, g)


def rmsnorm_bwd(x, g, dxn, dres):
    S = x.shape[0]

    def body(x_ref, g_ref, dxn_ref, dres_ref, dx_ref, dg_ref):
        @pl.when(pl.program_id(0) == 0)
        def _():
            dg_ref[...] = jnp.zeros_like(dg_ref)

        xv, gv, d = x_ref[...], g_ref[...], dxn_ref[...]
        r = lax.rsqrt(jnp.mean(xv * xv, axis=-1, keepdims=True) + EPS)
        gd = gv * d
        dx_ref[...] = dres_ref[...] + r * gd - xv * ((r * r * r) * jnp.mean(xv * gd, axis=-1, keepdims=True))
        dg_ref[...] += jnp.sum(d * (xv * r), axis=0, keepdims=True)

    row = pl.BlockSpec((NORM_T, D_MODEL), lambda i: (i, 0))
    vec = pl.BlockSpec((1, D_MODEL), lambda i: (0, 0))
    return pl.pallas_call(
        body, name="rmsnorm_bwd", grid=(S // NORM_T,),
        in_specs=[row, vec, row, row], out_specs=(row, vec),
        out_shape=(jax.ShapeDtypeStruct((S, D_MODEL), F32), jax.ShapeDtypeStruct((1, D_MODEL), F32)),
        compiler_params=_cp("arbitrary"),
    )(x, g, dxn, dres)


def loss_head(y, tgt):
    S = y.shape[0]

    def body(y_ref, t_ref, dy_ref, l_ref):
        @pl.when(pl.program_id(0) == 0)
        def _():
            l_ref[...] = jnp.zeros_like(l_ref)

        e = y_ref[...] - t_ref[...]
        dy_ref[...] = e * (1.0 / D_MODEL)
        l_ref[...] += jnp.sum(e * e) * (0.5 / D_MODEL)

    row = pl.BlockSpec((NORM_T, D_MODEL), lambda i: (i, 0))
    return pl.pallas_call(
        body, name="loss_head", grid=(S // NORM_T,),
        in_specs=[row, row], out_specs=(row, pl.BlockSpec((1, 128), lambda i: (0, 0))),
        out_shape=(jax.ShapeDtypeStruct((S, D_MODEL), F32), jax.ShapeDtypeStruct((1, 128), F32)),
        compiler_params=_cp("arbitrary"),
    )(y, tgt)


def _hg_block(qs, vs, zs, lb, sT, reverse):
    n = len(qs)
    row = lax.broadcasted_iota(jnp.int32, (HG_C, HG_C), 0)
    col = lax.broadcasted_iota(jnp.int32, (HG_C, HG_C), 1)
    tri = (row <= col) if reverse else (row >= col)
    m = tri.astype(F32)
    rsel = lax.broadcasted_iota(jnp.int32, (HG_C, 128), 0)
    ref_rows = ((rsel >= HG_C // 2) if reverse else (rsel <= HG_C // 2)).astype(F32)
    att, qdec, upd, keep = [None] * n, [None] * n, [None] * n, [None] * n
    for c in range(n):
        f = lb + (1.0 - lb) * _sigmoid(zs[c])
        kc = 1.0 - f
        lc = jnp.log(f)
        b = cumdot(m, lc)
        btot = jnp.sum(lc, axis=0, keepdims=True)
        bref = lax.stop_gradient(jnp.sum(lc * ref_rows, axis=0, keepdims=True))
        qe = qs[c] * jnp.exp(jnp.minimum(b - bref, 80.0))
        ke = kc * jnp.exp(jnp.minimum(bref - b, 80.0))
        att[c] = jnp.where(tri, dot_nt(qe, ke), 0.0)
        qdec[c] = qs[c] * jnp.exp(b)
        upd[c] = dot_tn(vs[c], kc * jnp.exp(btot - b))
        keep[c] = jnp.exp(btot)
    states = [None] * n
    for c in (range(n - 1, -1, -1) if reverse else range(n)):
        states[c] = sT
        sT = sT * keep[c] + upd[c]
    outs = [dot_nn(att[c], vs[c]) + dot_nt(qdec[c], states[c]) for c in range(n)]
    return outs, sT


def _chunks(ref, c, n):
    return [ref[i * c:(i + 1) * c, :] for i in range(n)]


def hg_scan_fwd(h, lbf, lbb):
    S = h.shape[0]
    nb = S // HG_T
    n = HG_T // HG_C

    def body(ins, outs, scr):
        qf, vf, zf, qb, vb, zb, lbf_ref, lbb_ref = ins
        of_ref, ob_ref, sf_ref, sb_ref = outs
        stf, stb = scr

        @pl.when(pl.program_id(1) == 0)
        def _():
            stf[...] = jnp.zeros_like(stf)
            stb[...] = jnp.zeros_like(stb)

        for (q, v, z, lb_ref, o_ref, s_ref, st, rev) in ((qf, vf, zf, lbf_ref, of_ref, sf_ref, stf, False),
                                                         (qb, vb, zb, lbb_ref, ob_ref, sb_ref, stb, True)):
            s_ref[0, 0] = st[...]
            outs, s_new = _hg_block(_chunks(q, HG_C, n), _chunks(v, HG_C, n), _chunks(z, HG_C, n),
                                    lb_ref[...], st[...], rev)
            for c in range(n):
                o_ref[c * HG_C:(c + 1) * HG_C, :] = outs[c]
            st[...] = s_new

    def col(off, rev):
        return pl.BlockSpec((HG_T, 128), (lambda hh, t: (nb - 1 - t, off // 128 + hh)) if rev
                            else (lambda hh, t: (t, off // 128 + hh)))

    lb_spec = pl.BlockSpec((1, 128), lambda hh, t: (0, hh))
    st_f = pl.BlockSpec((1, 1, 128, 128), lambda hh, t: (hh, t, 0, 0))
    st_b = pl.BlockSpec((1, 1, 128, 128), lambda hh, t: (hh, nb - 1 - t, 0, 0))
    outs, carried = _hosted(
        "hg_scan_fwd", body, None, grid=(HG_HEADS, nb),
        in_specs=[col(OFF_HG_Q, False), col(OFF_HG_V, False), col(OFF_HG_ZF, False),
                  col(OFF_HG_Q, True), col(OFF_HG_V, True), col(OFF_HG_ZB, True), lb_spec, lb_spec],
        out_specs=[col(0, False), col(0, True), st_f, st_b],
        out_shape=[jax.ShapeDtypeStruct((S, HG_W), F32), jax.ShapeDtypeStruct((S, HG_W), F32),
                   jax.ShapeDtypeStruct((HG_HEADS, nb, 128, 128), F32),
                   jax.ShapeDtypeStruct((HG_HEADS, nb, 128, 128), F32)],
        scratch_shapes=[pltpu.VMEM((128, 128), F32), pltpu.VMEM((128, 128), F32)],
        sem=("arbitrary", "arbitrary"), operands=(h, h, h, h, h, h, lbf, lbb))
    return tuple(outs)


def hg_scan_bwd(h, lbf, lbb, do, sf, sb, rider=None):
    S = h.shape[0]
    nb = S // HG_T
    n = HG_T // HG_C

    def body(ins, outs, scr):
        qf, vf, zf, dof, sfin, qb, vb, zb, dob, sbin, lbf_ref, lbb_ref = ins
        dqf, dvf, dzf, dqb, dvb, dzb, dlbf, dlbb = outs
        dsf, dsb = scr

        @pl.when(pl.program_id(1) == 0)
        def _():
            for r in (dsf, dsb, dlbf, dlbb):
                r[...] = jnp.zeros_like(r)

        for (q, v, z, dor, sin, lb_ref, dq, dv, dz, dlb, ds, rev) in (
                (qf, vf, zf, dof, sfin, lbf_ref, dqf, dvf, dzf, dlbf, dsf, False),
                (qb, vb, zb, dob, sbin, lbb_ref, dqb, dvb, dzb, dlbb, dsb, True)):
            fn = functools.partial(_hg_block, reverse=rev)
            _, vjp = jax.vjp(fn, _chunks(q, HG_C, n), _chunks(v, HG_C, n), _chunks(z, HG_C, n), lb_ref[...], sin[0, 0])
            dqs, dvs, dzs, dlb_v, ds_in = vjp((_chunks(dor, HG_C, n), ds[...]))
            for c in range(n):
                sl = slice(c * HG_C, (c + 1) * HG_C)
                dq[sl, :] = dqs[c]
                dv[sl, :] = dvs[c]
                dz[sl, :] = dzs[c]
            dlb[...] += dlb_v
            ds[...] = ds_in

    def col(off, fwd_scan):
        return pl.BlockSpec((HG_T, 128), (lambda hh, t: (nb - 1 - t, off // 128 + hh)) if fwd_scan
                            else (lambda hh, t: (t, off // 128 + hh)))

    lb_spec = pl.BlockSpec((1, 128), lambda hh, t: (0, hh))
    st_f = pl.BlockSpec((1, 1, 128, 128), lambda hh, t: (hh, nb - 1 - t, 0, 0))
    st_b = pl.BlockSpec((1, 1, 128, 128), lambda hh, t: (hh, t, 0, 0))
    full = jax.ShapeDtypeStruct((S, HG_W), F32)
    vec = jax.ShapeDtypeStruct((1, HG_W), F32)
    outs, carried = _hosted(
        "hg_scan_bwd", body, rider, grid=(HG_HEADS, nb),
        in_specs=[col(OFF_HG_Q, True), col(OFF_HG_V, True), col(OFF_HG_ZF, True), col(0, True), st_f,
                  col(OFF_HG_Q, False), col(OFF_HG_V, False), col(OFF_HG_ZB, False), col(0, False), st_b,
                  lb_spec, lb_spec],
        out_specs=[col(0, True), col(0, True), col(0, True), col(0, False), col(0, False), col(0, False),
                   lb_spec, lb_spec],
        out_shape=[full, full, full, full, full, full, vec, vec],
        scratch_shapes=[pltpu.VMEM((128, 128), F32), pltpu.VMEM((128, 128), F32)],
        sem=("arbitrary", "arbitrary"), operands=(h, h, h, do, sf, h, h, h, do, sb, lbf, lbb))
    return (tuple(outs), carried) if rider else tuple(outs)


GN_T = 1024


def _gated_norm(o, gate, g, center):
    if center:
        o = o - jnp.mean(o, axis=-1, keepdims=True)
    o = o * lax.rsqrt(jnp.mean(o * o, axis=-1, keepdims=True) + EPS)
    return (o * g) * (gate * _sigmoid(gate))


def gated_norm_fwd(name, of, ob, h, gate_off, g, center, y, y_off):
    S = of.shape[0]

    def body(of_ref, ob_ref, gate_ref, g_ref, *rest):
        rest[-1][...] = _gated_norm(of_ref[...] + ob_ref[...], gate_ref[...], g_ref[...], center).astype(BF16)

    blk = pl.BlockSpec((GN_T, 128), lambda hh, i: (i, hh))
    return pl.pallas_call(
        body, name=name, grid=(6, S // GN_T),
        in_specs=[blk, blk, pl.BlockSpec((GN_T, 128), lambda hh, i: (i, gate_off // 128 + hh)),
                  pl.BlockSpec((1, 128), lambda hh, i: (0, hh))] + ([] if y is None else [ANY]),
        out_specs=pl.BlockSpec((GN_T, 128), lambda hh, i: (i, y_off // 128 + hh)),
        out_shape=jax.ShapeDtypeStruct((S, D_MODEL), BF16),
        input_output_aliases={} if y is None else {4: 0},
        compiler_params=_cp("parallel", "parallel"),
    )(*((of, ob, h, g) + (() if y is None else (y,))))


def gated_norm_bwd(name, of, ob, h, gate_off, g, dy, dy_off, center):
    S = of.shape[0]

    def body(of_ref, ob_ref, gate_ref, g_ref, dy_ref, do_ref, dgate_ref, dg_ref):
        @pl.when(pl.program_id(1) == 0)
        def _():
            dg_ref[...] = jnp.zeros_like(dg_ref)

        fn = functools.partial(_gated_norm, center=center)
        _, vjp = jax.vjp(fn, of_ref[...] + ob_ref[...], gate_ref[...], g_ref[...])
        do, dgate, dg = vjp(dy_ref[...])
        do_ref[...] = do
        dgate_ref[...] = dgate
        dg_ref[...] += dg

    blk = pl.BlockSpec((GN_T, 128), lambda hh, i: (i, hh))
    vec = pl.BlockSpec((1, 128), lambda hh, i: (0, hh))
    return pl.pallas_call(
        body, name=name, grid=(6, S // GN_T),
        in_specs=[blk, blk, pl.BlockSpec((GN_T, 128), lambda hh, i: (i, gate_off // 128 + hh)), vec,
                  pl.BlockSpec((GN_T, 128), lambda hh, i: (i, dy_off // 128 + hh))],
        out_specs=(blk, blk, vec),
        out_shape=(jax.ShapeDtypeStruct((S, 768), F32), jax.ShapeDtypeStruct((S, 768), F32),
                   jax.ShapeDtypeStruct((1, 768), F32)),
        compiler_params=_cp("arbitrary", "arbitrary"),
    )(of, ob, h, g, dy)


def _ret_consts(S):
    half = RET_DK // 2
    inv = ROPE_BASE ** (-jnp.arange(half, dtype=F32) / half)
    ang = jnp.arange(S, dtype=F32)[:, None] * inv[None, :]
    cos, sin = jnp.cos(ang), jnp.sin(ang)
    cos_t = jnp.tile(jnp.concatenate([cos, cos], axis=1), (1, RET_HEADS))
    sin_t = jnp.tile(jnp.concatenate([-sin, sin], axis=1), (1, RET_HEADS))
    hidx = jnp.arange(RET_HEADS, dtype=F32)
    lg_f = jnp.log1p(-jnp.exp2(-5.0 - hidx))
    C = RET_CHUNK
    idx = jnp.arange(C, dtype=F32)
    rel = idx[:, None] - idx[None, :]

    def one(lg, reverse):
        lgc = lg[:, None]
        decay = jnp.where(rel >= 0, jnp.exp(lgc[:, :, None] * jnp.maximum(rel, 0.0)), 0.0)
        zeta = jnp.exp(lgc * (C - 1 - idx))
        xi = jnp.exp(lgc * (idx + 1))
        if reverse:
            decay = decay[:, ::-1, ::-1]
            zeta, xi = zeta[:, ::-1], xi[:, ::-1]
        wide = lambda t: jnp.repeat(t.T, RET_DK, axis=1)
        gam_w = jnp.broadcast_to(jnp.repeat(jnp.exp(lg * C), 128)[None, :], (8, RET_W))
        return decay, wide(xi), wide(zeta), gam_w

    hm = (jnp.arange(RET_QK_W)[None, :] // RET_DK == jnp.arange(8)[:, None]).astype(F32)
    return (cos_t, sin_t, hm) + one(lg_f, False) + one(lg_f[::-1], True)


def _rope(t, cos, sin_signed):
    lane = lax.broadcasted_iota(jnp.int32, t.shape, 1)
    first = (lane & (RET_DK - 1)) < RET_DK // 2
    partner = jnp.where(first, pltpu.roll(t, RET_QK_W - RET_DK // 2, 1), pltpu.roll(t, RET_DK // 2, 1))
    return t * cos + partner * sin_signed


def _ret_block(qs, ks, vs, st, dec, xi, zeta, gam, hms, reverse):
    n = len(qs)
    heads = range(RET_HEADS)
    qx = [q * xi for q in qs]
    kz = [k * zeta for k in ks]
    sc = [[dot_nt(qs[c] * hms[hh], ks[c]) * dec[hh] for hh in heads] for c in range(n)]
    upd = [[dot_tn(kz[c] * hms[hh], vs[c][hh]) for hh in heads] for c in range(n)]
    st = list(st)
    seen = [None] * n
    for c in (range(n - 1, -1, -1) if reverse else range(n)):
        seen[c] = list(st)
        st = [st[hh] * gam[hh] + upd[c][hh] for hh in heads]
    outs = [[dot_nn(sc[c][hh], vs[c][hh]) + dot_nn(qx[c], seen[c][hh]) for hh in heads] for c in range(n)]
    return outs, st


def _ret_inputs(q_ref, k_ref, v_ref, cos_ref, sin_ref):
    n = RET_T // RET_CHUNK
    qr = _rope(q_ref[...], cos_ref[...], sin_ref[...])
    kr = _rope(k_ref[...], cos_ref[...], sin_ref[...]) * (RET_DK ** -0.5)
    qs = [qr[c * RET_CHUNK:(c + 1) * RET_CHUNK] for c in range(n)]
    ks = [kr[c * RET_CHUNK:(c + 1) * RET_CHUNK] for c in range(n)]
    vs = [[v_ref[c * RET_CHUNK:(c + 1) * RET_CHUNK, hh * 128:(hh + 1) * 128] for hh in range(RET_HEADS)]
          for c in range(n)]
    return qs, ks, vs


def _ret_dir_consts(dec_ref, xi_ref, zeta_ref, gam_ref, hm_ref):
    dec = [dec_ref[hh] for hh in range(RET_HEADS)]
    gam = [gam_ref[0:1, hh * 128:(hh + 1) * 128] for hh in range(RET_HEADS)]
    hms = [hm_ref[hh:hh + 1, :] for hh in range(RET_HEADS)]
    return dec, xi_ref[...], zeta_ref[...], gam, hms


def _ret_rows(nb, rev):
    def rows(width, colblk):
        return pl.BlockSpec((RET_T, width), (lambda t: (nb - 1 - t, colblk)) if rev else (lambda t: (t, colblk)))
    return rows


def _const_spec(shape):
    nd = len(shape)
    return pl.BlockSpec(shape, lambda t: (0,) * nd)


def ret_scan_fwd(h, consts):
    S = h.shape[0]
    nb = S // RET_T
    n = RET_T // RET_CHUNK
    cos_t, sin_t, hm, dec_f, xi_f, zeta_f, gam_f, dec_b, xi_b, zeta_b, gam_b = consts

    def body(ins, outs, scr):
        qf, kf, vf, cf, sf, qb, kb, vb, cb, sb_, hm_ref, decf, xif, zetaf, gamf, decb, xib, zetab, gamb = ins
        of_ref, ob_ref, sfo, sbo = outs
        stf, stb = scr

        @pl.when(pl.program_id(0) == 0)
        def _():
            stf[...] = jnp.zeros_like(stf)
            stb[...] = jnp.zeros_like(stb)

        for (q, k, v, cs, sn, dr, xr, zr, gr, o_ref, so, st, rev) in (
                (qf, kf, vf, cf, sf, decf, xif, zetaf, gamf, of_ref, sfo, stf, False),
                (qb, kb, vb, cb, sb_, decb, xib, zetab, gamb, ob_ref, sbo, stb, True)):
            so[0] = st[...]
            qs, ks, vs = _ret_inputs(q, k, v, cs, sn)
            dec, xi, zeta, gam, hms = _ret_dir_consts(dr, xr, zr, gr, hm_ref)
            st_in = [st[:, hh * 128:(hh + 1) * 128] for hh in range(RET_HEADS)]
            outs, st_new = _ret_block(qs, ks, vs, st_in, dec, xi, zeta, gam, hms, rev)
            for c in range(n):
                for hh in range(RET_HEADS):
                    o_ref[c * RET_CHUNK:(c + 1) * RET_CHUNK, hh * 128:(hh + 1) * 128] = outs[c][hh]
            for hh in range(RET_HEADS):
                st[:, hh * 128:(hh + 1) * 128] = st_new[hh]

    rf, rb = _ret_rows(nb, False), _ret_rows(nb, True)
    cspecs = [_const_spec(a.shape) for a in (hm, dec_f, xi_f, zeta_f, gam_f, dec_b, xi_b, zeta_b, gam_b)]
    st_shape = jax.ShapeDtypeStruct((nb, RET_QK_W, RET_W), F32)
    qc, kc, vc = OFF_RET_Q // RET_QK_W, OFF_RET_K // RET_QK_W, OFF_RET_V // RET_W
    outs, carried = _hosted(
        "ret_scan_fwd", body, None, grid=(nb,),
        in_specs=[rf(RET_QK_W, qc), rf(RET_QK_W, kc), rf(RET_W, vc), rf(RET_QK_W, 0), rf(RET_QK_W, 0),
                  rb(RET_QK_W, qc), rb(RET_QK_W, kc), rb(RET_W, vc), rb(RET_QK_W, 0), rb(RET_QK_W, 0)] + cspecs,
        out_specs=[rf(RET_W, 0), rb(RET_W, 0),
                   pl.BlockSpec((1, RET_QK_W, RET_W), lambda t: (t, 0, 0)),
                   pl.BlockSpec((1, RET_QK_W, RET_W), lambda t: (nb - 1 - t, 0, 0))],
        out_shape=[jax.ShapeDtypeStruct((S, RET_W), F32), jax.ShapeDtypeStruct((S, RET_W), F32), st_shape, st_shape],
        scratch_shapes=[pltpu.VMEM((RET_QK_W, RET_W), F32), pltpu.VMEM((RET_QK_W, RET_W), F32)],
        sem=("arbitrary",),
        operands=(h, h, h, cos_t, sin_t, h, h, h, cos_t, sin_t, hm, dec_f, xi_f, zeta_f, gam_f, dec_b, xi_b, zeta_b,
                  gam_b))
    return tuple(outs)


def ret_scan_bwd(h, consts, do, sf, sb, rider=None):
    S = h.shape[0]
    nb = S // RET_T
    n = RET_T // RET_CHUNK
    cos_t, sin_t, hm, dec_f, xi_f, zeta_f, gam_f, dec_b, xi_b, zeta_b, gam_b = consts

    def body(ins, outs, scr):
        (qf, kf, vf, cf, sf_, dof, sfin, qb, kb, vb, cb, sb_, dob, sbin,
         hm_ref, decf, xif, zetaf, gamf, decb, xib, zetab, gamb) = ins
        dqf, dkf, dvf, dqb, dkb, dvb = outs
        dsf, dsb = scr

        @pl.when(pl.program_id(0) == 0)
        def _():
            dsf[...] = jnp.zeros_like(dsf)
            dsb[...] = jnp.zeros_like(dsb)

        for (q, k, v, cs, sn, dor, sin, dr, xr, zr, gr, dq, dk, dv, ds, rev) in (
                (qf, kf, vf, cf, sf_, dof, sfin, decf, xif, zetaf, gamf, dqf, dkf, dvf, dsf, False),
                (qb, kb, vb, cb, sb_, dob, sbin, decb, xib, zetab, gamb, dqb, dkb, dvb, dsb, True)):
            qs, ks, vs = _ret_inputs(q, k, v, cs, sn)
            dec, xi, zeta, gam, hms = _ret_dir_consts(dr, xr, zr, gr, hm_ref)
            st_in = [sin[0, :, hh * 128:(hh + 1) * 128] for hh in range(RET_HEADS)]
            fn = lambda a, b_, c_, d_: _ret_block(a, b_, c_, d_, dec, xi, zeta, gam, hms, rev)
            _, vjp = jax.vjp(fn, qs, ks, vs, st_in)
            dos = [[dor[c * RET_CHUNK:(c + 1) * RET_CHUNK, hh * 128:(hh + 1) * 128] for hh in range(RET_HEADS)]
                   for c in range(n)]
            dst = [ds[:, hh * 128:(hh + 1) * 128] for hh in range(RET_HEADS)]
            dqs, dks, dvs, dst_in = vjp((dos, dst))
            cosv, sinv = cs[...], sn[...]
            dq[...] = _rope(jnp.concatenate(dqs, axis=0), cosv, -sinv)
            dk[...] = _rope(jnp.concatenate(dks, axis=0) * (RET_DK ** -0.5), cosv, -sinv)
            for c in range(n):
                for hh in range(RET_HEADS):
                    dv[c * RET_CHUNK:(c + 1) * RET_CHUNK, hh * 128:(hh + 1) * 128] = dvs[c][hh]
            for hh in range(RET_HEADS):
                ds[:, hh * 128:(hh + 1) * 128] = dst_in[hh]

    rf, rb = _ret_rows(nb, True), _ret_rows(nb, False)
    cspecs = [_const_spec(a.shape) for a in (hm, dec_f, xi_f, zeta_f, gam_f, dec_b, xi_b, zeta_b, gam_b)]
    qk = jax.ShapeDtypeStruct((S, RET_QK_W), F32)
    vv = jax.ShapeDtypeStruct((S, RET_W), F32)
    qc, kc, vc = OFF_RET_Q // RET_QK_W, OFF_RET_K // RET_QK_W, OFF_RET_V // RET_W
    outs, carried = _hosted(
        "ret_scan_bwd", body, rider, grid=(nb,),
        in_specs=[rf(RET_QK_W, qc), rf(RET_QK_W, kc), rf(RET_W, vc), rf(RET_QK_W, 0), rf(RET_QK_W, 0), rf(RET_W, 0),
                  pl.BlockSpec((1, RET_QK_W, RET_W), lambda t: (nb - 1 - t, 0, 0)),
                  rb(RET_QK_W, qc), rb(RET_QK_W, kc), rb(RET_W, vc), rb(RET_QK_W, 0), rb(RET_QK_W, 0), rb(RET_W, 0),
                  pl.BlockSpec((1, RET_QK_W, RET_W), lambda t: (t, 0, 0))] + cspecs,
        out_specs=[rf(RET_QK_W, 0), rf(RET_QK_W, 0), rf(RET_W, 0), rb(RET_QK_W, 0), rb(RET_QK_W, 0), rb(RET_W, 0)],
        out_shape=[qk, qk, vv, qk, qk, vv],
        scratch_shapes=[pltpu.VMEM((RET_QK_W, RET_W), F32), pltpu.VMEM((RET_QK_W, RET_W), F32)],
        sem=("arbitrary",),
        operands=(h, h, h, cos_t, sin_t, do, sf, h, h, h, cos_t, sin_t, do, sb,
                  hm, dec_f, xi_f, zeta_f, gam_f, dec_b, xi_b, zeta_b, gam_b))
    return (tuple(outs), carried) if rider else tuple(outs)


def _t5_bucket(rel):
    nb = REL_BUCKETS // 2
    max_exact = nb // 2
    sign_off = jnp.where(rel > 0, nb, 0)
    n = jnp.abs(rel)
    nf = jnp.maximum(n, 1).astype(F32)
    large = max_exact + (jnp.log(nf / max_exact) / math.log(REL_MAX_DIST / max_exact)
                         * (nb - max_exact)).astype(jnp.int32)
    large = jnp.minimum(large, nb - 1)
    return sign_off + jnp.where(n < max_exact, n, large)


def _dil_buckets(dil):
    tq, tb = DIL_TQ, DIL_TQ + 2 * DIL_HALF
    rel_q = jnp.arange(tb)[None, :] - DIL_HALF - jnp.arange(tq)[:, None]
    rel_k = jnp.arange(tq)[None, :] + DIL_HALF - jnp.arange(tb)[:, None]
    return _t5_bucket(rel_q * dil), _t5_bucket(rel_k * dil)


def dil_view(h, g, dil):
    base = OFF_DIL + 3 * g * DIL_W
    if dil == 1:
        return h, IN_W, base
    return h[:, base:base + 3 * DIL_W].reshape(h.shape[0] // dil, dil * 3 * DIL_W), 3 * DIL_W, 0


def _dil_col(view, j):
    _, width, base = view
    return lambda r: (r * width + base + j * DIL_W) // DIL_W


def _dil_specs(L):
    nq = DIL_TQ // DIL_HALF
    last = L // DIL_HALF - 1

    def cur(colfn):
        return pl.BlockSpec((DIL_TQ, DIL_W), lambda r, n: (n, colfn(r)))

    def prev(colfn):
        return pl.BlockSpec((DIL_HALF, DIL_W), lambda r, n: (jnp.maximum(n * nq - 1, 0), colfn(r)))

    def nxt(colfn):
        return pl.BlockSpec((DIL_HALF, DIL_W), lambda r, n: (jnp.minimum((n + 1) * nq, last), colfn(r)))

    return prev, cur, nxt


def _slot(s):
    return slice(s * DIL_HD, (s + 1) * DIL_HD)


def _cat3(a, b, c, s):
    return jnp.concatenate([a[:, _slot(s)], b[:, _slot(s)], c[:, _slot(s)]], axis=0)


def dil_attn_fwd(view, S, g, dil, bias, qg, kg):
    L = S // dil
    hv = view[0]
    tb = DIL_TQ + 2 * DIL_HALF

    def body(q_ref, kp, kc, kn, vp, vc, vn, bias_ref, qg_ref, kg_ref, o_ref, lse_ref):
        n = pl.program_id(1)
        ii = lax.broadcasted_iota(jnp.int32, (DIL_TQ, tb), 0)
        jj = lax.broadcasted_iota(jnp.int32, (DIL_TQ, tb), 1)
        kabs = n * DIL_TQ - DIL_HALF + jj
        valid = (jnp.abs(jj - DIL_HALF - ii) <= DIL_HALF) & (kabs >= 0) & (kabs < L)
        for s in range(DIL_SLOTS):
            q = _head_rms(q_ref[:, _slot(s)], qg_ref[...]) * (DIL_HD ** -0.5)
            kb = _head_rms(_cat3(kp, kc, kn, s), kg_ref[...])
            sc = jnp.where(valid, _mxu(q, kb, NT) + bias_ref[s], NEG)
            m = jnp.max(sc, axis=-1, keepdims=True)
            p = jnp.exp(sc - m)
            den = jnp.sum(p, axis=-1, keepdims=True)
            o_ref[:, _slot(s)] = _mxu(p, _cat3(vp, vc, vn, s), NN) / den
            lse_ref[:, _slot(s)] = jnp.broadcast_to(m + jnp.log(den), (DIL_TQ, DIL_HD))

    prev, cur, nxt = _dil_specs(L)
    qc, kc_, vc_ = (_dil_col(view, j) for j in range(3))
    oc = lambda r: r
    vec = pl.BlockSpec((1, 128), lambda r, n: (0, 0))
    out = jax.ShapeDtypeStruct((L, dil * DIL_W), F32)
    o, lse = pl.pallas_call(
        body, name=f"dil_attn_fwd{g}", grid=(dil, L // DIL_TQ),
        in_specs=[cur(qc), prev(kc_), cur(kc_), nxt(kc_), prev(vc_), cur(vc_), nxt(vc_),
                  pl.BlockSpec((DIL_SLOTS, DIL_TQ, tb), lambda r, n: (0, 0, 0)), vec, vec],
        out_specs=(cur(oc), cur(oc)), out_shape=(out, out),
        compiler_params=_cp("parallel", "parallel"),
    )(hv, hv, hv, hv, hv, hv, hv, bias, qg, kg)
    return o.reshape(S, DIL_W), lse.reshape(S, DIL_W)


def dil_combine(os_, lses, y):
    S = os_[0].shape[0]

    def body(o1, o2, o3, l1, l2, l3, y_in, yc_ref, lt_ref, y_ref):
        a, b, c = l1[...], l2[...], l3[...]
        m = jnp.maximum(jnp.maximum(a, b), c)
        ea, eb, ec = jnp.exp(a - m), jnp.exp(b - m), jnp.exp(c - m)
        den = ea + eb + ec
        yc = (ea * o1[...] + eb * o2[...] + ec * o3[...]) / den
        yc_ref[...] = yc
        y_ref[...] = yc.astype(BF16)
        lt_ref[...] = m + jnp.log(den)

    blk = pl.BlockSpec((GN_T, DIL_W), lambda i: (i, 0))
    out = jax.ShapeDtypeStruct((S, DIL_W), F32)
    return pl.pallas_call(
        body, name="dil_combine", grid=(S // GN_T,), in_specs=[blk] * 6 + [ANY],
        out_specs=(blk, blk, pl.BlockSpec((GN_T, DIL_W), lambda i: (i, (HG_W + RET_W) // DIL_W))),
        out_shape=(out, out, jax.ShapeDtypeStruct(y.shape, y.dtype)), input_output_aliases={6: 2},
        compiler_params=_cp("parallel"),
    )(*os_, *lses, y)


def dil_delta(dy, yc):
    S = yc.shape[0]

    def body(dy_ref, y_ref, d_ref):
        d_ref[...] = jnp.broadcast_to(jnp.sum(dy_ref[...] * y_ref[...], axis=-1, keepdims=True), (GN_T, 128))

    return pl.pallas_call(
        body, name="dil_delta", grid=(S // GN_T, DIL_SLOTS),
        in_specs=[pl.BlockSpec((GN_T, 128), lambda i, s: (i, (HG_W + RET_W) // 128 + s)),
                  pl.BlockSpec((GN_T, 128), lambda i, s: (i, s))],
        out_specs=pl.BlockSpec((GN_T, 128), lambda i, s: (i, s)),
        out_shape=jax.ShapeDtypeStruct((S, DIL_W), F32), compiler_params=_cp("parallel", "parallel"),
    )(dy, yc)


def dil_attn_bwd(view, S, g, dil, bias_q, bias_k, qg, kg, dy, lse_t, delta):
    L = S // dil
    hv = view[0]
    if dil == 1:
        dyv, dyc = dy, lambda r: (HG_W + RET_W) // DIL_W
    else:
        dyv, dyc = dy[:, HG_W + RET_W:].reshape(L, dil * DIL_W), lambda r: r
    lv = lse_t.reshape(L, dil * DIL_W)
    dv_ = delta.reshape(L, dil * DIL_W)
    tq, tb = DIL_TQ, DIL_TQ + 2 * DIL_HALF
    scale = DIL_HD ** -0.5

    def body(qp, qc, qn, kp, kc, kn, vp, vc, vn, dp_, dc, dn, lp, lc, ln, ep, ec, en, bq_ref, bk_ref, qg_ref, kg_ref,
             dq_ref, dk_ref, dv_ref, dbias_ref, dqg_ref, dkg_ref):
        r, n = pl.program_id(0), pl.program_id(1)

        @pl.when((r == 0) & (n == 0))
        def _():
            for ref in (dbias_ref, dqg_ref, dkg_ref):
                ref[...] = jnp.zeros_like(ref)

        qgv, kgv = qg_ref[...], kg_ref[...]
        qfn = lambda t, gg: _head_rms(t, gg) * scale
        ii = lax.broadcasted_iota(jnp.int32, (tq, tb), 0)
        jj = lax.broadcasted_iota(jnp.int32, (tq, tb), 1)
        kabs = n * tq - DIL_HALF + jj
        valid = (jnp.abs(jj - DIL_HALF - ii) <= DIL_HALF) & (kabs >= 0) & (kabs < L)
        i2 = lax.broadcasted_iota(jnp.int32, (tb, tq), 0)
        j2 = lax.broadcasted_iota(jnp.int32, (tb, tq), 1)
        qabs = n * tq - DIL_HALF + i2
        valid2 = (jnp.abs(j2 + DIL_HALF - i2) <= DIL_HALF) & (qabs >= 0) & (qabs < L)
        for s in range(DIL_SLOTS):
            sl = _slot(s)
            one = slice(s * DIL_HD, s * DIL_HD + 1)
            qn_c, q_vjp = jax.vjp(qfn, qc[:, sl], qgv)
            k_band = _head_rms(_cat3(kp, kc, kn, s), kgv)
            sc = _mxu(qn_c, k_band, NT) + bq_ref[s]
            p = jnp.where(valid, jnp.exp(jnp.where(valid, sc, NEG) - lc[:, one]), 0.0)
            ds = p * (_mxu(dc[:, sl], _cat3(vp, vc, vn, s), NT) - ec[:, one])
            dbias_ref[s] += ds
            dq, dqg = q_vjp(_mxu(ds, k_band, NN))
            dq_ref[:, sl] = dq
            dqg_ref[s] += dqg
            kn_c, k_vjp = jax.vjp(_head_rms, kc[:, sl], kgv)
            q_band = qfn(_cat3(qp, qc, qn, s), qgv)
            do_band = _cat3(dp_, dc, dn, s)
            s2 = _mxu(q_band, kn_c, NT) + bk_ref[s]
            lse_band = jnp.concatenate([lp[:, one], lc[:, one], ln[:, one]], axis=0)
            delta_band = jnp.concatenate([ep[:, one], ec[:, one], en[:, one]], axis=0)
            p2 = jnp.where(valid2, jnp.exp(jnp.where(valid2, s2, NEG) - lse_band), 0.0)
            dv_ref[:, sl] = _mxu(p2, do_band, TN)
            ds2 = p2 * (_mxu(do_band, vc[:, sl], NT) - delta_band)
            dk, dkg = k_vjp(_mxu(ds2, q_band, TN))
            dk_ref[:, sl] = dk
            dkg_ref[s] += dkg

    prev, cur, nxt = _dil_specs(L)
    three = lambda colfn: [prev(colfn), cur(colfn), nxt(colfn)]
    qc_, kc_, vc_ = (_dil_col(view, j) for j in range(3))
    oc = lambda r: r
    vec = pl.BlockSpec((1, 128), lambda r, n: (0, 0))
    acc_vec = pl.BlockSpec((DIL_SLOTS, 1, 128), lambda r, n: (0, 0, 0))
    out = jax.ShapeDtypeStruct((L, dil * DIL_W), F32)
    dq, dk, dv, dbias, dqg, dkg = pl.pallas_call(
        body, name=f"dil_attn_bwd{g}", grid=(dil, L // tq),
        in_specs=three(qc_) + three(kc_) + three(vc_) + three(dyc) + three(oc) + three(oc)
        + [pl.BlockSpec((DIL_SLOTS, tq, tb), lambda r, n: (0, 0, 0)),
           pl.BlockSpec((DIL_SLOTS, tb, tq), lambda r, n: (0, 0, 0)), vec, vec],
        out_specs=(cur(oc), cur(oc), cur(oc), pl.BlockSpec((DIL_SLOTS, tq, tb), lambda r, n: (0, 0, 0)),
                   acc_vec, acc_vec),
        out_shape=(out, out, out, jax.ShapeDtypeStruct((DIL_SLOTS, tq, tb), F32),
                   jax.ShapeDtypeStruct((DIL_SLOTS, 1, 128), F32), jax.ShapeDtypeStruct((DIL_SLOTS, 1, 128), F32)),
        compiler_params=_cp("arbitrary", "arbitrary"),
    )(hv, hv, hv, hv, hv, hv, hv, hv, hv, dyv, dyv, dyv, lv, lv, lv, dv_, dv_, dv_, bias_q, bias_k, qg, kg)
    return dq.reshape(S, DIL_W), dk.reshape(S, DIL_W), dv.reshape(S, DIL_W), dbias, dqg, dkg


def _lb_eff(p):
    a = jnp.cumsum(jax.nn.softmax(p.astype(F32), axis=0), axis=0)
    return a - a[0:1]


def _dil_bias(rel_bias, g, dil):
    tbl = rel_bias[:, g * DIL_SLOTS:(g + 1) * DIL_SLOTS]
    return tuple(jnp.einsum("ijb,bs->sij", jax.nn.one_hot(b, REL_BUCKETS, dtype=F32), tbl,
                            precision=lax.Precision.HIGHEST) for b in _dil_buckets(dil))


def _big_weights(w):
    return w[0], w[1].reshape(D_MODEL, D_MODEL), w[2], w[3].reshape(D_FF, D_MODEL)


def _layer_fwd(x, l, prm, wts, rc, biases, gatherer=None):
    def carrying(name, call, *args):
        rider = gatherer.rider(l, name) if gatherer else None
        if rider is None:
            return call(*args)
        res, arrays = call(*args, rider)
        gatherer.done(l, name, arrays)
        return res

    weight = lambda t: _big_weights(wts[l])[t]
    row = lambda a: a[l][None]
    xn = rmsnorm_fwd(x, row(prm["norm_mix"]))
    h = carrying("in", proj_in, xn, weight(0))
    hof, hob, hsf, hsb = hg_scan_fwd(h, row(prm["lbf"]), row(prm["lbb"]))
    y = gated_norm_fwd("hg_out", hof, hob, h, OFF_HG_GATE, row(prm["hg_norm"]), False, None, 0)
    rof, rob, rsf, rsb = ret_scan_fwd(h, rc)
    y = gated_norm_fwd("ret_out", rof, rob, h, OFF_RET_GATE, row(prm["ret_norm"]), True, y, HG_W)
    os_, lses, views = [], [], []
    for g, (_, dil) in enumerate(DIL_GROUPS):
        views.append(dil_view(h, g, dil))
        o, lse = dil_attn_fwd(views[g], h.shape[0], g, dil, biases[g][0], row(prm["q_norm"]), row(prm["k_norm"]))
        os_.append(o)
        lses.append(lse)
    yc, lse_t, y = dil_combine(os_, lses, y)
    if gatherer:
        gatherer.alone(l, "mid")
    x2 = proj_out(y, weight(1), x)
    hm = rmsnorm_fwd(x2, row(prm["norm_mlp"]))
    u, act = carrying("up", proj_up, hm, weight(2))
    x3 = carrying("down", proj_down, act, weight(3), x2)
    saved = dict(x=x, xn=xn, h=h, hof=hof, hob=hob, hsf=hsf, hsb=hsb, rof=rof, rob=rob, rsf=rsf, rsb=rsb,
                 yc=yc, lse_t=lse_t, y=y, x2=x2, hm=hm, u=u, act=act, views=views)
    return x3, saved


def _layer_bwd(dx3, l, prm, wts, rc, biases, sv, reducer=None):
    def carrying(stage, group, call, *args):
        rider = getattr(reducer, stage + "_rider")(group) if reducer else None
        if rider is None:
            return call(*args)
        res, arrived = call(*args, rider)
        getattr(reducer, stage + "_done")(group, arrived)
        return res

    early, late = GradReducer.EARLY, GradReducer.LATE
    win_g, wout_g, wup_g, wdown_g = _big_weights(wts[l])
    row = lambda a: a[l][None]
    h = sv["h"]
    du = carrying("pair", late, bwd_down_act, dx3, wdown_g, sv["u"])
    g_down = wgrad("wgrad_down", sv["act"], dx3, m=D_FF, n=D_MODEL, n_shard=D_MODEL)
    dhm = bwd_up(du, wup_g)
    g_up = wgrad("wgrad_up", sv["hm"], du, m=D_MODEL, n=D_FF, n_shard=FF_SHARD)
    dx2, dg_mlp = rmsnorm_bwd(sv["x2"], row(prm["norm_mlp"]), dhm, dx3)
    dy = bwd_out(dx2, wout_g)
    g_out = wgrad("wgrad_out", sv["y"], dx2, m=D_MODEL, n=D_MODEL, n_shard=D_MODEL)
    g_out, g_down = g_out.reshape(N_CHIPS, D_MODEL // N_CHIPS, D_MODEL), g_down.reshape(N_CHIPS, D_FF // N_CHIPS, D_MODEL)
    if reducer:
        reducer.push(early, l, (g_out, g_up, g_down))
    hdo, hdgate, dg_hg = gated_norm_bwd("hg_out_bwd", sv["hof"], sv["hob"], h, OFF_HG_GATE, row(prm["hg_norm"]),
                                        dy, 0, False)
    hdqf, hdvf, hdzf, hdqb, hdvb, hdzb, dlbf, dlbb = carrying(
        "chip", late, hg_scan_bwd, h, row(prm["lbf"]), row(prm["lbb"]), hdo, sv["hsf"], sv["hsb"])
    rdo, rdgate, dg_ret = gated_norm_bwd("ret_out_bwd", sv["rof"], sv["rob"], h, OFF_RET_GATE, row(prm["ret_norm"]),
                                         dy, HG_W, True)
    rdqf, rdkf, rdvf, rdqb, rdkb, rdvb = carrying("pair", early, ret_scan_bwd, h, rc, rdo, sv["rsf"], sv["rsb"])
    delta = dil_delta(dy, sv["yc"])
    dil_parts, dbiases = [], []
    dqg = jnp.zeros((1, DIL_HD), F32)
    dkg = jnp.zeros((1, DIL_HD), F32)
    for g, (_, dil) in enumerate(DIL_GROUPS):
        dq, dk, dv, dbias, dqg_g, dkg_g = dil_attn_bwd(sv["views"][g], h.shape[0], g, dil, biases[g][0], biases[g][1],
                                                       row(prm["q_norm"]), row(prm["k_norm"]), dy, sv["lse_t"], delta)
        dil_parts += [dq, dk, dv]
        dbiases.append(dbias)
        dqg = dqg + jnp.sum(dqg_g, axis=0)
        dkg = dkg + jnp.sum(dkg_g, axis=0)
    dh = jnp.concatenate([hdqf + hdqb, hdvf + hdvb, hdzf, hdzb, hdgate,
                          rdqf + rdqb, rdkf + rdkb, rdvf + rdvb, rdgate] + dil_parts, axis=1).astype(BF16)
    dxn = carrying("chip", early, bwd_in, dh, win_g)
    g_in = wgrad("wgrad_in", sv["xn"], dh, m=D_MODEL, n=IN_W, n_shard=IN_SHARD)
    dx, dg_mix = rmsnorm_bwd(sv["x"], row(prm["norm_mix"]), dxn, dx2)
    small = dict(norm_mix=dg_mix, norm_mlp=dg_mlp, lbf=dlbf, lbb=dlbb, hg_norm=dg_hg, ret_norm=dg_ret,
                 q_norm=dqg, k_norm=dkg)
    if reducer:
        reducer.push(late, l, (g_in,))
    return dx, (g_in, g_out, g_up, g_down), small, dbiases


def _rel_bias_grad(dbias_layers):
    cols = []
    for g, (_, dil) in enumerate(DIL_GROUPS):
        bq, _ = _dil_buckets(dil)
        onehot = jax.nn.one_hot(bq, REL_BUCKETS, dtype=F32)
        tot = dbias_layers[0][g]
        for d in dbias_layers[1:]:
            tot = tot + d[g]
        cols.append(jnp.einsum("sij,ijb->bs", tot, onehot, precision=lax.Precision.HIGHEST))
    return jnp.concatenate(cols, axis=1)


def local_step(x, tgt, wts, prm_in, reducer=None, gatherer=None):
    S = x.shape[0]
    prm = dict(prm_in)
    prm["lbf"], lbf_vjp = jax.vjp(_lb_eff, prm_in["hg_lb_fwd"])
    prm["lbb"], lbb_vjp = jax.vjp(_lb_eff, prm_in["hg_lb_bwd"])
    rc = _ret_consts(S)
    biases = [_dil_bias(prm["rel_bias"], g, dil) for g, (_, dil) in enumerate(DIL_GROUPS)]
    saved = []
    for l in range(DEPTH):
        x, sv = _layer_fwd(x, l, prm, wts, rc, biases, gatherer)
        saved.append(sv)
    dx, loss_row = loss_head(x, tgt)
    big, small, dbias_layers = [None] * DEPTH, [None] * DEPTH, [None] * DEPTH
    for l in range(DEPTH - 1, -1, -1):
        dx, big[l], small[l], dbias_layers[l] = _layer_bwd(dx, l, prm, wts, rc, biases, saved[l], reducer)
    sg = {k: jnp.concatenate([small[l][k] for l in range(DEPTH)], axis=0) for k in small[0]}
    sg["rel_bias"] = _rel_bias_grad(dbias_layers)
    return loss_row[0, 0], dx, (reducer.finish() if reducer else big), sg, (lbf_vjp, lbb_vjp)


def _place():
    x, y, c = lax.axis_index("x"), lax.axis_index("y"), lax.axis_index("c")
    rels = [(1 - x, y), (x, 1 - y), (1 - x, 1 - y)]
    return x, y, c, 2 * x + y, rels


def _half(c, rows):
    return pl.ds(pl.multiple_of(c * (rows // 2), 16), rows // 2)


def place_own(name, p_arr, w, l):
    _, rows, cols = w.shape
    tr = 512

    def body(p_ref, w_ref, o_ref):
        o_ref[...] = w_ref[...].astype(BF16)

    return pl.pallas_call(
        body, name=name,
        grid_spec=pltpu.PrefetchScalarGridSpec(
            num_scalar_prefetch=1, grid=(rows // tr,),
            in_specs=[pl.BlockSpec((1, tr, cols), lambda i, p: (l, i, 0))],
            out_specs=pl.BlockSpec((1, tr, cols), lambda i, p: (p[0], i, 0))),
        out_shape=jax.ShapeDtypeStruct((N_CHIPS, rows, cols), BF16),
        compiler_params=_cp("parallel"),
    )(p_arr, w)


class WeightGatherer:
    PLAN = {
        (0, "in"): [("ici", 0, (1, 2, 3))],
        (0, "mid"): [("pass", 0, (1, 2, 3))],
        (0, "up"): [("ici", 1, (0, 1))],
        (0, "down"): [("pass", 1, (0, 1)), ("ici", 1, (2, 3))],
        (1, "in"): [("pass", 1, (2, 3)), ("ici", 2, (0, 1))],
        (1, "up"): [("pass", 2, (0, 1)), ("ici", 2, (2, 3))],
        (1, "down"): [("pass", 2, (2, 3))],
        (2, "in"): [("ici", 3, (0, 1))],
        (2, "up"): [("pass", 3, (0, 1)), ("ici", 3, (2, 3))],
        (2, "down"): [("pass", 3, (2, 3))],
    }

    def __init__(self, big_w):
        assert DEPTH == 4
        p_arr = (2 * lax.axis_index("x") + lax.axis_index("y")).astype(jnp.int32).reshape(1)
        self.w = [[place_own(f"place_own{t}", p_arr, w, l) for t, w in enumerate(big_w)] for l in range(DEPTH)]
        self.w[0][0:1] = run_alone("gather_first_chips", gather_ici_rider(self.w[0][0:1]))
        self.w[0][0:1] = run_alone("gather_first_cores", gather_pass_rider(self.w[0][0:1]))

    def rider(self, l, call):
        parts = [(gather_ici_rider if kind == "ici" else gather_pass_rider)([self.w[wl][t] for t in ts])
                 for kind, wl, ts in self.PLAN.get((l, call), ())]
        return functools.reduce(lambda a, b: a + b, parts) if parts else None

    def done(self, l, call, arrays):
        arrays = list(arrays)
        for _, wl, ts in self.PLAN[(l, call)]:
            for t in ts:
                self.w[wl][t] = arrays.pop(0)

    def alone(self, l, call):
        rider = self.rider(l, call)
        if rider:
            self.done(l, call, run_alone(f"gather_{call}", rider))


def run_alone(name, rider):
    n_in, n_out = len(rider.arrays), len(rider.out_shapes)

    def body(*refs):
        ins, outs, sems = refs[:n_in], refs[n_in:n_in + n_out], refs[n_in + n_out:]
        rider.start(ins, outs, sems)
        rider.finish(ins, outs, sems)

    return pl.pallas_call(
        body, name=name, in_specs=[ANY] * n_in, out_specs=[ANY] * n_out, out_shape=rider.out_shapes,
        scratch_shapes=[pltpu.SemaphoreType.DMA((n,)) for n in rider.sems],
        input_output_aliases=rider.aliases(0, 0),
    )(*rider.arrays)


def _both(cp):
    return (cp, cp)


def pair_exchange_rider(gs):
    n = len(gs)

    def ops(ins, outs, ssem, rsem):
        x, y, c, _, _ = _place()
        return [_both(pltpu.make_async_remote_copy(
            src_ref=ins[i].at[:, _half(1 - c, ins[i].shape[1]), :], dst_ref=outs[i],
            send_sem=ssem.at[i], recv_sem=rsem.at[i], device_id=(x, y, 1 - c), device_id_type=MESH)) for i in range(n)]

    return Rider(gs, [jax.ShapeDtypeStruct((N_CHIPS, g.shape[1] // 2, g.shape[2]), F32) for g in gs], [n, n], ops)


def gather_ici_rider(bufs):
    n = len(bufs)

    def ops(ins, outs, ssem, rsem):
        x, y, c, p, rels = _place()
        cps = []
        for i in range(n):
            mine = _half(c, outs[i].shape[1])
            for r, (rx, ry) in enumerate(rels):
                k = i * 3 + r
                peer = dict(device_id=(rx, ry, c), device_id_type=MESH, send_sem=ssem.at[k], recv_sem=rsem.at[k])
                own, landing = outs[i].at[p, mine], outs[i].at[2 * rx + ry, mine]
                cps.append((pltpu.make_async_remote_copy(src_ref=own, dst_ref=own, **peer),
                            pltpu.make_async_remote_copy(src_ref=landing, dst_ref=landing, **peer)))
        return cps

    return Rider(bufs, None, [3 * n, 3 * n], ops)


def gather_pass_rider(bufs):
    n = len(bufs)

    def ops(ins, outs, ssem, rsem):
        x, y, c, p, rels = _place()
        cps = []
        for i in range(n):
            rows = outs[i].shape[1]
            for r, (rx, ry) in enumerate(rels):
                k = i * 3 + r
                peer = dict(device_id=(x, y, 1 - c), device_id_type=MESH, send_sem=ssem.at[k], recv_sem=rsem.at[k])
                landed, theirs = outs[i].at[2 * rx + ry, _half(c, rows)], outs[i].at[2 * rx + ry, _half(1 - c, rows)]
                cps.append((pltpu.make_async_remote_copy(src_ref=landed, dst_ref=landed, **peer),
                            pltpu.make_async_remote_copy(src_ref=theirs, dst_ref=theirs, **peer)))
        return cps

    return Rider(bufs, None, [3 * n, 3 * n], ops)


def pair_add(name, c_arr, g, got):
    _, rows, cols = g.shape
    hr = rows // 2
    tr = 256
    nblk = hr // tr

    def body(c_ref, g_ref, r_ref, o32, o16):
        s = g_ref[...] + r_ref[...]
        o32[...] = s
        o16[...] = s.astype(BF16)

    blk = pl.BlockSpec((1, tr, cols), lambda pp, i, c_ref: (pp, i, 0))
    return pl.pallas_call(
        body, name=name,
        grid_spec=pltpu.PrefetchScalarGridSpec(
            num_scalar_prefetch=1, grid=(N_CHIPS, nblk),
            in_specs=[pl.BlockSpec((1, tr, cols), lambda pp, i, c_ref: (pp, c_ref[0] * nblk + i, 0)), blk],
            out_specs=(blk, blk)),
        out_shape=(jax.ShapeDtypeStruct((N_CHIPS, hr, cols), F32), jax.ShapeDtypeStruct((N_CHIPS, hr, cols), BF16)),
        compiler_params=_cp("parallel", "parallel"),
    )(c_arr, g, got)


def chip_exchange_rider(cs16):
    n = len(cs16)

    def ops(ins, outs, ssem, rsem):
        x, y, c, p, rels = _place()
        return [_both(pltpu.make_async_remote_copy(
            src_ref=ins[i].at[2 * rx + ry], dst_ref=outs[i].at[r], send_sem=ssem.at[i * 3 + r],
            recv_sem=rsem.at[i * 3 + r], device_id=(rx, ry, c), device_id_type=MESH))
            for i in range(n) for r, (rx, ry) in enumerate(rels)]

    return Rider(cs16, [jax.ShapeDtypeStruct((3,) + a.shape[1:], BF16) for a in cs16], [3 * n, 3 * n], ops)


def chip_sum(name, pc_arr, l, cs32, got, prev):
    _, hr, cols = cs32.shape
    tr = 256
    nblk = hr // tr

    def body(pc_ref, o_ref, g_ref, *rest):
        rest[-1][0] = ((o_ref[0] + g_ref[0].astype(F32)) + g_ref[1].astype(F32)) + g_ref[2].astype(F32)

    return pl.pallas_call(
        body, name=name,
        grid_spec=pltpu.PrefetchScalarGridSpec(
            num_scalar_prefetch=1, grid=(nblk,),
            in_specs=[pl.BlockSpec((1, tr, cols), lambda i, pc: (pc[0], i, 0)),
                      pl.BlockSpec((3, tr, cols), lambda i, pc: (0, i, 0))] + ([] if prev is None else [ANY]),
            out_specs=pl.BlockSpec((1, tr, cols), lambda i, pc: (l, pc[1] * nblk + i, 0))),
        out_shape=jax.ShapeDtypeStruct((DEPTH, 2 * hr, cols), F32),
        input_output_aliases={} if prev is None else {3: 0},
        compiler_params=_cp("arbitrary"),
    )(*((pc_arr, cs32, got) + (() if prev is None else (prev,))))


def grad_pair_share(halves):
    n_w = len(halves)
    n = n_w * DEPTH

    def body(*refs):
        bufs = refs[n_w:2 * n_w]
        ssem, rsem = refs[2 * n_w:]
        x, y, c, _, _ = _place()
        cps = []
        for t in range(n_w):
            for l in range(DEPTH):
                mine = bufs[t].at[l, _half(c, bufs[t].shape[1])]
                cp = pltpu.make_async_remote_copy(src_ref=mine, dst_ref=mine, send_sem=ssem.at[t * DEPTH + l],
                                                  recv_sem=rsem.at[t * DEPTH + l], device_id=(x, y, 1 - c),
                                                  device_id_type=MESH)
                cp.start()
                cps.append(cp)
        for t in range(n_w):
            for l in range(DEPTH):
                theirs = bufs[t].at[l, _half(1 - c, bufs[t].shape[1])]
                pltpu.make_async_remote_copy(src_ref=theirs, dst_ref=theirs, send_sem=ssem.at[t * DEPTH + l],
                                             recv_sem=rsem.at[t * DEPTH + l], device_id=(x, y, 1 - c),
                                             device_id_type=MESH).wait_recv()
        for cp in cps:
            cp.wait_send()

    return pl.pallas_call(
        body, name="grad_pair_share", in_specs=[ANY] * n_w, out_specs=[ANY] * n_w,
        out_shape=[jax.ShapeDtypeStruct(a.shape, F32) for a in halves],
        input_output_aliases={t: t for t in range(n_w)},
        scratch_shapes=[pltpu.SemaphoreType.DMA((n,)), pltpu.SemaphoreType.DMA((n,))],
    )(*halves)


SMALL_ROWS = 240


def small_allreduce(v):
    def body(v_ref, o_ref, buf, ssem, rsem):
        x, y, c, _, _ = _place()
        me = 4 * x + 2 * y + c
        buf[me] = v_ref[...]
        for d in range(N_DEV):
            @pl.when(me != d)
            def _():
                pltpu.make_async_remote_copy(
                    src_ref=v_ref, dst_ref=buf.at[me], send_sem=ssem.at[d], recv_sem=rsem.at[me],
                    device_id=(d // 4, (d // 2) % 2, d % 2), device_id_type=MESH).start()
        for d in range(N_DEV):
            @pl.when(me != d)
            def _():
                cp = pltpu.make_async_remote_copy(
                    src_ref=v_ref, dst_ref=buf.at[d], send_sem=ssem.at[d], recv_sem=rsem.at[d],
                    device_id=(d // 4, (d // 2) % 2, d % 2), device_id_type=MESH)
                cp.wait_recv()
                cp.wait_send()
        acc = buf[0]
        for d in range(1, N_DEV):
            acc = acc + buf[d]
        o_ref[...] = acc

    vm = pl.BlockSpec(memory_space=pltpu.VMEM)
    return pl.pallas_call(
        body, name="small_allreduce", in_specs=[vm], out_specs=vm,
        out_shape=jax.ShapeDtypeStruct(v.shape, F32),
        scratch_shapes=[pltpu.VMEM((N_DEV,) + v.shape, F32), pltpu.SemaphoreType.DMA((N_DEV,)),
                        pltpu.SemaphoreType.DMA((N_DEV,))],
    )(v)


class GradReducer:
    EARLY, LATE = (1, 2, 3), (0,)

    def __init__(self):
        self.c_arr = lax.axis_index("c").astype(jnp.int32).reshape(1)
        self.pc_arr = jnp.stack([2 * lax.axis_index("x") + lax.axis_index("y"), lax.axis_index("c")]).astype(jnp.int32)
        self.fresh = {}
        self.paired = {}
        self.acc = [None] * 4

    def push(self, group, l, gs):
        self.fresh[group] = (l, list(gs))

    def pair_rider(self, group):
        return pair_exchange_rider(self.fresh[group][1]) if group in self.fresh else None

    def pair_done(self, group, got):
        l, gs = self.fresh.pop(group)
        self.paired[group] = (l, [pair_add(f"pair_add{t}", self.c_arr, g, r) for t, g, r in zip(group, gs, got)])

    def chip_rider(self, group):
        return chip_exchange_rider([s16 for _, s16 in self.paired[group][1]]) if group in self.paired else None

    def chip_done(self, group, arrived):
        l, cs = self.paired.pop(group)
        for t, (s32, _), got in zip(group, cs, arrived):
            self.acc[t] = chip_sum(f"chip_sum{t}", self.pc_arr, l, s32, got, self.acc[t])

    def finish(self):
        self.pair_done(self.LATE, run_alone("grad_pair_exchange", self.pair_rider(self.LATE)))
        self.chip_done(self.LATE, run_alone("grad_chip_exchange", self.chip_rider(self.LATE)))
        return grad_pair_share(self.acc)


def adamw(name, w, g, m, v):
    shape = w.shape
    cols = shape[-1]
    flat = [t.reshape(-1, cols) for t in (w, g, m, v)]
    rows = flat[0].shape[0]
    tr = 128 if rows % 128 == 0 else rows

    def body(w_ref, g_ref, m_ref, v_ref, d_ref, mo_ref, vo_ref):
        gv = g_ref[...]
        mn = ADAM_B1 * m_ref[...] + (1.0 - ADAM_B1) * gv
        vn = ADAM_B2 * v_ref[...] + (1.0 - ADAM_B2) * jnp.square(gv)
        m_hat = mn / (1.0 - ADAM_B1 ** ADAM_STEP)
        v_hat = vn / (1.0 - ADAM_B2 ** ADAM_STEP)
        d_ref[...] = -ADAM_LR * (m_hat / (jnp.sqrt(v_hat) + ADAM_EPS) + ADAM_WD * w_ref[...])
        mo_ref[...] = mn
        vo_ref[...] = vn

    blk = pl.BlockSpec((tr, cols), lambda i: (i, 0))
    out = jax.ShapeDtypeStruct((rows, cols), F32)
    d, mo, vo = pl.pallas_call(
        body, name=name, grid=(rows // tr,), in_specs=[blk] * 4, out_specs=(blk, blk, blk),
        out_shape=(out, out, out), compiler_params=_cp("parallel"),
    )(*flat)
    return d.reshape(shape), mo.reshape(shape), vo.reshape(shape)


SMALL_NAMES = ("norm_mix", "norm_mlp", "hg_lb_fwd", "hg_lb_bwd", "hg_norm", "ret_norm", "q_norm", "k_norm", "rel_bias")


def _pack_small(d):
    flat = jnp.concatenate([d[k].reshape(-1) for k in SMALL_NAMES])
    return jnp.pad(flat, (0, SMALL_ROWS * 128 - flat.shape[0])).reshape(SMALL_ROWS, 128)


def _unpack_small(v, like):
    flat = v.reshape(-1)
    out, off = {}, 0
    for k in SMALL_NAMES:
        n = like[k].size
        out[k] = flat[off:off + n].reshape(like[k].shape)
        off += n
    return out


def kernel(x, w_in, w_out, w_up, w_down, norm_mix, norm_mlp, hg_lb_fwd, hg_lb_bwd, hg_norm, ret_norm, q_norm, k_norm, rel_bias, loss_target, m_w_in, m_w_out, m_w_up, m_w_down, m_norm_mix, m_norm_mlp, m_hg_lb_fwd, m_hg_lb_bwd, m_hg_norm, m_ret_norm, m_q_norm, m_k_norm, m_rel_bias, v_w_in, v_w_out, v_w_up, v_w_down, v_norm_mix, v_norm_mlp, v_hg_lb_fwd, v_hg_lb_bwd, v_hg_norm, v_ret_norm, v_q_norm, v_k_norm, v_rel_bias):
    big_w = (w_in, w_out, w_up, w_down)
    big_m = (m_w_in, m_w_out, m_w_up, m_w_down)
    big_v = (v_w_in, v_w_out, v_w_up, v_w_down)
    small_w = dict(zip(SMALL_NAMES, (norm_mix, norm_mlp, hg_lb_fwd, hg_lb_bwd, hg_norm, ret_norm, q_norm, k_norm, rel_bias)))
    small_m = dict(zip(SMALL_NAMES, (m_norm_mix, m_norm_mlp, m_hg_lb_fwd, m_hg_lb_bwd, m_hg_norm, m_ret_norm, m_q_norm,
                                     m_k_norm, m_rel_bias)))
    small_v = dict(zip(SMALL_NAMES, (v_norm_mix, v_norm_mlp, v_hg_lb_fwd, v_hg_lb_bwd, v_hg_norm, v_ret_norm, v_q_norm,
                                     v_k_norm, v_rel_bias)))

    gatherer = WeightGatherer(big_w)
    loss_part, dx, grads_big, sg, (lbf_vjp, lbb_vjp) = local_step(x[0], loss_target[0], gatherer.w, small_w,
                                                                  GradReducer(), gatherer)
    loss = lax.psum(loss_part, ("x", "y", "c"))

    sg = dict(sg)
    sg["hg_lb_fwd"], sg["hg_lb_bwd"] = sg.pop("lbf"), sg.pop("lbb")
    tot = _unpack_small(small_allreduce(_pack_small(sg)), small_w)
    tot["hg_lb_fwd"] = lbf_vjp(tot["hg_lb_fwd"])[0]
    tot["hg_lb_bwd"] = lbb_vjp(tot["hg_lb_bwd"])[0]
    grads_small = [tot[k] for k in SMALL_NAMES]

    upd_big = [adamw(f"adamw_big{t}", big_w[t], grads_big[t], big_m[t], big_v[t]) for t in range(4)]
    d_s, m_s, v_s = adamw("adamw_small", _pack_small(small_w), _pack_small(tot), _pack_small(small_m), _pack_small(small_v))
    upd_small = [_unpack_small(t, small_w) for t in (d_s, m_s, v_s)]

    outs = [loss, dx[None]] + list(grads_big) + grads_small
    for j in range(3):
        outs += [u[j] for u in upd_big] + [upd_small[j][k] for k in SMALL_NAMES]
    return tuple(outs)
```

```python
import functools
import math

import jax
import jax.numpy as jnp
from jax import lax
from jax.experimental import pallas as pl
from jax.experimental.pallas import tpu as pltpu

F32 = jnp.float32
BF16 = jnp.bfloat16
EPS = 1e-6

D_MODEL = 2048
DEPTH = 4
HG_HEADS = 6
HG_W = 768
RET_HEADS = 6
RET_DK = 64
RET_W = 768
RET_QK_W = RET_HEADS * RET_DK
RET_CHUNK = 128
RET_TILE = 128
ROPE_BASE = 10000.0
DIL_SLOTS = 4
DIL_HD = 128
DIL_GROUPS = ((128, 1), (512, 4), (2048, 16))
DIL_HALF = 64
DIL_W = 512
D_FF = 4 * D_MODEL
IN_W = 10752
REL_BUCKETS = 32
REL_MAX_DIST = 1024

OFF_HG_Q, OFF_HG_V, OFF_HG_ZF, OFF_HG_ZB, OFF_HG_GATE = 0, 768, 1536, 2304, 3072
OFF_RET_Q, OFF_RET_K, OFF_RET_V, OFF_RET_GATE = 3840, 4224, 4608, 5376
OFF_DIL = 6144

N_CHIPS = 4
N_DEV = 8
IN_SHARD = IN_W // N_CHIPS
FF_SHARD = D_FF // N_CHIPS

ADAM_LR, ADAM_B1, ADAM_B2, ADAM_EPS, ADAM_WD, ADAM_STEP = 0.001, 0.9, 0.999, 1e-08, 0.01, 10

VMEM_LIMIT = 56 * 1024 * 1024
HG_T = 512
HG_C = 64
RET_T = 256
DIL_TQ = 256
NEG = -1e30

NN = (((1,), (0,)), ((), ()))
NT = (((1,), (1,)), ((), ()))
TN = (((0,), (0,)), ((), ()))
MESH = pl.DeviceIdType.MESH
ANY = pl.BlockSpec(memory_space=pl.ANY)


def _cp(*sem):
    return pltpu.CompilerParams(dimension_semantics=sem, vmem_limit_bytes=VMEM_LIMIT)


def _mxu(a, b, dn):
    return lax.dot_general(a.astype(BF16), b.astype(BF16), dn, preferred_element_type=F32)


@jax.custom_vjp
def dot_nn(a, b):
    return _mxu(a, b, NN)


dot_nn.defvjp(lambda a, b: (_mxu(a, b, NN), (a, b)),
              lambda r, g: (_mxu(g, r[1], NT), _mxu(r[0], g, TN)))


@jax.custom_vjp
def dot_nt(a, b):
    return _mxu(a, b, NT)


dot_nt.defvjp(lambda a, b: (_mxu(a, b, NT), (a, b)),
              lambda r, g: (_mxu(g, r[1], NN), _mxu(g, r[0], TN)))


@jax.custom_vjp
def dot_tn(a, b):
    return _mxu(a, b, TN)


dot_tn.defvjp(lambda a, b: (_mxu(a, b, TN), (a, b)),
              lambda r, g: (_mxu(r[1], g, NT), _mxu(r[0], g, NN)))


def _split3(v):
    hi = v.astype(BF16)
    r1 = v - hi.astype(F32)
    mid = r1.astype(BF16)
    lo = (r1 - mid.astype(F32)).astype(BF16)
    return hi, mid, lo


def _exact_mask_dot(m, v, dn):
    mb = m.astype(BF16)
    hi, mid, lo = _split3(v)
    f = lambda p: lax.dot_general(mb, p, dn, preferred_element_type=F32)
    return (f(lo) + f(mid)) + f(hi)


@jax.custom_vjp
def cumdot(m, v):
    return _exact_mask_dot(m, v, NN)


cumdot.defvjp(lambda m, v: (_exact_mask_dot(m, v, NN), m),
              lambda m, g: (jnp.zeros_like(m), _exact_mask_dot(m, g, TN)))


def _sigmoid(z):
    return 1.0 / (1.0 + jnp.exp(-z))


def _head_rms(t, g):
    return t * lax.rsqrt(jnp.mean(t * t, axis=-1, keepdims=True) + EPS) * g


class Rider:
    def __init__(self, arrays, out_shapes, sems, ops):
        self.arrays, self.sems, self.ops = list(arrays), list(sems), ops
        self.in_place = out_shapes is None
        self.out_shapes = [jax.ShapeDtypeStruct(a.shape, a.dtype) for a in arrays] if self.in_place else list(out_shapes)

    def aliases(self, n_in, n_out):
        return {n_in + i: n_out + i for i in range(len(self.arrays))} if self.in_place else {}

    def start(self, ins, outs, sems):
        for send, _ in self.ops(ins, outs, *sems):
            send.start()

    def finish(self, ins, outs, sems):
        cps = self.ops(ins, outs, *sems)
        for _, arrive in cps:
            arrive.wait_recv()
        for send, _ in cps:
            send.wait_send()

    def __add__(self, other):
        assert self.in_place and other.in_place
        na, sa = len(self.arrays), len(self.sems)
        ops = lambda ins, outs, *sems: (self.ops(ins[:na], outs[:na], *sems[:sa])
                                        + other.ops(ins[na:], outs[na:], *sems[sa:]))
        return Rider(self.arrays + other.arrays, None, self.sems + other.sems, ops)


def _hosted(name, body, rider, *, grid, in_specs, out_specs, out_shape, scratch_shapes, sem, operands):
    n_in, n_out, n_scr = len(in_specs), len(out_specs), len(scratch_shapes)
    r_in = len(rider.arrays) if rider else 0
    r_out = len(rider.out_shapes) if rider else 0
    last = tuple(g - 1 for g in grid)

    def kernel_body(*refs):
        ins, refs = refs[:n_in], refs[n_in:]
        rins, refs = refs[:r_in], refs[r_in:]
        outs, refs = refs[:n_out], refs[n_out:]
        routs, refs = refs[:r_out], refs[r_out:]
        scr, rsems = refs[:n_scr], refs[n_scr:]
        if rider:
            ids = [pl.program_id(d) for d in range(len(grid))]
            first = functools.reduce(lambda p, q: p & q, [i == 0 for i in ids])
            done = functools.reduce(lambda p, q: p & q, [i == e for i, e in zip(ids, last)])
            pl.when(first)(lambda: rider.start(rins, routs, rsems))
        body(ins, outs, scr)
        if rider:
            pl.when(done)(lambda: rider.finish(rins, routs, rsems))

    res = pl.pallas_call(
        kernel_body, name=name, grid=grid,
        in_specs=list(in_specs) + [ANY] * r_in,
        out_specs=list(out_specs) + [ANY] * r_out,
        out_shape=list(out_shape) + (rider.out_shapes if rider else []),
        scratch_shapes=list(scratch_shapes) + ([pltpu.SemaphoreType.DMA((n,)) for n in rider.sems] if rider else []),
        input_output_aliases=rider.aliases(n_in, n_out) if rider else {},
        compiler_params=_cp(*(("arbitrary",) * len(grid) if rider else sem)),
    )(*operands, *(rider.arrays if rider else []))
    return res[:n_out], res[n_out:]


def _mm(name, a, b, *, mode, grid, a_spec, b_spec, tm, tn, extras=(), extra_specs=(), epi, out_shape, out_specs,
        rider=None):
    nk = grid[2]
    single = not isinstance(out_shape, (tuple, list))
    if single:
        out_shape, out_specs = [out_shape], [out_specs]

    def body(ins, outs, scr):
        a_ref, b_ref, ex = ins[0], ins[1], ins[2:]
        product = lambda: _mxu(a_ref[...], b_ref[...], {"nn": NN, "nt": NT, "tn": TN}[mode])
        if nk == 1:
            epi(product(), ex, outs)
            return
        acc = scr[0]
        k = pl.program_id(2)

        @pl.when(k == 0)
        def _():
            acc[...] = jnp.zeros_like(acc)

        acc[...] += product()

        @pl.when(k == nk - 1)
        def _():
            epi(acc[...], ex, outs)

    outs, carried = _hosted(name, body, rider, grid=grid, in_specs=[a_spec, b_spec, *extra_specs],
                            out_specs=out_specs, out_shape=out_shape,
                            scratch_shapes=[] if nk == 1 else [pltpu.VMEM((tm, tn), F32)],
                            sem=("parallel", "parallel", "arbitrary"), operands=(a, b, *extras))
    res = outs[0] if single else tuple(outs)
    return (res, carried) if rider else res


def _epi_store(acc, ex, outs):
    outs[0][...] = acc.astype(outs[0].dtype)


def _epi_residual(acc, ex, outs):
    outs[0][...] = ex[0][...] + acc


def _epi_up(acc, ex, outs):
    outs[0][...] = acc
    outs[1][...] = jnp.square(jnp.maximum(acc, 0.0)).astype(BF16)


def _epi_dact(acc, ex, outs):
    outs[0][...] = (acc * (2.0 * jnp.maximum(ex[0][...], 0.0))).astype(BF16)


def _ij(i, j, k):
    return (i, j)


def proj_in(xn, win_g, rider=None):
    S = xn.shape[0]
    tm, tn = 512, IN_SHARD
    return _mm("proj_in", xn, win_g, mode="nn", grid=(IN_W // tn, S // tm, 1), tm=tm, tn=tn,
               a_spec=pl.BlockSpec((tm, D_MODEL), lambda j, i, k: (i, 0)),
               b_spec=pl.BlockSpec((None, D_MODEL, tn), lambda j, i, k: (j, 0, 0)),
               epi=_epi_store, out_shape=jax.ShapeDtypeStruct((S, IN_W), F32),
               out_specs=pl.BlockSpec((tm, tn), lambda j, i, k: (i, j)), rider=rider)


def proj_out(y, wout_g, x):
    S = y.shape[0]
    tm, tn = 1024, 1024
    return _mm("proj_out", y, wout_g, mode="nn", grid=(S // tm, D_MODEL // tn, 1), tm=tm, tn=tn,
               a_spec=pl.BlockSpec((tm, D_MODEL), lambda i, j, k: (i, 0)),
               b_spec=pl.BlockSpec((D_MODEL, tn), lambda i, j, k: (0, j)),
               extras=(x,), extra_specs=(pl.BlockSpec((tm, tn), _ij),),
               epi=_epi_residual, out_shape=jax.ShapeDtypeStruct((S, D_MODEL), F32),
               out_specs=pl.BlockSpec((tm, tn), _ij))


def proj_up(hm, wup_g, rider=None):
    S = hm.shape[0]
    tm, tn = 1024, 1024
    return _mm("proj_up", hm, wup_g, mode="nn", grid=(S // tm, D_FF // tn, 1), tm=tm, tn=tn,
               a_spec=pl.BlockSpec((tm, D_MODEL), lambda i, j, k: (i, 0)),
               b_spec=pl.BlockSpec((None, D_MODEL, tn), lambda i, j, k: (j // 2, 0, j % 2)),
               epi=_epi_up,
               out_shape=(jax.ShapeDtypeStruct((S, D_FF), F32), jax.ShapeDtypeStruct((S, D_FF), BF16)),
               out_specs=(pl.BlockSpec((tm, tn), _ij), pl.BlockSpec((tm, tn), _ij)), rider=rider)


def proj_down(a, wdown_g, x, rider=None):
    S = a.shape[0]
    tm, tn, tk = 1024, 1024, 2048
    return _mm("proj_down", a, wdown_g, mode="nn", grid=(S // tm, D_MODEL // tn, D_FF // tk), tm=tm, tn=tn,
               a_spec=pl.BlockSpec((tm, tk), lambda i, j, k: (i, k)),
               b_spec=pl.BlockSpec((tk, tn), lambda i, j, k: (k, j)),
               extras=(x,), extra_specs=(pl.BlockSpec((tm, tn), _ij),),
               epi=_epi_residual, out_shape=jax.ShapeDtypeStruct((S, D_MODEL), F32),
               out_specs=pl.BlockSpec((tm, tn), _ij), rider=rider)


def bwd_down_act(dx, wdown_g, u, rider=None):
    S = dx.shape[0]
    tm, tn = 1024, 1024
    return _mm("bwd_down_act", dx, wdown_g, mode="nt", grid=(S // tm, D_FF // tn, 1), tm=tm, tn=tn,
               a_spec=pl.BlockSpec((tm, D_MODEL), lambda i, j, k: (i, 0)),
               b_spec=pl.BlockSpec((tn, D_MODEL), lambda i, j, k: (j, 0)),
               extras=(u,), extra_specs=(pl.BlockSpec((tm, tn), _ij),),
               epi=_epi_dact, out_shape=jax.ShapeDtypeStruct((S, D_FF), BF16),
               out_specs=pl.BlockSpec((tm, tn), _ij), rider=rider)


def bwd_up(du, wup_g):
    S = du.shape[0]
    tm, tn, tk = 1024, 1024, FF_SHARD
    return _mm("bwd_up", du, wup_g, mode="nt", grid=(S // tm, D_MODEL // tn, D_FF // tk), tm=tm, tn=tn,
               a_spec=pl.BlockSpec((tm, tk), lambda i, j, k: (i, k)),
               b_spec=pl.BlockSpec((None, tn, tk), lambda i, j, k: (k, j, 0)),
               epi=_epi_store, out_shape=jax.ShapeDtypeStruct((S, D_MODEL), F32),
               out_specs=pl.BlockSpec((tm, tn), _ij))


def bwd_out(dx, wout_g):
    S = dx.shape[0]
    tm, tn = 1024, 1024
    return _mm("bwd_out", dx, wout_g, mode="nt", grid=(S // tm, D_MODEL // tn, 1), tm=tm, tn=tn,
               a_spec=pl.BlockSpec((tm, D_MODEL), lambda i, j, k: (i, 0)),
               b_spec=pl.BlockSpec((tn, D_MODEL), lambda i, j, k: (j, 0)),
               epi=_epi_store, out_shape=jax.ShapeDtypeStruct((S, D_MODEL), F32),
               out_specs=pl.BlockSpec((tm, tn), _ij))


def bwd_in(dh, win_g, rider=None):
    S = dh.shape[0]
    tm, tn, tk = 1024, 1024, IN_SHARD
    return _mm("bwd_in", dh, win_g, mode="nt", grid=(S // tm, D_MODEL // tn, IN_W // tk), tm=tm, tn=tn,
               a_spec=pl.BlockSpec((tm, tk), lambda i, j, k: (i, k)),
               b_spec=pl.BlockSpec((None, tn, tk), lambda i, j, k: (k, j, 0)),
               epi=_epi_store, out_shape=jax.ShapeDtypeStruct((S, D_MODEL), F32),
               out_specs=pl.BlockSpec((tm, tn), _ij), rider=rider)


def wgrad(name, a, g, *, m, n, n_shard):
    S = a.shape[0]
    tm, tn, tk = (512, IN_SHARD, 1024) if n_shard == IN_SHARD else (1024, 1024, 1024)
    per = n_shard // tn
    if n_shard == n:
        out_shape = jax.ShapeDtypeStruct((m, n), F32)
        out_spec = pl.BlockSpec((tm, tn), _ij)
    else:
        out_shape = jax.ShapeDtypeStruct((N_CHIPS, m, n_shard), F32)
        out_spec = pl.BlockSpec((None, tm, tn), lambda i, j, k: (j // per, i, j % per))
    return _mm(name, a, g, mode="tn", grid=(m // tm, n // tn, S // tk), tm=tm, tn=tn,
               a_spec=pl.BlockSpec((tk, tm), lambda i, j, k: (k, i)),
               b_spec=pl.BlockSpec((tk, tn), lambda i, j, k: (k, j)),
               epi=_epi_store, out_shape=out_shape, out_specs=out_spec)


NORM_T = 512


def rmsnorm_fwd(x, g):
    S = x.shape[0]

    def body(x_ref, g_ref, o_ref):
        xv = x_ref[...]
        r = lax.rsqrt(jnp.mean(xv * xv, axis=-1, keepdims=True) + EPS)
        o_ref[...] = ((xv * r) * g_ref[...]).astype(BF16)

    return pl.pallas_call(
        body, name="rmsnorm_fwd", grid=(S // NORM_T,),
        in_specs=[pl.BlockSpec((NORM_T, D_MODEL), lambda i: (i, 0)), pl.BlockSpec((1, D_MODEL), lambda i: (0, 0))],
        out_specs=pl.BlockSpec((NORM_T, D_MODEL), lambda i: (i, 0)),
        out_shape=jax.ShapeDtypeStruct((S, D_MODEL), BF16), compiler_params=_cp("parallel"),
    )(x, g)


def rmsnorm_bwd(x, g, dxn, dres):
    S = x.shape[0]

    def body(x_ref, g_ref, dxn_ref, dres_ref, dx_ref, dg_ref):
        @pl.when(pl.program_id(0) == 0)
        def _():
            dg_ref[...] = jnp.zeros_like(dg_ref)

        xv, gv, d = x_ref[...], g_ref[...], dxn_ref[...]
        r = lax.rsqrt(jnp.mean(xv * xv, axis=-1, keepdims=True) + EPS)
        gd = gv * d
        dx_ref[...] = dres_ref[...] + r * gd - xv * ((r * r * r) * jnp.mean(xv * gd, axis=-1, keepdims=True))
        dg_ref[...] += jnp.sum(d * (xv * r), axis=0, keepdims=True)

    row = pl.BlockSpec((NORM_T, D_MODEL), lambda i: (i, 0))
    vec = pl.BlockSpec((1, D_MODEL), lambda i: (0, 0))
    return pl.pallas_call(
        body, name="rmsnorm_bwd", grid=(S // NORM_T,),
        in_specs=[row, vec, row, row], out_specs=(row, vec),
        out_shape=(jax.ShapeDtypeStruct((S, D_MODEL), F32), jax.ShapeDtypeStruct((1, D_MODEL), F32)),
        compiler_params=_cp("arbitrary"),
    )(x, g, dxn, dres)


def loss_head(y, tgt):
    S = y.shape[0]

    def body(y_ref, t_ref, dy_ref, l_ref):
        @pl.when(pl.program_id(0) == 0)
        def _():
            l_ref[...] = jnp.zeros_like(l_ref)

        e = y_ref[...] - t_ref[...]
        dy_ref[...] = e * (1.0 / D_MODEL)
        l_ref[...] += jnp.sum(e * e) * (0.5 / D_MODEL)

    row = pl.BlockSpec((NORM_T, D_MODEL), lambda i: (i, 0))
    return pl.pallas_call(
        body, name="loss_head", grid=(S // NORM_T,),
        in_specs=[row, row], out_specs=(row, pl.BlockSpec((1, 128), lambda i: (0, 0))),
        out_shape=(jax.ShapeDtypeStruct((S, D_MODEL), F32), jax.ShapeDtypeStruct((1, 128), F32)),
        compiler_params=_cp("arbitrary"),
    )(y, tgt)


def _hg_block(qs, vs, zs, lb, sT, reverse):
    n = len(qs)
    row = lax.broadcasted_iota(jnp.int32, (HG_C, HG_C), 0)
    col = lax.broadcasted_iota(jnp.int32, (HG_C, HG_C), 1)
    tri = (row <= col) if reverse else (row >= col)
    m = tri.astype(F32)
    rsel = lax.broadcasted_iota(jnp.int32, (HG_C, 128), 0)
    ref_rows = ((rsel >= HG_C // 2) if reverse else (rsel <= HG_C // 2)).astype(F32)
    att, qdec, upd, keep = [None] * n, [None] * n, [None] * n, [None] * n
    for c in range(n):
        f = lb + (1.0 - lb) * _sigmoid(zs[c])
        kc = 1.0 - f
        lc = jnp.log(f)
        b = cumdot(m, lc)
        btot = jnp.sum(lc, axis=0, keepdims=True)
        bref = lax.stop_gradient(jnp.sum(lc * ref_rows, axis=0, keepdims=True))
        qe = qs[c] * jnp.exp(jnp.minimum(b - bref, 80.0))
        ke = kc * jnp.exp(jnp.minimum(bref - b, 80.0))
        att[c] = jnp.where(tri, dot_nt(qe, ke), 0.0)
        qdec[c] = qs[c] * jnp.exp(b)
        upd[c] = dot_tn(vs[c], kc * jnp.exp(btot - b))
        keep[c] = jnp.exp(btot)
    states = [None] * n
    for c in (range(n - 1, -1, -1) if reverse else range(n)):
        states[c] = sT
        sT = sT * keep[c] + upd[c]
    outs = [dot_nn(att[c], vs[c]) + dot_nt(qdec[c], states[c]) for c in range(n)]
    return outs, sT


def _chunks(ref, c, n):
    return [ref[i * c:(i + 1) * c, :] for i in range(n)]


def hg_scan_fwd(h, lbf, lbb):
    S = h.shape[0]
    nb = S // HG_T
    n = HG_T // HG_C

    def body(ins, outs, scr):
        qf, vf, zf, qb, vb, zb, lbf_ref, lbb_ref = ins
        of_ref, ob_ref, sf_ref, sb_ref = outs
        stf, stb = scr

        @pl.when(pl.program_id(1) == 0)
        def _():
            stf[...] = jnp.zeros_like(stf)
            stb[...] = jnp.zeros_like(stb)

        for (q, v, z, lb_ref, o_ref, s_ref, st, rev) in ((qf, vf, zf, lbf_ref, of_ref, sf_ref, stf, False),
                                                         (qb, vb, zb, lbb_ref, ob_ref, sb_ref, stb, True)):
            s_ref[0, 0] = st[...]
            outs, s_new = _hg_block(_chunks(q, HG_C, n), _chunks(v, HG_C, n), _chunks(z, HG_C, n),
                                    lb_ref[...], st[...], rev)
            for c in range(n):
                o_ref[c * HG_C:(c + 1) * HG_C, :] = outs[c]
            st[...] = s_new

    def col(off, rev):
        return pl.BlockSpec((HG_T, 128), (lambda hh, t: (nb - 1 - t, off // 128 + hh)) if rev
                            else (lambda hh, t: (t, off // 128 + hh)))

    lb_spec = pl.BlockSpec((1, 128), lambda hh, t: (0, hh))
    st_f = pl.BlockSpec((1, 1, 128, 128), lambda hh, t: (hh, t, 0, 0))
    st_b = pl.BlockSpec((1, 1, 128, 128), lambda hh, t: (hh, nb - 1 - t, 0, 0))
    outs, carried = _hosted(
        "hg_scan_fwd", body, None, grid=(HG_HEADS, nb),
        in_specs=[col(OFF_HG_Q, False), col(OFF_HG_V, False), col(OFF_HG_ZF, False),
                  col(OFF_HG_Q, True), col(OFF_HG_V, True), col(OFF_HG_ZB, True), lb_spec, lb_spec],
        out_specs=[col(0, False), col(0, True), st_f, st_b],
        out_shape=[jax.ShapeDtypeStruct((S, HG_W), F32), jax.ShapeDtypeStruct((S, HG_W), F32),
                   jax.ShapeDtypeStruct((HG_HEADS, nb, 128, 128), F32),
                   jax.ShapeDtypeStruct((HG_HEADS, nb, 128, 128), F32)],
        scratch_shapes=[pltpu.VMEM((128, 128), F32), pltpu.VMEM((128, 128), F32)],
        sem=("arbitrary", "arbitrary"), operands=(h, h, h, h, h, h, lbf, lbb))
    return tuple(outs)


def hg_scan_bwd(h, lbf, lbb, do, sf, sb, rider=None):
    S = h.shape[0]
    nb = S // HG_T
    n = HG_T // HG_C

    def body(ins, outs, scr):
        qf, vf, zf, dof, sfin, qb, vb, zb, dob, sbin, lbf_ref, lbb_ref = ins
        dqf, dvf, dzf, dqb, dvb, dzb, dlbf, dlbb = outs
        dsf, dsb = scr

        @pl.when(pl.program_id(1) == 0)
        def _():
            for r in (dsf, dsb, dlbf, dlbb):
                r[...] = jnp.zeros_like(r)

        for (q, v, z, dor, sin, lb_ref, dq, dv, dz, dlb, ds, rev) in (
                (qf, vf, zf, dof, sfin, lbf_ref, dqf, dvf, dzf, dlbf, dsf, False),
                (qb, vb, zb, dob, sbin, lbb_ref, dqb, dvb, dzb, dlbb, dsb, True)):
            fn = functools.partial(_hg_block, reverse=rev)
            _, vjp = jax.vjp(fn, _chunks(q, HG_C, n), _chunks(v, HG_C, n), _chunks(z, HG_C, n), lb_ref[...], sin[0, 0])
            dqs, dvs, dzs, dlb_v, ds_in = vjp((_chunks(dor, HG_C, n), ds[...]))
            for c in range(n):
                sl = slice(c * HG_C, (c + 1) * HG_C)
                dq[sl, :] = dqs[c]
                dv[sl, :] = dvs[c]
                dz[sl, :] = dzs[c]
            dlb[...] += dlb_v
            ds[...] = ds_in

    def col(off, fwd_scan):
        return pl.BlockSpec((HG_T, 128), (lambda hh, t: (nb - 1 - t, off // 128 + hh)) if fwd_scan
                            else (lambda hh, t: (t, off // 128 + hh)))

    lb_spec = pl.BlockSpec((1, 128), lambda hh, t: (0, hh))
    st_f = pl.BlockSpec((1, 1, 128, 128), lambda hh, t: (hh, nb - 1 - t, 0, 0))
    st_b = pl.BlockSpec((1, 1, 128, 128), lambda hh, t: (hh, t, 0, 0))
    full = jax.ShapeDtypeStruct((S, HG_W), F32)
    vec = jax.ShapeDtypeStruct((1, HG_W), F32)
    outs, carried = _hosted(
        "hg_scan_bwd", body, rider, grid=(HG_HEADS, nb),
        in_specs=[col(OFF_HG_Q, True), col(OFF_HG_V, True), col(OFF_HG_ZF, True), col(0, True), st_f,
                  col(OFF_HG_Q, False), col(OFF_HG_V, False), col(OFF_HG_ZB, False), col(0, False), st_b,
                  lb_spec, lb_spec],
        out_specs=[col(0, True), col(0, True), col(0, True), col(0, False), col(0, False), col(0, False),
                   lb_spec, lb_spec],
        out_shape=[full, full, full, full, full, full, vec, vec],
        scratch_shapes=[pltpu.VMEM((128, 128), F32), pltpu.VMEM((128, 128), F32)],
        sem=("arbitrary", "arbitrary"), operands=(h, h, h, do, sf, h, h, h, do, sb, lbf, lbb))
    return (tuple(outs), carried) if rider else tuple(outs)


GN_T = 1024


def _gated_norm(o, gate, g, center):
    if center:
        o = o - jnp.mean(o, axis=-1, keepdims=True)
    o = o * lax.rsqrt(jnp.mean(o * o, axis=-1, keepdims=True) + EPS)
    return (o * g) * (gate * _sigmoid(gate))


def gated_norm_fwd(name, of, ob, h, gate_off, g, center, y, y_off):
    S = of.shape[0]

    def body(of_ref, ob_ref, gate_ref, g_ref, *rest):
        rest[-1][...] = _gated_norm(of_ref[...] + ob_ref[...], gate_ref[...], g_ref[...], center).astype(BF16)

    blk = pl.BlockSpec((GN_T, 128), lambda hh, i: (i, hh))
    return pl.pallas_call(
        body, name=name, grid=(6, S // GN_T),
        in_specs=[blk, blk, pl.BlockSpec((GN_T, 128), lambda hh, i: (i, gate_off // 128 + hh)),
                  pl.BlockSpec((1, 128), lambda hh, i: (0, hh))] + ([] if y is None else [ANY]),
        out_specs=pl.BlockSpec((GN_T, 128), lambda hh, i: (i, y_off // 128 + hh)),
        out_shape=jax.ShapeDtypeStruct((S, D_MODEL), BF16),
        input_output_aliases={} if y is None else {4: 0},
        compiler_params=_cp("parallel", "parallel"),
    )(*((of, ob, h, g) + (() if y is None else (y,))))


def gated_norm_bwd(name, of, ob, h, gate_off, g, dy, dy_off, center):
    S = of.shape[0]

    def body(of_ref, ob_ref, gate_ref, g_ref, dy_ref, do_ref, dgate_ref, dg_ref):
        @pl.when(pl.program_id(1) == 0)
        def _():
            dg_ref[...] = jnp.zeros_like(dg_ref)

        fn = functools.partial(_gated_norm, center=center)
        _, vjp = jax.vjp(fn, of_ref[...] + ob_ref[...], gate_ref[...], g_ref[...])
        do, dgate, dg = vjp(dy_ref[...])
        do_ref[...] = do
        dgate_ref[...] = dgate
        dg_ref[...] += dg

    blk = pl.BlockSpec((GN_T, 128), lambda hh, i: (i, hh))
    vec = pl.BlockSpec((1, 128), lambda hh, i: (0, hh))
    return pl.pallas_call(
        body, name=name, grid=(6, S // GN_T),
        in_specs=[blk, blk, pl.BlockSpec((GN_T, 128), lambda hh, i: (i, gate_off // 128 + hh)), vec,
                  pl.BlockSpec((GN_T, 128), lambda hh, i: (i, dy_off // 128 + hh))],
        out_specs=(blk, blk, vec),
        out_shape=(jax.ShapeDtypeStruct((S, 768), F32), jax.ShapeDtypeStruct((S, 768), F32),
                   jax.ShapeDtypeStruct((1, 768), F32)),
        compiler_params=_cp("arbitrary", "arbitrary"),
    )(of, ob, h, g, dy)


def _ret_consts(S):
    half = RET_DK // 2
    inv = ROPE_BASE ** (-jnp.arange(half, dtype=F32) / half)
    ang = jnp.arange(S, dtype=F32)[:, None] * inv[None, :]
    cos, sin = jnp.cos(ang), jnp.sin(ang)
    cos_t = jnp.tile(jnp.concatenate([cos, cos], axis=1), (1, RET_HEADS))
    sin_t = jnp.tile(jnp.concatenate([-sin, sin], axis=1), (1, RET_HEADS))
    hidx = jnp.arange(RET_HEADS, dtype=F32)
    lg_f = jnp.log1p(-jnp.exp2(-5.0 - hidx))
    C = RET_CHUNK
    idx = jnp.arange(C, dtype=F32)
    rel = idx[:, None] - idx[None, :]

    def one(lg, reverse):
        lgc = lg[:, None]
        decay = jnp.where(rel >= 0, jnp.exp(lgc[:, :, None] * jnp.maximum(rel, 0.0)), 0.0)
        zeta = jnp.exp(lgc * (C - 1 - idx))
        xi = jnp.exp(lgc * (idx + 1))
        if reverse:
            decay = decay[:, ::-1, ::-1]
            zeta, xi = zeta[:, ::-1], xi[:, ::-1]
        wide = lambda t: jnp.repeat(t.T, RET_DK, axis=1)
        gam_w = jnp.broadcast_to(jnp.repeat(jnp.exp(lg * C), 128)[None, :], (8, RET_W))
        return decay, wide(xi), wide(zeta), gam_w

    hm = (jnp.arange(RET_QK_W)[None, :] // RET_DK == jnp.arange(8)[:, None]).astype(F32)
    return (cos_t, sin_t, hm) + one(lg_f, False) + one(lg_f[::-1], True)


def _rope(t, cos, sin_signed):
    lane = lax.broadcasted_iota(jnp.int32, t.shape, 1)
    first = (lane & (RET_DK - 1)) < RET_DK // 2
    partner = jnp.where(first, pltpu.roll(t, RET_QK_W - RET_DK // 2, 1), pltpu.roll(t, RET_DK // 2, 1))
    return t * cos + partner * sin_signed


def _ret_block(qs, ks, vs, st, dec, xi, zeta, gam, hms, reverse):
    n = len(qs)
    heads = range(RET_HEADS)
    tile = lambda hh: hh * RET_DK // RET_TILE
    qx = [[q * x for q, x in zip(qs[c], xi)] for c in range(n)]
    kz = [[k * z for k, z in zip(ks[c], zeta)] for c in range(n)]
    sc = [[dot_nt(qs[c][tile(hh)] * hms[hh], ks[c][tile(hh)]) * dec[hh] for hh in heads] for c in range(n)]
    upd = [[dot_tn(kz[c][tile(hh)] * hms[hh], vs[c][hh]) for hh in heads] for c in range(n)]
    st = list(st)
    seen = [None] * n
    for c in (range(n - 1, -1, -1) if reverse else range(n)):
        seen[c] = list(st)
        st = [st[hh] * gam[hh] + upd[c][hh] for hh in heads]
    outs = [[dot_nn(sc[c][hh], vs[c][hh]) + dot_nn(qx[c][tile(hh)], seen[c][hh]) for hh in heads] for c in range(n)]
    return outs, st


def _ret_inputs(q_ref, k_ref, v_ref, cos_ref, sin_ref):
    n = RET_T // RET_CHUNK
    qr = _rope(q_ref[...], cos_ref[...], sin_ref[...])
    kr = _rope(k_ref[...], cos_ref[...], sin_ref[...]) * (RET_DK ** -0.5)
    tiles = lambda t, c: [t[c * RET_CHUNK:(c + 1) * RET_CHUNK, p * RET_TILE:(p + 1) * RET_TILE]
                          for p in range(RET_QK_W // RET_TILE)]
    qs = [tiles(qr, c) for c in range(n)]
    ks = [tiles(kr, c) for c in range(n)]
    vs = [[v_ref[c * RET_CHUNK:(c + 1) * RET_CHUNK, hh * 128:(hh + 1) * 128] for hh in range(RET_HEADS)]
          for c in range(n)]
    return qs, ks, vs


def _ret_dir_consts(dec_ref, xi_ref, zeta_ref, gam_ref, hm_ref):
    dec = [dec_ref[hh] for hh in range(RET_HEADS)]
    gam = [gam_ref[0:1, hh * 128:(hh + 1) * 128] for hh in range(RET_HEADS)]
    lanes = lambda p: slice(p * RET_TILE, (p + 1) * RET_TILE)
    hms = [hm_ref[hh:hh + 1, lanes(hh * RET_DK // RET_TILE)] for hh in range(RET_HEADS)]
    n_tiles = RET_QK_W // RET_TILE
    return (dec, [xi_ref[:, lanes(p)] for p in range(n_tiles)], [zeta_ref[:, lanes(p)] for p in range(n_tiles)],
            gam, hms)


def _ret_rows(nb, rev):
    def rows(width, colblk):
        return pl.BlockSpec((RET_T, width), (lambda t: (nb - 1 - t, colblk)) if rev else (lambda t: (t, colblk)))
    return rows


def _const_spec(shape):
    nd = len(shape)
    return pl.BlockSpec(shape, lambda t: (0,) * nd)


def ret_scan_fwd(h, consts):
    S = h.shape[0]
    nb = S // RET_T
    n = RET_T // RET_CHUNK
    cos_t, sin_t, hm, dec_f, xi_f, zeta_f, gam_f, dec_b, xi_b, zeta_b, gam_b = consts

    def body(ins, outs, scr):
        qf, kf, vf, cf, sf, qb, kb, vb, cb, sb_, hm_ref, decf, xif, zetaf, gamf, decb, xib, zetab, gamb = ins
        of_ref, ob_ref, sfo, sbo = outs
        stf, stb = scr

        @pl.when(pl.program_id(0) == 0)
        def _():
            stf[...] = jnp.zeros_like(stf)
            stb[...] = jnp.zeros_like(stb)

        for (q, k, v, cs, sn, dr, xr, zr, gr, o_ref, so, st, rev) in (
                (qf, kf, vf, cf, sf, decf, xif, zetaf, gamf, of_ref, sfo, stf, False),
                (qb, kb, vb, cb, sb_, decb, xib, zetab, gamb, ob_ref, sbo, stb, True)):
            so[0] = st[...]
            qs, ks, vs = _ret_inputs(q, k, v, cs, sn)
            dec, xi, zeta, gam, hms = _ret_dir_consts(dr, xr, zr, gr, hm_ref)
            st_in = [st[:, hh * 128:(hh + 1) * 128] for hh in range(RET_HEADS)]
            outs, st_new = _ret_block(qs, ks, vs, st_in, dec, xi, zeta, gam, hms, rev)
            for c in range(n):
                for hh in range(RET_HEADS):
                    o_ref[c * RET_CHUNK:(c + 1) * RET_CHUNK, hh * 128:(hh + 1) * 128] = outs[c][hh]
            for hh in range(RET_HEADS):
                st[:, hh * 128:(hh + 1) * 128] = st_new[hh]

    rf, rb = _ret_rows(nb, False), _ret_rows(nb, True)
    cspecs = [_const_spec(a.shape) for a in (hm, dec_f, xi_f, zeta_f, gam_f, dec_b, xi_b, zeta_b, gam_b)]
    st_shape = jax.ShapeDtypeStruct((nb, RET_TILE, RET_W), F32)
    qc, kc, vc = OFF_RET_Q // RET_QK_W, OFF_RET_K // RET_QK_W, OFF_RET_V // RET_W
    outs, carried = _hosted(
        "ret_scan_fwd", body, None, grid=(nb,),
        in_specs=[rf(RET_QK_W, qc), rf(RET_QK_W, kc), rf(RET_W, vc), rf(RET_QK_W, 0), rf(RET_QK_W, 0),
                  rb(RET_QK_W, qc), rb(RET_QK_W, kc), rb(RET_W, vc), rb(RET_QK_W, 0), rb(RET_QK_W, 0)] + cspecs,
        out_specs=[rf(RET_W, 0), rb(RET_W, 0),
                   pl.BlockSpec((1, RET_TILE, RET_W), lambda t: (t, 0, 0)),
                   pl.BlockSpec((1, RET_TILE, RET_W), lambda t: (nb - 1 - t, 0, 0))],
        out_shape=[jax.ShapeDtypeStruct((S, RET_W), F32), jax.ShapeDtypeStruct((S, RET_W), F32), st_shape, st_shape],
        scratch_shapes=[pltpu.VMEM((RET_TILE, RET_W), F32), pltpu.VMEM((RET_TILE, RET_W), F32)],
        sem=("arbitrary",),
        operands=(h, h, h, cos_t, sin_t, h, h, h, cos_t, sin_t, hm, dec_f, xi_f, zeta_f, gam_f, dec_b, xi_b, zeta_b,
                  gam_b))
    return tuple(outs)


def ret_scan_bwd(h, consts, do, sf, sb, rider=None):
    S = h.shape[0]
    nb = S // RET_T
    n = RET_T // RET_CHUNK
    cos_t, sin_t, hm, dec_f, xi_f, zeta_f, gam_f, dec_b, xi_b, zeta_b, gam_b = consts

    def body(ins, outs, scr):
        (qf, kf, vf, cf, sf_, dof, sfin, qb, kb, vb, cb, sb_, dob, sbin,
         hm_ref, decf, xif, zetaf, gamf, decb, xib, zetab, gamb) = ins
        dqf, dkf, dvf, dqb, dkb, dvb = outs
        dsf, dsb = scr

        @pl.when(pl.program_id(0) == 0)
        def _():
            dsf[...] = jnp.zeros_like(dsf)
            dsb[...] = jnp.zeros_like(dsb)

        for (q, k, v, cs, sn, dor, sin, dr, xr, zr, gr, dq, dk, dv, ds, rev) in (
                (qf, kf, vf, cf, sf_, dof, sfin, decf, xif, zetaf, gamf, dqf, dkf, dvf, dsf, False),
                (qb, kb, vb, cb, sb_, dob, sbin, decb, xib, zetab, gamb, dqb, dkb, dvb, dsb, True)):
            qs, ks, vs = _ret_inputs(q, k, v, cs, sn)
            dec, xi, zeta, gam, hms = _ret_dir_consts(dr, xr, zr, gr, hm_ref)
            st_in = [sin[0, :, hh * 128:(hh + 1) * 128] for hh in range(RET_HEADS)]
            fn = lambda a, b_, c_, d_: _ret_block(a, b_, c_, d_, dec, xi, zeta, gam, hms, rev)
            _, vjp = jax.vjp(fn, qs, ks, vs, st_in)
            dos = [[dor[c * RET_CHUNK:(c + 1) * RET_CHUNK, hh * 128:(hh + 1) * 128] for hh in range(RET_HEADS)]
                   for c in range(n)]
            dst = [ds[:, hh * 128:(hh + 1) * 128] for hh in range(RET_HEADS)]
            dqs, dks, dvs, dst_in = vjp((dos, dst))
            cosv, sinv = cs[...], sn[...]
            whole = lambda parts: jnp.concatenate([jnp.concatenate(p, axis=1) for p in parts], axis=0)
            dq[...] = _rope(whole(dqs), cosv, -sinv)
            dk[...] = _rope(whole(dks) * (RET_DK ** -0.5), cosv, -sinv)
            for c in range(n):
                for hh in range(RET_HEADS):
                    dv[c * RET_CHUNK:(c + 1) * RET_CHUNK, hh * 128:(hh + 1) * 128] = dvs[c][hh]
            for hh in range(RET_HEADS):
                ds[:, hh * 128:(hh + 1) * 128] = dst_in[hh]

    rf, rb = _ret_rows(nb, True), _ret_rows(nb, False)
    cspecs = [_const_spec(a.shape) for a in (hm, dec_f, xi_f, zeta_f, gam_f, dec_b, xi_b, zeta_b, gam_b)]
    qk = jax.ShapeDtypeStruct((S, RET_QK_W), F32)
    vv = jax.ShapeDtypeStruct((S, RET_W), F32)
    qc, kc, vc = OFF_RET_Q // RET_QK_W, OFF_RET_K // RET_QK_W, OFF_RET_V // RET_W
    outs, carried = _hosted(
        "ret_scan_bwd", body, rider, grid=(nb,),
        in_specs=[rf(RET_QK_W, qc), rf(RET_QK_W, kc), rf(RET_W, vc), rf(RET_QK_W, 0), rf(RET_QK_W, 0), rf(RET_W, 0),
                  pl.BlockSpec((1, RET_TILE, RET_W), lambda t: (nb - 1 - t, 0, 0)),
                  rb(RET_QK_W, qc), rb(RET_QK_W, kc), rb(RET_W, vc), rb(RET_QK_W, 0), rb(RET_QK_W, 0), rb(RET_W, 0),
                  pl.BlockSpec((1, RET_TILE, RET_W), lambda t: (t, 0, 0))] + cspecs,
        out_specs=[rf(RET_QK_W, 0), rf(RET_QK_W, 0), rf(RET_W, 0), rb(RET_QK_W, 0), rb(RET_QK_W, 0), rb(RET_W, 0)],
        out_shape=[qk, qk, vv, qk, qk, vv],
        scratch_shapes=[pltpu.VMEM((RET_TILE, RET_W), F32), pltpu.VMEM((RET_TILE, RET_W), F32)],
        sem=("arbitrary",),
        operands=(h, h, h, cos_t, sin_t, do, sf, h, h, h, cos_t, sin_t, do, sb,
                  hm, dec_f, xi_f, zeta_f, gam_f, dec_b, xi_b, zeta_b, gam_b))
    return (tuple(outs), carried) if rider else tuple(outs)


def _t5_bucket(rel):
    nb = REL_BUCKETS // 2
    max_exact = nb // 2
    sign_off = jnp.where(rel > 0, nb, 0)
    n = jnp.abs(rel)
    nf = jnp.maximum(n, 1).astype(F32)
    large = max_exact + (jnp.log(nf / max_exact) / math.log(REL_MAX_DIST / max_exact)
                         * (nb - max_exact)).astype(jnp.int32)
    large = jnp.minimum(large, nb - 1)
    return sign_off + jnp.where(n < max_exact, n, large)


def _dil_buckets(dil):
    tq, tb = DIL_TQ, DIL_TQ + 2 * DIL_HALF
    rel_q = jnp.arange(tb)[None, :] - DIL_HALF - jnp.arange(tq)[:, None]
    rel_k = jnp.arange(tq)[None, :] + DIL_HALF - jnp.arange(tb)[:, None]
    return _t5_bucket(rel_q * dil), _t5_bucket(rel_k * dil)


def dil_view(h, g, dil):
    base = OFF_DIL + 3 * g * DIL_W
    if dil == 1:
        return h, IN_W, base
    return h[:, base:base + 3 * DIL_W].reshape(h.shape[0] // dil, dil * 3 * DIL_W), 3 * DIL_W, 0


def _dil_col(view, j):
    _, width, base = view
    return lambda r: (r * width + base + j * DIL_W) // DIL_W


def _dil_specs(L):
    nq = DIL_TQ // DIL_HALF
    last = L // DIL_HALF - 1

    def cur(colfn):
        return pl.BlockSpec((DIL_TQ, DIL_W), lambda r, n: (n, colfn(r)))

    def prev(colfn):
        return pl.BlockSpec((DIL_HALF, DIL_W), lambda r, n: (jnp.maximum(n * nq - 1, 0), colfn(r)))

    def nxt(colfn):
        return pl.BlockSpec((DIL_HALF, DIL_W), lambda r, n: (jnp.minimum((n + 1) * nq, last), colfn(r)))

    return prev, cur, nxt


def _slot(s):
    return slice(s * DIL_HD, (s + 1) * DIL_HD)


def _cat3(a, b, c, s):
    return jnp.concatenate([a[:, _slot(s)], b[:, _slot(s)], c[:, _slot(s)]], axis=0)


def dil_attn_fwd(view, S, g, dil, bias, qg, kg):
    L = S // dil
    hv = view[0]
    tb = DIL_TQ + 2 * DIL_HALF

    def body(q_ref, kp, kc, kn, vp, vc, vn, bias_ref, qg_ref, kg_ref, o_ref, lse_ref):
        n = pl.program_id(1)
        ii = lax.broadcasted_iota(jnp.int32, (DIL_TQ, tb), 0)
        jj = lax.broadcasted_iota(jnp.int32, (DIL_TQ, tb), 1)
        kabs = n * DIL_TQ - DIL_HALF + jj
        valid = (jnp.abs(jj - DIL_HALF - ii) <= DIL_HALF) & (kabs >= 0) & (kabs < L)
        for s in range(DIL_SLOTS):
            q = _head_rms(q_ref[:, _slot(s)], qg_ref[...]) * (DIL_HD ** -0.5)
            kb = _head_rms(_cat3(kp, kc, kn, s), kg_ref[...])
            sc = jnp.where(valid, _mxu(q, kb, NT) + bias_ref[s], NEG)
            m = jnp.max(sc, axis=-1, keepdims=True)
            p = jnp.exp(sc - m)
            den = jnp.sum(p, axis=-1, keepdims=True)
            o_ref[:, _slot(s)] = _mxu(p, _cat3(vp, vc, vn, s), NN) / den
            lse_ref[:, _slot(s)] = jnp.broadcast_to(m + jnp.log(den), (DIL_TQ, DIL_HD))

    prev, cur, nxt = _dil_specs(L)
    qc, kc_, vc_ = (_dil_col(view, j) for j in range(3))
    oc = lambda r: r
    vec = pl.BlockSpec((1, 128), lambda r, n: (0, 0))
    out = jax.ShapeDtypeStruct((L, dil * DIL_W), F32)
    o, lse = pl.pallas_call(
        body, name=f"dil_attn_fwd{g}", grid=(dil, L // DIL_TQ),
        in_specs=[cur(qc), prev(kc_), cur(kc_), nxt(kc_), prev(vc_), cur(vc_), nxt(vc_),
                  pl.BlockSpec((DIL_SLOTS, DIL_TQ, tb), lambda r, n: (0, 0, 0)), vec, vec],
        out_specs=(cur(oc), cur(oc)), out_shape=(out, out),
        compiler_params=_cp("parallel", "parallel"),
    )(hv, hv, hv, hv, hv, hv, hv, bias, qg, kg)
    return o.reshape(S, DIL_W), lse.reshape(S, DIL_W)


def dil_combine(os_, lses, y):
    S = os_[0].shape[0]

    def body(o1, o2, o3, l1, l2, l3, y_in, yc_ref, lt_ref, y_ref):
        a, b, c = l1[...], l2[...], l3[...]
        m = jnp.maximum(jnp.maximum(a, b), c)
        ea, eb, ec = jnp.exp(a - m), jnp.exp(b - m), jnp.exp(c - m)
        den = ea + eb + ec
        yc = (ea * o1[...] + eb * o2[...] + ec * o3[...]) / den
        yc_ref[...] = yc
        y_ref[...] = yc.astype(BF16)
        lt_ref[...] = m + jnp.log(den)

    blk = pl.BlockSpec((GN_T, DIL_W), lambda i: (i, 0))
    out = jax.ShapeDtypeStruct((S, DIL_W), F32)
    return pl.pallas_call(
        body, name="dil_combine", grid=(S // GN_T,), in_specs=[blk] * 6 + [ANY],
        out_specs=(blk, blk, pl.BlockSpec((GN_T, DIL_W), lambda i: (i, (HG_W + RET_W) // DIL_W))),
        out_shape=(out, out, jax.ShapeDtypeStruct(y.shape, y.dtype)), input_output_aliases={6: 2},
        compiler_params=_cp("parallel"),
    )(*os_, *lses, y)


def dil_delta(dy, yc):
    S = yc.shape[0]

    def body(dy_ref, y_ref, d_ref):
        d_ref[...] = jnp.broadcast_to(jnp.sum(dy_ref[...] * y_ref[...], axis=-1, keepdims=True), (GN_T, 128))

    return pl.pallas_call(
        body, name="dil_delta", grid=(S // GN_T, DIL_SLOTS),
        in_specs=[pl.BlockSpec((GN_T, 128), lambda i, s: (i, (HG_W + RET_W) // 128 + s)),
                  pl.BlockSpec((GN_T, 128), lambda i, s: (i, s))],
        out_specs=pl.BlockSpec((GN_T, 128), lambda i, s: (i, s)),
        out_shape=jax.ShapeDtypeStruct((S, DIL_W), F32), compiler_params=_cp("parallel", "parallel"),
    )(dy, yc)


def dil_attn_bwd(view, S, g, dil, bias_q, bias_k, qg, kg, dy, lse_t, delta):
    L = S // dil
    hv = view[0]
    if dil == 1:
        dyv, dyc = dy, lambda r: (HG_W + RET_W) // DIL_W
    else:
        dyv, dyc = dy[:, HG_W + RET_W:].reshape(L, dil * DIL_W), lambda r: r
    lv = lse_t.reshape(L, dil * DIL_W)
    dv_ = delta.reshape(L, dil * DIL_W)
    tq, tb = DIL_TQ, DIL_TQ + 2 * DIL_HALF
    scale = DIL_HD ** -0.5

    def body(qp, qc, qn, kp, kc, kn, vp, vc, vn, dp_, dc, dn, lp, lc, ln, ep, ec, en, bq_ref, bk_ref, qg_ref, kg_ref,
             dq_ref, dk_ref, dv_ref, dbias_ref, dqg_ref, dkg_ref):
        r, n = pl.program_id(0), pl.program_id(1)

        @pl.when((r == 0) & (n == 0))
        def _():
            for ref in (dbias_ref, dqg_ref, dkg_ref):
                ref[...] = jnp.zeros_like(ref)

        qgv, kgv = qg_ref[...], kg_ref[...]
        qfn = lambda t, gg: _head_rms(t, gg) * scale
        ii = lax.broadcasted_iota(jnp.int32, (tq, tb), 0)
        jj = lax.broadcasted_iota(jnp.int32, (tq, tb), 1)
        kabs = n * tq - DIL_HALF + jj
        valid = (jnp.abs(jj - DIL_HALF - ii) <= DIL_HALF) & (kabs >= 0) & (kabs < L)
        i2 = lax.broadcasted_iota(jnp.int32, (tb, tq), 0)
        j2 = lax.broadcasted_iota(jnp.int32, (tb, tq), 1)
        qabs = n * tq - DIL_HALF + i2
        valid2 = (jnp.abs(j2 + DIL_HALF - i2) <= DIL_HALF) & (qabs >= 0) & (qabs < L)
        for s in range(DIL_SLOTS):
            sl = _slot(s)
            one = slice(s * DIL_HD, s * DIL_HD + 1)
            qn_c, q_vjp = jax.vjp(qfn, qc[:, sl], qgv)
            k_band = _head_rms(_cat3(kp, kc, kn, s), kgv)
            sc = _mxu(qn_c, k_band, NT) + bq_ref[s]
            p = jnp.where(valid, jnp.exp(jnp.where(valid, sc, NEG) - lc[:, one]), 0.0)
            ds = p * (_mxu(dc[:, sl], _cat3(vp, vc, vn, s), NT) - ec[:, one])
            dbias_ref[s] += ds
            dq, dqg = q_vjp(_mxu(ds, k_band, NN))
            dq_ref[:, sl] = dq
            dqg_ref[s] += dqg
            kn_c, k_vjp = jax.vjp(_head_rms, kc[:, sl], kgv)
            q_band = qfn(_cat3(qp, qc, qn, s), qgv)
            do_band = _cat3(dp_, dc, dn, s)
            s2 = _mxu(q_band, kn_c, NT) + bk_ref[s]
            lse_band = jnp.concatenate([lp[:, one], lc[:, one], ln[:, one]], axis=0)
            delta_band = jnp.concatenate([ep[:, one], ec[:, one], en[:, one]], axis=0)
            p2 = jnp.where(valid2, jnp.exp(jnp.where(valid2, s2, NEG) - lse_band), 0.0)
            dv_ref[:, sl] = _mxu(p2, do_band, TN)
            ds2 = p2 * (_mxu(do_band, vc[:, sl], NT) - delta_band)
            dk, dkg = k_vjp(_mxu(ds2, q_band, TN))
            dk_ref[:, sl] = dk
            dkg_ref[s] += dkg

    prev, cur, nxt = _dil_specs(L)
    three = lambda colfn: [prev(colfn), cur(colfn), nxt(colfn)]
    qc_, kc_, vc_ = (_dil_col(view, j) for j in range(3))
    oc = lambda r: r
    vec = pl.BlockSpec((1, 128), lambda r, n: (0, 0))
    acc_vec = pl.BlockSpec((DIL_SLOTS, 1, 128), lambda r, n: (0, 0, 0))
    out = jax.ShapeDtypeStruct((L, dil * DIL_W), F32)
    dq, dk, dv, dbias, dqg, dkg = pl.pallas_call(
        body, name=f"dil_attn_bwd{g}", grid=(dil, L // tq),
        in_specs=three(qc_) + three(kc_) + three(vc_) + three(dyc) + three(oc) + three(oc)
        + [pl.BlockSpec((DIL_SLOTS, tq, tb), lambda r, n: (0, 0, 0)),
           pl.BlockSpec((DIL_SLOTS, tb, tq), lambda r, n: (0, 0, 0)), vec, vec],
        out_specs=(cur(oc), cur(oc), cur(oc), pl.BlockSpec((DIL_SLOTS, tq, tb), lambda r, n: (0, 0, 0)),
                   acc_vec, acc_vec),
        out_shape=(out, out, out, jax.ShapeDtypeStruct((DIL_SLOTS, tq, tb), F32),
                   jax.ShapeDtypeStruct((DIL_SLOTS, 1, 128), F32), jax.ShapeDtypeStruct((DIL_SLOTS, 1, 128), F32)),
        compiler_params=_cp("arbitrary", "arbitrary"),
    )(hv, hv, hv, hv, hv, hv, hv, hv, hv, dyv, dyv, dyv, lv, lv, lv, dv_, dv_, dv_, bias_q, bias_k, qg, kg)
    return dq.reshape(S, DIL_W), dk.reshape(S, DIL_W), dv.reshape(S, DIL_W), dbias, dqg, dkg


def _lb_eff(p):
    a = jnp.cumsum(jax.nn.softmax(p.astype(F32), axis=0), axis=0)
    return a - a[0:1]


def _dil_bias(rel_bias, g, dil):
    tbl = rel_bias[:, g * DIL_SLOTS:(g + 1) * DIL_SLOTS]
    return tuple(jnp.einsum("ijb,bs->sij", jax.nn.one_hot(b, REL_BUCKETS, dtype=F32), tbl,
                            precision=lax.Precision.HIGHEST) for b in _dil_buckets(dil))


def _big_weights(w):
    return w[0], w[1].reshape(D_MODEL, D_MODEL), w[2], w[3].reshape(D_FF, D_MODEL)


def _layer_fwd(x, l, prm, wts, rc, biases, gatherer=None):
    def carrying(name, call, *args):
        rider = gatherer.rider(l, name) if gatherer else None
        if rider is None:
            return call(*args)
        res, arrays = call(*args, rider)
        gatherer.done(l, name, arrays)
        return res

    weight = lambda t: _big_weights(wts[l])[t]
    row = lambda a: a[l][None]
    xn = rmsnorm_fwd(x, row(prm["norm_mix"]))
    h = carrying("in", proj_in, xn, weight(0))
    hof, hob, hsf, hsb = hg_scan_fwd(h, row(prm["lbf"]), row(prm["lbb"]))
    y = gated_norm_fwd("hg_out", hof, hob, h, OFF_HG_GATE, row(prm["hg_norm"]), False, None, 0)
    rof, rob, rsf, rsb = ret_scan_fwd(h, rc)
    y = gated_norm_fwd("ret_out", rof, rob, h, OFF_RET_GATE, row(prm["ret_norm"]), True, y, HG_W)
    os_, lses, views = [], [], []
    for g, (_, dil) in enumerate(DIL_GROUPS):
        views.append(dil_view(h, g, dil))
        o, lse = dil_attn_fwd(views[g], h.shape[0], g, dil, biases[g][0], row(prm["q_norm"]), row(prm["k_norm"]))
        os_.append(o)
        lses.append(lse)
    yc, lse_t, y = dil_combine(os_, lses, y)
    if gatherer:
        gatherer.alone(l, "mid")
    x2 = proj_out(y, weight(1), x)
    hm = rmsnorm_fwd(x2, row(prm["norm_mlp"]))
    u, act = carrying("up", proj_up, hm, weight(2))
    x3 = carrying("down", proj_down, act, weight(3), x2)
    saved = dict(x=x, xn=xn, h=h, hof=hof, hob=hob, hsf=hsf, hsb=hsb, rof=rof, rob=rob, rsf=rsf, rsb=rsb,
                 yc=yc, lse_t=lse_t, y=y, x2=x2, hm=hm, u=u, act=act, views=views)
    return x3, saved


def _layer_bwd(dx3, l, prm, wts, rc, biases, sv, reducer=None):
    def carrying(stage, group, call, *args):
        rider = getattr(reducer, stage + "_rider")(group) if reducer else None
        if rider is None:
            return call(*args)
        res, arrived = call(*args, rider)
        getattr(reducer, stage + "_done")(group, arrived)
        return res

    early, late = GradReducer.EARLY, GradReducer.LATE
    win_g, wout_g, wup_g, wdown_g = _big_weights(wts[l])
    row = lambda a: a[l][None]
    h = sv["h"]
    du = carrying("pair", late, bwd_down_act, dx3, wdown_g, sv["u"])
    g_down = wgrad("wgrad_down", sv["act"], dx3, m=D_FF, n=D_MODEL, n_shard=D_MODEL)
    dhm = bwd_up(du, wup_g)
    g_up = wgrad("wgrad_up", sv["hm"], du, m=D_MODEL, n=D_FF, n_shard=FF_SHARD)
    dx2, dg_mlp = rmsnorm_bwd(sv["x2"], row(prm["norm_mlp"]), dhm, dx3)
    dy = bwd_out(dx2, wout_g)
    g_out = wgrad("wgrad_out", sv["y"], dx2, m=D_MODEL, n=D_MODEL, n_shard=D_MODEL)
    g_out, g_down = g_out.reshape(N_CHIPS, D_MODEL // N_CHIPS, D_MODEL), g_down.reshape(N_CHIPS, D_FF // N_CHIPS, D_MODEL)
    if reducer:
        reducer.push(early, l, (g_out, g_up, g_down))
    hdo, hdgate, dg_hg = gated_norm_bwd("hg_out_bwd", sv["hof"], sv["hob"], h, OFF_HG_GATE, row(prm["hg_norm"]),
                                        dy, 0, False)
    hdqf, hdvf, hdzf, hdqb, hdvb, hdzb, dlbf, dlbb = carrying(
        "chip", late, hg_scan_bwd, h, row(prm["lbf"]), row(prm["lbb"]), hdo, sv["hsf"], sv["hsb"])
    rdo, rdgate, dg_ret = gated_norm_bwd("ret_out_bwd", sv["rof"], sv["rob"], h, OFF_RET_GATE, row(prm["ret_norm"]),
                                         dy, HG_W, True)
    rdqf, rdkf, rdvf, rdqb, rdkb, rdvb = carrying("pair", early, ret_scan_bwd, h, rc, rdo, sv["rsf"], sv["rsb"])
    delta = dil_delta(dy, sv["yc"])
    dil_parts, dbiases = [], []
    dqg = jnp.zeros((1, DIL_HD), F32)
    dkg = jnp.zeros((1, DIL_HD), F32)
    for g, (_, dil) in enumerate(DIL_GROUPS):
        dq, dk, dv, dbias, dqg_g, dkg_g = dil_attn_bwd(sv["views"][g], h.shape[0], g, dil, biases[g][0], biases[g][1],
                                                       row(prm["q_norm"]), row(prm["k_norm"]), dy, sv["lse_t"], delta)
        dil_parts += [dq, dk, dv]
        dbiases.append(dbias)
        dqg = dqg + jnp.sum(dqg_g, axis=0)
        dkg = dkg + jnp.sum(dkg_g, axis=0)
    dh = jnp.concatenate([hdqf + hdqb, hdvf + hdvb, hdzf, hdzb, hdgate,
                          rdqf + rdqb, rdkf + rdkb, rdvf + rdvb, rdgate] + dil_parts, axis=1).astype(BF16)
    dxn = carrying("chip", early, bwd_in, dh, win_g)
    g_in = wgrad("wgrad_in", sv["xn"], dh, m=D_MODEL, n=IN_W, n_shard=IN_SHARD)
    dx, dg_mix = rmsnorm_bwd(sv["x"], row(prm["norm_mix"]), dxn, dx2)
    small = dict(norm_mix=dg_mix, norm_mlp=dg_mlp, lbf=dlbf, lbb=dlbb, hg_norm=dg_hg, ret_norm=dg_ret,
                 q_norm=dqg, k_norm=dkg)
    if reducer:
        reducer.push(late, l, (g_in,))
    return dx, (g_in, g_out, g_up, g_down), small, dbiases


def _rel_bias_grad(dbias_layers):
    cols = []
    for g, (_, dil) in enumerate(DIL_GROUPS):
        bq, _ = _dil_buckets(dil)
        onehot = jax.nn.one_hot(bq, REL_BUCKETS, dtype=F32)
        tot = dbias_layers[0][g]
        for d in dbias_layers[1:]:
            tot = tot + d[g]
        cols.append(jnp.einsum("sij,ijb->bs", tot, onehot, precision=lax.Precision.HIGHEST))
    return jnp.concatenate(cols, axis=1)


def local_step(x, tgt, wts, prm_in, reducer=None, gatherer=None):
    S = x.shape[0]
    prm = dict(prm_in)
    prm["lbf"], lbf_vjp = jax.vjp(_lb_eff, prm_in["hg_lb_fwd"])
    prm["lbb"], lbb_vjp = jax.vjp(_lb_eff, prm_in["hg_lb_bwd"])
    rc = _ret_consts(S)
    biases = [_dil_bias(prm["rel_bias"], g, dil) for g, (_, dil) in enumerate(DIL_GROUPS)]
    saved = []
    for l in range(DEPTH):
        x, sv = _layer_fwd(x, l, prm, wts, rc, biases, gatherer)
        saved.append(sv)
    dx, loss_row = loss_head(x, tgt)
    big, small, dbias_layers = [None] * DEPTH, [None] * DEPTH, [None] * DEPTH
    for l in range(DEPTH - 1, -1, -1):
        dx, big[l], small[l], dbias_layers[l] = _layer_bwd(dx, l, prm, wts, rc, biases, saved[l], reducer)
    sg = {k: jnp.concatenate([small[l][k] for l in range(DEPTH)], axis=0) for k in small[0]}
    sg["rel_bias"] = _rel_bias_grad(dbias_layers)
    return loss_row[0, 0], dx, (reducer.finish() if reducer else big), sg, (lbf_vjp, lbb_vjp)


def _place():
    x, y, c = lax.axis_index("x"), lax.axis_index("y"), lax.axis_index("c")
    rels = [(1 - x, y), (x, 1 - y), (1 - x, 1 - y)]
    return x, y, c, 2 * x + y, rels


def _half(c, rows):
    return pl.ds(pl.multiple_of(c * (rows // 2), 16), rows // 2)


def place_own(name, p_arr, w, l):
    _, rows, cols = w.shape
    tr = 512

    def body(p_ref, w_ref, o_ref):
        o_ref[...] = w_ref[...].astype(BF16)

    return pl.pallas_call(
        body, name=name,
        grid_spec=pltpu.PrefetchScalarGridSpec(
            num_scalar_prefetch=1, grid=(rows // tr,),
            in_specs=[pl.BlockSpec((1, tr, cols), lambda i, p: (l, i, 0))],
            out_specs=pl.BlockSpec((1, tr, cols), lambda i, p: (p[0], i, 0))),
        out_shape=jax.ShapeDtypeStruct((N_CHIPS, rows, cols), BF16),
        compiler_params=_cp("parallel"),
    )(p_arr, w)


class WeightGatherer:
    PLAN = {
        (0, "in"): [("ici", 0, (1, 2, 3))],
        (0, "mid"): [("pass", 0, (1, 2, 3))],
        (0, "up"): [("ici", 1, (0, 1))],
        (0, "down"): [("pass", 1, (0, 1)), ("ici", 1, (2, 3))],
        (1, "in"): [("pass", 1, (2, 3)), ("ici", 2, (0, 1))],
        (1, "up"): [("pass", 2, (0, 1)), ("ici", 2, (2, 3))],
        (1, "down"): [("pass", 2, (2, 3))],
        (2, "in"): [("ici", 3, (0, 1))],
        (2, "up"): [("pass", 3, (0, 1)), ("ici", 3, (2, 3))],
        (2, "down"): [("pass", 3, (2, 3))],
    }

    def __init__(self, big_w):
        assert DEPTH == 4
        p_arr = (2 * lax.axis_index("x") + lax.axis_index("y")).astype(jnp.int32).reshape(1)
        self.w = [[place_own(f"place_own{t}", p_arr, w, l) for t, w in enumerate(big_w)] for l in range(DEPTH)]
        self.w[0][0:1] = run_alone("gather_first_chips", gather_ici_rider(self.w[0][0:1]))
        self.w[0][0:1] = run_alone("gather_first_cores", gather_pass_rider(self.w[0][0:1]))

    def rider(self, l, call):
        parts = [(gather_ici_rider if kind == "ici" else gather_pass_rider)([self.w[wl][t] for t in ts])
                 for kind, wl, ts in self.PLAN.get((l, call), ())]
        return functools.reduce(lambda a, b: a + b, parts) if parts else None

    def done(self, l, call, arrays):
        arrays = list(arrays)
        for _, wl, ts in self.PLAN[(l, call)]:
            for t in ts:
                self.w[wl][t] = arrays.pop(0)

    def alone(self, l, call):
        rider = self.rider(l, call)
        if rider:
            self.done(l, call, run_alone(f"gather_{call}", rider))


def run_alone(name, rider):
    n_in, n_out = len(rider.arrays), len(rider.out_shapes)

    def body(*refs):
        ins, outs, sems = refs[:n_in], refs[n_in:n_in + n_out], refs[n_in + n_out:]
        rider.start(ins, outs, sems)
        rider.finish(ins, outs, sems)

    return pl.pallas_call(
        body, name=name, in_specs=[ANY] * n_in, out_specs=[ANY] * n_out, out_shape=rider.out_shapes,
        scratch_shapes=[pltpu.SemaphoreType.DMA((n,)) for n in rider.sems],
        input_output_aliases=rider.aliases(0, 0),
    )(*rider.arrays)


def _both(cp):
    return (cp, cp)


def pair_exchange_rider(gs):
    n = len(gs)

    def ops(ins, outs, ssem, rsem):
        x, y, c, _, _ = _place()
        return [_both(pltpu.make_async_remote_copy(
            src_ref=ins[i].at[:, _half(1 - c, ins[i].shape[1]), :], dst_ref=outs[i],
            send_sem=ssem.at[i], recv_sem=rsem.at[i], device_id=(x, y, 1 - c), device_id_type=MESH)) for i in range(n)]

    return Rider(gs, [jax.ShapeDtypeStruct((N_CHIPS, g.shape[1] // 2, g.shape[2]), F32) for g in gs], [n, n], ops)


def gather_ici_rider(bufs):
    n = len(bufs)

    def ops(ins, outs, ssem, rsem):
        x, y, c, p, rels = _place()
        cps = []
        for i in range(n):
            mine = _half(c, outs[i].shape[1])
            for r, (rx, ry) in enumerate(rels):
                k = i * 3 + r
                peer = dict(device_id=(rx, ry, c), device_id_type=MESH, send_sem=ssem.at[k], recv_sem=rsem.at[k])
                own, landing = outs[i].at[p, mine], outs[i].at[2 * rx + ry, mine]
                cps.append((pltpu.make_async_remote_copy(src_ref=own, dst_ref=own, **peer),
                            pltpu.make_async_remote_copy(src_ref=landing, dst_ref=landing, **peer)))
        return cps

    return Rider(bufs, None, [3 * n, 3 * n], ops)


def gather_pass_rider(bufs):
    n = len(bufs)

    def ops(ins, outs, ssem, rsem):
        x, y, c, p, rels = _place()
        cps = []
        for i in range(n):
            rows = outs[i].shape[1]
            for r, (rx, ry) in enumerate(rels):
                k = i * 3 + r
                peer = dict(device_id=(x, y, 1 - c), device_id_type=MESH, send_sem=ssem.at[k], recv_sem=rsem.at[k])
                landed, theirs = outs[i].at[2 * rx + ry, _half(c, rows)], outs[i].at[2 * rx + ry, _half(1 - c, rows)]
                cps.append((pltpu.make_async_remote_copy(src_ref=landed, dst_ref=landed, **peer),
                            pltpu.make_async_remote_copy(src_ref=theirs, dst_ref=theirs, **peer)))
        return cps

    return Rider(bufs, None, [3 * n, 3 * n], ops)


def pair_add(name, c_arr, g, got):
    _, rows, cols = g.shape
    hr = rows // 2
    tr = 256
    nblk = hr // tr

    def body(c_ref, g_ref, r_ref, o32, o16):
        s = g_ref[...] + r_ref[...]
        o32[...] = s
        o16[...] = s.astype(BF16)

    blk = pl.BlockSpec((1, tr, cols), lambda pp, i, c_ref: (pp, i, 0))
    return pl.pallas_call(
        body, name=name,
        grid_spec=pltpu.PrefetchScalarGridSpec(
            num_scalar_prefetch=1, grid=(N_CHIPS, nblk),
            in_specs=[pl.BlockSpec((1, tr, cols), lambda pp, i, c_ref: (pp, c_ref[0] * nblk + i, 0)), blk],
            out_specs=(blk, blk)),
        out_shape=(jax.ShapeDtypeStruct((N_CHIPS, hr, cols), F32), jax.ShapeDtypeStruct((N_CHIPS, hr, cols), BF16)),
        compiler_params=_cp("parallel", "parallel"),
    )(c_arr, g, got)


def chip_exchange_rider(cs16):
    n = len(cs16)

    def ops(ins, outs, ssem, rsem):
        x, y, c, p, rels = _place()
        return [_both(pltpu.make_async_remote_copy(
            src_ref=ins[i].at[2 * rx + ry], dst_ref=outs[i].at[r], send_sem=ssem.at[i * 3 + r],
            recv_sem=rsem.at[i * 3 + r], device_id=(rx, ry, c), device_id_type=MESH))
            for i in range(n) for r, (rx, ry) in enumerate(rels)]

    return Rider(cs16, [jax.ShapeDtypeStruct((3,) + a.shape[1:], BF16) for a in cs16], [3 * n, 3 * n], ops)


def chip_sum(name, pc_arr, l, cs32, got, prev):
    _, hr, cols = cs32.shape
    tr = 256
    nblk = hr // tr

    def body(pc_ref, o_ref, g_ref, *rest):
        rest[-1][0] = ((o_ref[0] + g_ref[0].astype(F32)) + g_ref[1].astype(F32)) + g_ref[2].astype(F32)

    return pl.pallas_call(
        body, name=name,
        grid_spec=pltpu.PrefetchScalarGridSpec(
            num_scalar_prefetch=1, grid=(nblk,),
            in_specs=[pl.BlockSpec((1, tr, cols), lambda i, pc: (pc[0], i, 0)),
                      pl.BlockSpec((3, tr, cols), lambda i, pc: (0, i, 0))] + ([] if prev is None else [ANY]),
            out_specs=pl.BlockSpec((1, tr, cols), lambda i, pc: (l, pc[1] * nblk + i, 0))),
        out_shape=jax.ShapeDtypeStruct((DEPTH, 2 * hr, cols), F32),
        input_output_aliases={} if prev is None else {3: 0},
        compiler_params=_cp("arbitrary"),
    )(*((pc_arr, cs32, got) + (() if prev is None else (prev,))))


def grad_pair_share(halves):
    n_w = len(halves)
    n = n_w * DEPTH

    def body(*refs):
        bufs = refs[n_w:2 * n_w]
        ssem, rsem = refs[2 * n_w:]
        x, y, c, _, _ = _place()
        cps = []
        for t in range(n_w):
            for l in range(DEPTH):
                mine = bufs[t].at[l, _half(c, bufs[t].shape[1])]
                cp = pltpu.make_async_remote_copy(src_ref=mine, dst_ref=mine, send_sem=ssem.at[t * DEPTH + l],
                                                  recv_sem=rsem.at[t * DEPTH + l], device_id=(x, y, 1 - c),
                                                  device_id_type=MESH)
                cp.start()
                cps.append(cp)
        for t in range(n_w):
            for l in range(DEPTH):
                theirs = bufs[t].at[l, _half(1 - c, bufs[t].shape[1])]
                pltpu.make_async_remote_copy(src_ref=theirs, dst_ref=theirs, send_sem=ssem.at[t * DEPTH + l],
                                             recv_sem=rsem.at[t * DEPTH + l], device_id=(x, y, 1 - c),
                                             device_id_type=MESH).wait_recv()
        for cp in cps:
            cp.wait_send()

    return pl.pallas_call(
        body, name="grad_pair_share", in_specs=[ANY] * n_w, out_specs=[ANY] * n_w,
        out_shape=[jax.ShapeDtypeStruct(a.shape, F32) for a in halves],
        input_output_aliases={t: t for t in range(n_w)},
        scratch_shapes=[pltpu.SemaphoreType.DMA((n,)), pltpu.SemaphoreType.DMA((n,))],
    )(*halves)


SMALL_ROWS = 240


def small_allreduce(v):
    def body(v_ref, o_ref, buf, ssem, rsem):
        x, y, c, _, _ = _place()
        me = 4 * x + 2 * y + c
        buf[me] = v_ref[...]
        for d in range(N_DEV):
            @pl.when(me != d)
            def _():
                pltpu.make_async_remote_copy(
                    src_ref=v_ref, dst_ref=buf.at[me], send_sem=ssem.at[d], recv_sem=rsem.at[me],
                    device_id=(d // 4, (d // 2) % 2, d % 2), device_id_type=MESH).start()
        for d in range(N_DEV):
            @pl.when(me != d)
            def _():
                cp = pltpu.make_async_remote_copy(
                    src_ref=v_ref, dst_ref=buf.at[d], send_sem=ssem.at[d], recv_sem=rsem.at[d],
                    device_id=(d // 4, (d // 2) % 2, d % 2), device_id_type=MESH)
                cp.wait_recv()
                cp.wait_send()
        acc = buf[0]
        for d in range(1, N_DEV):
            acc = acc + buf[d]
        o_ref[...] = acc

    vm = pl.BlockSpec(memory_space=pltpu.VMEM)
    return pl.pallas_call(
        body, name="small_allreduce", in_specs=[vm], out_specs=vm,
        out_shape=jax.ShapeDtypeStruct(v.shape, F32),
        scratch_shapes=[pltpu.VMEM((N_DEV,) + v.shape, F32), pltpu.SemaphoreType.DMA((N_DEV,)),
                        pltpu.SemaphoreType.DMA((N_DEV,))],
    )(v)


class GradReducer:
    EARLY, LATE = (1, 2, 3), (0,)

    def __init__(self):
        self.c_arr = lax.axis_index("c").astype(jnp.int32).reshape(1)
        self.pc_arr = jnp.stack([2 * lax.axis_index("x") + lax.axis_index("y"), lax.axis_index("c")]).astype(jnp.int32)
        self.fresh = {}
        self.paired = {}
        self.acc = [None] * 4

    def push(self, group, l, gs):
        self.fresh[group] = (l, list(gs))

    def pair_rider(self, group):
        return pair_exchange_rider(self.fresh[group][1]) if group in self.fresh else None

    def pair_done(self, group, got):
        l, gs = self.fresh.pop(group)
        self.paired[group] = (l, [pair_add(f"pair_add{t}", self.c_arr, g, r) for t, g, r in zip(group, gs, got)])

    def chip_rider(self, group):
        return chip_exchange_rider([s16 for _, s16 in self.paired[group][1]]) if group in self.paired else None

    def chip_done(self, group, arrived):
        l, cs = self.paired.pop(group)
        for t, (s32, _), got in zip(group, cs, arrived):
            self.acc[t] = chip_sum(f"chip_sum{t}", self.pc_arr, l, s32, got, self.acc[t])

    def finish(self):
        self.pair_done(self.LATE, run_alone("grad_pair_exchange", self.pair_rider(self.LATE)))
        self.chip_done(self.LATE, run_alone("grad_chip_exchange", self.chip_rider(self.LATE)))
        return grad_pair_share(self.acc)


def adamw(name, w, g, m, v):
    shape = w.shape
    cols = shape[-1]
    flat = [t.reshape(-1, cols) for t in (w, g, m, v)]
    rows = flat[0].shape[0]
    tr = 128 if rows % 128 == 0 else rows

    def body(w_ref, g_ref, m_ref, v_ref, d_ref, mo_ref, vo_ref):
        gv = g_ref[...]
        mn = ADAM_B1 * m_ref[...] + (1.0 - ADAM_B1) * gv
        vn = ADAM_B2 * v_ref[...] + (1.0 - ADAM_B2) * jnp.square(gv)
        m_hat = mn / (1.0 - ADAM_B1 ** ADAM_STEP)
        v_hat = vn / (1.0 - ADAM_B2 ** ADAM_STEP)
        d_ref[...] = -ADAM_LR * (m_hat / (jnp.sqrt(v_hat) + ADAM_EPS) + ADAM_WD * w_ref[...])
        mo_ref[...] = mn
        vo_ref[...] = vn

    blk = pl.BlockSpec((tr, cols), lambda i: (i, 0))
    out = jax.ShapeDtypeStruct((rows, cols), F32)
    d, mo, vo = pl.pallas_call(
        body, name=name, grid=(rows // tr,), in_specs=[blk] * 4, out_specs=(blk, blk, blk),
        out_shape=(out, out, out), compiler_params=_cp("parallel"),
    )(*flat)
    return d.reshape(shape), mo.reshape(shape), vo.reshape(shape)


SMALL_NAMES = ("norm_mix", "norm_mlp", "hg_lb_fwd", "hg_lb_bwd", "hg_norm", "ret_norm", "q_norm", "k_norm", "rel_bias")


def _pack_small(d):
    flat = jnp.concatenate([d[k].reshape(-1) for k in SMALL_NAMES])
    return jnp.pad(flat, (0, SMALL_ROWS * 128 - flat.shape[0])).reshape(SMALL_ROWS, 128)


def _unpack_small(v, like):
    flat = v.reshape(-1)
    out, off = {}, 0
    for k in SMALL_NAMES:
        n = like[k].size
        out[k] = flat[off:off + n].reshape(like[k].shape)
        off += n
    return out


def kernel(x, w_in, w_out, w_up, w_down, norm_mix, norm_mlp, hg_lb_fwd, hg_lb_bwd, hg_norm, ret_norm, q_norm, k_norm, rel_bias, loss_target, m_w_in, m_w_out, m_w_up, m_w_down, m_norm_mix, m_norm_mlp, m_hg_lb_fwd, m_hg_lb_bwd, m_hg_norm, m_ret_norm, m_q_norm, m_k_norm, m_rel_bias, v_w_in, v_w_out, v_w_up, v_w_down, v_norm_mix, v_norm_mlp, v_hg_lb_fwd, v_hg_lb_bwd, v_hg_norm, v_ret_norm, v_q_norm, v_k_norm, v_rel_bias):
    big_w = (w_in, w_out, w_up, w_down)
    big_m = (m_w_in, m_w_out, m_w_up, m_w_down)
    big_v = (v_w_in, v_w_out, v_w_up, v_w_down)
    small_w = dict(zip(SMALL_NAMES, (norm_mix, norm_mlp, hg_lb_fwd, hg_lb_bwd, hg_norm, ret_norm, q_norm, k_norm, rel_bias)))
    small_m = dict(zip(SMALL_NAMES, (m_norm_mix, m_norm_mlp, m_hg_lb_fwd, m_hg_lb_bwd, m_hg_norm, m_ret_norm, m_q_norm,
                                     m_k_norm, m_rel_bias)))
    small_v = dict(zip(SMALL_NAMES, (v_norm_mix, v_norm_mlp, v_hg_lb_fwd, v_hg_lb_bwd, v_hg_norm, v_ret_norm, v_q_norm,
                                     v_k_norm, v_rel_bias)))

    gatherer = WeightGatherer(big_w)
    loss_part, dx, grads_big, sg, (lbf_vjp, lbb_vjp) = local_step(x[0], loss_target[0], gatherer.w, small_w,
                                                                  GradReducer(), gatherer)
    loss = lax.psum(loss_part, ("x", "y", "c"))

    sg = dict(sg)
    sg["hg_lb_fwd"], sg["hg_lb_bwd"] = sg.pop("lbf"), sg.pop("lbb")
    tot = _unpack_small(small_allreduce(_pack_small(sg)), small_w)
    tot["hg_lb_fwd"] = lbf_vjp(tot["hg_lb_fwd"])[0]
    tot["hg_lb_bwd"] = lbb_vjp(tot["hg_lb_bwd"])[0]
    grads_small = [tot[k] for k in SMALL_NAMES]

    upd_big = [adamw(f"adamw_big{t}", big_w[t], grads_big[t], big_m[t], big_v[t]) for t in range(4)]
    d_s, m_s, v_s = adamw("adamw_small", _pack_small(small_w), _pack_small(tot), _pack_small(small_m), _pack_small(small_v))
    upd_small = [_unpack_small(t, small_w) for t in (d_s, m_s, v_s)]

    outs = [loss, dx[None]] + list(grads_big) + grads_small
    for j in range(3):
        outs += [u[j] for u in upd_big] + [upd_small[j][k] for k in SMALL_NAMES]
    return tuple(outs)
```

```python
import functools
import math

import jax
import jax.numpy as jnp
from jax import lax
from jax.experimental import pallas as pl
from jax.experimental.pallas import tpu as pltpu

F32 = jnp.float32
BF16 = jnp.bfloat16
EPS = 1e-6

D_MODEL = 2048
DEPTH = 4
HG_HEADS = 6
HG_W = 768
RET_HEADS = 6
RET_DK = 64
RET_W = 768
RET_QK_W = RET_HEADS * RET_DK
RET_CHUNK = 128
RET_TILE = 128
ROPE_BASE = 10000.0
DIL_SLOTS = 4
DIL_HD = 128
DIL_GROUPS = ((128, 1), (512, 4), (2048, 16))
DIL_HALF = 64
DIL_W = 512
D_FF = 4 * D_MODEL
IN_W = 10752
REL_BUCKETS = 32
REL_MAX_DIST = 1024

OFF_HG_Q, OFF_HG_V, OFF_HG_ZF, OFF_HG_ZB, OFF_HG_GATE = 0, 768, 1536, 2304, 3072
OFF_RET_Q, OFF_RET_K, OFF_RET_V, OFF_RET_GATE = 3840, 4224, 4608, 5376
OFF_DIL = 6144

N_CHIPS = 4
N_DEV = 8
IN_SHARD = IN_W // N_CHIPS
FF_SHARD = D_FF // N_CHIPS

ADAM_LR, ADAM_B1, ADAM_B2, ADAM_EPS, ADAM_WD, ADAM_STEP = 0.001, 0.9, 0.999, 1e-08, 0.01, 10

VMEM_LIMIT = 56 * 1024 * 1024
HG_T = 512
HG_C = 64
RET_T = 256
DIL_TQ = 256
NEG = -1e30

NN = (((1,), (0,)), ((), ()))
NT = (((1,), (1,)), ((), ()))
TN = (((0,), (0,)), ((), ()))
MESH = pl.DeviceIdType.MESH
ANY = pl.BlockSpec(memory_space=pl.ANY)


def _cp(*sem):
    return pltpu.CompilerParams(dimension_semantics=sem, vmem_limit_bytes=VMEM_LIMIT)


def _mxu(a, b, dn):
    return lax.dot_general(a.astype(BF16), b.astype(BF16), dn, preferred_element_type=F32)


@jax.custom_vjp
def dot_nn(a, b):
    return _mxu(a, b, NN)


dot_nn.defvjp(lambda a, b: (_mxu(a, b, NN), (a, b)),
              lambda r, g: (_mxu(g, r[1], NT), _mxu(r[0], g, TN)))


@jax.custom_vjp
def dot_nt(a, b):
    return _mxu(a, b, NT)


dot_nt.defvjp(lambda a, b: (_mxu(a, b, NT), (a, b)),
              lambda r, g: (_mxu(g, r[1], NN), _mxu(g, r[0], TN)))


@jax.custom_vjp
def dot_tn(a, b):
    return _mxu(a, b, TN)


dot_tn.defvjp(lambda a, b: (_mxu(a, b, TN), (a, b)),
              lambda r, g: (_mxu(r[1], g, NT), _mxu(r[0], g, NN)))


def _split3(v):
    hi = v.astype(BF16)
    r1 = v - hi.astype(F32)
    mid = r1.astype(BF16)
    lo = (r1 - mid.astype(F32)).astype(BF16)
    return hi, mid, lo


def _exact_mask_dot(m, v, dn):
    mb = m.astype(BF16)
    hi, mid, lo = _split3(v)
    f = lambda p: lax.dot_general(mb, p, dn, preferred_element_type=F32)
    return (f(lo) + f(mid)) + f(hi)


@jax.custom_vjp
def cumdot(m, v):
    return _exact_mask_dot(m, v, NN)


cumdot.defvjp(lambda m, v: (_exact_mask_dot(m, v, NN), m),
              lambda m, g: (jnp.zeros_like(m), _exact_mask_dot(m, g, TN)))


def _sigmoid(z):
    return 1.0 / (1.0 + jnp.exp(-z))


def _head_rms(t, g):
    return t * lax.rsqrt(jnp.mean(t * t, axis=-1, keepdims=True) + EPS) * g


class Rider:
    def __init__(self, arrays, out_shapes, sems, ops):
        self.arrays, self.sems, self.ops = list(arrays), list(sems), ops
        self.in_place = out_shapes is None
        self.out_shapes = [jax.ShapeDtypeStruct(a.shape, a.dtype) for a in arrays] if self.in_place else list(out_shapes)

    def aliases(self, n_in, n_out):
        return {n_in + i: n_out + i for i in range(len(self.arrays))} if self.in_place else {}

    def start(self, ins, outs, sems):
        for send, _ in self.ops(ins, outs, *sems):
            send.start()

    def finish(self, ins, outs, sems):
        cps = self.ops(ins, outs, *sems)
        for _, arrive in cps:
            arrive.wait_recv()
        for send, _ in cps:
            send.wait_send()

    def __add__(self, other):
        assert self.in_place and other.in_place
        na, sa = len(self.arrays), len(self.sems)
        ops = lambda ins, outs, *sems: (self.ops(ins[:na], outs[:na], *sems[:sa])
                                        + other.ops(ins[na:], outs[na:], *sems[sa:]))
        return Rider(self.arrays + other.arrays, None, self.sems + other.sems, ops)


def _hosted(name, body, rider, *, grid, in_specs, out_specs, out_shape, scratch_shapes, sem, operands):
    n_in, n_out, n_scr = len(in_specs), len(out_specs), len(scratch_shapes)
    r_in = len(rider.arrays) if rider else 0
    r_out = len(rider.out_shapes) if rider else 0
    last = tuple(g - 1 for g in grid)

    def kernel_body(*refs):
        ins, refs = refs[:n_in], refs[n_in:]
        rins, refs = refs[:r_in], refs[r_in:]
        outs, refs = refs[:n_out], refs[n_out:]
        routs, refs = refs[:r_out], refs[r_out:]
        scr, rsems = refs[:n_scr], refs[n_scr:]
        if rider:
            ids = [pl.program_id(d) for d in range(len(grid))]
            first = functools.reduce(lambda p, q: p & q, [i == 0 for i in ids])
            done = functools.reduce(lambda p, q: p & q, [i == e for i, e in zip(ids, last)])
            pl.when(first)(lambda: rider.start(rins, routs, rsems))
        body(ins, outs, scr)
        if rider:
            pl.when(done)(lambda: rider.finish(rins, routs, rsems))

    res = pl.pallas_call(
        kernel_body, name=name, grid=grid,
        in_specs=list(in_specs) + [ANY] * r_in,
        out_specs=list(out_specs) + [ANY] * r_out,
        out_shape=list(out_shape) + (rider.out_shapes if rider else []),
        scratch_shapes=list(scratch_shapes) + ([pltpu.SemaphoreType.DMA((n,)) for n in rider.sems] if rider else []),
        input_output_aliases=rider.aliases(n_in, n_out) if rider else {},
        compiler_params=_cp(*(("arbitrary",) * len(grid) if rider else sem)),
    )(*operands, *(rider.arrays if rider else []))
    return res[:n_out], res[n_out:]


def _mm(name, a, b, *, mode, grid, a_spec, b_spec, tm, tn, extras=(), extra_specs=(), epi, out_shape, out_specs,
        rider=None):
    nk = grid[2]
    single = not isinstance(out_shape, (tuple, list))
    if single:
        out_shape, out_specs = [out_shape], [out_specs]

    def body(ins, outs, scr):
        a_ref, b_ref, ex = ins[0], ins[1], ins[2:]
        product = lambda: _mxu(a_ref[...], b_ref[...], {"nn": NN, "nt": NT, "tn": TN}[mode])
        if nk == 1:
            epi(product(), ex, outs)
            return
        acc = scr[0]
        k = pl.program_id(2)

        @pl.when(k == 0)
        def _():
            acc[...] = jnp.zeros_like(acc)

        acc[...] += product()

        @pl.when(k == nk - 1)
        def _():
            epi(acc[...], ex, outs)

    outs, carried = _hosted(name, body, rider, grid=grid, in_specs=[a_spec, b_spec, *extra_specs],
                            out_specs=out_specs, out_shape=out_shape,
                            scratch_shapes=[] if nk == 1 else [pltpu.VMEM((tm, tn), F32)],
                            sem=("parallel", "parallel", "arbitrary"), operands=(a, b, *extras))
    res = outs[0] if single else tuple(outs)
    return (res, carried) if rider else res


def _epi_store(acc, ex, outs):
    outs[0][...] = acc.astype(outs[0].dtype)


def _epi_residual(acc, ex, outs):
    outs[0][...] = ex[0][...] + acc


def _epi_up(acc, ex, outs):
    outs[0][...] = acc
    outs[1][...] = jnp.square(jnp.maximum(acc, 0.0)).astype(BF16)


def _epi_dact(acc, ex, outs):
    outs[0][...] = (acc * (2.0 * jnp.maximum(ex[0][...], 0.0))).astype(BF16)


def _ij(i, j, k):
    return (i, j)


def proj_in(xn, win_g, rider=None):
    S = xn.shape[0]
    tm, tn = 512, IN_SHARD
    return _mm("proj_in", xn, win_g, mode="nn", grid=(IN_W // tn, S // tm, 1), tm=tm, tn=tn,
               a_spec=pl.BlockSpec((tm, D_MODEL), lambda j, i, k: (i, 0)),
               b_spec=pl.BlockSpec((None, D_MODEL, tn), lambda j, i, k: (j, 0, 0)),
               epi=_epi_store, out_shape=jax.ShapeDtypeStruct((S, IN_W), F32),
               out_specs=pl.BlockSpec((tm, tn), lambda j, i, k: (i, j)), rider=rider)


def proj_out(y, wout_g, x):
    S = y.shape[0]
    tm, tn = 1024, 1024
    return _mm("proj_out", y, wout_g, mode="nn", grid=(S // tm, D_MODEL // tn, 1), tm=tm, tn=tn,
               a_spec=pl.BlockSpec((tm, D_MODEL), lambda i, j, k: (i, 0)),
               b_spec=pl.BlockSpec((D_MODEL, tn), lambda i, j, k: (0, j)),
               extras=(x,), extra_specs=(pl.BlockSpec((tm, tn), _ij),),
               epi=_epi_residual, out_shape=jax.ShapeDtypeStruct((S, D_MODEL), F32),
               out_specs=pl.BlockSpec((tm, tn), _ij))


def proj_up(hm, wup_g, rider=None):
    S = hm.shape[0]
    tm, tn = 1024, 1024
    return _mm("proj_up", hm, wup_g, mode="nn", grid=(S // tm, D_FF // tn, 1), tm=tm, tn=tn,
               a_spec=pl.BlockSpec((tm, D_MODEL), lambda i, j, k: (i, 0)),
               b_spec=pl.BlockSpec((None, D_MODEL, tn), lambda i, j, k: (j // 2, 0, j % 2)),
               epi=_epi_up,
               out_shape=(jax.ShapeDtypeStruct((S, D_FF), F32), jax.ShapeDtypeStruct((S, D_FF), BF16)),
               out_specs=(pl.BlockSpec((tm, tn), _ij), pl.BlockSpec((tm, tn), _ij)), rider=rider)


def proj_down(a, wdown_g, x, rider=None):
    S = a.shape[0]
    tm, tn, tk = 1024, 1024, 2048
    return _mm("proj_down", a, wdown_g, mode="nn", grid=(S // tm, D_MODEL // tn, D_FF // tk), tm=tm, tn=tn,
               a_spec=pl.BlockSpec((tm, tk), lambda i, j, k: (i, k)),
               b_spec=pl.BlockSpec((tk, tn), lambda i, j, k: (k, j)),
               extras=(x,), extra_specs=(pl.BlockSpec((tm, tn), _ij),),
               epi=_epi_residual, out_shape=jax.ShapeDtypeStruct((S, D_MODEL), F32),
               out_specs=pl.BlockSpec((tm, tn), _ij), rider=rider)


def bwd_down_act(dx, wdown_g, u, rider=None):
    S = dx.shape[0]
    tm, tn = 1024, 1024
    return _mm("bwd_down_act", dx, wdown_g, mode="nt", grid=(S // tm, D_FF // tn, 1), tm=tm, tn=tn,
               a_spec=pl.BlockSpec((tm, D_MODEL), lambda i, j, k: (i, 0)),
               b_spec=pl.BlockSpec((tn, D_MODEL), lambda i, j, k: (j, 0)),
               extras=(u,), extra_specs=(pl.BlockSpec((tm, tn), _ij),),
               epi=_epi_dact, out_shape=jax.ShapeDtypeStruct((S, D_FF), BF16),
               out_specs=pl.BlockSpec((tm, tn), _ij), rider=rider)


def bwd_up(du, wup_g):
    S = du.shape[0]
    tm, tn, tk = 1024, 1024, FF_SHARD
    return _mm("bwd_up", du, wup_g, mode="nt", grid=(S // tm, D_MODEL // tn, D_FF // tk), tm=tm, tn=tn,
               a_spec=pl.BlockSpec((tm, tk), lambda i, j, k: (i, k)),
               b_spec=pl.BlockSpec((None, tn, tk), lambda i, j, k: (k, j, 0)),
               epi=_epi_store, out_shape=jax.ShapeDtypeStruct((S, D_MODEL), F32),
               out_specs=pl.BlockSpec((tm, tn), _ij))


def bwd_out(dx, wout_g):
    S = dx.shape[0]
    tm, tn = 1024, 1024
    return _mm("bwd_out", dx, wout_g, mode="nt", grid=(S // tm, D_MODEL // tn, 1), tm=tm, tn=tn,
               a_spec=pl.BlockSpec((tm, D_MODEL), lambda i, j, k: (i, 0)),
               b_spec=pl.BlockSpec((tn, D_MODEL), lambda i, j, k: (j, 0)),
               epi=_epi_store, out_shape=jax.ShapeDtypeStruct((S, D_MODEL), F32),
               out_specs=pl.BlockSpec((tm, tn), _ij))


def bwd_in(dh, win_g, rider=None):
    S = dh.shape[0]
    tm, tn, tk = 1024, 1024, IN_SHARD
    return _mm("bwd_in", dh, win_g, mode="nt", grid=(S // tm, D_MODEL // tn, IN_W // tk), tm=tm, tn=tn,
               a_spec=pl.BlockSpec((tm, tk), lambda i, j, k: (i, k)),
               b_spec=pl.BlockSpec((None, tn, tk), lambda i, j, k: (k, j, 0)),
               epi=_epi_store, out_shape=jax.ShapeDtypeStruct((S, D_MODEL), F32),
               out_specs=pl.BlockSpec((tm, tn), _ij), rider=rider)


def wgrad(name, a, g, *, m, n, n_shard):
    S = a.shape[0]
    tm, tn, tk = (512, IN_SHARD, 1024) if n_shard == IN_SHARD else (1024, 1024, 1024)
    per = n_shard // tn
    if n_shard == n:
        out_shape = jax.ShapeDtypeStruct((m, n), F32)
        out_spec = pl.BlockSpec((tm, tn), _ij)
    else:
        out_shape = jax.ShapeDtypeStruct((N_CHIPS, m, n_shard), F32)
        out_spec = pl.BlockSpec((None, tm, tn), lambda i, j, k: (j // per, i, j % per))
    return _mm(name, a, g, mode="tn", grid=(m // tm, n // tn, S // tk), tm=tm, tn=tn,
               a_spec=pl.BlockSpec((tk, tm), lambda i, j, k: (k, i)),
               b_spec=pl.BlockSpec((tk, tn), lambda i, j, k: (k, j)),
               epi=_epi_store, out_shape=out_shape, out_specs=out_spec)


NORM_T = 512


def rmsnorm_fwd(x, g):
    S = x.shape[0]

    def body(x_ref, g_ref, o_ref):
        xv = x_ref[...]
        r = lax.rsqrt(jnp.mean(xv * xv, axis=-1, keepdims=True) + EPS)
        o_ref[...] = ((xv * r) * g_ref[...]).astype(BF16)

    return pl.pallas_call(
        body, name="rmsnorm_fwd", grid=(S // NORM_T,),
        in_specs=[pl.BlockSpec((NORM_T, D_MODEL), lambda i: (i, 0)), pl.BlockSpec((1, D_MODEL), lambda i: (0, 0))],
        out_specs=pl.BlockSpec((NORM_T, D_MODEL), lambda i: (i, 0)),
        out_shape=jax.ShapeDtypeStruct((S, D_MODEL), BF16), compiler_params=_cp("parallel"),
    )(x, g)


def rmsnorm_bwd(x, g, dxn, dres):
    S = x.shape[0]

    def body(x_ref, g_ref, dxn_ref, dres_ref, dx_ref, dxh_ref, dg_ref):
        @pl.when(pl.program_id(0) == 0)
        def _():
            dg_ref[...] = jnp.zeros_like(dg_ref)

        xv, gv, d = x_ref[...], g_ref[...], dxn_ref[...]
        r = lax.rsqrt(jnp.mean(xv * xv, axis=-1, keepdims=True) + EPS)
        gd = gv * d
        dx = dres_ref[...] + r * gd - xv * ((r * r * r) * jnp.mean(xv * gd, axis=-1, keepdims=True))
        dx_ref[...] = dx
        dxh_ref[...] = dx.astype(BF16)
        dg_ref[...] += jnp.sum(d * (xv * r), axis=0, keepdims=True)

    row = pl.BlockSpec((NORM_T, D_MODEL), lambda i: (i, 0))
    vec = pl.BlockSpec((1, D_MODEL), lambda i: (0, 0))
    return pl.pallas_call(
        body, name="rmsnorm_bwd", grid=(S // NORM_T,),
        in_specs=[row, vec, row, row], out_specs=(row, row, vec),
        out_shape=(jax.ShapeDtypeStruct((S, D_MODEL), F32), jax.ShapeDtypeStruct((S, D_MODEL), BF16),
                   jax.ShapeDtypeStruct((1, D_MODEL), F32)),
        compiler_params=_cp("arbitrary"),
    )(x, g, dxn, dres)


def loss_head(y, tgt):
    S = y.shape[0]

    def body(y_ref, t_ref, dy_ref, dyh_ref, l_ref):
        @pl.when(pl.program_id(0) == 0)
        def _():
            l_ref[...] = jnp.zeros_like(l_ref)

        e = y_ref[...] - t_ref[...]
        dy_ref[...] = e * (1.0 / D_MODEL)
        dyh_ref[...] = (e * (1.0 / D_MODEL)).astype(BF16)
        l_ref[...] += jnp.sum(e * e) * (0.5 / D_MODEL)

    row = pl.BlockSpec((NORM_T, D_MODEL), lambda i: (i, 0))
    return pl.pallas_call(
        body, name="loss_head", grid=(S // NORM_T,),
        in_specs=[row, row], out_specs=(row, row, pl.BlockSpec((1, 128), lambda i: (0, 0))),
        out_shape=(jax.ShapeDtypeStruct((S, D_MODEL), F32), jax.ShapeDtypeStruct((S, D_MODEL), BF16),
                   jax.ShapeDtypeStruct((1, 128), F32)),
        compiler_params=_cp("arbitrary"),
    )(y, tgt)


def _hg_block(qs, vs, zs, lb, sT, reverse):
    n = len(qs)
    row = lax.broadcasted_iota(jnp.int32, (HG_C, HG_C), 0)
    col = lax.broadcasted_iota(jnp.int32, (HG_C, HG_C), 1)
    tri = (row <= col) if reverse else (row >= col)
    m = tri.astype(F32)
    rsel = lax.broadcasted_iota(jnp.int32, (HG_C, 128), 0)
    ref_rows = ((rsel >= HG_C // 2) if reverse else (rsel <= HG_C // 2)).astype(F32)
    att, qdec, upd, keep = [None] * n, [None] * n, [None] * n, [None] * n
    for c in range(n):
        f = lb + (1.0 - lb) * _sigmoid(zs[c])
        kc = 1.0 - f
        lc = jnp.log(f)
        b = cumdot(m, lc)
        btot = jnp.sum(lc, axis=0, keepdims=True)
        bref = lax.stop_gradient(jnp.sum(lc * ref_rows, axis=0, keepdims=True))
        qe = qs[c] * jnp.exp(jnp.minimum(b - bref, 80.0))
        ke = kc * jnp.exp(jnp.minimum(bref - b, 80.0))
        att[c] = jnp.where(tri, dot_nt(qe, ke), 0.0)
        qdec[c] = qs[c] * jnp.exp(b)
        upd[c] = dot_tn(vs[c], kc * jnp.exp(btot - b))
        keep[c] = jnp.exp(btot)
    states = [None] * n
    for c in (range(n - 1, -1, -1) if reverse else range(n)):
        states[c] = sT
        sT = sT * keep[c] + upd[c]
    outs = [dot_nn(att[c], vs[c]) + dot_nt(qdec[c], states[c]) for c in range(n)]
    return outs, sT


def _chunks(ref, c, n):
    return [ref[i * c:(i + 1) * c, :] for i in range(n)]


def hg_scan_fwd(h, lbf, lbb):
    S = h.shape[0]
    nb = S // HG_T
    n = HG_T // HG_C

    def body(ins, outs, scr):
        qf, vf, zf, qb, vb, zb, lbf_ref, lbb_ref = ins
        of_ref, ob_ref, sf_ref, sb_ref = outs
        stf, stb = scr

        @pl.when(pl.program_id(1) == 0)
        def _():
            stf[...] = jnp.zeros_like(stf)
            stb[...] = jnp.zeros_like(stb)

        for (q, v, z, lb_ref, o_ref, s_ref, st, rev) in ((qf, vf, zf, lbf_ref, of_ref, sf_ref, stf, False),
                                                         (qb, vb, zb, lbb_ref, ob_ref, sb_ref, stb, True)):
            s_ref[0, 0] = st[...]
            outs, s_new = _hg_block(_chunks(q, HG_C, n), _chunks(v, HG_C, n), _chunks(z, HG_C, n),
                                    lb_ref[...], st[...], rev)
            for c in range(n):
                o_ref[c * HG_C:(c + 1) * HG_C, :] = outs[c]
            st[...] = s_new

    def col(off, rev):
        return pl.BlockSpec((HG_T, 128), (lambda hh, t: (nb - 1 - t, off // 128 + hh)) if rev
                            else (lambda hh, t: (t, off // 128 + hh)))

    lb_spec = pl.BlockSpec((1, 128), lambda hh, t: (0, hh))
    st_f = pl.BlockSpec((1, 1, 128, 128), lambda hh, t: (hh, t, 0, 0))
    st_b = pl.BlockSpec((1, 1, 128, 128), lambda hh, t: (hh, nb - 1 - t, 0, 0))
    outs, carried = _hosted(
        "hg_scan_fwd", body, None, grid=(HG_HEADS, nb),
        in_specs=[col(OFF_HG_Q, False), col(OFF_HG_V, False), col(OFF_HG_ZF, False),
                  col(OFF_HG_Q, True), col(OFF_HG_V, True), col(OFF_HG_ZB, True), lb_spec, lb_spec],
        out_specs=[col(0, False), col(0, True), st_f, st_b],
        out_shape=[jax.ShapeDtypeStruct((S, HG_W), F32), jax.ShapeDtypeStruct((S, HG_W), F32),
                   jax.ShapeDtypeStruct((HG_HEADS, nb, 128, 128), F32),
                   jax.ShapeDtypeStruct((HG_HEADS, nb, 128, 128), F32)],
        scratch_shapes=[pltpu.VMEM((128, 128), F32), pltpu.VMEM((128, 128), F32)],
        sem=("arbitrary", "arbitrary"), operands=(h, h, h, h, h, h, lbf, lbb))
    return tuple(outs)


def hg_scan_bwd(h, lbf, lbb, do, sf, sb, rider=None):
    S = h.shape[0]
    nb = S // HG_T
    n = HG_T // HG_C

    def body(ins, outs, scr):
        qf, vf, zf, dof, sfin, qb, vb, zb, dob, sbin, lbf_ref, lbb_ref = ins
        dqf, dvf, dzf, dqb, dvb, dzb, dlbf, dlbb = outs
        dsf, dsb = scr

        @pl.when(pl.program_id(1) == 0)
        def _():
            for r in (dsf, dsb, dlbf, dlbb):
                r[...] = jnp.zeros_like(r)

        for (q, v, z, dor, sin, lb_ref, dq, dv, dz, dlb, ds, rev) in (
                (qf, vf, zf, dof, sfin, lbf_ref, dqf, dvf, dzf, dlbf, dsf, False),
                (qb, vb, zb, dob, sbin, lbb_ref, dqb, dvb, dzb, dlbb, dsb, True)):
            fn = functools.partial(_hg_block, reverse=rev)
            _, vjp = jax.vjp(fn, _chunks(q, HG_C, n), _chunks(v, HG_C, n), _chunks(z, HG_C, n), lb_ref[...], sin[0, 0])
            dqs, dvs, dzs, dlb_v, ds_in = vjp((_chunks(dor, HG_C, n), ds[...]))
            for c in range(n):
                sl = slice(c * HG_C, (c + 1) * HG_C)
                dq[sl, :] = dqs[c]
                dv[sl, :] = dvs[c]
                dz[sl, :] = dzs[c]
            dlb[...] += dlb_v
            ds[...] = ds_in

    def col(off, fwd_scan):
        return pl.BlockSpec((HG_T, 128), (lambda hh, t: (nb - 1 - t, off // 128 + hh)) if fwd_scan
                            else (lambda hh, t: (t, off // 128 + hh)))

    lb_spec = pl.BlockSpec((1, 128), lambda hh, t: (0, hh))
    st_f = pl.BlockSpec((1, 1, 128, 128), lambda hh, t: (hh, nb - 1 - t, 0, 0))
    st_b = pl.BlockSpec((1, 1, 128, 128), lambda hh, t: (hh, t, 0, 0))
    full = jax.ShapeDtypeStruct((S, HG_W), F32)
    vec = jax.ShapeDtypeStruct((1, HG_W), F32)
    outs, carried = _hosted(
        "hg_scan_bwd", body, rider, grid=(HG_HEADS, nb),
        in_specs=[col(OFF_HG_Q, True), col(OFF_HG_V, True), col(OFF_HG_ZF, True), col(0, True), st_f,
                  col(OFF_HG_Q, False), col(OFF_HG_V, False), col(OFF_HG_ZB, False), col(0, False), st_b,
                  lb_spec, lb_spec],
        out_specs=[col(0, True), col(0, True), col(0, True), col(0, False), col(0, False), col(0, False),
                   lb_spec, lb_spec],
        out_shape=[full, full, full, full, full, full, vec, vec],
        scratch_shapes=[pltpu.VMEM((128, 128), F32), pltpu.VMEM((128, 128), F32)],
        sem=("arbitrary", "arbitrary"), operands=(h, h, h, do, sf, h, h, h, do, sb, lbf, lbb))
    return (tuple(outs), carried) if rider else tuple(outs)


GN_T = 1024


def _gated_norm(o, gate, g, center):
    if center:
        o = o - jnp.mean(o, axis=-1, keepdims=True)
    o = o * lax.rsqrt(jnp.mean(o * o, axis=-1, keepdims=True) + EPS)
    return (o * g) * (gate * _sigmoid(gate))


def gated_norm_fwd(name, of, ob, h, gate_off, g, center, y, y_off):
    S = of.shape[0]

    def body(of_ref, ob_ref, gate_ref, g_ref, *rest):
        rest[-1][...] = _gated_norm(of_ref[...] + ob_ref[...], gate_ref[...], g_ref[...], center).astype(BF16)

    blk = pl.BlockSpec((GN_T, 128), lambda hh, i: (i, hh))
    return pl.pallas_call(
        body, name=name, grid=(6, S // GN_T),
        in_specs=[blk, blk, pl.BlockSpec((GN_T, 128), lambda hh, i: (i, gate_off // 128 + hh)),
                  pl.BlockSpec((1, 128), lambda hh, i: (0, hh))] + ([] if y is None else [ANY]),
        out_specs=pl.BlockSpec((GN_T, 128), lambda hh, i: (i, y_off // 128 + hh)),
        out_shape=jax.ShapeDtypeStruct((S, D_MODEL), BF16),
        input_output_aliases={} if y is None else {4: 0},
        compiler_params=_cp("parallel", "parallel"),
    )(*((of, ob, h, g) + (() if y is None else (y,))))


def gated_norm_bwd(name, of, ob, h, gate_off, g, dy, dy_off, center):
    S = of.shape[0]

    def body(of_ref, ob_ref, gate_ref, g_ref, dy_ref, do_ref, dgate_ref, dg_ref):
        @pl.when(pl.program_id(1) == 0)
        def _():
            dg_ref[...] = jnp.zeros_like(dg_ref)

        fn = functools.partial(_gated_norm, center=center)
        _, vjp = jax.vjp(fn, of_ref[...] + ob_ref[...], gate_ref[...], g_ref[...])
        do, dgate, dg = vjp(dy_ref[...])
        do_ref[...] = do
        dgate_ref[...] = dgate
        dg_ref[...] += dg

    blk = pl.BlockSpec((GN_T, 128), lambda hh, i: (i, hh))
    vec = pl.BlockSpec((1, 128), lambda hh, i: (0, hh))
    return pl.pallas_call(
        body, name=name, grid=(6, S // GN_T),
        in_specs=[blk, blk, pl.BlockSpec((GN_T, 128), lambda hh, i: (i, gate_off // 128 + hh)), vec,
                  pl.BlockSpec((GN_T, 128), lambda hh, i: (i, dy_off // 128 + hh))],
        out_specs=(blk, blk, vec),
        out_shape=(jax.ShapeDtypeStruct((S, 768), F32), jax.ShapeDtypeStruct((S, 768), F32),
                   jax.ShapeDtypeStruct((1, 768), F32)),
        compiler_params=_cp("arbitrary", "arbitrary"),
    )(of, ob, h, g, dy)


def _ret_consts(S):
    half = RET_DK // 2
    inv = ROPE_BASE ** (-jnp.arange(half, dtype=F32) / half)
    ang = jnp.arange(S, dtype=F32)[:, None] * inv[None, :]
    cos, sin = jnp.cos(ang), jnp.sin(ang)
    cos_t = jnp.tile(jnp.concatenate([cos, cos], axis=1), (1, RET_HEADS))
    sin_t = jnp.tile(jnp.concatenate([-sin, sin], axis=1), (1, RET_HEADS))
    hidx = jnp.arange(RET_HEADS, dtype=F32)
    lg_f = jnp.log1p(-jnp.exp2(-5.0 - hidx))
    C = RET_CHUNK
    idx = jnp.arange(C, dtype=F32)
    rel = idx[:, None] - idx[None, :]

    def one(lg, reverse):
        lgc = lg[:, None]
        decay = jnp.where(rel >= 0, jnp.exp(lgc[:, :, None] * jnp.maximum(rel, 0.0)), 0.0)
        zeta = jnp.exp(lgc * (C - 1 - idx))
        xi = jnp.exp(lgc * (idx + 1))
        if reverse:
            decay = decay[:, ::-1, ::-1]
            zeta, xi = zeta[:, ::-1], xi[:, ::-1]
        wide = lambda t: jnp.repeat(t.T, RET_DK, axis=1)
        gam_w = jnp.broadcast_to(jnp.repeat(jnp.exp(lg * C), 128)[None, :], (8, RET_W))
        return decay, wide(xi), wide(zeta), gam_w

    hm = (jnp.arange(RET_QK_W)[None, :] // RET_DK == jnp.arange(8)[:, None]).astype(F32)
    return (cos_t, sin_t, hm) + one(lg_f, False) + one(lg_f[::-1], True)


def _rope(t, cos, sin_signed):
    lane = lax.broadcasted_iota(jnp.int32, t.shape, 1)
    first = (lane & (RET_DK - 1)) < RET_DK // 2
    partner = jnp.where(first, pltpu.roll(t, RET_QK_W - RET_DK // 2, 1), pltpu.roll(t, RET_DK // 2, 1))
    return t * cos + partner * sin_signed


def _ret_block(qs, ks, vs, st, dec, xi, zeta, gam, hms, reverse):
    n = len(qs)
    heads = range(RET_HEADS)
    tile = lambda hh: hh * RET_DK // RET_TILE
    qx = [[q * x for q, x in zip(qs[c], xi)] for c in range(n)]
    kz = [[k * z for k, z in zip(ks[c], zeta)] for c in range(n)]
    sc = [[dot_nt(qs[c][tile(hh)] * hms[hh], ks[c][tile(hh)]) * dec[hh] for hh in heads] for c in range(n)]
    upd = [[dot_tn(kz[c][tile(hh)] * hms[hh], vs[c][hh]) for hh in heads] for c in range(n)]
    st = list(st)
    seen = [None] * n
    for c in (range(n - 1, -1, -1) if reverse else range(n)):
        seen[c] = list(st)
        st = [st[hh] * gam[hh] + upd[c][hh] for hh in heads]
    outs = [[dot_nn(sc[c][hh], vs[c][hh]) + dot_nn(qx[c][tile(hh)], seen[c][hh]) for hh in heads] for c in range(n)]
    return outs, st


def _ret_inputs(q_ref, k_ref, v_ref, cos_ref, sin_ref):
    n = RET_T // RET_CHUNK
    qr = _rope(q_ref[...], cos_ref[...], sin_ref[...])
    kr = _rope(k_ref[...], cos_ref[...], sin_ref[...]) * (RET_DK ** -0.5)
    tiles = lambda t, c: [t[c * RET_CHUNK:(c + 1) * RET_CHUNK, p * RET_TILE:(p + 1) * RET_TILE]
                          for p in range(RET_QK_W // RET_TILE)]
    qs = [tiles(qr, c) for c in range(n)]
    ks = [tiles(kr, c) for c in range(n)]
    vs = [[v_ref[c * RET_CHUNK:(c + 1) * RET_CHUNK, hh * 128:(hh + 1) * 128] for hh in range(RET_HEADS)]
          for c in range(n)]
    return qs, ks, vs


def _ret_dir_consts(dec_ref, xi_ref, zeta_ref, gam_ref, hm_ref):
    dec = [dec_ref[hh] for hh in range(RET_HEADS)]
    gam = [gam_ref[0:1, hh * 128:(hh + 1) * 128] for hh in range(RET_HEADS)]
    lanes = lambda p: slice(p * RET_TILE, (p + 1) * RET_TILE)
    hms = [hm_ref[hh:hh + 1, lanes(hh * RET_DK // RET_TILE)] for hh in range(RET_HEADS)]
    n_tiles = RET_QK_W // RET_TILE
    return (dec, [xi_ref[:, lanes(p)] for p in range(n_tiles)], [zeta_ref[:, lanes(p)] for p in range(n_tiles)],
            gam, hms)


def _ret_rows(nb, rev):
    def rows(width, colblk):
        return pl.BlockSpec((RET_T, width), (lambda t: (nb - 1 - t, colblk)) if rev else (lambda t: (t, colblk)))
    return rows


def _const_spec(shape):
    nd = len(shape)
    return pl.BlockSpec(shape, lambda t: (0,) * nd)


def ret_scan_fwd(h, consts):
    S = h.shape[0]
    nb = S // RET_T
    n = RET_T // RET_CHUNK
    cos_t, sin_t, hm, dec_f, xi_f, zeta_f, gam_f, dec_b, xi_b, zeta_b, gam_b = consts

    def body(ins, outs, scr):
        qf, kf, vf, cf, sf, qb, kb, vb, cb, sb_, hm_ref, decf, xif, zetaf, gamf, decb, xib, zetab, gamb = ins
        of_ref, ob_ref, sfo, sbo = outs
        stf, stb = scr

        @pl.when(pl.program_id(0) == 0)
        def _():
            stf[...] = jnp.zeros_like(stf)
            stb[...] = jnp.zeros_like(stb)

        for (q, k, v, cs, sn, dr, xr, zr, gr, o_ref, so, st, rev) in (
                (qf, kf, vf, cf, sf, decf, xif, zetaf, gamf, of_ref, sfo, stf, False),
                (qb, kb, vb, cb, sb_, decb, xib, zetab, gamb, ob_ref, sbo, stb, True)):
            so[0] = st[...]
            qs, ks, vs = _ret_inputs(q, k, v, cs, sn)
            dec, xi, zeta, gam, hms = _ret_dir_consts(dr, xr, zr, gr, hm_ref)
            st_in = [st[:, hh * 128:(hh + 1) * 128] for hh in range(RET_HEADS)]
            outs, st_new = _ret_block(qs, ks, vs, st_in, dec, xi, zeta, gam, hms, rev)
            for c in range(n):
                for hh in range(RET_HEADS):
                    o_ref[c * RET_CHUNK:(c + 1) * RET_CHUNK, hh * 128:(hh + 1) * 128] = outs[c][hh]
            for hh in range(RET_HEADS):
                st[:, hh * 128:(hh + 1) * 128] = st_new[hh]

    rf, rb = _ret_rows(nb, False), _ret_rows(nb, True)
    cspecs = [_const_spec(a.shape) for a in (hm, dec_f, xi_f, zeta_f, gam_f, dec_b, xi_b, zeta_b, gam_b)]
    st_shape = jax.ShapeDtypeStruct((nb, RET_TILE, RET_W), F32)
    qc, kc, vc = OFF_RET_Q // RET_QK_W, OFF_RET_K // RET_QK_W, OFF_RET_V // RET_W
    outs, carried = _hosted(
        "ret_scan_fwd", body, None, grid=(nb,),
        in_specs=[rf(RET_QK_W, qc), rf(RET_QK_W, kc), rf(RET_W, vc), rf(RET_QK_W, 0), rf(RET_QK_W, 0),
                  rb(RET_QK_W, qc), rb(RET_QK_W, kc), rb(RET_W, vc), rb(RET_QK_W, 0), rb(RET_QK_W, 0)] + cspecs,
        out_specs=[rf(RET_W, 0), rb(RET_W, 0),
                   pl.BlockSpec((1, RET_TILE, RET_W), lambda t: (t, 0, 0)),
                   pl.BlockSpec((1, RET_TILE, RET_W), lambda t: (nb - 1 - t, 0, 0))],
        out_shape=[jax.ShapeDtypeStruct((S, RET_W), F32), jax.ShapeDtypeStruct((S, RET_W), F32), st_shape, st_shape],
        scratch_shapes=[pltpu.VMEM((RET_TILE, RET_W), F32), pltpu.VMEM((RET_TILE, RET_W), F32)],
        sem=("arbitrary",),
        operands=(h, h, h, cos_t, sin_t, h, h, h, cos_t, sin_t, hm, dec_f, xi_f, zeta_f, gam_f, dec_b, xi_b, zeta_b,
                  gam_b))
    return tuple(outs)


def ret_scan_bwd(h, consts, do, sf, sb, rider=None):
    S = h.shape[0]
    nb = S // RET_T
    n = RET_T // RET_CHUNK
    cos_t, sin_t, hm, dec_f, xi_f, zeta_f, gam_f, dec_b, xi_b, zeta_b, gam_b = consts

    def body(ins, outs, scr):
        (qf, kf, vf, cf, sf_, dof, sfin, qb, kb, vb, cb, sb_, dob, sbin,
         hm_ref, decf, xif, zetaf, gamf, decb, xib, zetab, gamb) = ins
        dqf, dkf, dvf, dqb, dkb, dvb = outs
        dsf, dsb = scr

        @pl.when(pl.program_id(0) == 0)
        def _():
            dsf[...] = jnp.zeros_like(dsf)
            dsb[...] = jnp.zeros_like(dsb)

        for (q, k, v, cs, sn, dor, sin, dr, xr, zr, gr, dq, dk, dv, ds, rev) in (
                (qf, kf, vf, cf, sf_, dof, sfin, decf, xif, zetaf, gamf, dqf, dkf, dvf, dsf, False),
                (qb, kb, vb, cb, sb_, dob, sbin, decb, xib, zetab, gamb, dqb, dkb, dvb, dsb, True)):
            qs, ks, vs = _ret_inputs(q, k, v, cs, sn)
            dec, xi, zeta, gam, hms = _ret_dir_consts(dr, xr, zr, gr, hm_ref)
            st_in = [sin[0, :, hh * 128:(hh + 1) * 128] for hh in range(RET_HEADS)]
            fn = lambda a, b_, c_, d_: _ret_block(a, b_, c_, d_, dec, xi, zeta, gam, hms, rev)
            _, vjp = jax.vjp(fn, qs, ks, vs, st_in)
            dos = [[dor[c * RET_CHUNK:(c + 1) * RET_CHUNK, hh * 128:(hh + 1) * 128] for hh in range(RET_HEADS)]
                   for c in range(n)]
            dst = [ds[:, hh * 128:(hh + 1) * 128] for hh in range(RET_HEADS)]
            dqs, dks, dvs, dst_in = vjp((dos, dst))
            cosv, sinv = cs[...], sn[...]
            whole = lambda parts: jnp.concatenate([jnp.concatenate(p, axis=1) for p in parts], axis=0)
            dq[...] = _rope(whole(dqs), cosv, -sinv)
            dk[...] = _rope(whole(dks) * (RET_DK ** -0.5), cosv, -sinv)
            for c in range(n):
                for hh in range(RET_HEADS):
                    dv[c * RET_CHUNK:(c + 1) * RET_CHUNK, hh * 128:(hh + 1) * 128] = dvs[c][hh]
            for hh in range(RET_HEADS):
                ds[:, hh * 128:(hh + 1) * 128] = dst_in[hh]

    rf, rb = _ret_rows(nb, True), _ret_rows(nb, False)
    cspecs = [_const_spec(a.shape) for a in (hm, dec_f, xi_f, zeta_f, gam_f, dec_b, xi_b, zeta_b, gam_b)]
    qk = jax.ShapeDtypeStruct((S, RET_QK_W), F32)
    vv = jax.ShapeDtypeStruct((S, RET_W), F32)
    qc, kc, vc = OFF_RET_Q // RET_QK_W, OFF_RET_K // RET_QK_W, OFF_RET_V // RET_W
    outs, carried = _hosted(
        "ret_scan_bwd", body, rider, grid=(nb,),
        in_specs=[rf(RET_QK_W, qc), rf(RET_QK_W, kc), rf(RET_W, vc), rf(RET_QK_W, 0), rf(RET_QK_W, 0), rf(RET_W, 0),
                  pl.BlockSpec((1, RET_TILE, RET_W), lambda t: (nb - 1 - t, 0, 0)),
                  rb(RET_QK_W, qc), rb(RET_QK_W, kc), rb(RET_W, vc), rb(RET_QK_W, 0), rb(RET_QK_W, 0), rb(RET_W, 0),
                  pl.BlockSpec((1, RET_TILE, RET_W), lambda t: (t, 0, 0))] + cspecs,
        out_specs=[rf(RET_QK_W, 0), rf(RET_QK_W, 0), rf(RET_W, 0), rb(RET_QK_W, 0), rb(RET_QK_W, 0), rb(RET_W, 0)],
        out_shape=[qk, qk, vv, qk, qk, vv],
        scratch_shapes=[pltpu.VMEM((RET_TILE, RET_W), F32), pltpu.VMEM((RET_TILE, RET_W), F32)],
        sem=("arbitrary",),
        operands=(h, h, h, cos_t, sin_t, do, sf, h, h, h, cos_t, sin_t, do, sb,
                  hm, dec_f, xi_f, zeta_f, gam_f, dec_b, xi_b, zeta_b, gam_b))
    return (tuple(outs), carried) if rider else tuple(outs)


def _t5_bucket(rel):
    nb = REL_BUCKETS // 2
    max_exact = nb // 2
    sign_off = jnp.where(rel > 0, nb, 0)
    n = jnp.abs(rel)
    nf = jnp.maximum(n, 1).astype(F32)
    large = max_exact + (jnp.log(nf / max_exact) / math.log(REL_MAX_DIST / max_exact)
                         * (nb - max_exact)).astype(jnp.int32)
    large = jnp.minimum(large, nb - 1)
    return sign_off + jnp.where(n < max_exact, n, large)


def _dil_buckets(dil):
    tq, tb = DIL_TQ, DIL_TQ + 2 * DIL_HALF
    rel_q = jnp.arange(tb)[None, :] - DIL_HALF - jnp.arange(tq)[:, None]
    rel_k = jnp.arange(tq)[None, :] + DIL_HALF - jnp.arange(tb)[:, None]
    return _t5_bucket(rel_q * dil), _t5_bucket(rel_k * dil)


def dil_view(h, g, dil):
    base = OFF_DIL + 3 * g * DIL_W
    if dil == 1:
        return h, IN_W, base
    return h[:, base:base + 3 * DIL_W].reshape(h.shape[0] // dil, dil * 3 * DIL_W), 3 * DIL_W, 0


def _dil_col(view, j):
    _, width, base = view
    return lambda r: (r * width + base + j * DIL_W) // DIL_W


def _dil_specs(L):
    nq = DIL_TQ // DIL_HALF
    last = L // DIL_HALF - 1

    def cur(colfn):
        return pl.BlockSpec((DIL_TQ, DIL_W), lambda r, n: (n, colfn(r)))

    def prev(colfn):
        return pl.BlockSpec((DIL_HALF, DIL_W), lambda r, n: (jnp.maximum(n * nq - 1, 0), colfn(r)))

    def nxt(colfn):
        return pl.BlockSpec((DIL_HALF, DIL_W), lambda r, n: (jnp.minimum((n + 1) * nq, last), colfn(r)))

    return prev, cur, nxt


def _slot(s):
    return slice(s * DIL_HD, (s + 1) * DIL_HD)


def _cat3(a, b, c, s):
    return jnp.concatenate([a[:, _slot(s)], b[:, _slot(s)], c[:, _slot(s)]], axis=0)


def dil_attn_fwd(view, S, g, dil, bias, qg, kg):
    L = S // dil
    hv = view[0]
    tb = DIL_TQ + 2 * DIL_HALF

    def body(q_ref, kp, kc, kn, vp, vc, vn, bias_ref, qg_ref, kg_ref, o_ref, lse_ref):
        n = pl.program_id(1)
        ii = lax.broadcasted_iota(jnp.int32, (DIL_TQ, tb), 0)
        jj = lax.broadcasted_iota(jnp.int32, (DIL_TQ, tb), 1)
        kabs = n * DIL_TQ - DIL_HALF + jj
        valid = (jnp.abs(jj - DIL_HALF - ii) <= DIL_HALF) & (kabs >= 0) & (kabs < L)
        for s in range(DIL_SLOTS):
            q = _head_rms(q_ref[:, _slot(s)], qg_ref[...]) * (DIL_HD ** -0.5)
            kb = _head_rms(_cat3(kp, kc, kn, s), kg_ref[...])
            sc = jnp.where(valid, _mxu(q, kb, NT) + bias_ref[s], NEG)
            m = jnp.max(sc, axis=-1, keepdims=True)
            p = jnp.exp(sc - m)
            den = jnp.sum(p, axis=-1, keepdims=True)
            o_ref[:, _slot(s)] = _mxu(p, _cat3(vp, vc, vn, s), NN) / den
            lse_ref[:, _slot(s)] = jnp.broadcast_to(m + jnp.log(den), (DIL_TQ, DIL_HD))

    prev, cur, nxt = _dil_specs(L)
    qc, kc_, vc_ = (_dil_col(view, j) for j in range(3))
    oc = lambda r: r
    vec = pl.BlockSpec((1, 128), lambda r, n: (0, 0))
    out = jax.ShapeDtypeStruct((L, dil * DIL_W), F32)
    o, lse = pl.pallas_call(
        body, name=f"dil_attn_fwd{g}", grid=(dil, L // DIL_TQ),
        in_specs=[cur(qc), prev(kc_), cur(kc_), nxt(kc_), prev(vc_), cur(vc_), nxt(vc_),
                  pl.BlockSpec((DIL_SLOTS, DIL_TQ, tb), lambda r, n: (0, 0, 0)), vec, vec],
        out_specs=(cur(oc), cur(oc)), out_shape=(out, out),
        compiler_params=_cp("parallel", "parallel"),
    )(hv, hv, hv, hv, hv, hv, hv, bias, qg, kg)
    return o.reshape(S, DIL_W), lse.reshape(S, DIL_W)


def dil_combine(os_, lses, y):
    S = os_[0].shape[0]

    def body(o1, o2, o3, l1, l2, l3, y_in, yc_ref, lt_ref, y_ref):
        a, b, c = l1[...], l2[...], l3[...]
        m = jnp.maximum(jnp.maximum(a, b), c)
        ea, eb, ec = jnp.exp(a - m), jnp.exp(b - m), jnp.exp(c - m)
        den = ea + eb + ec
        yc = (ea * o1[...] + eb * o2[...] + ec * o3[...]) / den
        yc_ref[...] = yc
        y_ref[...] = yc.astype(BF16)
        lt_ref[...] = m + jnp.log(den)

    blk = pl.BlockSpec((GN_T, DIL_W), lambda i: (i, 0))
    out = jax.ShapeDtypeStruct((S, DIL_W), F32)
    return pl.pallas_call(
        body, name="dil_combine", grid=(S // GN_T,), in_specs=[blk] * 6 + [ANY],
        out_specs=(blk, blk, pl.BlockSpec((GN_T, DIL_W), lambda i: (i, (HG_W + RET_W) // DIL_W))),
        out_shape=(out, out, jax.ShapeDtypeStruct(y.shape, y.dtype)), input_output_aliases={6: 2},
        compiler_params=_cp("parallel"),
    )(*os_, *lses, y)


def dil_delta(dy, yc):
    S = yc.shape[0]

    def body(dy_ref, y_ref, d_ref):
        d_ref[...] = jnp.broadcast_to(jnp.sum(dy_ref[...] * y_ref[...], axis=-1, keepdims=True), (GN_T, 128))

    return pl.pallas_call(
        body, name="dil_delta", grid=(S // GN_T, DIL_SLOTS),
        in_specs=[pl.BlockSpec((GN_T, 128), lambda i, s: (i, (HG_W + RET_W) // 128 + s)),
                  pl.BlockSpec((GN_T, 128), lambda i, s: (i, s))],
        out_specs=pl.BlockSpec((GN_T, 128), lambda i, s: (i, s)),
        out_shape=jax.ShapeDtypeStruct((S, DIL_W), F32), compiler_params=_cp("parallel", "parallel"),
    )(dy, yc)


def dil_attn_bwd(view, S, g, dil, bias_q, bias_k, qg, kg, dy, lse_t, delta):
    L = S // dil
    hv = view[0]
    if dil == 1:
        dyv, dyc = dy, lambda r: (HG_W + RET_W) // DIL_W
    else:
        dyv, dyc = dy[:, HG_W + RET_W:].reshape(L, dil * DIL_W), lambda r: r
    lv = lse_t.reshape(L, dil * DIL_W)
    dv_ = delta.reshape(L, dil * DIL_W)
    tq, tb = DIL_TQ, DIL_TQ + 2 * DIL_HALF
    scale = DIL_HD ** -0.5

    def body(qp, qc, qn, kp, kc, kn, vp, vc, vn, dp_, dc, dn, lp, lc, ln, ep, ec, en, bq_ref, bk_ref, qg_ref, kg_ref,
             dq_ref, dk_ref, dv_ref, dbias_ref, dqg_ref, dkg_ref):
        r, n = pl.program_id(0), pl.program_id(1)

        @pl.when((r == 0) & (n == 0))
        def _():
            for ref in (dbias_ref, dqg_ref, dkg_ref):
                ref[...] = jnp.zeros_like(ref)

        qgv, kgv = qg_ref[...], kg_ref[...]
        qfn = lambda t, gg: _head_rms(t, gg) * scale
        ii = lax.broadcasted_iota(jnp.int32, (tq, tb), 0)
        jj = lax.broadcasted_iota(jnp.int32, (tq, tb), 1)
        kabs = n * tq - DIL_HALF + jj
        valid = (jnp.abs(jj - DIL_HALF - ii) <= DIL_HALF) & (kabs >= 0) & (kabs < L)
        i2 = lax.broadcasted_iota(jnp.int32, (tb, tq), 0)
        j2 = lax.broadcasted_iota(jnp.int32, (tb, tq), 1)
        qabs = n * tq - DIL_HALF + i2
        valid2 = (jnp.abs(j2 + DIL_HALF - i2) <= DIL_HALF) & (qabs >= 0) & (qabs < L)
        for s in range(DIL_SLOTS):
            sl = _slot(s)
            one = slice(s * DIL_HD, s * DIL_HD + 1)
            qn_c, q_vjp = jax.vjp(qfn, qc[:, sl], qgv)
            k_band = _head_rms(_cat3(kp, kc, kn, s), kgv)
            sc = _mxu(qn_c, k_band, NT) + bq_ref[s]
            p = jnp.where(valid, jnp.exp(jnp.where(valid, sc, NEG) - lc[:, one]), 0.0)
            ds = p * (_mxu(dc[:, sl], _cat3(vp, vc, vn, s), NT) - ec[:, one])
            dbias_ref[s] += ds
            dq, dqg = q_vjp(_mxu(ds, k_band, NN))
            dq_ref[:, sl] = dq
            dqg_ref[s] += dqg
            kn_c, k_vjp = jax.vjp(_head_rms, kc[:, sl], kgv)
            q_band = qfn(_cat3(qp, qc, qn, s), qgv)
            do_band = _cat3(dp_, dc, dn, s)
            s2 = _mxu(q_band, kn_c, NT) + bk_ref[s]
            lse_band = jnp.concatenate([lp[:, one], lc[:, one], ln[:, one]], axis=0)
            delta_band = jnp.concatenate([ep[:, one], ec[:, one], en[:, one]], axis=0)
            p2 = jnp.where(valid2, jnp.exp(jnp.where(valid2, s2, NEG) - lse_band), 0.0)
            dv_ref[:, sl] = _mxu(p2, do_band, TN)
            ds2 = p2 * (_mxu(do_band, vc[:, sl], NT) - delta_band)
            dk, dkg = k_vjp(_mxu(ds2, q_band, TN))
            dk_ref[:, sl] = dk
            dkg_ref[s] += dkg

    prev, cur, nxt = _dil_specs(L)
    three = lambda colfn: [prev(colfn), cur(colfn), nxt(colfn)]
    qc_, kc_, vc_ = (_dil_col(view, j) for j in range(3))
    oc = lambda r: r
    vec = pl.BlockSpec((1, 128), lambda r, n: (0, 0))
    acc_vec = pl.BlockSpec((DIL_SLOTS, 1, 128), lambda r, n: (0, 0, 0))
    out = jax.ShapeDtypeStruct((L, dil * DIL_W), F32)
    dq, dk, dv, dbias, dqg, dkg = pl.pallas_call(
        body, name=f"dil_attn_bwd{g}", grid=(dil, L // tq),
        in_specs=three(qc_) + three(kc_) + three(vc_) + three(dyc) + three(oc) + three(oc)
        + [pl.BlockSpec((DIL_SLOTS, tq, tb), lambda r, n: (0, 0, 0)),
           pl.BlockSpec((DIL_SLOTS, tb, tq), lambda r, n: (0, 0, 0)), vec, vec],
        out_specs=(cur(oc), cur(oc), cur(oc), pl.BlockSpec((DIL_SLOTS, tq, tb), lambda r, n: (0, 0, 0)),
                   acc_vec, acc_vec),
        out_shape=(out, out, out, jax.ShapeDtypeStruct((DIL_SLOTS, tq, tb), F32),
                   jax.ShapeDtypeStruct((DIL_SLOTS, 1, 128), F32), jax.ShapeDtypeStruct((DIL_SLOTS, 1, 128), F32)),
        compiler_params=_cp("arbitrary", "arbitrary"),
    )(hv, hv, hv, hv, hv, hv, hv, hv, hv, dyv, dyv, dyv, lv, lv, lv, dv_, dv_, dv_, bias_q, bias_k, qg, kg)
    return dq.reshape(S, DIL_W), dk.reshape(S, DIL_W), dv.reshape(S, DIL_W), dbias, dqg, dkg


def _lb_eff(p):
    a = jnp.cumsum(jax.nn.softmax(p.astype(F32), axis=0), axis=0)
    return a - a[0:1]


def _dil_bias(rel_bias, g, dil):
    tbl = rel_bias[:, g * DIL_SLOTS:(g + 1) * DIL_SLOTS]
    return tuple(jnp.einsum("ijb,bs->sij", jax.nn.one_hot(b, REL_BUCKETS, dtype=F32), tbl,
                            precision=lax.Precision.HIGHEST) for b in _dil_buckets(dil))


def _big_weights(w):
    return w[0], w[1].reshape(D_MODEL, D_MODEL), w[2], w[3].reshape(D_FF, D_MODEL)


def _layer_fwd(x, l, prm, wts, rc, biases, gatherer=None):
    def carrying(name, call, *args):
        rider = gatherer.rider(l, name) if gatherer else None
        if rider is None:
            return call(*args)
        res, arrays = call(*args, rider)
        gatherer.done(l, name, arrays)
        return res

    weight = lambda t: _big_weights(wts[l])[t]
    row = lambda a: a[l][None]
    xn = rmsnorm_fwd(x, row(prm["norm_mix"]))
    h = carrying("in", proj_in, xn, weight(0))
    hof, hob, hsf, hsb = hg_scan_fwd(h, row(prm["lbf"]), row(prm["lbb"]))
    y = gated_norm_fwd("hg_out", hof, hob, h, OFF_HG_GATE, row(prm["hg_norm"]), False, None, 0)
    rof, rob, rsf, rsb = ret_scan_fwd(h, rc)
    y = gated_norm_fwd("ret_out", rof, rob, h, OFF_RET_GATE, row(prm["ret_norm"]), True, y, HG_W)
    os_, lses, views = [], [], []
    for g, (_, dil) in enumerate(DIL_GROUPS):
        views.append(dil_view(h, g, dil))
        o, lse = dil_attn_fwd(views[g], h.shape[0], g, dil, biases[g][0], row(prm["q_norm"]), row(prm["k_norm"]))
        os_.append(o)
        lses.append(lse)
    yc, lse_t, y = dil_combine(os_, lses, y)
    if gatherer:
        gatherer.alone(l, "mid")
    x2 = proj_out(y, weight(1), x)
    hm = rmsnorm_fwd(x2, row(prm["norm_mlp"]))
    u, act = carrying("up", proj_up, hm, weight(2))
    x3 = carrying("down", proj_down, act, weight(3), x2)
    saved = dict(x=x, xn=xn, h=h, hof=hof, hob=hob, hsf=hsf, hsb=hsb, rof=rof, rob=rob, rsf=rsf, rsb=rsb,
                 yc=yc, lse_t=lse_t, y=y, x2=x2, hm=hm, u=u, act=act, views=views)
    return x3, saved


def _layer_bwd(dx3, dx3h, l, prm, wts, rc, biases, sv, reducer=None):
    def carrying(stage, group, call, *args):
        rider = getattr(reducer, stage + "_rider")(group) if reducer else None
        if rider is None:
            return call(*args)
        res, arrived = call(*args, rider)
        getattr(reducer, stage + "_done")(group, arrived)
        return res

    early, late = GradReducer.EARLY, GradReducer.LATE
    win_g, wout_g, wup_g, wdown_g = _big_weights(wts[l])
    row = lambda a: a[l][None]
    h = sv["h"]
    du = carrying("pair", late, bwd_down_act, dx3h, wdown_g, sv["u"])
    g_down = wgrad("wgrad_down", sv["act"], dx3h, m=D_FF, n=D_MODEL, n_shard=D_MODEL)
    dhm = bwd_up(du, wup_g)
    g_up = wgrad("wgrad_up", sv["hm"], du, m=D_MODEL, n=D_FF, n_shard=FF_SHARD)
    dx2, dx2h, dg_mlp = rmsnorm_bwd(sv["x2"], row(prm["norm_mlp"]), dhm, dx3)
    dy = bwd_out(dx2h, wout_g)
    g_out = wgrad("wgrad_out", sv["y"], dx2h, m=D_MODEL, n=D_MODEL, n_shard=D_MODEL)
    g_out, g_down = g_out.reshape(N_CHIPS, D_MODEL // N_CHIPS, D_MODEL), g_down.reshape(N_CHIPS, D_FF // N_CHIPS, D_MODEL)
    if reducer:
        reducer.push(early, l, (g_out, g_up, g_down))
    hdo, hdgate, dg_hg = gated_norm_bwd("hg_out_bwd", sv["hof"], sv["hob"], h, OFF_HG_GATE, row(prm["hg_norm"]),
                                        dy, 0, False)
    hdqf, hdvf, hdzf, hdqb, hdvb, hdzb, dlbf, dlbb = carrying(
        "chip", late, hg_scan_bwd, h, row(prm["lbf"]), row(prm["lbb"]), hdo, sv["hsf"], sv["hsb"])
    rdo, rdgate, dg_ret = gated_norm_bwd("ret_out_bwd", sv["rof"], sv["rob"], h, OFF_RET_GATE, row(prm["ret_norm"]),
                                         dy, HG_W, True)
    rdqf, rdkf, rdvf, rdqb, rdkb, rdvb = carrying("pair", early, ret_scan_bwd, h, rc, rdo, sv["rsf"], sv["rsb"])
    delta = dil_delta(dy, sv["yc"])
    dil_parts, dbiases = [], []
    dqg = jnp.zeros((1, DIL_HD), F32)
    dkg = jnp.zeros((1, DIL_HD), F32)
    for g, (_, dil) in enumerate(DIL_GROUPS):
        dq, dk, dv, dbias, dqg_g, dkg_g = dil_attn_bwd(sv["views"][g], h.shape[0], g, dil, biases[g][0], biases[g][1],
                                                       row(prm["q_norm"]), row(prm["k_norm"]), dy, sv["lse_t"], delta)
        dil_parts += [dq, dk, dv]
        dbiases.append(dbias)
        dqg = dqg + jnp.sum(dqg_g, axis=0)
        dkg = dkg + jnp.sum(dkg_g, axis=0)
    dh = jnp.concatenate([hdqf + hdqb, hdvf + hdvb, hdzf, hdzb, hdgate,
                          rdqf + rdqb, rdkf + rdkb, rdvf + rdvb, rdgate] + dil_parts, axis=1).astype(BF16)
    dxn = carrying("chip", early, bwd_in, dh, win_g)
    g_in = wgrad("wgrad_in", sv["xn"], dh, m=D_MODEL, n=IN_W, n_shard=IN_SHARD)
    dx, dxh, dg_mix = rmsnorm_bwd(sv["x"], row(prm["norm_mix"]), dxn, dx2)
    small = dict(norm_mix=dg_mix, norm_mlp=dg_mlp, lbf=dlbf, lbb=dlbb, hg_norm=dg_hg, ret_norm=dg_ret,
                 q_norm=dqg, k_norm=dkg)
    if reducer:
        reducer.push(late, l, (g_in,))
    return (dx, dxh), (g_in, g_out, g_up, g_down), small, dbiases


def _rel_bias_grad(dbias_layers):
    cols = []
    for g, (_, dil) in enumerate(DIL_GROUPS):
        bq, _ = _dil_buckets(dil)
        onehot = jax.nn.one_hot(bq, REL_BUCKETS, dtype=F32)
        tot = dbias_layers[0][g]
        for d in dbias_layers[1:]:
            tot = tot + d[g]
        cols.append(jnp.einsum("sij,ijb->bs", tot, onehot, precision=lax.Precision.HIGHEST))
    return jnp.concatenate(cols, axis=1)


def local_step(x, tgt, wts, prm_in, reducer=None, gatherer=None):
    S = x.shape[0]
    prm = dict(prm_in)
    prm["lbf"], lbf_vjp = jax.vjp(_lb_eff, prm_in["hg_lb_fwd"])
    prm["lbb"], lbb_vjp = jax.vjp(_lb_eff, prm_in["hg_lb_bwd"])
    rc = _ret_consts(S)
    biases = [_dil_bias(prm["rel_bias"], g, dil) for g, (_, dil) in enumerate(DIL_GROUPS)]
    saved = []
    for l in range(DEPTH):
        x, sv = _layer_fwd(x, l, prm, wts, rc, biases, gatherer)
        saved.append(sv)
    dx, dxh, loss_row = loss_head(x, tgt)
    big, small, dbias_layers = [None] * DEPTH, [None] * DEPTH, [None] * DEPTH
    for l in range(DEPTH - 1, -1, -1):
        (dx, dxh), big[l], small[l], dbias_layers[l] = _layer_bwd(dx, dxh, l, prm, wts, rc, biases, saved[l], reducer)
    sg = {k: jnp.concatenate([small[l][k] for l in range(DEPTH)], axis=0) for k in small[0]}
    sg["rel_bias"] = _rel_bias_grad(dbias_layers)
    return loss_row[0, 0], dx, (reducer.finish() if reducer else big), sg, (lbf_vjp, lbb_vjp)


def _place():
    x, y, c = lax.axis_index("x"), lax.axis_index("y"), lax.axis_index("c")
    rels = [(1 - x, y), (x, 1 - y), (1 - x, 1 - y)]
    return x, y, c, 2 * x + y, rels


def _half(c, rows):
    return pl.ds(pl.multiple_of(c * (rows // 2), 16), rows // 2)


def place_own(name, p_arr, w, l):
    _, rows, cols = w.shape
    tr = 512

    def body(p_ref, w_ref, o_ref):
        o_ref[...] = w_ref[...].astype(BF16)

    return pl.pallas_call(
        body, name=name,
        grid_spec=pltpu.PrefetchScalarGridSpec(
            num_scalar_prefetch=1, grid=(rows // tr,),
            in_specs=[pl.BlockSpec((1, tr, cols), lambda i, p: (l, i, 0))],
            out_specs=pl.BlockSpec((1, tr, cols), lambda i, p: (p[0], i, 0))),
        out_shape=jax.ShapeDtypeStruct((N_CHIPS, rows, cols), BF16),
        compiler_params=_cp("parallel"),
    )(p_arr, w)


class WeightGatherer:
    PLAN = {
        (0, "in"): [("ici", 0, (1, 2, 3))],
        (0, "mid"): [("pass", 0, (1, 2, 3))],
        (0, "up"): [("ici", 1, (0, 1))],
        (0, "down"): [("pass", 1, (0, 1)), ("ici", 1, (2, 3))],
        (1, "in"): [("pass", 1, (2, 3)), ("ici", 2, (0, 1))],
        (1, "up"): [("pass", 2, (0, 1)), ("ici", 2, (2, 3))],
        (1, "down"): [("pass", 2, (2, 3))],
        (2, "in"): [("ici", 3, (0, 1))],
        (2, "up"): [("pass", 3, (0, 1)), ("ici", 3, (2, 3))],
        (2, "down"): [("pass", 3, (2, 3))],
    }

    def __init__(self, big_w):
        assert DEPTH == 4
        p_arr = (2 * lax.axis_index("x") + lax.axis_index("y")).astype(jnp.int32).reshape(1)
        self.w = [[place_own(f"place_own{t}", p_arr, w, l) for t, w in enumerate(big_w)] for l in range(DEPTH)]
        self.w[0][0:1] = run_alone("gather_first_chips", gather_ici_rider(self.w[0][0:1]))
        self.w[0][0:1] = run_alone("gather_first_cores", gather_pass_rider(self.w[0][0:1]))

    def rider(self, l, call):
        parts = [(gather_ici_rider if kind == "ici" else gather_pass_rider)([self.w[wl][t] for t in ts])
                 for kind, wl, ts in self.PLAN.get((l, call), ())]
        return functools.reduce(lambda a, b: a + b, parts) if parts else None

    def done(self, l, call, arrays):
        arrays = list(arrays)
        for _, wl, ts in self.PLAN[(l, call)]:
            for t in ts:
                self.w[wl][t] = arrays.pop(0)

    def alone(self, l, call):
        rider = self.rider(l, call)
        if rider:
            self.done(l, call, run_alone(f"gather_{call}", rider))


def run_alone(name, rider):
    n_in, n_out = len(rider.arrays), len(rider.out_shapes)

    def body(*refs):
        ins, outs, sems = refs[:n_in], refs[n_in:n_in + n_out], refs[n_in + n_out:]
        rider.start(ins, outs, sems)
        rider.finish(ins, outs, sems)

    return pl.pallas_call(
        body, name=name, in_specs=[ANY] * n_in, out_specs=[ANY] * n_out, out_shape=rider.out_shapes,
        scratch_shapes=[pltpu.SemaphoreType.DMA((n,)) for n in rider.sems],
        input_output_aliases=rider.aliases(0, 0),
    )(*rider.arrays)


def _both(cp):
    return (cp, cp)


def pair_exchange_rider(gs):
    n = len(gs)

    def ops(ins, outs, ssem, rsem):
        x, y, c, _, _ = _place()
        return [_both(pltpu.make_async_remote_copy(
            src_ref=ins[i].at[:, _half(1 - c, ins[i].shape[1]), :], dst_ref=outs[i],
            send_sem=ssem.at[i], recv_sem=rsem.at[i], device_id=(x, y, 1 - c), device_id_type=MESH)) for i in range(n)]

    return Rider(gs, [jax.ShapeDtypeStruct((N_CHIPS, g.shape[1] // 2, g.shape[2]), F32) for g in gs], [n, n], ops)


def gather_ici_rider(bufs):
    n = len(bufs)

    def ops(ins, outs, ssem, rsem):
        x, y, c, p, rels = _place()
        cps = []
        for i in range(n):
            mine = _half(c, outs[i].shape[1])
            for r, (rx, ry) in enumerate(rels):
                k = i * 3 + r
                peer = dict(device_id=(rx, ry, c), device_id_type=MESH, send_sem=ssem.at[k], recv_sem=rsem.at[k])
                own, landing = outs[i].at[p, mine], outs[i].at[2 * rx + ry, mine]
                cps.append((pltpu.make_async_remote_copy(src_ref=own, dst_ref=own, **peer),
                            pltpu.make_async_remote_copy(src_ref=landing, dst_ref=landing, **peer)))
        return cps

    return Rider(bufs, None, [3 * n, 3 * n], ops)


def gather_pass_rider(bufs):
    n = len(bufs)

    def ops(ins, outs, ssem, rsem):
        x, y, c, p, rels = _place()
        cps = []
        for i in range(n):
            rows = outs[i].shape[1]
            for r, (rx, ry) in enumerate(rels):
                k = i * 3 + r
                peer = dict(device_id=(x, y, 1 - c), device_id_type=MESH, send_sem=ssem.at[k], recv_sem=rsem.at[k])
                landed, theirs = outs[i].at[2 * rx + ry, _half(c, rows)], outs[i].at[2 * rx + ry, _half(1 - c, rows)]
                cps.append((pltpu.make_async_remote_copy(src_ref=landed, dst_ref=landed, **peer),
                            pltpu.make_async_remote_copy(src_ref=theirs, dst_ref=theirs, **peer)))
        return cps

    return Rider(bufs, None, [3 * n, 3 * n], ops)


def pair_add(name, c_arr, g, got):
    _, rows, cols = g.shape
    hr = rows // 2
    tr = 256
    nblk = hr // tr

    def body(c_ref, g_ref, r_ref, o32, o16):
        s = g_ref[...] + r_ref[...]
        o32[...] = s
        o16[...] = s.astype(BF16)

    blk = pl.BlockSpec((1, tr, cols), lambda pp, i, c_ref: (pp, i, 0))
    return pl.pallas_call(
        body, name=name,
        grid_spec=pltpu.PrefetchScalarGridSpec(
            num_scalar_prefetch=1, grid=(N_CHIPS, nblk),
            in_specs=[pl.BlockSpec((1, tr, cols), lambda pp, i, c_ref: (pp, c_ref[0] * nblk + i, 0)), blk],
            out_specs=(blk, blk)),
        out_shape=(jax.ShapeDtypeStruct((N_CHIPS, hr, cols), F32), jax.ShapeDtypeStruct((N_CHIPS, hr, cols), BF16)),
        compiler_params=_cp("parallel", "parallel"),
    )(c_arr, g, got)


def chip_exchange_rider(cs16):
    n = len(cs16)

    def ops(ins, outs, ssem, rsem):
        x, y, c, p, rels = _place()
        return [_both(pltpu.make_async_remote_copy(
            src_ref=ins[i].at[2 * rx + ry], dst_ref=outs[i].at[r], send_sem=ssem.at[i * 3 + r],
            recv_sem=rsem.at[i * 3 + r], device_id=(rx, ry, c), device_id_type=MESH))
            for i in range(n) for r, (rx, ry) in enumerate(rels)]

    return Rider(cs16, [jax.ShapeDtypeStruct((3,) + a.shape[1:], BF16) for a in cs16], [3 * n, 3 * n], ops)


def chip_sum(name, pc_arr, l, cs32, got, prev):
    _, hr, cols = cs32.shape
    tr = 256
    nblk = hr // tr

    def body(pc_ref, o_ref, g_ref, *rest):
        rest[-1][0] = ((o_ref[0] + g_ref[0].astype(F32)) + g_ref[1].astype(F32)) + g_ref[2].astype(F32)

    return pl.pallas_call(
        body, name=name,
        grid_spec=pltpu.PrefetchScalarGridSpec(
            num_scalar_prefetch=1, grid=(nblk,),
            in_specs=[pl.BlockSpec((1, tr, cols), lambda i, pc: (pc[0], i, 0)),
                      pl.BlockSpec((3, tr, cols), lambda i, pc: (0, i, 0))] + ([] if prev is None else [ANY]),
            out_specs=pl.BlockSpec((1, tr, cols), lambda i, pc: (l, pc[1] * nblk + i, 0))),
        out_shape=jax.ShapeDtypeStruct((DEPTH, 2 * hr, cols), F32),
        input_output_aliases={} if prev is None else {3: 0},
        compiler_params=_cp("arbitrary"),
    )(*((pc_arr, cs32, got) + (() if prev is None else (prev,))))


def grad_pair_share(halves):
    n_w = len(halves)
    n = n_w * DEPTH

    def body(*refs):
        bufs = refs[n_w:2 * n_w]
        ssem, rsem = refs[2 * n_w:]
        x, y, c, _, _ = _place()
        cps = []
        for t in range(n_w):
            for l in range(DEPTH):
                mine = bufs[t].at[l, _half(c, bufs[t].shape[1])]
                cp = pltpu.make_async_remote_copy(src_ref=mine, dst_ref=mine, send_sem=ssem.at[t * DEPTH + l],
                                                  recv_sem=rsem.at[t * DEPTH + l], device_id=(x, y, 1 - c),
                                                  device_id_type=MESH)
                cp.start()
                cps.append(cp)
        for t in range(n_w):
            for l in range(DEPTH):
                theirs = bufs[t].at[l, _half(1 - c, bufs[t].shape[1])]
                pltpu.make_async_remote_copy(src_ref=theirs, dst_ref=theirs, send_sem=ssem.at[t * DEPTH + l],
                                             recv_sem=rsem.at[t * DEPTH + l], device_id=(x, y, 1 - c),
                                             device_id_type=MESH).wait_recv()
        for cp in cps:
            cp.wait_send()

    return pl.pallas_call(
        body, name="grad_pair_share", in_specs=[ANY] * n_w, out_specs=[ANY] * n_w,
        out_shape=[jax.ShapeDtypeStruct(a.shape, F32) for a in halves],
        input_output_aliases={t: t for t in range(n_w)},
        scratch_shapes=[pltpu.SemaphoreType.DMA((n,)), pltpu.SemaphoreType.DMA((n,))],
    )(*halves)


SMALL_ROWS = 240


def small_allreduce(v):
    def body(v_ref, o_ref, buf, ssem, rsem):
        x, y, c, _, _ = _place()
        me = 4 * x + 2 * y + c
        buf[me] = v_ref[...]
        for d in range(N_DEV):
            @pl.when(me != d)
            def _():
                pltpu.make_async_remote_copy(
                    src_ref=v_ref, dst_ref=buf.at[me], send_sem=ssem.at[d], recv_sem=rsem.at[me],
                    device_id=(d // 4, (d // 2) % 2, d % 2), device_id_type=MESH).start()
        for d in range(N_DEV):
            @pl.when(me != d)
            def _():
                cp = pltpu.make_async_remote_copy(
                    src_ref=v_ref, dst_ref=buf.at[d], send_sem=ssem.at[d], recv_sem=rsem.at[d],
                    device_id=(d // 4, (d // 2) % 2, d % 2), device_id_type=MESH)
                cp.wait_recv()
                cp.wait_send()
        acc = buf[0]
        for d in range(1, N_DEV):
            acc = acc + buf[d]
        o_ref[...] = acc

    vm = pl.BlockSpec(memory_space=pltpu.VMEM)
    return pl.pallas_call(
        body, name="small_allreduce", in_specs=[vm], out_specs=vm,
        out_shape=jax.ShapeDtypeStruct(v.shape, F32),
        scratch_shapes=[pltpu.VMEM((N_DEV,) + v.shape, F32), pltpu.SemaphoreType.DMA((N_DEV,)),
                        pltpu.SemaphoreType.DMA((N_DEV,))],
    )(v)


class GradReducer:
    EARLY, LATE = (1, 2, 3), (0,)

    def __init__(self):
        self.c_arr = lax.axis_index("c").astype(jnp.int32).reshape(1)
        self.pc_arr = jnp.stack([2 * lax.axis_index("x") + lax.axis_index("y"), lax.axis_index("c")]).astype(jnp.int32)
        self.fresh = {}
        self.paired = {}
        self.acc = [None] * 4

    def push(self, group, l, gs):
        self.fresh[group] = (l, list(gs))

    def pair_rider(self, group):
        return pair_exchange_rider(self.fresh[group][1]) if group in self.fresh else None

    def pair_done(self, group, got):
        l, gs = self.fresh.pop(group)
        self.paired[group] = (l, [pair_add(f"pair_add{t}", self.c_arr, g, r) for t, g, r in zip(group, gs, got)])

    def chip_rider(self, group):
        return chip_exchange_rider([s16 for _, s16 in self.paired[group][1]]) if group in self.paired else None

    def chip_done(self, group, arrived):
        l, cs = self.paired.pop(group)
        for t, (s32, _), got in zip(group, cs, arrived):
            self.acc[t] = chip_sum(f"chip_sum{t}", self.pc_arr, l, s32, got, self.acc[t])

    def finish(self):
        self.pair_done(self.LATE, run_alone("grad_pair_exchange", self.pair_rider(self.LATE)))
        self.chip_done(self.LATE, run_alone("grad_chip_exchange", self.chip_rider(self.LATE)))
        return grad_pair_share(self.acc)


def adamw(name, w, g, m, v):
    shape = w.shape
    cols = shape[-1]
    flat = [t.reshape(-1, cols) for t in (w, g, m, v)]
    rows = flat[0].shape[0]
    tr = 128 if rows % 128 == 0 else rows

    def body(w_ref, g_ref, m_ref, v_ref, d_ref, mo_ref, vo_ref):
        gv = g_ref[...]
        mn = ADAM_B1 * m_ref[...] + (1.0 - ADAM_B1) * gv
        vn = ADAM_B2 * v_ref[...] + (1.0 - ADAM_B2) * jnp.square(gv)
        m_hat = mn / (1.0 - ADAM_B1 ** ADAM_STEP)
        v_hat = vn / (1.0 - ADAM_B2 ** ADAM_STEP)
        d_ref[...] = -ADAM_LR * (m_hat / (jnp.sqrt(v_hat) + ADAM_EPS) + ADAM_WD * w_ref[...])
        mo_ref[...] = mn
        vo_ref[...] = vn

    blk = pl.BlockSpec((tr, cols), lambda i: (i, 0))
    out = jax.ShapeDtypeStruct((rows, cols), F32)
    d, mo, vo = pl.pallas_call(
        body, name=name, grid=(rows // tr,), in_specs=[blk] * 4, out_specs=(blk, blk, blk),
        out_shape=(out, out, out), compiler_params=_cp("parallel"),
    )(*flat)
    return d.reshape(shape), mo.reshape(shape), vo.reshape(shape)


SMALL_NAMES = ("norm_mix", "norm_mlp", "hg_lb_fwd", "hg_lb_bwd", "hg_norm", "ret_norm", "q_norm", "k_norm", "rel_bias")


def _pack_small(d):
    flat = jnp.concatenate([d[k].reshape(-1) for k in SMALL_NAMES])
    return jnp.pad(flat, (0, SMALL_ROWS * 128 - flat.shape[0])).reshape(SMALL_ROWS, 128)


def _unpack_small(v, like):
    flat = v.reshape(-1)
    out, off = {}, 0
    for k in SMALL_NAMES:
        n = like[k].size
        out[k] = flat[off:off + n].reshape(like[k].shape)
        off += n
    return out


def kernel(x, w_in, w_out, w_up, w_down, norm_mix, norm_mlp, hg_lb_fwd, hg_lb_bwd, hg_norm, ret_norm, q_norm, k_norm, rel_bias, loss_target, m_w_in, m_w_out, m_w_up, m_w_down, m_norm_mix, m_norm_mlp, m_hg_lb_fwd, m_hg_lb_bwd, m_hg_norm, m_ret_norm, m_q_norm, m_k_norm, m_rel_bias, v_w_in, v_w_out, v_w_up, v_w_down, v_norm_mix, v_norm_mlp, v_hg_lb_fwd, v_hg_lb_bwd, v_hg_norm, v_ret_norm, v_q_norm, v_k_norm, v_rel_bias):
    big_w = (w_in, w_out, w_up, w_down)
    big_m = (m_w_in, m_w_out, m_w_up, m_w_down)
    big_v = (v_w_in, v_w_out, v_w_up, v_w_down)
    small_w = dict(zip(SMALL_NAMES, (norm_mix, norm_mlp, hg_lb_fwd, hg_lb_bwd, hg_norm, ret_norm, q_norm, k_norm, rel_bias)))
    small_m = dict(zip(SMALL_NAMES, (m_norm_mix, m_norm_mlp, m_hg_lb_fwd, m_hg_lb_bwd, m_hg_norm, m_ret_norm, m_q_norm,
                                     m_k_norm, m_rel_bias)))
    small_v = dict(zip(SMALL_NAMES, (v_norm_mix, v_norm_mlp, v_hg_lb_fwd, v_hg_lb_bwd, v_hg_norm, v_ret_norm, v_q_norm,
                                     v_k_norm, v_rel_bias)))

    gatherer = WeightGatherer(big_w)
    loss_part, dx, grads_big, sg, (lbf_vjp, lbb_vjp) = local_step(x[0], loss_target[0], gatherer.w, small_w,
                                                                  GradReducer(), gatherer)
    loss = lax.psum(loss_part, ("x", "y", "c"))

    sg = dict(sg)
    sg["hg_lb_fwd"], sg["hg_lb_bwd"] = sg.pop("lbf"), sg.pop("lbb")
    tot = _unpack_small(small_allreduce(_pack_small(sg)), small_w)
    tot["hg_lb_fwd"] = lbf_vjp(tot["hg_lb_fwd"])[0]
    tot["hg_lb_bwd"] = lbb_vjp(tot["hg_lb_bwd"])[0]
    grads_small = [tot[k] for k in SMALL_NAMES]

    upd_big = [adamw(f"adamw_big{t}", big_w[t], grads_big[t], big_m[t], big_v[t]) for t in range(4)]
    d_s, m_s, v_s = adamw("adamw_small", _pack_small(small_w), _pack_small(tot), _pack_small(small_m), _pack_small(small_v))
    upd_small = [_unpack_small(t, small_w) for t in (d_s, m_s, v_s)]

    outs = [loss, dx[None]] + list(grads_big) + grads_small
    for j in range(3):
        outs += [u[j] for u in upd_big] + [upd_small[j][k] for k in SMALL_NAMES]
    return tuple(outs)
```
